```python
import math
import jax
import jax.numpy as jnp
from jax import lax
import numpy as np

D_MODEL = 1024
BATCH = 4
SEQ = 4096
DEPTH = 2
DEC_BATCH = 32
DEC_SEQ = 32
PAST_LEN = 2048

CHUNK = 64
N_META = 16
N_MIXERS = 2
N_GLA_LAYERS = (DEPTH + 1) // 2
N_MLA_LAYERS = DEPTH // 2
D_FF = 2816
RMS_EPS = 1e-6
GLA_HEADS = 4
GLA_KEY_DIM = D_MODEL // 2
GLA_VALUE_DIM = D_MODEL
GLA_DK = GLA_KEY_DIM // GLA_HEADS
GLA_DV = GLA_VALUE_DIM // GLA_HEADS
GLA_GATE_RANK = 16
GLA_GATE_NORMALIZER = 16.0
GLA_BLOCK = CHUNK
GLA_IN_DIM = 2 * GLA_KEY_DIM + 2 * GLA_VALUE_DIM + GLA_GATE_RANK
MLA_HEADS = 8
MLA_Q_RANK = 384
MLA_KV_RANK = 256
MLA_NOPE = 128
MLA_ROPE = 64
MLA_V = 128
MLA_DOWN_DIM = MLA_Q_RANK + MLA_KV_RANK + MLA_ROPE
ROPE_THETA = 10000.0
Q_BLOCK = 128

kernel_name = "hybrid_gla_mla_macaron_stream_step"


def rmsnorm(x, w):
    x32 = x.astype(jnp.float32)
    y = x32 * lax.rsqrt(jnp.mean(x32 * x32, axis=-1, keepdims=True) + RMS_EPS)
    return (y * w.astype(jnp.float32)).astype(x.dtype)


def swiglu_half(h, norm_w, w_gate, w_up, w_down):
    hn = rmsnorm(h, norm_w)
    return h + 0.5 * ((jax.nn.silu(hn @ w_gate) * (hn @ w_up)) @ w_down)


def apply_rope(x, pos):
    half = x.shape[-1] // 2
    inv = ROPE_THETA ** (-jnp.arange(half, dtype=jnp.float32) / half)
    ang = pos[:, None] * inv[None, :]
    cos = jnp.cos(ang)[None, :, None, :]
    sin = jnp.sin(ang)[None, :, None, :]
    x32 = x.astype(jnp.float32)
    x1, x2 = x32[..., :half], x32[..., half:]
    return jnp.concatenate([x1 * cos - x2 * sin, x2 * cos + x1 * sin], axis=-1).astype(x.dtype)


def gla_mixer(xn, s0, w_in, w_gate_up, b_gate, head_norm, w_out):
    B, L, _ = xn.shape
    proj = xn @ w_in
    q, k, v, r, gd = jnp.split(proj, [GLA_KEY_DIM, 2 * GLA_KEY_DIM, 2 * GLA_KEY_DIM + GLA_VALUE_DIM,
                                      2 * GLA_KEY_DIM + 2 * GLA_VALUE_DIM], axis=-1)
    glog = jax.nn.log_sigmoid((gd @ w_gate_up + b_gate).astype(jnp.float32)) / GLA_GATE_NORMALIZER
    pad = (-L) % GLA_BLOCK
    nb = (L + pad) // GLA_BLOCK

    def blocks(t, hd):
        t = t.astype(jnp.float32).reshape(B, L, GLA_HEADS, hd)
        t = jnp.pad(t, ((0, 0), (pad, 0), (0, 0), (0, 0)))
        return t.reshape(B, nb, GLA_BLOCK, GLA_HEADS, hd).transpose(1, 0, 3, 2, 4)

    qb = blocks(q, GLA_DK) * (GLA_DK ** -0.5)
    kb = blocks(k, GLA_DK)
    vb = blocks(v, GLA_DV)
    gb = jnp.cumsum(blocks(glog, GLA_DK), axis=-2)
    causal = jnp.tril(jnp.ones((GLA_BLOCK, GLA_BLOCK), dtype=bool))

    def step(S, inp):
        qc, kc, vc, gc = inp
        qg = qc * jnp.exp(gc)
        a = jnp.einsum('bhid,bhjd->bhij', qg, kc * jnp.exp(-gc))
        a = jnp.where(causal, a, 0.0)
        o = jnp.einsum('bhij,bhjv->bhiv', a, vc) + jnp.einsum('bhid,bhdv->bhiv', qg, S)
        glast = gc[:, :, -1, :]
        kd = kc * jnp.exp(glast[:, :, None, :] - gc)
        S = S * jnp.exp(glast)[..., None] + jnp.einsum('bhjd,bhjv->bhdv', kd, vc)
        return S, o

    S, o = lax.scan(step, s0.astype(jnp.float32), (qb, kb, vb, gb))
    o = o.transpose(1, 0, 3, 2, 4).reshape(B, nb * GLA_BLOCK, GLA_HEADS, GLA_DV)[:, pad:]
    o = rmsnorm(o, head_norm) * jax.nn.silu(r.astype(jnp.float32).reshape(B, L, GLA_HEADS, GLA_DV))
    y = o.reshape(B, L, GLA_VALUE_DIM).astype(xn.dtype) @ w_out
    return y, S.astype(xn.dtype)


def chunk_causal_attention(q_lat, q_rope, k_lat, k_rope, q_ids, k_ids):
    B, Lq, H, R = q_lat.shape
    qblk = min(Q_BLOCK, Lq)
    pad = (-Lq) % qblk
    nb = (Lq + pad) // qblk
    q_ids = jnp.pad(q_ids, (0, pad), constant_values=np.iinfo(np.int32).max)
    scale = (MLA_NOPE + MLA_ROPE) ** -0.5

    def to_blocks(t):
        t = jnp.pad(t, ((0, 0), (0, pad), (0, 0), (0, 0)))
        return t.reshape(B, nb, qblk, H, t.shape[-1]).transpose(1, 0, 2, 3, 4)

    def attend(blk):
        ql, qr, qi = blk
        s = (jnp.einsum('bqhr,bkr->bhqk', ql, k_lat).astype(jnp.float32)
             + jnp.einsum('bqhe,bke->bhqk', qr, k_rope).astype(jnp.float32)) * scale
        mask = (k_ids[None, :] <= qi[:, None])[None, None]
        p = jax.nn.softmax(jnp.where(mask, s, -jnp.inf), axis=-1).astype(k_lat.dtype)
        return jnp.einsum('bhqk,bkr->bqhr', p, k_lat)

    o = lax.map(attend, (to_blocks(q_lat), to_blocks(q_rope), q_ids.reshape(nb, qblk)))
    return o.transpose(1, 0, 2, 3, 4).reshape(B, nb * qblk, H, R)[:, :Lq]


def mla_mixer(xn, pos, q_ids, k_ids, past_lat, past_rope, w_down, q_norm, w_uq, kv_norm, w_uk, w_uv, w_out):
    B, L, _ = xn.shape
    cq, ckv, kr = jnp.split(xn @ w_down, [MLA_Q_RANK, MLA_Q_RANK + MLA_KV_RANK], axis=-1)
    q = (rmsnorm(cq, q_norm) @ w_uq).reshape(B, L, MLA_HEADS, MLA_NOPE + MLA_ROPE)
    q_nope = q[..., :MLA_NOPE]
    q_rope = apply_rope(q[..., MLA_NOPE:], pos)
    ckv = rmsnorm(ckv, kv_norm)
    kr = apply_rope(kr[:, :, None, :], pos)[:, :, 0]
    q_lat = jnp.einsum('blhn,rhn->blhr', q_nope, w_uk)
    if past_lat is None:
        keys_lat, keys_rope = ckv, kr
    else:
        keys_lat = jnp.concatenate([past_lat.astype(ckv.dtype), ckv], axis=1)
        keys_rope = jnp.concatenate([past_rope.astype(kr.dtype), kr], axis=1)
    o_lat = chunk_causal_attention(q_lat, q_rope, keys_lat, keys_rope, q_ids, k_ids)
    o = jnp.einsum('blhr,rhv->blhv', o_lat, w_uv).reshape(B, L, MLA_HEADS * MLA_V)
    return o @ w_out, ckv, kr


def setup_inputs(seed: int = 0) -> dict:
    key = jax.random.key(seed)
    ks = jax.random.split(key, 32)
    f32 = jnp.float32

    def nrm(k, shape, scale):
        return jax.random.normal(k, shape, f32) * scale

    def gain(k, shape):
        return 1.0 + 0.02 * jax.random.normal(k, shape, f32)

    return {
        "x_prompt": nrm(ks[0], (BATCH, SEQ, D_MODEL), 1.0),
        "x_sample": nrm(ks[1], (DEC_BATCH, DEC_SEQ, D_MODEL), 1.0),
        "state_gla": nrm(ks[2], (N_GLA_LAYERS, DEC_BATCH, GLA_HEADS, GLA_DK, GLA_DV), 1.0),
        "cache_mla_latent": nrm(ks[3], (N_MLA_LAYERS, DEC_BATCH, PAST_LEN, MLA_KV_RANK), 1.0),
        "cache_mla_rope": nrm(ks[4], (N_MLA_LAYERS, DEC_BATCH, PAST_LEN, MLA_ROPE), 1.0),
        "meta_tokens": nrm(ks[5], (N_META, D_MODEL), 1.0),
        "ffn1_norm": gain(ks[6], (DEPTH, D_MODEL)),
        "ffn1_w_gate": nrm(ks[7], (DEPTH, D_MODEL, D_FF), D_MODEL ** -0.5),
        "ffn1_w_up": nrm(ks[8], (DEPTH, D_MODEL, D_FF), D_MODEL ** -0.5),
        "ffn1_w_down": nrm(ks[9], (DEPTH, D_FF, D_MODEL), D_FF ** -0.5),
        "mix_norm": gain(ks[10], (DEPTH, D_MODEL)),
        "gla_w_in": nrm(ks[11], (N_GLA_LAYERS, D_MODEL, GLA_IN_DIM), D_MODEL ** -0.5),
        "gla_w_gate_up": nrm(ks[12], (N_GLA_LAYERS, GLA_GATE_RANK, GLA_KEY_DIM), GLA_GATE_RANK ** -0.5),
        "gla_b_gate": nrm(ks[13], (N_GLA_LAYERS, GLA_KEY_DIM), 0.01),
        "gla_head_norm": gain(ks[14], (N_GLA_LAYERS, GLA_DV)),
        "gla_w_out": nrm(ks[15], (N_GLA_LAYERS, GLA_VALUE_DIM, D_MODEL), GLA_VALUE_DIM ** -0.5),
        "mla_w_down": nrm(ks[16], (N_MLA_LAYERS, D_MODEL, MLA_DOWN_DIM), D_MODEL ** -0.5),
        "mla_q_norm": gain(ks[17], (N_MLA_LAYERS, MLA_Q_RANK)),
        "mla_w_uq": nrm(ks[18], (N_MLA_LAYERS, MLA_Q_RANK, MLA_HEADS * (MLA_NOPE + MLA_ROPE)), MLA_Q_RANK ** -0.5),
        "mla_kv_norm": gain(ks[19], (N_MLA_LAYERS, MLA_KV_RANK)),
        "mla_w_uk": nrm(ks[20], (N_MLA_LAYERS, MLA_KV_RANK, MLA_HEADS, MLA_NOPE), MLA_KV_RANK ** -0.5),
        "mla_w_uv": nrm(ks[21], (N_MLA_LAYERS, MLA_KV_RANK, MLA_HEADS, MLA_V), MLA_KV_RANK ** -0.5),
        "mla_w_out": nrm(ks[22], (N_MLA_LAYERS, MLA_HEADS * MLA_V, D_MODEL), (MLA_HEADS * MLA_V) ** -0.5),
        "ffn2_norm": gain(ks[23], (DEPTH, D_MODEL)),
        "ffn2_w_gate": nrm(ks[24], (DEPTH, D_MODEL, D_FF), D_MODEL ** -0.5),
        "ffn2_w_up": nrm(ks[25], (DEPTH, D_MODEL, D_FF), D_MODEL ** -0.5),
        "ffn2_w_down": nrm(ks[26], (DEPTH, D_FF, D_MODEL), D_FF ** -0.5),
        "final_norm": gain(ks[27], (D_MODEL,)),
    }


def reference(x_prompt, x_sample, state_gla, cache_mla_latent, cache_mla_rope, meta_tokens,
              ffn1_norm, ffn1_w_gate, ffn1_w_up, ffn1_w_down, mix_norm,
              gla_w_in, gla_w_gate_up, gla_b_gate, gla_head_norm, gla_w_out,
              mla_w_down, mla_q_norm, mla_w_uq, mla_kv_norm, mla_w_uk, mla_w_uv, mla_w_out,
              ffn2_norm, ffn2_w_gate, ffn2_w_up, ffn2_w_down, final_norm):
    Bp, n_frames, _ = x_prompt.shape
    Bs, Ls, _ = x_sample.shape
    past = cache_mla_latent.shape[2]
    hp = jnp.concatenate([jnp.broadcast_to(meta_tokens[None].astype(x_prompt.dtype), (Bp, N_META, D_MODEL)),
                          x_prompt], axis=1)
    Lp = hp.shape[1]
    pos_p = jnp.arange(Lp, dtype=jnp.float32)
    ids_p = jnp.concatenate([jnp.full((N_META,), -1, jnp.int32),
                             jnp.arange(n_frames, dtype=jnp.int32) // CHUNK])
    hs = x_sample
    frame_s = past + jnp.arange(Ls, dtype=jnp.int32)
    pos_s = frame_s.astype(jnp.float32)
    ids_s = frame_s // CHUNK
    kids_s = jnp.concatenate([jnp.arange(past, dtype=jnp.int32) // CHUNK, ids_s])

    gla_sp, gla_ss, lat_p, rope_p, lat_s, rope_s = [], [], [], [], [], []
    for i in range(DEPTH):
        hp = swiglu_half(hp, ffn1_norm[i], ffn1_w_gate[i], ffn1_w_up[i], ffn1_w_down[i])
        hs = swiglu_half(hs, ffn1_norm[i], ffn1_w_gate[i], ffn1_w_up[i], ffn1_w_down[i])
        xp = rmsnorm(hp, mix_norm[i])
        xs = rmsnorm(hs, mix_norm[i])
        j = i // N_MIXERS
        if i % N_MIXERS == 0:
            s0 = jnp.zeros((Bp, GLA_HEADS, GLA_DK, GLA_DV), xp.dtype)
            dp, sp = gla_mixer(xp, s0, gla_w_in[j], gla_w_gate_up[j], gla_b_gate[j], gla_head_norm[j], gla_w_out[j])
            ds, ss = gla_mixer(xs, state_gla[j], gla_w_in[j], gla_w_gate_up[j], gla_b_gate[j], gla_head_norm[j], gla_w_out[j])
            gla_sp.append(sp)
            gla_ss.append(ss)
        else:
            dp, cp, rp = mla_mixer(xp, pos_p, ids_p, ids_p, None, None, mla_w_down[j], mla_q_norm[j], mla_w_uq[j],
                                   mla_kv_norm[j], mla_w_uk[j], mla_w_uv[j], mla_w_out[j])
            ds, cs, rs = mla_mixer(xs, pos_s, ids_s, kids_s, cache_mla_latent[j], cache_mla_rope[j], mla_w_down[j],
                                   mla_q_norm[j], mla_w_uq[j], mla_kv_norm[j], mla_w_uk[j], mla_w_uv[j], mla_w_out[j])
            lat_p.append(cp)
            rope_p.append(rp)
            lat_s.append(cs)
            rope_s.append(rs)
        hp = hp + dp
        hs = hs + ds
        hp = swiglu_half(hp, ffn2_norm[i], ffn2_w_gate[i], ffn2_w_up[i], ffn2_w_down[i])
        hs = swiglu_half(hs, ffn2_norm[i], ffn2_w_gate[i], ffn2_w_up[i], ffn2_w_down[i])

    y_prompt = rmsnorm(hp[:, N_META:], final_norm)
    y_sample = rmsnorm(hs, final_norm)
    return (y_prompt, y_sample, jnp.stack(gla_sp, 0), jnp.stack(gla_ss, 0),
            jnp.stack(lat_p, 0), jnp.stack(rope_p, 0), jnp.stack(lat_s, 0), jnp.stack(rope_s, 0))
```

```python
import functools

import jax
import jax.numpy as jnp
from jax import lax
from jax.experimental import pallas as pl
from jax.experimental.pallas import tpu as pltpu

F32 = jnp.float32
BF16 = jnp.bfloat16

D_MODEL = 1024
D_FF = 2816
RMS_EPS = 1e-6
N_META = 16
CHUNK = 64

GLA_HEADS = 4
GLA_DK = 128
GLA_DV = 256
GLA_KEY = GLA_HEADS * GLA_DK
GLA_VAL = GLA_HEADS * GLA_DV
GLA_RANK = 16
GLA_GATE_NORMALIZER = 16.0

MLA_HEADS = 8
MLA_Q_RANK = 384
MLA_KV_RANK = 256
MLA_NOPE = 128
MLA_ROPE = 64
MLA_V = 128
ROPE_THETA = 10000.0
QK_W = MLA_KV_RANK + 128

LANES = 128
VMEM_LIMIT = 48 * 1024 * 1024

TM = 448
FF_CHUNK = 256
TQ = 256

NT_DIMS = (((1,), (1,)), ((), ()))
TN_DIMS = (((0,), (0,)), ((), ()))


def _dot(a, b):
    return jnp.dot(a, b, preferred_element_type=F32)


def _dot_nt(a, b):
    return lax.dot_general(a, b, NT_DIMS, preferred_element_type=F32)


def _dot_tn(a, b):
    return lax.dot_general(a, b, TN_DIMS, preferred_element_type=F32)


def _rms(x, w):
    return x * lax.rsqrt(jnp.mean(x * x, axis=-1, keepdims=True) + RMS_EPS) * w


def _silu(x):
    return x * jax.nn.sigmoid(x)


def _params(*sem):
    return pltpu.CompilerParams(dimension_semantics=sem, vmem_limit_bytes=VMEM_LIMIT)


def _resident(shape):
    zeros = (0,) * len(shape)
    return pl.BlockSpec(shape, lambda *_: zeros, pipeline_mode=pl.Buffered(1))


def _rows(width, tm=TM):
    return pl.BlockSpec((tm, width), lambda i: (i, 0))


def _ffn_kernel(x_ref, nw_ref, wg_ref, wu_ref, wd_ref, *rest, final_norm):
    if final_norm:
        fw_ref, o_ref, h_scr = rest
    else:
        o_ref, h_scr = rest
    x = x_ref[...]
    xn = _rms(x, nw_ref[...]).astype(BF16)
    for c in range(D_FF // FF_CHUNK):
        sl = slice(c * FF_CHUNK, (c + 1) * FF_CHUNK)
        g = _dot(xn, wg_ref[:, sl])
        u = _dot(xn, wu_ref[:, sl])
        h_scr[:, sl] = (_silu(g) * u).astype(BF16)
    y = x + 0.5 * _dot(h_scr[...], wd_ref[...])
    if final_norm:
        y = _rms(y, fw_ref[...])
    o_ref[...] = y


def _ffn(x, norm_w, wg, wu, wd, final_w=None):
    n = x.shape[0]
    final_norm = final_w is not None
    ins = [x, norm_w.reshape(1, D_MODEL), wg, wu, wd]
    specs = [_rows(D_MODEL), _resident((1, D_MODEL)), _resident((D_MODEL, D_FF)),
             _resident((D_MODEL, D_FF)), _resident((D_FF, D_MODEL))]
    if final_norm:
        ins.append(final_w.reshape(1, D_MODEL))
        specs.append(_resident((1, D_MODEL)))
    return pl.pallas_call(
        functools.partial(_ffn_kernel, final_norm=final_norm),
        grid=(n // TM,),
        in_specs=specs,
        out_specs=_rows(D_MODEL),
        out_shape=jax.ShapeDtypeStruct((n, D_MODEL), F32),
        scratch_shapes=[pltpu.VMEM((TM, D_FF), BF16)],
        compiler_params=_params("parallel"),
        name="ffn_final" if final_norm else "ffn",
    )(*ins)


def _gla_proj_kernel(x_ref, nw_ref, wm_ref, wgd_ref, wgu_ref, bg_ref,
                     q_ref, k_ref, v_ref, r_ref, gl_ref):
    xn = _rms(x_ref[...], nw_ref[...]).astype(BF16)
    q_ref[...] = _dot(xn, wm_ref[:, 0:GLA_KEY])
    k_ref[...] = _dot(xn, wm_ref[:, GLA_KEY:2 * GLA_KEY])
    v_ref[...] = _dot(xn, wm_ref[:, 2 * GLA_KEY:2 * GLA_KEY + GLA_VAL]).astype(BF16)
    r_ref[...] = _dot(xn, wm_ref[:, 2 * GLA_KEY + GLA_VAL:]).astype(BF16)
    gd = _dot(xn, wgd_ref[...])
    z = _dot(gd.astype(BF16), wgu_ref[...]) + bg_ref[...]
    log_sig = jnp.minimum(z, 0.0) - jnp.log1p(jnp.exp(-jnp.abs(z)))
    gl_ref[...] = log_sig * (1.0 / GLA_GATE_NORMALIZER)


def _gla_proj(x, norm_w, w_main, w_gd, w_gu, b_gate):
    n = x.shape[0]
    main_w = 2 * GLA_KEY + 2 * GLA_VAL
    return pl.pallas_call(
        _gla_proj_kernel,
        grid=(n // TM,),
        in_specs=[_rows(D_MODEL), _resident((1, D_MODEL)), _resident((D_MODEL, main_w)),
                  _resident((D_MODEL, LANES)), _resident((LANES, GLA_KEY)), _resident((1, GLA_KEY))],
        out_specs=[_rows(GLA_KEY), _rows(GLA_KEY), _rows(GLA_VAL), _rows(GLA_VAL), _rows(GLA_KEY)],
        out_shape=[jax.ShapeDtypeStruct((n, GLA_KEY), F32), jax.ShapeDtypeStruct((n, GLA_KEY), F32),
                   jax.ShapeDtypeStruct((n, GLA_VAL), BF16), jax.ShapeDtypeStruct((n, GLA_VAL), BF16),
                   jax.ShapeDtypeStruct((n, GLA_KEY), F32)],
        compiler_params=_params("parallel"),
        name="gla_proj",
    )(x, norm_w.reshape(1, D_MODEL), w_main, w_gd, w_gu, b_gate.reshape(1, GLA_KEY))


def _gla_scan_kernel(*refs, chunk, n_chunks, n_steps, has_s0, has_prev):
    refs = list(refs)
    q_ref, k_ref, v_ref, gl_ref = refs[:4]
    pos = 4
    s0_ref = None
    if has_s0:
        s0_ref = refs[pos]
        pos += 1
    if has_prev:
        pos += 1
    o_ref, s_out_ref, st_scr = refs[pos:pos + 3]
    step = pl.program_id(1)

    @pl.when(step == 0)
    def _():
        for h in range(GLA_HEADS):
            if has_s0:
                st_scr[h] = s0_ref[0, h].T
            else:
                st_scr[h] = jnp.zeros((GLA_DV, GLA_DK), F32)

    row = lax.broadcasted_iota(jnp.int32, (chunk, chunk), 0)
    col = lax.broadcasted_iota(jnp.int32, (chunk, chunk), 1)
    causal = row >= col
    tri = jnp.where(causal, 1.0, 0.0).astype(BF16)
    scale = GLA_DK ** -0.5

    for c in range(n_chunks):
        rows = slice(c * chunk, (c + 1) * chunk)
        glog = gl_ref[rows, :]
        hi = glog.astype(BF16)
        lo = (glog - hi.astype(F32)).astype(BF16)
        gc = _dot(tri, hi) + _dot(tri, lo)
        glast = gc[chunk - 1:chunk, :]
        q = q_ref[rows, :] * scale
        k = k_ref[rows, :]
        qg = (q * jnp.exp(gc)).astype(BF16)
        kg = (k * jnp.exp(-gc)).astype(BF16)
        kd = (k * jnp.exp(glast - gc)).astype(BF16)
        decay = jnp.exp(glast)
        for h in range(GLA_HEADS):
            ks = slice(h * GLA_DK, (h + 1) * GLA_DK)
            vs = slice(h * GLA_DV, (h + 1) * GLA_DV)
            a = jnp.where(causal, _dot_nt(qg[:, ks], kg[:, ks]), 0.0).astype(BF16)
            vh = v_ref[rows, vs]
            st = st_scr[h]
            o_ref[rows, vs] = _dot(a, vh) + _dot_nt(qg[:, ks], st.astype(BF16))
            st_scr[h] = st * decay[:, ks] + _dot_tn(vh, kd[:, ks])

    @pl.when(step == n_steps - 1)
    def _():
        for h in range(GLA_HEADS):
            s_out_ref[0, h] = st_scr[h].T


def _gla_scan(q, k, v, gl, s0, o_prev, *, row0, n_batch, n_steps, chunk, n_chunks, name):
    n = q.shape[0]
    rb = chunk * n_chunks
    base = row0 // rb

    def rmap(b, s):
        return (base + b * n_steps + s, 0)

    def smap(b, s):
        return (b, 0, 0, 0)

    ins = [q, k, v, gl]
    specs = [pl.BlockSpec((rb, GLA_KEY), rmap), pl.BlockSpec((rb, GLA_KEY), rmap),
             pl.BlockSpec((rb, GLA_VAL), rmap), pl.BlockSpec((rb, GLA_KEY), rmap)]
    state_block = (1, GLA_HEADS, GLA_DK, GLA_DV)
    if s0 is not None:
        ins.append(s0)
        specs.append(pl.BlockSpec(state_block, smap))
    aliases = {}
    if o_prev is not None:
        aliases = {len(ins): 0}
        ins.append(o_prev)
        specs.append(pl.BlockSpec(memory_space=pl.ANY))
    kern = functools.partial(_gla_scan_kernel, chunk=chunk, n_chunks=n_chunks, n_steps=n_steps,
                             has_s0=s0 is not None, has_prev=o_prev is not None)
    return pl.pallas_call(
        kern,
        grid=(n_batch, n_steps),
        in_specs=specs,
        out_specs=[pl.BlockSpec((rb, GLA_VAL), rmap), pl.BlockSpec(state_block, smap)],
        out_shape=[jax.ShapeDtypeStruct((n, GLA_VAL), F32),
                   jax.ShapeDtypeStruct((n_batch,) + state_block[1:], F32)],
        scratch_shapes=[pltpu.VMEM((GLA_HEADS, GLA_DV, GLA_DK), F32)],
        input_output_aliases=aliases,
        compiler_params=_params("parallel", "arbitrary"),
        name=name,
    )(*ins)


def _gla_out_kernel(o_ref, r_ref, h_ref, hn_ref, wo_ref, y_ref):
    parts = []
    for h in range(GLA_HEADS):
        vs = slice(h * GLA_DV, (h + 1) * GLA_DV)
        on = _rms(o_ref[:, vs], hn_ref[...])
        parts.append((on * _silu(r_ref[:, vs].astype(F32))).astype(BF16))
    y_ref[...] = h_ref[...] + _dot(jnp.concatenate(parts, axis=1), wo_ref[...])


def _gla_out(o, r, h, head_norm, w_out):
    n = h.shape[0]
    return pl.pallas_call(
        _gla_out_kernel,
        grid=(n // TM,),
        in_specs=[_rows(GLA_VAL), _rows(GLA_VAL), _rows(D_MODEL), _resident((1, GLA_DV)),
                  _resident((GLA_VAL, D_MODEL))],
        out_specs=_rows(D_MODEL),
        out_shape=jax.ShapeDtypeStruct((n, D_MODEL), F32),
        compiler_params=_params("parallel"),
        name="gla_out",
    )(o, r, h, head_norm.reshape(1, GLA_DV), w_out)


def _mla_proj_kernel(x_ref, nw_ref, wd_ref, qn_ref, wqn_ref, wqr_ref, kvn_ref, wuk_ref,
                     cos_ref, sin_ref, q_out, kc_out, ckv_out, kr_out):
    tm = x_ref.shape[0]
    scale = (MLA_NOPE + MLA_ROPE) ** -0.5
    xn = _rms(x_ref[...], nw_ref[...]).astype(BF16)
    xd = _dot(xn, wd_ref[...])
    cqn = _rms(xd[:, :MLA_Q_RANK], qn_ref[...]).astype(BF16)
    ckv = _rms(xd[:, MLA_Q_RANK:MLA_Q_RANK + MLA_KV_RANK], kvn_ref[...])
    cos_t = cos_ref[...]
    sin_t = sin_ref[...]
    lane = lax.broadcasted_iota(jnp.int32, (tm, LANES), 1)
    first_half = (lane & (MLA_ROPE // 2)) == 0

    def rope(x):
        swapped = jnp.where(first_half, pltpu.roll(x, LANES - MLA_ROPE // 2, 1),
                            pltpu.roll(x, MLA_ROPE // 2, 1))
        return x * cos_t + swapped * sin_t

    kr2 = rope(xd[:, MLA_Q_RANK + MLA_KV_RANK:])
    ckv_out[...] = ckv
    kr_out[...] = kr2[:, :MLA_ROPE]
    kc_out[:, :MLA_KV_RANK] = ckv.astype(BF16)
    kc_out[:, MLA_KV_RANK:] = jnp.where(lane < MLA_ROPE, kr2, 0.0).astype(BF16)

    qn = _dot(cqn, wqn_ref[...])
    qr = _dot(cqn, wqr_ref[...])
    for c in range(MLA_HEADS // 2):
        rr = rope(qr[:, c * LANES:(c + 1) * LANES]) * scale
        for e in range(2):
            h = 2 * c + e
            ql = _dot(qn[:, h * MLA_NOPE:(h + 1) * MLA_NOPE].astype(BF16), wuk_ref[h]) * scale
            q_out[h, :, :MLA_KV_RANK] = ql.astype(BF16)
            rot = rr if e == 0 else pltpu.roll(rr, MLA_ROPE, 1)
            q_out[h, :, MLA_KV_RANK:] = rot.astype(BF16)


def _mla_proj(x, norm_w, w_down, q_norm, w_qn, w_qr, kv_norm, w_uk_t, cos_t, sin_t):
    n = x.shape[0]
    return pl.pallas_call(
        _mla_proj_kernel,
        grid=(n // TM,),
        in_specs=[_rows(D_MODEL), _resident((1, D_MODEL)), _resident(w_down.shape),
                  _resident((1, MLA_Q_RANK)), _resident(w_qn.shape), _resident(w_qr.shape),
                  _resident((1, MLA_KV_RANK)), _resident(w_uk_t.shape),
                  _rows(LANES), _rows(LANES)],
        out_specs=[pl.BlockSpec((MLA_HEADS, TM, QK_W), lambda i: (0, i, 0)),
                   _rows(QK_W), _rows(MLA_KV_RANK), _rows(MLA_ROPE)],
        out_shape=[jax.ShapeDtypeStruct((MLA_HEADS, n, QK_W), BF16),
                   jax.ShapeDtypeStruct((n, QK_W), BF16),
                   jax.ShapeDtypeStruct((n, MLA_KV_RANK), F32),
                   jax.ShapeDtypeStruct((n, MLA_ROPE), F32)],
        compiler_params=_params("parallel"),
        name="mla_proj",
    )(x, norm_w.reshape(1, D_MODEL), w_down, q_norm.reshape(1, MLA_Q_RANK), w_qn, w_qr,
      kv_norm.reshape(1, MLA_KV_RANK), w_uk_t, cos_t, sin_t)


def _attn_prompt_kernel(q_ref, kf_ref, km_ref, o_ref, m_scr, l_scr, acc_scr):
    i = pl.program_id(1)
    rows = MLA_HEADS * TQ
    q = q_ref[...].reshape(rows, QK_W)

    def update(kt, mask, first):
        s = _dot_nt(q, kt)
        if mask is not None:
            s = jnp.where(mask, s, -jnp.inf)
        m_cur = jnp.max(s, axis=1, keepdims=True)
        if first:
            p = jnp.exp(s - m_cur)
            m_scr[...] = m_cur
            l_scr[...] = jnp.sum(p, axis=1, keepdims=True)
            acc_scr[...] = _dot(p.astype(BF16), kt[:, :MLA_KV_RANK])
        else:
            m_old = m_scr[...]
            m_new = jnp.maximum(m_old, m_cur)
            alpha = jnp.exp(m_old - m_new)
            p = jnp.exp(s - m_new)
            m_scr[...] = m_new
            l_scr[...] = alpha * l_scr[...] + jnp.sum(p, axis=1, keepdims=True)
            acc_scr[...] = alpha * acc_scr[...] + _dot(p.astype(BF16), kt[:, :MLA_KV_RANK])

    update(km_ref[...], None, True)

    def body(j, carry):
        update(kf_ref[pl.ds(pl.multiple_of(j * TQ, TQ), TQ), :], None, False)
        return carry

    lax.fori_loop(0, i, body, 0)

    row = lax.broadcasted_iota(jnp.int32, (rows, TQ), 0)
    col = lax.broadcasted_iota(jnp.int32, (rows, TQ), 1)
    visible = (col // CHUNK) <= ((row % TQ) // CHUNK)
    update(kf_ref[pl.ds(pl.multiple_of(i * TQ, TQ), TQ), :], visible, False)

    o = acc_scr[...] / l_scr[...]
    o_ref[...] = o.astype(BF16).reshape(MLA_HEADS, TQ, MLA_KV_RANK)


def _attn_prompt(q, kc, *, n_batch, seq, meta0):
    n = kc.shape[0]
    steps = seq // TQ
    rows = MLA_HEADS * TQ
    return pl.pallas_call(
        _attn_prompt_kernel,
        grid=(n_batch, steps),
        in_specs=[pl.BlockSpec((MLA_HEADS, TQ, QK_W), lambda b, i: (0, b * steps + i, 0)),
                  pl.BlockSpec((seq, QK_W), lambda b, i: (b, 0)),
                  pl.BlockSpec((N_META, QK_W), lambda b, i: (meta0 // N_META + b, 0))],
        out_specs=pl.BlockSpec((MLA_HEADS, TQ, MLA_KV_RANK), lambda b, i: (0, b * steps + i, 0)),
        out_shape=jax.ShapeDtypeStruct((MLA_HEADS, n, MLA_KV_RANK), BF16),
        scratch_shapes=[pltpu.VMEM((rows, 1), F32), pltpu.VMEM((rows, 1), F32),
                        pltpu.VMEM((rows, MLA_KV_RANK), F32)],
        compiler_params=_params("parallel", "arbitrary"),
        name="mla_attn_prompt",
    )(q, kc, kc)


def _attn_full_kernel(*refs, n_q, has_past):
    if has_past:
        q_ref, kn_ref, pl_ref, pr_ref, _, o_ref = refs
    else:
        q_ref, kn_ref, _, o_ref = refs
    rows = MLA_HEADS * n_q
    q = q_ref[...].reshape(rows, QK_W)
    kn = kn_ref[...]
    s_n = _dot_nt(q, kn)
    m = jnp.max(s_n, axis=1, keepdims=True)
    if has_past:
        lat = pl_ref[0].astype(BF16)
        rp = pr_ref[0].astype(BF16)
        s_p = (_dot_nt(q[:, :MLA_KV_RANK], lat)
               + _dot_nt(q[:, MLA_KV_RANK:MLA_KV_RANK + MLA_ROPE], rp))
        m = jnp.maximum(m, jnp.max(s_p, axis=1, keepdims=True))
    p_n = jnp.exp(s_n - m)
    l = jnp.sum(p_n, axis=1, keepdims=True)
    acc = _dot(p_n.astype(BF16), kn[:, :MLA_KV_RANK])
    if has_past:
        p_p = jnp.exp(s_p - m)
        l = l + jnp.sum(p_p, axis=1, keepdims=True)
        acc = acc + _dot(p_p.astype(BF16), lat)
    o_ref[...] = (acc / l).astype(BF16).reshape(MLA_HEADS, n_q, MLA_KV_RANK)


def _attn_full(q, kc, o_prev, past_lat, past_rope, *, row0, n_batch, n_q, name):
    base = row0 // n_q
    has_past = past_lat is not None
    ins = [q, kc]
    specs = [pl.BlockSpec((MLA_HEADS, n_q, QK_W), lambda b: (0, base + b, 0)),
             pl.BlockSpec((n_q, QK_W), lambda b: (base + b, 0))]
    if has_past:
        past = past_lat.shape[1]
        ins += [past_lat, past_rope]
        specs += [pl.BlockSpec((1, past, MLA_KV_RANK), lambda b: (b, 0, 0)),
                  pl.BlockSpec((1, past, MLA_ROPE), lambda b: (b, 0, 0))]
    aliases = {len(ins): 0}
    ins.append(o_prev)
    specs.append(pl.BlockSpec(memory_space=pl.ANY))
    return pl.pallas_call(
        functools.partial(_attn_full_kernel, n_q=n_q, has_past=has_past),
        grid=(n_batch,),
        in_specs=specs,
        out_specs=pl.BlockSpec((MLA_HEADS, n_q, MLA_KV_RANK), lambda b: (0, base + b, 0)),
        out_shape=jax.ShapeDtypeStruct(o_prev.shape, BF16),
        input_output_aliases=aliases,
        compiler_params=_params("parallel"),
        name=name,
    )(*ins)


def _mla_out_kernel(ol_ref, h_ref, wuv_ref, wo_ref, y_ref):
    parts = [_dot(ol_ref[h], wuv_ref[h]).astype(BF16) for h in range(MLA_HEADS)]
    y_ref[...] = h_ref[...] + _dot(jnp.concatenate(parts, axis=1), wo_ref[...])


def _mla_out(o_lat, h, w_uv_t, w_out):
    n = h.shape[0]
    return pl.pallas_call(
        _mla_out_kernel,
        grid=(n // TM,),
        in_specs=[pl.BlockSpec((MLA_HEADS, TM, MLA_KV_RANK), lambda i: (0, i, 0)), _rows(D_MODEL),
                  _resident(w_uv_t.shape), _resident(w_out.shape)],
        out_specs=_rows(D_MODEL),
        out_shape=jax.ShapeDtypeStruct((n, D_MODEL), F32),
        compiler_params=_params("parallel"),
        name="mla_out",
    )(o_lat, h, w_uv_t, w_out)


def _rope_tables(pos):
    half = MLA_ROPE // 2
    inv = ROPE_THETA ** (-jnp.arange(half, dtype=F32) / half)
    ang = pos[:, None] * inv[None, :]
    cos = jnp.cos(ang)
    sin = jnp.sin(ang)
    reps = LANES // MLA_ROPE
    return (jnp.tile(cos, (1, 2 * reps)), jnp.tile(jnp.concatenate([-sin, sin], axis=1), (1, reps)))


def kernel(x_prompt, x_sample, state_gla, cache_mla_latent, cache_mla_rope, meta_tokens, ffn1_norm, ffn1_w_gate, ffn1_w_up, ffn1_w_down, mix_norm, gla_w_in, gla_w_gate_up, gla_b_gate, gla_head_norm, gla_w_out, mla_w_down, mla_q_norm, mla_w_uq, mla_kv_norm, mla_w_uk, mla_w_uv, mla_w_out, ffn2_norm, ffn2_w_gate, ffn2_w_up, ffn2_w_down, final_norm):
    bp, seq, _ = x_prompt.shape
    bs, ls, _ = x_sample.shape
    past = cache_mla_latent.shape[2]
    n_fr = bp * seq
    n_sm = bs * ls
    meta0 = n_fr + n_sm
    n_tok = meta0 + bp * N_META
    assert n_tok % TM == 0 and seq % TQ == 0 and TQ % CHUNK == 0 and past % CHUNK == 0
    assert ls <= CHUNK and N_META <= CHUNK

    h = jnp.concatenate([x_prompt.reshape(n_fr, D_MODEL), x_sample.reshape(n_sm, D_MODEL),
                         jnp.tile(meta_tokens.astype(F32), (bp, 1))], axis=0)

    depth = ffn1_norm.shape[0]
    gla_sp, gla_ss, lat_p, rope_p, lat_s, rope_s = [], [], [], [], [], []
    for i in range(depth):
        h = _ffn(h, ffn1_norm[i], ffn1_w_gate[i].astype(BF16), ffn1_w_up[i].astype(BF16),
                 ffn1_w_down[i].astype(BF16))
        j = i // 2
        if i % 2 == 0:
            w_in = gla_w_in[j]
            main_w = 2 * GLA_KEY + 2 * GLA_VAL
            w_gd = jnp.pad(w_in[:, main_w:], ((0, 0), (0, LANES - GLA_RANK))).astype(BF16)
            w_gu = jnp.pad(gla_w_gate_up[j], ((0, LANES - GLA_RANK), (0, 0))).astype(BF16)
            q, k, v, r, gl = _gla_proj(h, mix_norm[i], w_in[:, :main_w].astype(BF16), w_gd, w_gu,
                                       gla_b_gate[j])
            o, s_meta = _gla_scan(q, k, v, gl, None, None, row0=meta0, n_batch=bp, n_steps=1,
                                  chunk=N_META, n_chunks=1, name="gla_scan_meta")
            o, s_p = _gla_scan(q, k, v, gl, s_meta, o, row0=0, n_batch=bp, n_steps=seq // TQ,
                               chunk=CHUNK, n_chunks=TQ // CHUNK, name="gla_scan_frames")
            o, s_s = _gla_scan(q, k, v, gl, state_gla[j], o, row0=n_fr, n_batch=bs, n_steps=1,
                               chunk=ls, n_chunks=1, name="gla_scan_sample")
            gla_sp.append(s_p)
            gla_ss.append(s_s)
            h = _gla_out(o, r, h, gla_head_norm[j], gla_w_out[j].astype(BF16))
        else:
            pos = jnp.concatenate([
                jnp.tile(N_META + jnp.arange(seq, dtype=F32), bp),
                jnp.tile(past + jnp.arange(ls, dtype=F32), bs),
                jnp.tile(jnp.arange(N_META, dtype=F32), bp)])
            cos_t, sin_t = _rope_tables(pos)
            w_down = mla_w_down[j]
            w_down = jnp.concatenate([w_down, w_down[:, -MLA_ROPE:]], axis=1).astype(BF16)
            w_uq = mla_w_uq[j].reshape(MLA_Q_RANK, MLA_HEADS, MLA_NOPE + MLA_ROPE)
            w_qn = w_uq[:, :, :MLA_NOPE].reshape(MLA_Q_RANK, MLA_HEADS * MLA_NOPE).astype(BF16)
            w_qr = w_uq[:, :, MLA_NOPE:].reshape(MLA_Q_RANK, MLA_HEADS * MLA_ROPE).astype(BF16)
            w_uk_t = jnp.transpose(mla_w_uk[j], (1, 2, 0)).astype(BF16)
            w_uv_t = jnp.transpose(mla_w_uv[j], (1, 0, 2)).astype(BF16)
            qa, kc, ckv, kr = _mla_proj(h, mix_norm[i], w_down, mla_q_norm[j], w_qn, w_qr,
                                        mla_kv_norm[j], w_uk_t, cos_t, sin_t)
            o_lat = _attn_prompt(qa, kc, n_batch=bp, seq=seq, meta0=meta0)
            o_lat = _attn_full(qa, kc, o_lat, cache_mla_latent[j], cache_mla_rope[j],
                               row0=n_fr, n_batch=bs, n_q=ls, name="mla_attn_sample")
            o_lat = _attn_full(qa, kc, o_lat, None, None,
                               row0=meta0, n_batch=bp, n_q=N_META, name="mla_attn_meta")
            h = _mla_out(o_lat, h, w_uv_t, mla_w_out[j].astype(BF16))

            def seq_major(t, width):
                return jnp.concatenate([t[meta0:].reshape(bp, N_META, width),
                                        t[:n_fr].reshape(bp, seq, width)], axis=1)

            lat_p.append(seq_major(ckv, MLA_KV_RANK))
            rope_p.append(seq_major(kr, MLA_ROPE))
            lat_s.append(ckv[n_fr:meta0].reshape(bs, ls, MLA_KV_RANK))
            rope_s.append(kr[n_fr:meta0].reshape(bs, ls, MLA_ROPE))
        last = i == depth - 1
        h = _ffn(h, ffn2_norm[i], ffn2_w_gate[i].astype(BF16), ffn2_w_up[i].astype(BF16),
                 ffn2_w_down[i].astype(BF16), final_norm if last else None)

    y_prompt = h[:n_fr].reshape(bp, seq, D_MODEL)
    y_sample = h[n_fr:meta0].reshape(bs, ls, D_MODEL)
    return (y_prompt, y_sample, jnp.stack(gla_sp, 0), jnp.stack(gla_ss, 0),
            jnp.stack(lat_p, 0), jnp.stack(rope_p, 0), jnp.stack(lat_s, 0), jnp.stack(rope_s, 0))
```

```python
import functools

import jax
import jax.numpy as jnp
from jax import lax
from jax.experimental import pallas as pl
from jax.experimental.pallas import tpu as pltpu

F32 = jnp.float32
BF16 = jnp.bfloat16

D_MODEL = 1024
D_FF = 2816
RMS_EPS = 1e-6
N_META = 16
CHUNK = 64

GLA_HEADS = 4
GLA_DK = 128
GLA_DV = 256
GLA_KEY = GLA_HEADS * GLA_DK
GLA_VAL = GLA_HEADS * GLA_DV
GLA_RANK = 16
GLA_GATE_NORMALIZER = 16.0

MLA_HEADS = 8
MLA_Q_RANK = 384
MLA_KV_RANK = 256
MLA_NOPE = 128
MLA_ROPE = 64
MLA_V = 128
ROPE_THETA = 10000.0
QK_W = MLA_KV_RANK + 128
LOG2_E = 1.4426950408889634

LANES = 128
VMEM_LIMIT = 48 * 1024 * 1024

TM = 448
FF_CHUNK = 256
TQ = 256
HEADS_PER_PASS = 8
ATT_COLS = HEADS_PER_PASS * TQ

NT_DIMS = (((1,), (1,)), ((), ()))
TN_DIMS = (((0,), (0,)), ((), ()))


def _dot(a, b):
    return jnp.dot(a, b, preferred_element_type=F32)


def _dot_nt(a, b):
    return lax.dot_general(a, b, NT_DIMS, preferred_element_type=F32)


def _dot_tn(a, b):
    return lax.dot_general(a, b, TN_DIMS, preferred_element_type=F32)


def _rms(x, w):
    return x * lax.rsqrt(jnp.mean(x * x, axis=-1, keepdims=True) + RMS_EPS) * w


def _silu(x):
    return x * jax.nn.sigmoid(x)


def _params(*sem):
    return pltpu.CompilerParams(dimension_semantics=sem, vmem_limit_bytes=VMEM_LIMIT)


def _resident(shape):
    zeros = (0,) * len(shape)
    return pl.BlockSpec(shape, lambda *_: zeros, pipeline_mode=pl.Buffered(1))


def _rows(width, tm=TM):
    return pl.BlockSpec((tm, width), lambda i: (i, 0))


def _ffn_kernel(x_ref, nw_ref, wg_ref, wu_ref, wd_ref, *rest, final_norm):
    if final_norm:
        fw_ref, o_ref, h_scr = rest
    else:
        o_ref, h_scr = rest
    x = x_ref[...]
    xn = _rms(x, nw_ref[...]).astype(BF16)
    for c in range(D_FF // FF_CHUNK):
        sl = slice(c * FF_CHUNK, (c + 1) * FF_CHUNK)
        g = _dot(xn, wg_ref[:, sl])
        u = _dot(xn, wu_ref[:, sl])
        h_scr[:, sl] = (_silu(g) * u).astype(BF16)
    y = x + 0.5 * _dot(h_scr[...], wd_ref[...])
    if final_norm:
        y = _rms(y, fw_ref[...])
    o_ref[...] = y


def _ffn(x, norm_w, wg, wu, wd, final_w=None):
    n = x.shape[0]
    final_norm = final_w is not None
    ins = [x, norm_w.reshape(1, D_MODEL), wg, wu, wd]
    specs = [_rows(D_MODEL), _resident((1, D_MODEL)), _resident((D_MODEL, D_FF)),
             _resident((D_MODEL, D_FF)), _resident((D_FF, D_MODEL))]
    if final_norm:
        ins.append(final_w.reshape(1, D_MODEL))
        specs.append(_resident((1, D_MODEL)))
    return pl.pallas_call(
        functools.partial(_ffn_kernel, final_norm=final_norm),
        grid=(n // TM,),
        in_specs=specs,
        out_specs=_rows(D_MODEL),
        out_shape=jax.ShapeDtypeStruct((n, D_MODEL), F32),
        scratch_shapes=[pltpu.VMEM((TM, D_FF), BF16)],
        compiler_params=_params("parallel"),
        name="ffn_final" if final_norm else "ffn",
    )(*ins)


def _gla_proj_kernel(x_ref, nw_ref, wm_ref, wgd_ref, wgu_ref, bg_ref,
                     q_ref, k_ref, v_ref, r_ref, gl_ref):
    xn = _rms(x_ref[...], nw_ref[...]).astype(BF16)
    q_ref[...] = _dot(xn, wm_ref[:, 0:GLA_KEY])
    k_ref[...] = _dot(xn, wm_ref[:, GLA_KEY:2 * GLA_KEY])
    v_ref[...] = _dot(xn, wm_ref[:, 2 * GLA_KEY:2 * GLA_KEY + GLA_VAL]).astype(BF16)
    r_ref[...] = _dot(xn, wm_ref[:, 2 * GLA_KEY + GLA_VAL:]).astype(BF16)
    gd = _dot(xn, wgd_ref[...])
    z = _dot(gd.astype(BF16), wgu_ref[...]) + bg_ref[...]
    log_sig = jnp.minimum(z, 0.0) - jnp.log1p(jnp.exp(-jnp.abs(z)))
    gl_ref[...] = log_sig * (1.0 / GLA_GATE_NORMALIZER)


def _gla_proj(x, norm_w, w_main, w_gd, w_gu, b_gate):
    n = x.shape[0]
    main_w = 2 * GLA_KEY + 2 * GLA_VAL
    return pl.pallas_call(
        _gla_proj_kernel,
        grid=(n // TM,),
        in_specs=[_rows(D_MODEL), _resident((1, D_MODEL)), _resident((D_MODEL, main_w)),
                  _resident((D_MODEL, LANES)), _resident((LANES, GLA_KEY)), _resident((1, GLA_KEY))],
        out_specs=[_rows(GLA_KEY), _rows(GLA_KEY), _rows(GLA_VAL), _rows(GLA_VAL), _rows(GLA_KEY)],
        out_shape=[jax.ShapeDtypeStruct((n, GLA_KEY), F32), jax.ShapeDtypeStruct((n, GLA_KEY), F32),
                   jax.ShapeDtypeStruct((n, GLA_VAL), BF16), jax.ShapeDtypeStruct((n, GLA_VAL), BF16),
                   jax.ShapeDtypeStruct((n, GLA_KEY), F32)],
        compiler_params=_params("parallel"),
        name="gla_proj",
    )(x, norm_w.reshape(1, D_MODEL), w_main, w_gd, w_gu, b_gate.reshape(1, GLA_KEY))


def _gla_scan_kernel(*refs, chunk, n_chunks, n_steps, has_s0, has_prev):
    refs = list(refs)
    q_ref, k_ref, v_ref, gl_ref = refs[:4]
    pos = 4
    s0_ref = None
    if has_s0:
        s0_ref = refs[pos]
        pos += 1
    if has_prev:
        pos += 1
    o_ref, s_out_ref, st_scr = refs[pos:pos + 3]
    step = pl.program_id(1)

    @pl.when(step == 0)
    def _():
        for h in range(GLA_HEADS):
            if has_s0:
                st_scr[h] = s0_ref[0, h].T
            else:
                st_scr[h] = jnp.zeros((GLA_DV, GLA_DK), F32)

    row = lax.broadcasted_iota(jnp.int32, (chunk, chunk), 0)
    col = lax.broadcasted_iota(jnp.int32, (chunk, chunk), 1)
    causal = row >= col
    tri = jnp.where(causal, 1.0, 0.0).astype(BF16)
    scale = GLA_DK ** -0.5

    for c in range(n_chunks):
        rows = slice(c * chunk, (c + 1) * chunk)
        glog = gl_ref[rows, :]
        hi = glog.astype(BF16)
        lo = (glog - hi.astype(F32)).astype(BF16)
        gc = _dot(tri, hi) + _dot(tri, lo)
        glast = gc[chunk - 1:chunk, :]
        q = q_ref[rows, :] * scale
        k = k_ref[rows, :]
        qg = (q * jnp.exp(gc)).astype(BF16)
        kg = (k * jnp.exp(-gc)).astype(BF16)
        kd = (k * jnp.exp(glast - gc)).astype(BF16)
        decay = jnp.exp(glast)
        for h in range(GLA_HEADS):
            ks = slice(h * GLA_DK, (h + 1) * GLA_DK)
            vs = slice(h * GLA_DV, (h + 1) * GLA_DV)
            a = jnp.where(causal, _dot_nt(qg[:, ks], kg[:, ks]), 0.0).astype(BF16)
            vh = v_ref[rows, vs]
            st = st_scr[h]
            o_ref[rows, vs] = _dot(a, vh) + _dot_nt(qg[:, ks], st.astype(BF16))
            st_scr[h] = st * decay[:, ks] + _dot_tn(vh, kd[:, ks])

    @pl.when(step == n_steps - 1)
    def _():
        for h in range(GLA_HEADS):
            s_out_ref[0, h] = st_scr[h].T


def _gla_scan(q, k, v, gl, s0, o_prev, *, row0, n_batch, n_steps, chunk, n_chunks, name):
    n = q.shape[0]
    rb = chunk * n_chunks
    base = row0 // rb

    def rmap(b, s):
        return (base + b * n_steps + s, 0)

    def smap(b, s):
        return (b, 0, 0, 0)

    ins = [q, k, v, gl]
    specs = [pl.BlockSpec((rb, GLA_KEY), rmap), pl.BlockSpec((rb, GLA_KEY), rmap),
             pl.BlockSpec((rb, GLA_VAL), rmap), pl.BlockSpec((rb, GLA_KEY), rmap)]
    state_block = (1, GLA_HEADS, GLA_DK, GLA_DV)
    if s0 is not None:
        ins.append(s0)
        specs.append(pl.BlockSpec(state_block, smap))
    aliases = {}
    if o_prev is not None:
        aliases = {len(ins): 0}
        ins.append(o_prev)
        specs.append(pl.BlockSpec(memory_space=pl.ANY))
    kern = functools.partial(_gla_scan_kernel, chunk=chunk, n_chunks=n_chunks, n_steps=n_steps,
                             has_s0=s0 is not None, has_prev=o_prev is not None)
    return pl.pallas_call(
        kern,
        grid=(n_batch, n_steps),
        in_specs=specs,
        out_specs=[pl.BlockSpec((rb, GLA_VAL), rmap), pl.BlockSpec(state_block, smap)],
        out_shape=[jax.ShapeDtypeStruct((n, GLA_VAL), F32),
                   jax.ShapeDtypeStruct((n_batch,) + state_block[1:], F32)],
        scratch_shapes=[pltpu.VMEM((GLA_HEADS, GLA_DV, GLA_DK), F32)],
        input_output_aliases=aliases,
        compiler_params=_params("parallel", "arbitrary"),
        name=name,
    )(*ins)


def _gla_out_kernel(o_ref, r_ref, h_ref, hn_ref, wo_ref, y_ref):
    parts = []
    for h in range(GLA_HEADS):
        vs = slice(h * GLA_DV, (h + 1) * GLA_DV)
        on = _rms(o_ref[:, vs], hn_ref[...])
        parts.append((on * _silu(r_ref[:, vs].astype(F32))).astype(BF16))
    y_ref[...] = h_ref[...] + _dot(jnp.concatenate(parts, axis=1), wo_ref[...])


def _gla_out(o, r, h, head_norm, w_out):
    n = h.shape[0]
    return pl.pallas_call(
        _gla_out_kernel,
        grid=(n // TM,),
        in_specs=[_rows(GLA_VAL), _rows(GLA_VAL), _rows(D_MODEL), _resident((1, GLA_DV)),
                  _resident((GLA_VAL, D_MODEL))],
        out_specs=_rows(D_MODEL),
        out_shape=jax.ShapeDtypeStruct((n, D_MODEL), F32),
        compiler_params=_params("parallel"),
        name="gla_out",
    )(o, r, h, head_norm.reshape(1, GLA_DV), w_out)


def _mla_proj_kernel(x_ref, nw_ref, wd_ref, qn_ref, wqn_ref, wqr_ref, kvn_ref, wuk_ref,
                     cos_ref, sin_ref, q_out, kc_out, ckv_out, kr_out):
    tm = x_ref.shape[0]
    scale = (MLA_NOPE + MLA_ROPE) ** -0.5 * LOG2_E
    xn = _rms(x_ref[...], nw_ref[...]).astype(BF16)
    xd = _dot(xn, wd_ref[...])
    cqn = _rms(xd[:, :MLA_Q_RANK], qn_ref[...]).astype(BF16)
    ckv = _rms(xd[:, MLA_Q_RANK:MLA_Q_RANK + MLA_KV_RANK], kvn_ref[...])
    cos_t = cos_ref[...]
    sin_t = sin_ref[...]
    lane = lax.broadcasted_iota(jnp.int32, (tm, LANES), 1)
    first_half = (lane & (MLA_ROPE // 2)) == 0

    def rope(x):
        swapped = jnp.where(first_half, pltpu.roll(x, LANES - MLA_ROPE // 2, 1),
                            pltpu.roll(x, MLA_ROPE // 2, 1))
        return x * cos_t + swapped * sin_t

    kr2 = rope(xd[:, MLA_Q_RANK + MLA_KV_RANK:])
    ckv_out[...] = ckv
    kr_out[...] = kr2[:, :MLA_ROPE]
    kc_out[:, :MLA_KV_RANK] = ckv.astype(BF16)
    kc_out[:, MLA_KV_RANK:] = jnp.where(lane < MLA_ROPE, kr2, 0.0).astype(BF16)

    qn = _dot(cqn, wqn_ref[...])
    qr = _dot(cqn, wqr_ref[...])
    for c in range(MLA_HEADS // 2):
        rr = rope(qr[:, c * LANES:(c + 1) * LANES]) * scale
        for e in range(2):
            h = 2 * c + e
            ql = _dot(qn[:, h * MLA_NOPE:(h + 1) * MLA_NOPE].astype(BF16), wuk_ref[h]) * scale
            q_out[h, :, :MLA_KV_RANK] = ql.astype(BF16)
            rot = rr if e == 0 else pltpu.roll(rr, MLA_ROPE, 1)
            q_out[h, :, MLA_KV_RANK:] = rot.astype(BF16)


def _mla_proj(x, norm_w, w_down, q_norm, w_qn, w_qr, kv_norm, w_uk_t, cos_t, sin_t):
    n = x.shape[0]
    return pl.pallas_call(
        _mla_proj_kernel,
        grid=(n // TM,),
        in_specs=[_rows(D_MODEL), _resident((1, D_MODEL)), _resident(w_down.shape),
                  _resident((1, MLA_Q_RANK)), _resident(w_qn.shape), _resident(w_qr.shape),
                  _resident((1, MLA_KV_RANK)), _resident(w_uk_t.shape),
                  _rows(LANES), _rows(LANES)],
        out_specs=[pl.BlockSpec((MLA_HEADS, TM, QK_W), lambda i: (0, i, 0)),
                   _rows(QK_W), _rows(MLA_KV_RANK), _rows(MLA_ROPE)],
        out_shape=[jax.ShapeDtypeStruct((MLA_HEADS, n, QK_W), BF16),
                   jax.ShapeDtypeStruct((n, QK_W), BF16),
                   jax.ShapeDtypeStruct((n, MLA_KV_RANK), F32),
                   jax.ShapeDtypeStruct((n, MLA_ROPE), F32)],
        compiler_params=_params("parallel"),
        name="mla_proj",
    )(x, norm_w.reshape(1, D_MODEL), w_down, q_norm.reshape(1, MLA_Q_RANK), w_qn, w_qr,
      kv_norm.reshape(1, MLA_KV_RANK), w_uk_t, cos_t, sin_t)


def _attn_prompt_kernel(q_ref, kf_ref, km_ref, vtf_ref, vtm_ref, o_ref, m_scr, l_scr, acc_scr):
    i = pl.program_id(1)

    def update(keys, values_t, mask, first):
        for c in range(MLA_HEADS // HEADS_PER_PASS):
            cs = slice(c * ATT_COLS, (c + 1) * ATT_COLS)
            q = q_ref[c * HEADS_PER_PASS:(c + 1) * HEADS_PER_PASS].reshape(ATT_COLS, QK_W)
            s = _dot_nt(keys, q)
            if mask is not None:
                s = jnp.where(mask, s, -jnp.inf)
            m_cur = jnp.max(s, axis=0, keepdims=True)
            if first:
                m_new = m_cur
            else:
                m_old = m_scr[:, cs]
                m_new = jnp.maximum(m_old, m_cur)
                alpha = jnp.exp2(m_old - m_new)
            p = jnp.exp2(s - m_new)
            p_sum = jnp.sum(p, axis=0, keepdims=True)
            pb = p.astype(BF16)
            pv = None
            for vt, rs in values_t:
                term = _dot(vt, pb[rs, :])
                pv = term if pv is None else pv + term
            if first:
                l_scr[:, cs] = p_sum
                acc_scr[:, cs] = pv
            else:
                l_scr[:, cs] = alpha * l_scr[:, cs] + p_sum
                acc_scr[:, cs] = alpha * acc_scr[:, cs] + pv
            m_scr[:, cs] = m_new

    key = lax.broadcasted_iota(jnp.int32, (TQ + N_META, ATT_COLS), 0)
    qry = lax.broadcasted_iota(jnp.int32, (TQ + N_META, ATT_COLS), 1) % TQ
    visible = (key >= TQ) | ((key // CHUNK) <= (qry // CHUNK))
    keys0 = jnp.concatenate([kf_ref[i], km_ref[...]], axis=0)
    update(keys0, [(vtf_ref[i], slice(0, TQ)), (vtm_ref[0], slice(TQ, TQ + N_META))], visible, True)

    def body(j, carry):
        update(kf_ref[j], [(vtf_ref[j], slice(0, TQ))], None, False)
        return carry

    lax.fori_loop(0, i, body, 0)

    for h in range(MLA_HEADS):
        cs = slice(h * TQ, (h + 1) * TQ)
        o_ref[h] = (acc_scr[:, cs] / l_scr[:, cs]).T.astype(BF16)


def _attn_prompt(q, kc, *, n_batch, seq, meta0):
    n = kc.shape[0]
    steps = seq // TQ
    n_fr = n_batch * seq
    k_tiles = kc[:n_fr].reshape(n_batch * steps, TQ, QK_W)
    vt_tiles = jnp.swapaxes(k_tiles[:, :, :MLA_KV_RANK], 1, 2)
    vt_meta = jnp.swapaxes(kc[meta0:, :MLA_KV_RANK].reshape(n_batch, N_META, MLA_KV_RANK), 1, 2)
    cols = MLA_HEADS * TQ
    return pl.pallas_call(
        _attn_prompt_kernel,
        grid=(n_batch, steps),
        in_specs=[pl.BlockSpec((MLA_HEADS, TQ, QK_W), lambda b, i: (0, b * steps + i, 0)),
                  pl.BlockSpec((steps, TQ, QK_W), lambda b, i: (b, 0, 0)),
                  pl.BlockSpec((N_META, QK_W), lambda b, i: (meta0 // N_META + b, 0)),
                  pl.BlockSpec((steps, MLA_KV_RANK, TQ), lambda b, i: (b, 0, 0)),
                  pl.BlockSpec((1, MLA_KV_RANK, N_META), lambda b, i: (b, 0, 0))],
        out_specs=pl.BlockSpec((MLA_HEADS, TQ, MLA_KV_RANK), lambda b, i: (0, b * steps + i, 0)),
        out_shape=jax.ShapeDtypeStruct((MLA_HEADS, n, MLA_KV_RANK), BF16),
        scratch_shapes=[pltpu.VMEM((1, cols), F32), pltpu.VMEM((1, cols), F32),
                        pltpu.VMEM((MLA_KV_RANK, cols), F32)],
        compiler_params=_params("parallel", "arbitrary"),
        name="mla_attn_prompt",
    )(q, k_tiles, kc, vt_tiles, vt_meta)


def _attn_full_kernel(*refs, n_q, has_past):
    if has_past:
        q_ref, kn_ref, pl_ref, pr_ref, _, o_ref = refs
    else:
        q_ref, kn_ref, _, o_ref = refs
    rows = MLA_HEADS * n_q
    q = q_ref[...].reshape(rows, QK_W)
    kn = kn_ref[...]
    s_n = _dot_nt(q, kn)
    m = jnp.max(s_n, axis=1, keepdims=True)
    if has_past:
        lat = pl_ref[0].astype(BF16)
        rp = pr_ref[0].astype(BF16)
        s_p = (_dot_nt(q[:, :MLA_KV_RANK], lat)
               + _dot_nt(q[:, MLA_KV_RANK:MLA_KV_RANK + MLA_ROPE], rp))
        m = jnp.maximum(m, jnp.max(s_p, axis=1, keepdims=True))
    p_n = jnp.exp2(s_n - m)
    l = jnp.sum(p_n, axis=1, keepdims=True)
    acc = _dot(p_n.astype(BF16), kn[:, :MLA_KV_RANK])
    if has_past:
        p_p = jnp.exp2(s_p - m)
        l = l + jnp.sum(p_p, axis=1, keepdims=True)
        acc = acc + _dot(p_p.astype(BF16), lat)
    o_ref[...] = (acc / l).astype(BF16).reshape(MLA_HEADS, n_q, MLA_KV_RANK)


def _attn_full(q, kc, o_prev, past_lat, past_rope, *, row0, n_batch, n_q, name):
    base = row0 // n_q
    has_past = past_lat is not None
    ins = [q, kc]
    specs = [pl.BlockSpec((MLA_HEADS, n_q, QK_W), lambda b: (0, base + b, 0)),
             pl.BlockSpec((n_q, QK_W), lambda b: (base + b, 0))]
    if has_past:
        past = past_lat.shape[1]
        ins += [past_lat, past_rope]
        specs += [pl.BlockSpec((1, past, MLA_KV_RANK), lambda b: (b, 0, 0)),
                  pl.BlockSpec((1, past, MLA_ROPE), lambda b: (b, 0, 0))]
    aliases = {len(ins): 0}
    ins.append(o_prev)
    specs.append(pl.BlockSpec(memory_space=pl.ANY))
    return pl.pallas_call(
        functools.partial(_attn_full_kernel, n_q=n_q, has_past=has_past),
        grid=(n_batch,),
        in_specs=specs,
        out_specs=pl.BlockSpec((MLA_HEADS, n_q, MLA_KV_RANK), lambda b: (0, base + b, 0)),
        out_shape=jax.ShapeDtypeStruct(o_prev.shape, BF16),
        input_output_aliases=aliases,
        compiler_params=_params("parallel"),
        name=name,
    )(*ins)


def _mla_out_kernel(ol_ref, h_ref, wuv_ref, wo_ref, y_ref):
    parts = [_dot(ol_ref[h], wuv_ref[h]).astype(BF16) for h in range(MLA_HEADS)]
    y_ref[...] = h_ref[...] + _dot(jnp.concatenate(parts, axis=1), wo_ref[...])


def _mla_out(o_lat, h, w_uv_t, w_out):
    n = h.shape[0]
    return pl.pallas_call(
        _mla_out_kernel,
        grid=(n // TM,),
        in_specs=[pl.BlockSpec((MLA_HEADS, TM, MLA_KV_RANK), lambda i: (0, i, 0)), _rows(D_MODEL),
                  _resident(w_uv_t.shape), _resident(w_out.shape)],
        out_specs=_rows(D_MODEL),
        out_shape=jax.ShapeDtypeStruct((n, D_MODEL), F32),
        compiler_params=_params("parallel"),
        name="mla_out",
    )(o_lat, h, w_uv_t, w_out)


def _rope_tables(pos):
    half = MLA_ROPE // 2
    inv = ROPE_THETA ** (-jnp.arange(half, dtype=F32) / half)
    ang = pos[:, None] * inv[None, :]
    cos = jnp.cos(ang)
    sin = jnp.sin(ang)
    reps = LANES // MLA_ROPE
    return (jnp.tile(cos, (1, 2 * reps)), jnp.tile(jnp.concatenate([-sin, sin], axis=1), (1, reps)))


def kernel(x_prompt, x_sample, state_gla, cache_mla_latent, cache_mla_rope, meta_tokens, ffn1_norm, ffn1_w_gate, ffn1_w_up, ffn1_w_down, mix_norm, gla_w_in, gla_w_gate_up, gla_b_gate, gla_head_norm, gla_w_out, mla_w_down, mla_q_norm, mla_w_uq, mla_kv_norm, mla_w_uk, mla_w_uv, mla_w_out, ffn2_norm, ffn2_w_gate, ffn2_w_up, ffn2_w_down, final_norm):
    bp, seq, _ = x_prompt.shape
    bs, ls, _ = x_sample.shape
    past = cache_mla_latent.shape[2]
    n_fr = bp * seq
    n_sm = bs * ls
    meta0 = n_fr + n_sm
    n_tok = meta0 + bp * N_META
    assert n_tok % TM == 0 and seq % TQ == 0 and TQ % CHUNK == 0 and past % CHUNK == 0
    assert ls <= CHUNK and N_META <= CHUNK

    h = jnp.concatenate([x_prompt.reshape(n_fr, D_MODEL), x_sample.reshape(n_sm, D_MODEL),
                         jnp.tile(meta_tokens.astype(F32), (bp, 1))], axis=0)

    depth = ffn1_norm.shape[0]
    gla_sp, gla_ss, lat_p, rope_p, lat_s, rope_s = [], [], [], [], [], []
    for i in range(depth):
        h = _ffn(h, ffn1_norm[i], ffn1_w_gate[i].astype(BF16), ffn1_w_up[i].astype(BF16),
                 ffn1_w_down[i].astype(BF16))
        j = i // 2
        if i % 2 == 0:
            w_in = gla_w_in[j]
            main_w = 2 * GLA_KEY + 2 * GLA_VAL
            w_gd = jnp.pad(w_in[:, main_w:], ((0, 0), (0, LANES - GLA_RANK))).astype(BF16)
            w_gu = jnp.pad(gla_w_gate_up[j], ((0, LANES - GLA_RANK), (0, 0))).astype(BF16)
            q, k, v, r, gl = _gla_proj(h, mix_norm[i], w_in[:, :main_w].astype(BF16), w_gd, w_gu,
                                       gla_b_gate[j])
            o, s_meta = _gla_scan(q, k, v, gl, None, None, row0=meta0, n_batch=bp, n_steps=1,
                                  chunk=N_META, n_chunks=1, name="gla_scan_meta")
            o, s_p = _gla_scan(q, k, v, gl, s_meta, o, row0=0, n_batch=bp, n_steps=seq // TQ,
                               chunk=CHUNK, n_chunks=TQ // CHUNK, name="gla_scan_frames")
            o, s_s = _gla_scan(q, k, v, gl, state_gla[j], o, row0=n_fr, n_batch=bs, n_steps=1,
                               chunk=ls, n_chunks=1, name="gla_scan_sample")
            gla_sp.append(s_p)
            gla_ss.append(s_s)
            h = _gla_out(o, r, h, gla_head_norm[j], gla_w_out[j].astype(BF16))
        else:
            pos = jnp.concatenate([
                jnp.tile(N_META + jnp.arange(seq, dtype=F32), bp),
                jnp.tile(past + jnp.arange(ls, dtype=F32), bs),
                jnp.tile(jnp.arange(N_META, dtype=F32), bp)])
            cos_t, sin_t = _rope_tables(pos)
            w_down = mla_w_down[j]
            w_down = jnp.concatenate([w_down, w_down[:, -MLA_ROPE:]], axis=1).astype(BF16)
            w_uq = mla_w_uq[j].reshape(MLA_Q_RANK, MLA_HEADS, MLA_NOPE + MLA_ROPE)
            w_qn = w_uq[:, :, :MLA_NOPE].reshape(MLA_Q_RANK, MLA_HEADS * MLA_NOPE).astype(BF16)
            w_qr = w_uq[:, :, MLA_NOPE:].reshape(MLA_Q_RANK, MLA_HEADS * MLA_ROPE).astype(BF16)
            w_uk_t = jnp.transpose(mla_w_uk[j], (1, 2, 0)).astype(BF16)
            w_uv_t = jnp.transpose(mla_w_uv[j], (1, 0, 2)).astype(BF16)
            qa, kc, ckv, kr = _mla_proj(h, mix_norm[i], w_down, mla_q_norm[j], w_qn, w_qr,
                                        mla_kv_norm[j], w_uk_t, cos_t, sin_t)
            o_lat = _attn_prompt(qa, kc, n_batch=bp, seq=seq, meta0=meta0)
            o_lat = _attn_full(qa, kc, o_lat, cache_mla_latent[j], cache_mla_rope[j],
                               row0=n_fr, n_batch=bs, n_q=ls, name="mla_attn_sample")
            o_lat = _attn_full(qa, kc, o_lat, None, None,
                               row0=meta0, n_batch=bp, n_q=N_META, name="mla_attn_meta")
            h = _mla_out(o_lat, h, w_uv_t, mla_w_out[j].astype(BF16))

            def seq_major(t, width):
                return jnp.concatenate([t[meta0:].reshape(bp, N_META, width),
                                        t[:n_fr].reshape(bp, seq, width)], axis=1)

            lat_p.append(seq_major(ckv, MLA_KV_RANK))
            rope_p.append(seq_major(kr, MLA_ROPE))
            lat_s.append(ckv[n_fr:meta0].reshape(bs, ls, MLA_KV_RANK))
            rope_s.append(kr[n_fr:meta0].reshape(bs, ls, MLA_ROPE))
        last = i == depth - 1
        h = _ffn(h, ffn2_norm[i], ffn2_w_gate[i].astype(BF16), ffn2_w_up[i].astype(BF16),
                 ffn2_w_down[i].astype(BF16), final_norm if last else None)

    y_prompt = h[:n_fr].reshape(bp, seq, D_MODEL)
    y_sample = h[n_fr:meta0].reshape(bs, ls, D_MODEL)
    return (y_prompt, y_sample, jnp.stack(gla_sp, 0), jnp.stack(gla_ss, 0),
            jnp.stack(lat_p, 0), jnp.stack(rope_p, 0), jnp.stack(lat_s, 0), jnp.stack(rope_s, 0))
```

```python
import functools

import jax
import jax.numpy as jnp
from jax import lax
from jax.experimental import pallas as pl
from jax.experimental.pallas import tpu as pltpu

F32 = jnp.float32
BF16 = jnp.bfloat16

D_MODEL = 1024
D_FF = 2816
RMS_EPS = 1e-6
N_META = 16
CHUNK = 64

GLA_HEADS = 4
GLA_DK = 128
GLA_DV = 256
GLA_KEY = GLA_HEADS * GLA_DK
GLA_VAL = GLA_HEADS * GLA_DV
GLA_RANK = 16
GLA_GATE_NORMALIZER = 16.0

MLA_HEADS = 8
MLA_Q_RANK = 384
MLA_KV_RANK = 256
MLA_NOPE = 128
MLA_ROPE = 64
MLA_V = 128
ROPE_THETA = 10000.0
QK_W = MLA_KV_RANK + 128
LOG2_E = 1.4426950408889634

LANES = 128
VMEM_LIMIT = 56 * 1024 * 1024

TM = 512
FF_CHUNK = 256
TQ = 256

NT_DIMS = (((1,), (1,)), ((), ()))
TN_DIMS = (((0,), (0,)), ((), ()))


def _dot(a, b):
    return jnp.dot(a, b, preferred_element_type=F32)


def _dot_nt(a, b):
    return lax.dot_general(a, b, NT_DIMS, preferred_element_type=F32)


def _dot_tn(a, b):
    return lax.dot_general(a, b, TN_DIMS, preferred_element_type=F32)


def _rms(x, w):
    return x * lax.rsqrt(jnp.mean(x * x, axis=-1, keepdims=True) + RMS_EPS) * w


def _silu(x):
    return x * jax.nn.sigmoid(x)


def _params(*sem):
    return pltpu.CompilerParams(dimension_semantics=sem, vmem_limit_bytes=VMEM_LIMIT)


def _resident(shape):
    zeros = (0,) * len(shape)
    return pl.BlockSpec(shape, lambda *_: zeros, pipeline_mode=pl.Buffered(1))


def _whole(shape):
    zeros = (0,) * len(shape)
    return pl.BlockSpec(shape, lambda *_: zeros)


def _tiles(block, row_axis=0, first=0, count=None):
    nd = len(block)

    def imap(i):
        t = i - first
        if count is not None:
            t = jnp.clip(t, 0, count - 1)
        idx = [0] * nd
        idx[row_axis] = t
        return tuple(idx)

    return pl.BlockSpec(block, imap)


def _select(i, refs, firsts):
    val = refs[0][...]
    for ref, first in zip(refs[1:], firsts[1:]):
        val = jnp.where(i >= first, ref[...], val)
    return val


def _ffn_compute(x, nw_ref, wg_ref, wu_ref, wd_ref, h_scr):
    m = x.shape[0]
    xn = _rms(x, nw_ref[...]).astype(BF16)
    for c in range(D_FF // FF_CHUNK):
        sl = slice(c * FF_CHUNK, (c + 1) * FF_CHUNK)
        g = _dot(xn, wg_ref[:, sl])
        u = _dot(xn, wu_ref[:, sl])
        h_scr[:m, sl] = (_silu(g) * u).astype(BF16)
    return x + 0.5 * _dot(h_scr[:m, :], wd_ref[...])


def _gla_proj_compute(x, nw_ref, wm_ref, wgd_ref, wgu_ref, bg_ref):
    xn = _rms(x, nw_ref[...]).astype(BF16)
    q = _dot(xn, wm_ref[:, 0:GLA_KEY]).astype(BF16)
    k = _dot(xn, wm_ref[:, GLA_KEY:2 * GLA_KEY]).astype(BF16)
    v = _dot(xn, wm_ref[:, 2 * GLA_KEY:2 * GLA_KEY + GLA_VAL]).astype(BF16)
    r = _dot(xn, wm_ref[:, 2 * GLA_KEY + GLA_VAL:]).astype(BF16)
    gd = _dot(xn, wgd_ref[...])
    z = _dot(gd.astype(BF16), wgu_ref[...]) + bg_ref[...]
    log_sig = jnp.minimum(z, 0.0) - jnp.log1p(jnp.exp(-jnp.abs(z)))
    return q, k, v, r, log_sig * (1.0 / GLA_GATE_NORMALIZER)


def _gla_out_compute(o, r, h, hn_ref, wo_ref):
    parts = []
    for hd in range(GLA_HEADS):
        vs = slice(hd * GLA_DV, (hd + 1) * GLA_DV)
        on = _rms(o[:, vs].astype(F32), hn_ref[...])
        parts.append((on * _silu(r[:, vs].astype(F32))).astype(BF16))
    return h + _dot(jnp.concatenate(parts, axis=1), wo_ref[...])


def _mla_proj_compute(x, cos_t, sin_t, nw_ref, wd_ref, qn_ref, wqn_ref, wqr_ref, kvn_ref, wuk_ref,
                      q_out, kc_out, ckv_out, kr_out, vt_out):
    m = x.shape[0]
    scale = (MLA_NOPE + MLA_ROPE) ** -0.5 * LOG2_E
    xn = _rms(x, nw_ref[...]).astype(BF16)
    xd = _dot(xn, wd_ref[...])
    cqn = _rms(xd[:, :MLA_Q_RANK], qn_ref[...]).astype(BF16)
    ckv = _rms(xd[:, MLA_Q_RANK:MLA_Q_RANK + MLA_KV_RANK], kvn_ref[...])
    lane = lax.broadcasted_iota(jnp.int32, (m, LANES), 1)
    first_half = (lane & (MLA_ROPE // 2)) == 0

    def rope(t):
        swapped = jnp.where(first_half, pltpu.roll(t, LANES - MLA_ROPE // 2, 1),
                            pltpu.roll(t, MLA_ROPE // 2, 1))
        return t * cos_t + swapped * sin_t

    kr2 = rope(xd[:, MLA_Q_RANK + MLA_KV_RANK:])
    ckv_out[...] = ckv
    kr_out[...] = kr2[:, :MLA_ROPE]
    kc_out[:, :MLA_KV_RANK] = ckv.astype(BF16)
    kc_out[:, MLA_KV_RANK:] = jnp.where(lane < MLA_ROPE, kr2, 0.0).astype(BF16)
    if vt_out is not None:
        for j in range(m // TQ):
            vt_out[j] = ckv[j * TQ:(j + 1) * TQ, :].T.astype(BF16)

    qn = _dot(cqn, wqn_ref[...])
    qr = _dot(cqn, wqr_ref[...])
    for c in range(MLA_HEADS // 2):
        rr = rope(qr[:, c * LANES:(c + 1) * LANES]) * scale
        for e in range(2):
            hd = 2 * c + e
            ql = _dot(qn[:, hd * MLA_NOPE:(hd + 1) * MLA_NOPE].astype(BF16), wuk_ref[hd]) * scale
            q_out[hd, :, :MLA_KV_RANK] = ql.astype(BF16)
            rot = rr if e == 0 else pltpu.roll(rr, MLA_ROPE, 1)
            q_out[hd, :, MLA_KV_RANK:] = rot.astype(BF16)


def _mla_out_compute(o_lat, h, wuv_ref, wo_ref):
    parts = [_dot(o_lat[hd], wuv_ref[hd]).astype(BF16) for hd in range(MLA_HEADS)]
    return h + _dot(jnp.concatenate(parts, axis=1), wo_ref[...])


def _ffn_specs():
    return [_resident((1, D_MODEL)), _resident((D_MODEL, D_FF)), _resident((D_MODEL, D_FF)),
            _resident((D_FF, D_MODEL))]


def _ffn_scratch():
    return [pltpu.VMEM((TM, D_FF), BF16)]


def _ffn_gla_proj_kernel(*refs, x_firsts, n_tiles):
    ns = len(x_firsts)
    x_refs, xm_ref = refs[:ns], refs[ns]
    ffn_w = refs[ns + 1:ns + 5]
    proj_w = refs[ns + 5:ns + 10]
    outs = refs[ns + 10:ns + 16]
    outs_m = refs[ns + 16:ns + 22]
    h_scr = refs[ns + 22]
    i = pl.program_id(0)

    def body(x, o):
        y = _ffn_compute(x, *ffn_w, h_scr)
        o[0][...] = y
        for ref, val in zip(o[1:], _gla_proj_compute(y, *proj_w)):
            ref[...] = val

    @pl.when(i < n_tiles)
    def _():
        body(_select(i, x_refs, x_firsts), outs)

    @pl.when(i == n_tiles)
    def _():
        body(xm_ref[...], outs_m)


def _ffn_gla_proj(x_segs, x_meta, ffn_w, proj_w):
    n_tiles = sum(s[2] for s in x_segs)
    n_main = n_tiles * TM
    n_mt = x_meta.shape[0]
    main_w = 2 * GLA_KEY + 2 * GLA_VAL
    widths = [(D_MODEL, F32), (GLA_KEY, BF16), (GLA_KEY, BF16), (GLA_VAL, BF16), (GLA_VAL, BF16),
              (GLA_KEY, F32)]
    in_specs = [_tiles((TM, D_MODEL), 0, f, c) for _, f, c in x_segs]
    in_specs += [_whole((n_mt, D_MODEL))] + _ffn_specs()
    in_specs += [_resident((1, D_MODEL)), _resident((D_MODEL, main_w)), _resident((D_MODEL, LANES)),
                 _resident((LANES, GLA_KEY)), _resident((1, GLA_KEY))]
    return pl.pallas_call(
        functools.partial(_ffn_gla_proj_kernel, x_firsts=tuple(s[1] for s in x_segs), n_tiles=n_tiles),
        grid=(n_tiles + 1,),
        in_specs=in_specs,
        out_specs=[_tiles((TM, w), 0, 0, n_tiles) for w, _ in widths]
        + [_whole((n_mt, w)) for w, _ in widths],
        out_shape=[jax.ShapeDtypeStruct((n_main, w), d) for w, d in widths]
        + [jax.ShapeDtypeStruct((n_mt, w), d) for w, d in widths],
        scratch_shapes=_ffn_scratch(),
        compiler_params=_params("arbitrary"),
        name="ffn_gla_proj",
    )(*[s[0] for s in x_segs], x_meta, *ffn_w, *proj_w)


def _gla_scan_kernel(*refs, chunk, n_chunks, n_steps, has_s0):
    q_ref, k_ref, v_ref, gl_ref = refs[:4]
    s0_ref = refs[4] if has_s0 else None
    o_ref, s_out_ref, st_scr = refs[4 + has_s0:]
    step = pl.program_id(1)

    @pl.when(step == 0)
    def _():
        for h in range(GLA_HEADS):
            if has_s0:
                st_scr[h] = s0_ref[0, h].T
            else:
                st_scr[h] = jnp.zeros((GLA_DV, GLA_DK), F32)

    row = lax.broadcasted_iota(jnp.int32, (chunk, chunk), 0)
    col = lax.broadcasted_iota(jnp.int32, (chunk, chunk), 1)
    causal = row >= col
    tri = jnp.where(causal, 1.0, 0.0).astype(BF16)
    scale = GLA_DK ** -0.5

    for c in range(n_chunks):
        rows = slice(c * chunk, (c + 1) * chunk)
        glog = gl_ref[rows, :]
        hi = glog.astype(BF16)
        lo = (glog - hi.astype(F32)).astype(BF16)
        gc = _dot(tri, hi) + _dot(tri, lo)
        glast = gc[chunk - 1:chunk, :]
        q = q_ref[rows, :].astype(F32) * scale
        k = k_ref[rows, :].astype(F32)
        qg = (q * jnp.exp(gc)).astype(BF16)
        kg = (k * jnp.exp(-gc)).astype(BF16)
        kd = (k * jnp.exp(glast - gc)).astype(BF16)
        decay = jnp.exp(glast)
        for h in range(GLA_HEADS):
            ks = slice(h * GLA_DK, (h + 1) * GLA_DK)
            vs = slice(h * GLA_DV, (h + 1) * GLA_DV)
            a = jnp.where(causal, _dot_nt(qg[:, ks], kg[:, ks]), 0.0).astype(BF16)
            vh = v_ref[rows, vs]
            st = st_scr[h]
            o_ref[rows, vs] = (_dot(a, vh) + _dot_nt(qg[:, ks], st.astype(BF16))).astype(BF16)
            st_scr[h] = st * decay[:, ks] + _dot_tn(vh, kd[:, ks])

    @pl.when(step == n_steps - 1)
    def _():
        for h in range(GLA_HEADS):
            s_out_ref[0, h] = st_scr[h].T


def _gla_scan(q, k, v, gl, s0, *, row0, n_batch, n_steps, chunk, n_chunks, name):
    rb = chunk * n_chunks
    base = row0 // rb

    def rmap(b, s):
        return (base + b * n_steps + s, 0)

    def omap(b, s):
        return (b * n_steps + s, 0)

    def smap(b, s):
        return (b, 0, 0, 0)

    ins = [q, k, v, gl]
    specs = [pl.BlockSpec((rb, GLA_KEY), rmap), pl.BlockSpec((rb, GLA_KEY), rmap),
             pl.BlockSpec((rb, GLA_VAL), rmap), pl.BlockSpec((rb, GLA_KEY), rmap)]
    state_block = (1, GLA_HEADS, GLA_DK, GLA_DV)
    if s0 is not None:
        ins.append(s0)
        specs.append(pl.BlockSpec(state_block, smap))
    kern = functools.partial(_gla_scan_kernel, chunk=chunk, n_chunks=n_chunks, n_steps=n_steps,
                             has_s0=s0 is not None)
    return pl.pallas_call(
        kern,
        grid=(n_batch, n_steps),
        in_specs=specs,
        out_specs=[pl.BlockSpec((rb, GLA_VAL), omap), pl.BlockSpec(state_block, smap)],
        out_shape=[jax.ShapeDtypeStruct((n_batch * n_steps * rb, GLA_VAL), BF16),
                   jax.ShapeDtypeStruct((n_batch,) + state_block[1:], F32)],
        scratch_shapes=[pltpu.VMEM((GLA_HEADS, GLA_DV, GLA_DK), F32)],
        compiler_params=_params("parallel", "arbitrary"),
        name=name,
    )(*ins)


def _gla_out_ffn_kernel(*refs, o_firsts, n_tiles):
    ns = len(o_firsts)
    o_refs = refs[:ns]
    r_ref, h_ref, om_ref, rm_ref, hm_ref, hn_ref, wo_ref = refs[ns:ns + 7]
    ffn_w = refs[ns + 7:ns + 11]
    y_ref, ym_ref, h_scr = refs[ns + 11:]
    i = pl.program_id(0)

    def body(o, r, h):
        return _ffn_compute(_gla_out_compute(o, r, h, hn_ref, wo_ref), *ffn_w, h_scr)

    @pl.when(i < n_tiles)
    def _():
        y_ref[...] = body(_select(i, o_refs, o_firsts), r_ref[...], h_ref[...])

    @pl.when(i == n_tiles)
    def _():
        ym_ref[...] = body(om_ref[...], rm_ref[...], hm_ref[...])


def _gla_out_ffn(o_segs, r, h, o_meta, r_meta, h_meta, head_norm, w_out, ffn_w):
    n_tiles = sum(s[2] for s in o_segs)
    n_mt = h_meta.shape[0]
    in_specs = [_tiles((TM, GLA_VAL), 0, f, c) for _, f, c in o_segs]
    in_specs += [_tiles((TM, GLA_VAL), 0, 0, n_tiles), _tiles((TM, D_MODEL), 0, 0, n_tiles),
                 _whole((n_mt, GLA_VAL)), _whole((n_mt, GLA_VAL)), _whole((n_mt, D_MODEL)),
                 _resident((1, GLA_DV)), _resident((GLA_VAL, D_MODEL))] + _ffn_specs()
    return pl.pallas_call(
        functools.partial(_gla_out_ffn_kernel, o_firsts=tuple(s[1] for s in o_segs), n_tiles=n_tiles),
        grid=(n_tiles + 1,),
        in_specs=in_specs,
        out_specs=[_tiles((TM, D_MODEL), 0, 0, n_tiles), _whole((n_mt, D_MODEL))],
        out_shape=[jax.ShapeDtypeStruct(h.shape, F32), jax.ShapeDtypeStruct(h_meta.shape, F32)],
        scratch_shapes=_ffn_scratch(),
        compiler_params=_params("arbitrary"),
        name="gla_out_ffn",
    )(*[s[0] for s in o_segs], r, h, o_meta, r_meta, h_meta, head_norm.reshape(1, GLA_DV), w_out, *ffn_w)


def _ffn_mla_proj_kernel(*refs, n_tiles):
    x_ref, cos_ref, sin_ref, xm_ref, cosm_ref, sinm_ref = refs[:6]
    ffn_w = refs[6:10]
    proj_w = refs[10:17]
    y_ref, q_out, kc_out, ckv_out, kr_out, vt_out = refs[17:23]
    ym_ref, qm_out, kcm_out, ckvm_out, krm_out = refs[23:28]
    h_scr = refs[28]
    i = pl.program_id(0)

    @pl.when(i < n_tiles)
    def _():
        y = _ffn_compute(x_ref[...], *ffn_w, h_scr)
        y_ref[...] = y
        _mla_proj_compute(y, cos_ref[...], sin_ref[...], *proj_w, q_out, kc_out, ckv_out, kr_out, vt_out)

    @pl.when(i == n_tiles)
    def _():
        y = _ffn_compute(xm_ref[...], *ffn_w, h_scr)
        ym_ref[...] = y
        _mla_proj_compute(y, cosm_ref[...], sinm_ref[...], *proj_w, qm_out, kcm_out, ckvm_out, krm_out,
                          None)


def _ffn_mla_proj(x, x_meta, cos_t, sin_t, cos_m, sin_m, ffn_w, proj_w):
    n_main = x.shape[0]
    n_tiles = n_main // TM
    n_mt = x_meta.shape[0]
    kt_per_tile = TM // TQ

    def rows(w):
        return _tiles((TM, w), 0, 0, n_tiles)

    in_specs = [rows(D_MODEL), rows(LANES), rows(LANES), _whole((n_mt, D_MODEL)), _whole((n_mt, LANES)),
                _whole((n_mt, LANES))] + _ffn_specs()
    in_specs += [_resident(w.shape) for w in proj_w]
    return pl.pallas_call(
        functools.partial(_ffn_mla_proj_kernel, n_tiles=n_tiles),
        grid=(n_tiles + 1,),
        in_specs=in_specs,
        out_specs=[rows(D_MODEL), _tiles((MLA_HEADS, TM, QK_W), 1, 0, n_tiles), rows(QK_W),
                   rows(MLA_KV_RANK), rows(MLA_ROPE),
                   _tiles((kt_per_tile, MLA_KV_RANK, TQ), 0, 0, n_tiles),
                   _whole((n_mt, D_MODEL)), _whole((MLA_HEADS, n_mt, QK_W)), _whole((n_mt, QK_W)),
                   _whole((n_mt, MLA_KV_RANK)), _whole((n_mt, MLA_ROPE))],
        out_shape=[jax.ShapeDtypeStruct((n_main, D_MODEL), F32),
                   jax.ShapeDtypeStruct((MLA_HEADS, n_main, QK_W), BF16),
                   jax.ShapeDtypeStruct((n_main, QK_W), BF16),
                   jax.ShapeDtypeStruct((n_main, MLA_KV_RANK), F32),
                   jax.ShapeDtypeStruct((n_main, MLA_ROPE), F32),
                   jax.ShapeDtypeStruct((n_main // TQ, MLA_KV_RANK, TQ), BF16),
                   jax.ShapeDtypeStruct((n_mt, D_MODEL), F32),
                   jax.ShapeDtypeStruct((MLA_HEADS, n_mt, QK_W), BF16),
                   jax.ShapeDtypeStruct((n_mt, QK_W), BF16),
                   jax.ShapeDtypeStruct((n_mt, MLA_KV_RANK), F32),
                   jax.ShapeDtypeStruct((n_mt, MLA_ROPE), F32)],
        scratch_shapes=_ffn_scratch(),
        compiler_params=_params("arbitrary"),
        name="ffn_mla_proj",
    )(x, cos_t, sin_t, x_meta, cos_m, sin_m, *ffn_w, *proj_w)


def _attn_prompt_kernel(q_ref, kf_ref, km_ref, vtf_ref, vtm_ref, o_ref, m_scr, l_scr, acc_scr):
    i = pl.program_id(1)
    cols = MLA_HEADS * TQ
    q = q_ref[...].reshape(cols, QK_W)

    def update(keys, values_t, mask, first):
        s = _dot_nt(keys, q)
        if mask is not None:
            s = jnp.where(mask, s, -jnp.inf)
        m_cur = jnp.max(s, axis=0, keepdims=True)
        if first:
            m_new = m_cur
        else:
            m_old = m_scr[...]
            m_new = jnp.maximum(m_old, m_cur)
            alpha = jnp.exp2(m_old - m_new)
        p = jnp.exp2(s - m_new)
        p_sum = jnp.sum(p, axis=0, keepdims=True)
        pb = p.astype(BF16)
        pv = None
        for vt, rs in values_t:
            term = _dot(vt, pb[rs, :])
            pv = term if pv is None else pv + term
        if first:
            l_scr[...] = p_sum
            acc_scr[...] = pv
        else:
            l_scr[...] = alpha * l_scr[...] + p_sum
            acc_scr[...] = alpha * acc_scr[...] + pv
        m_scr[...] = m_new

    key = lax.broadcasted_iota(jnp.int32, (TQ + N_META, cols), 0)
    qry = lax.broadcasted_iota(jnp.int32, (TQ + N_META, cols), 1) % TQ
    visible = (key >= TQ) | ((key // CHUNK) <= (qry // CHUNK))
    keys0 = jnp.concatenate([kf_ref[i], km_ref[...]], axis=0)
    update(keys0, [(vtf_ref[i], slice(0, TQ)), (vtm_ref[0], slice(TQ, TQ + N_META))], visible, True)

    def body(j, carry):
        update(kf_ref[j], [(vtf_ref[j], slice(0, TQ))], None, False)
        return carry

    lax.fori_loop(0, i, body, 0)

    for h in range(MLA_HEADS):
        cs = slice(h * TQ, (h + 1) * TQ)
        o_ref[h] = (acc_scr[:, cs] / l_scr[:, cs]).T.astype(BF16)


def _attn_prompt(q, k_tiles, kc_meta, vt_tiles, vt_meta, *, n_batch, seq):
    steps = seq // TQ
    cols = MLA_HEADS * TQ
    return pl.pallas_call(
        _attn_prompt_kernel,
        grid=(n_batch, steps),
        in_specs=[pl.BlockSpec((MLA_HEADS, TQ, QK_W), lambda b, i: (0, b * steps + i, 0)),
                  pl.BlockSpec((steps, TQ, QK_W), lambda b, i: (b, 0, 0)),
                  pl.BlockSpec((N_META, QK_W), lambda b, i: (b, 0)),
                  pl.BlockSpec((steps, MLA_KV_RANK, TQ), lambda b, i: (b, 0, 0)),
                  pl.BlockSpec((1, MLA_KV_RANK, N_META), lambda b, i: (b, 0, 0))],
        out_specs=pl.BlockSpec((MLA_HEADS, TQ, MLA_KV_RANK), lambda b, i: (0, b * steps + i, 0)),
        out_shape=jax.ShapeDtypeStruct((MLA_HEADS, n_batch * seq, MLA_KV_RANK), BF16),
        scratch_shapes=[pltpu.VMEM((1, cols), F32), pltpu.VMEM((1, cols), F32),
                        pltpu.VMEM((MLA_KV_RANK, cols), F32)],
        compiler_params=_params("parallel", "arbitrary"),
        name="mla_attn_prompt",
    )(q, k_tiles, kc_meta, vt_tiles, vt_meta)


def _attn_full_kernel(*refs, n_q, has_past):
    if has_past:
        q_ref, kn_ref, pl_ref, pr_ref, o_ref = refs
    else:
        q_ref, kn_ref, o_ref = refs
    rows = MLA_HEADS * n_q
    q = q_ref[...].reshape(rows, QK_W)
    kn = kn_ref[...]
    s_n = _dot_nt(q, kn)
    m = jnp.max(s_n, axis=1, keepdims=True)
    if has_past:
        lat = pl_ref[0].astype(BF16)
        rp = pr_ref[0].astype(BF16)
        s_p = (_dot_nt(q[:, :MLA_KV_RANK], lat)
               + _dot_nt(q[:, MLA_KV_RANK:MLA_KV_RANK + MLA_ROPE], rp))
        m = jnp.maximum(m, jnp.max(s_p, axis=1, keepdims=True))
    p_n = jnp.exp2(s_n - m)
    l = jnp.sum(p_n, axis=1, keepdims=True)
    acc = _dot(p_n.astype(BF16), kn[:, :MLA_KV_RANK])
    if has_past:
        p_p = jnp.exp2(s_p - m)
        l = l + jnp.sum(p_p, axis=1, keepdims=True)
        acc = acc + _dot(p_p.astype(BF16), lat)
    o_ref[...] = (acc / l).astype(BF16).reshape(MLA_HEADS, n_q, MLA_KV_RANK)


def _attn_full(q, kc, past_lat, past_rope, *, row0, n_batch, n_q, name):
    base = row0 // n_q
    has_past = past_lat is not None
    ins = [q, kc]
    specs = [pl.BlockSpec((MLA_HEADS, n_q, QK_W), lambda b: (0, base + b, 0)),
             pl.BlockSpec((n_q, QK_W), lambda b: (base + b, 0))]
    if has_past:
        past = past_lat.shape[1]
        ins += [past_lat, past_rope]
        specs += [pl.BlockSpec((1, past, MLA_KV_RANK), lambda b: (b, 0, 0)),
                  pl.BlockSpec((1, past, MLA_ROPE), lambda b: (b, 0, 0))]
    return pl.pallas_call(
        functools.partial(_attn_full_kernel, n_q=n_q, has_past=has_past),
        grid=(n_batch,),
        in_specs=specs,
        out_specs=pl.BlockSpec((MLA_HEADS, n_q, MLA_KV_RANK), lambda b: (0, b, 0)),
        out_shape=jax.ShapeDtypeStruct((MLA_HEADS, n_batch * n_q, MLA_KV_RANK), BF16),
        compiler_params=_params("parallel"),
        name=name,
    )(*ins)


def _mla_out_ffn_final_kernel(*refs, firsts, n_tiles):
    ns = len(firsts)
    o_refs = refs[:ns]
    h_ref, om_ref, hm_ref, wuv_ref, wo_ref = refs[ns:ns + 5]
    ffn_w = refs[ns + 5:ns + 9]
    fw_ref = refs[ns + 9]
    y_refs = refs[ns + 10:2 * ns + 10]
    ym_ref, h_scr = refs[2 * ns + 10:]
    i = pl.program_id(0)

    def body(o_lat, h):
        y = _mla_out_compute(o_lat, h, wuv_ref, wo_ref)
        return _rms(_ffn_compute(y, *ffn_w, h_scr), fw_ref[...])

    @pl.when(i < n_tiles)
    def _():
        y = body(_select(i, o_refs, firsts), h_ref[...])
        bounds = list(firsts[1:]) + [n_tiles]
        for y_ref, lo, hi in zip(y_refs, firsts, bounds):
            @pl.when((i >= lo) & (i < hi))
            def _(y_ref=y_ref):
                y_ref[...] = y

    @pl.when(i == n_tiles)
    def _():
        ym_ref[...] = body(om_ref[...], hm_ref[...])


def _mla_out_ffn_final(o_segs, h, o_meta, h_meta, w_uv_t, w_out, ffn_w, final_w):
    n_tiles = sum(s[2] for s in o_segs)
    n_mt = h_meta.shape[0]
    firsts = tuple(s[1] for s in o_segs)
    in_specs = [_tiles((MLA_HEADS, TM, MLA_KV_RANK), 1, f, c) for _, f, c in o_segs]
    in_specs += [_tiles((TM, D_MODEL), 0, 0, n_tiles), _whole(o_meta.shape), _whole(h_meta.shape),
                 _resident(w_uv_t.shape), _resident(w_out.shape)]
    in_specs += _ffn_specs() + [_resident((1, D_MODEL))]
    return pl.pallas_call(
        functools.partial(_mla_out_ffn_final_kernel, firsts=firsts, n_tiles=n_tiles),
        grid=(n_tiles + 1,),
        in_specs=in_specs,
        out_specs=[_tiles((TM, D_MODEL), 0, f, c) for _, f, c in o_segs] + [_whole((n_mt, D_MODEL))],
        out_shape=[jax.ShapeDtypeStruct((c * TM, D_MODEL), F32) for _, _, c in o_segs]
        + [jax.ShapeDtypeStruct((n_mt, D_MODEL), F32)],
        scratch_shapes=_ffn_scratch(),
        compiler_params=_params("arbitrary"),
        name="mla_out_ffn_final",
    )(*[s[0] for s in o_segs], h, o_meta, h_meta, w_uv_t, w_out, *ffn_w, final_w.reshape(1, D_MODEL))


def _rope_tables(pos):
    half = MLA_ROPE // 2
    inv = ROPE_THETA ** (-jnp.arange(half, dtype=F32) / half)
    ang = pos[:, None] * inv[None, :]
    cos = jnp.cos(ang)
    sin = jnp.sin(ang)
    reps = LANES // MLA_ROPE
    return (jnp.tile(cos, (1, 2 * reps)), jnp.tile(jnp.concatenate([-sin, sin], axis=1), (1, reps)))


def kernel(x_prompt, x_sample, state_gla, cache_mla_latent, cache_mla_rope, meta_tokens, ffn1_norm, ffn1_w_gate, ffn1_w_up, ffn1_w_down, mix_norm, gla_w_in, gla_w_gate_up, gla_b_gate, gla_head_norm, gla_w_out, mla_w_down, mla_q_norm, mla_w_uq, mla_kv_norm, mla_w_uk, mla_w_uv, mla_w_out, ffn2_norm, ffn2_w_gate, ffn2_w_up, ffn2_w_down, final_norm):
    bp, seq, _ = x_prompt.shape
    bs, ls, _ = x_sample.shape
    past = cache_mla_latent.shape[2]
    n_fr = bp * seq
    n_sm = bs * ls
    n_mt = bp * N_META
    assert ffn1_norm.shape[0] == 2
    assert n_fr % TM == 0 and n_sm % TM == 0 and TM % TQ == 0 and seq % TQ == 0 and TQ % CHUNK == 0
    assert past % CHUNK == 0 and ls <= CHUNK and N_META <= CHUNK
    fr_tiles = n_fr // TM
    sm_tiles = n_sm // TM

    def ffn_weights(norm, wg, wu, wd, i):
        return (norm[i].reshape(1, D_MODEL), wg[i].astype(BF16), wu[i].astype(BF16), wd[i].astype(BF16))

    x_meta = jnp.tile(meta_tokens.astype(F32), (bp, 1))
    w_in = gla_w_in[0]
    main_w = 2 * GLA_KEY + 2 * GLA_VAL
    gla_proj_w = (mix_norm[0].reshape(1, D_MODEL), w_in[:, :main_w].astype(BF16),
                  jnp.pad(w_in[:, main_w:], ((0, 0), (0, LANES - GLA_RANK))).astype(BF16),
                  jnp.pad(gla_w_gate_up[0], ((0, LANES - GLA_RANK), (0, 0))).astype(BF16),
                  gla_b_gate[0].reshape(1, GLA_KEY))
    (h, q, k, v, r, gl, h_m, q_m, k_m, v_m, r_m, gl_m) = _ffn_gla_proj(
        [(x_prompt.reshape(n_fr, D_MODEL), 0, fr_tiles), (x_sample.reshape(n_sm, D_MODEL), fr_tiles, sm_tiles)],
        x_meta, ffn_weights(ffn1_norm, ffn1_w_gate, ffn1_w_up, ffn1_w_down, 0), gla_proj_w)
    o_m, s_meta = _gla_scan(q_m, k_m, v_m, gl_m, None, row0=0, n_batch=bp, n_steps=1,
                            chunk=N_META, n_chunks=1, name="gla_scan_meta")
    o_f, s_p = _gla_scan(q, k, v, gl, s_meta, row0=0, n_batch=bp, n_steps=seq // TQ,
                         chunk=CHUNK, n_chunks=TQ // CHUNK, name="gla_scan_frames")
    o_s, s_s = _gla_scan(q, k, v, gl, state_gla[0], row0=n_fr, n_batch=bs, n_steps=1,
                         chunk=ls, n_chunks=1, name="gla_scan_sample")
    h, h_m = _gla_out_ffn([(o_f, 0, fr_tiles), (o_s, fr_tiles, sm_tiles)], r, h, o_m, r_m, h_m,
                          gla_head_norm[0], gla_w_out[0].astype(BF16),
                          ffn_weights(ffn2_norm, ffn2_w_gate, ffn2_w_up, ffn2_w_down, 0))

    pos = jnp.concatenate([jnp.tile(N_META + jnp.arange(seq, dtype=F32), bp),
                           jnp.tile(past + jnp.arange(ls, dtype=F32), bs)])
    cos_t, sin_t = _rope_tables(pos)
    cos_m, sin_m = _rope_tables(jnp.tile(jnp.arange(N_META, dtype=F32), bp))
    w_down = mla_w_down[0]
    w_down = jnp.concatenate([w_down, w_down[:, -MLA_ROPE:]], axis=1).astype(BF16)
    w_uq = mla_w_uq[0].reshape(MLA_Q_RANK, MLA_HEADS, MLA_NOPE + MLA_ROPE)
    w_qn = w_uq[:, :, :MLA_NOPE].reshape(MLA_Q_RANK, MLA_HEADS * MLA_NOPE).astype(BF16)
    w_qr = w_uq[:, :, MLA_NOPE:].reshape(MLA_Q_RANK, MLA_HEADS * MLA_ROPE).astype(BF16)
    w_uk_t = jnp.transpose(mla_w_uk[0], (1, 2, 0)).astype(BF16)
    w_uv_t = jnp.transpose(mla_w_uv[0], (1, 0, 2)).astype(BF16)
    mla_proj_w = (mix_norm[1].reshape(1, D_MODEL), w_down, mla_q_norm[0].reshape(1, MLA_Q_RANK), w_qn, w_qr,
                  mla_kv_norm[0].reshape(1, MLA_KV_RANK), w_uk_t)
    (h, qa, kc, ckv, kr, vt, h_m, qa_m, kc_m, ckv_m, kr_m) = _ffn_mla_proj(
        h, h_m, cos_t, sin_t, cos_m, sin_m,
        ffn_weights(ffn1_norm, ffn1_w_gate, ffn1_w_up, ffn1_w_down, 1), mla_proj_w)
    k_tiles = kc.reshape((n_fr + n_sm) // TQ, TQ, QK_W)
    vt_meta = jnp.swapaxes(kc_m[:, :MLA_KV_RANK].reshape(bp, N_META, MLA_KV_RANK), 1, 2)
    ol_f = _attn_prompt(qa, k_tiles, kc_m, vt, vt_meta, n_batch=bp, seq=seq)
    ol_s = _attn_full(qa, kc, cache_mla_latent[0], cache_mla_rope[0], row0=n_fr, n_batch=bs, n_q=ls,
                      name="mla_attn_sample")
    ol_m = _attn_full(qa_m, kc_m, None, None, row0=0, n_batch=bp, n_q=N_META, name="mla_attn_meta")
    y_prompt, y_sample, _ = _mla_out_ffn_final(
        [(ol_f, 0, fr_tiles), (ol_s, fr_tiles, sm_tiles)], h, ol_m, h_m, w_uv_t, mla_w_out[0].astype(BF16),
        ffn_weights(ffn2_norm, ffn2_w_gate, ffn2_w_up, ffn2_w_down, 1), final_norm)

    def seq_major(t_main, t_meta, width):
        return jnp.concatenate([t_meta.reshape(bp, N_META, width), t_main[:n_fr].reshape(bp, seq, width)],
                               axis=1)

    return (y_prompt.reshape(bp, seq, D_MODEL), y_sample.reshape(bs, ls, D_MODEL),
            s_p[None], s_s[None],
            seq_major(ckv, ckv_m, MLA_KV_RANK)[None], seq_major(kr, kr_m, MLA_ROPE)[None],
            ckv[n_fr:].reshape(bs, ls, MLA_KV_RANK)[None], kr[n_fr:].reshape(bs, ls, MLA_ROPE)[None])
```

```python
import functools

import jax
import jax.numpy as jnp
from jax import lax
from jax.experimental import pallas as pl
from jax.experimental.pallas import tpu as pltpu

F32 = jnp.float32
BF16 = jnp.bfloat16

D_MODEL = 1024
D_FF = 2816
RMS_EPS = 1e-6
N_META = 16
CHUNK = 64

GLA_HEADS = 4
GLA_DK = 128
GLA_DV = 256
GLA_KEY = GLA_HEADS * GLA_DK
GLA_VAL = GLA_HEADS * GLA_DV
GLA_RANK = 16
GLA_GATE_NORMALIZER = 16.0

MLA_HEADS = 8
MLA_Q_RANK = 384
MLA_KV_RANK = 256
MLA_NOPE = 128
MLA_ROPE = 64
MLA_V = 128
ROPE_THETA = 10000.0
QK_W = MLA_KV_RANK + 128
LOG2_E = 1.4426950408889634

LANES = 128
VMEM_LIMIT = 56 * 1024 * 1024

TM = 512
FF_CHUNK = 256
TQ = 256

NT_DIMS = (((1,), (1,)), ((), ()))
TN_DIMS = (((0,), (0,)), ((), ()))


def _dot(a, b):
    return jnp.dot(a, b, preferred_element_type=F32)


def _dot_nt(a, b):
    return lax.dot_general(a, b, NT_DIMS, preferred_element_type=F32)


def _dot_tn(a, b):
    return lax.dot_general(a, b, TN_DIMS, preferred_element_type=F32)


def _rms(x, w):
    return x * lax.rsqrt(jnp.mean(x * x, axis=-1, keepdims=True) + RMS_EPS) * w


def _silu(x):
    return x * jax.nn.sigmoid(x)


def _params(*sem):
    return pltpu.CompilerParams(dimension_semantics=sem, vmem_limit_bytes=VMEM_LIMIT)


def _resident(shape):
    zeros = (0,) * len(shape)
    return pl.BlockSpec(shape, lambda *_: zeros, pipeline_mode=pl.Buffered(1))


def _whole(shape):
    zeros = (0,) * len(shape)
    return pl.BlockSpec(shape, lambda *_: zeros)


def _tiles(block, row_axis=0, first=0, count=None):
    nd = len(block)

    def imap(i):
        t = i - first
        if count is not None:
            t = jnp.clip(t, 0, count - 1)
        idx = [0] * nd
        idx[row_axis] = t
        return tuple(idx)

    return pl.BlockSpec(block, imap)


def _select(i, refs, firsts):
    val = refs[0][...]
    for ref, first in zip(refs[1:], firsts[1:]):
        val = jnp.where(i >= first, ref[...], val)
    return val


def _ffn_compute(x, nw_ref, wg_ref, wu_ref, wd_ref, h_scr):
    m = x.shape[0]
    xn = _rms(x, nw_ref[...]).astype(BF16)
    for c in range(D_FF // FF_CHUNK):
        sl = slice(c * FF_CHUNK, (c + 1) * FF_CHUNK)
        g = _dot(xn, wg_ref[:, sl])
        u = _dot(xn, wu_ref[:, sl])
        h_scr[:m, sl] = (_silu(g) * u).astype(BF16)
    return x + 0.5 * _dot(h_scr[:m, :], wd_ref[...])


def _gla_proj_compute(x, nw_ref, wm_ref, wgd_ref, wgu_ref, bg_ref):
    xn = _rms(x, nw_ref[...]).astype(BF16)
    q = _dot(xn, wm_ref[:, 0:GLA_KEY]).astype(BF16)
    k = _dot(xn, wm_ref[:, GLA_KEY:2 * GLA_KEY]).astype(BF16)
    v = _dot(xn, wm_ref[:, 2 * GLA_KEY:2 * GLA_KEY + GLA_VAL]).astype(BF16)
    r = _dot(xn, wm_ref[:, 2 * GLA_KEY + GLA_VAL:]).astype(BF16)
    gd = _dot(xn, wgd_ref[...])
    z = _dot(gd.astype(BF16), wgu_ref[...]) + bg_ref[...]
    log_sig = jnp.minimum(z, 0.0) - jnp.log1p(jnp.exp(-jnp.abs(z)))
    return q, k, v, r, log_sig * (1.0 / GLA_GATE_NORMALIZER)


def _gla_out_compute(o, r, h, hn_ref, wo_ref):
    parts = []
    for hd in range(GLA_HEADS):
        vs = slice(hd * GLA_DV, (hd + 1) * GLA_DV)
        on = _rms(o[:, vs].astype(F32), hn_ref[...])
        parts.append((on * _silu(r[:, vs].astype(F32))).astype(BF16))
    return h + _dot(jnp.concatenate(parts, axis=1), wo_ref[...])


def _mla_proj_compute(x, cos_t, sin_t, nw_ref, wd_ref, qn_ref, wqn_ref, wqr_ref, kvn_ref, wuk_ref,
                      q_out, kc_out, ckv_out, kr_out, vt_out):
    m = x.shape[0]
    scale = (MLA_NOPE + MLA_ROPE) ** -0.5 * LOG2_E
    xn = _rms(x, nw_ref[...]).astype(BF16)
    xd = _dot(xn, wd_ref[...])
    cqn = _rms(xd[:, :MLA_Q_RANK], qn_ref[...]).astype(BF16)
    ckv = _rms(xd[:, MLA_Q_RANK:MLA_Q_RANK + MLA_KV_RANK], kvn_ref[...])
    lane = lax.broadcasted_iota(jnp.int32, (m, LANES), 1)
    first_half = (lane & (MLA_ROPE // 2)) == 0

    def rope(t):
        swapped = jnp.where(first_half, pltpu.roll(t, LANES - MLA_ROPE // 2, 1),
                            pltpu.roll(t, MLA_ROPE // 2, 1))
        return t * cos_t + swapped * sin_t

    kr2 = rope(xd[:, MLA_Q_RANK + MLA_KV_RANK:])
    ckv_out[...] = ckv
    kr_out[...] = kr2[:, :MLA_ROPE]
    kc_out[:, :MLA_KV_RANK] = ckv.astype(BF16)
    kc_out[:, MLA_KV_RANK:] = jnp.where(lane < MLA_ROPE, kr2, 0.0).astype(BF16)
    if vt_out is not None:
        for j in range(m // TQ):
            vt_out[j] = ckv[j * TQ:(j + 1) * TQ, :].T.astype(BF16)

    qn = _dot(cqn, wqn_ref[...])
    qr = _dot(cqn, wqr_ref[...])
    for c in range(MLA_HEADS // 2):
        rr = rope(qr[:, c * LANES:(c + 1) * LANES]) * scale
        for e in range(2):
            hd = 2 * c + e
            ql = _dot(qn[:, hd * MLA_NOPE:(hd + 1) * MLA_NOPE].astype(BF16), wuk_ref[hd]) * scale
            q_out[hd, :, :MLA_KV_RANK] = ql.astype(BF16)
            rot = rr if e == 0 else pltpu.roll(rr, MLA_ROPE, 1)
            q_out[hd, :, MLA_KV_RANK:] = rot.astype(BF16)


def _mla_out_compute(o_lat, h, wuv_ref, wo_ref):
    parts = [_dot(o_lat[hd], wuv_ref[hd]).astype(BF16) for hd in range(MLA_HEADS)]
    return h + _dot(jnp.concatenate(parts, axis=1), wo_ref[...])


def _ffn_specs():
    return [_resident((1, D_MODEL)), _resident((D_MODEL, D_FF)), _resident((D_MODEL, D_FF)),
            _resident((D_FF, D_MODEL))]


def _ffn_scratch():
    return [pltpu.VMEM((TM, D_FF), BF16)]


def _ffn_gla_proj_kernel(*refs, x_firsts, n_tiles):
    ns = len(x_firsts)
    x_refs, xm_ref = refs[:ns], refs[ns]
    ffn_w = refs[ns + 1:ns + 5]
    proj_w = refs[ns + 5:ns + 10]
    outs = refs[ns + 10:ns + 16]
    outs_m = refs[ns + 16:ns + 22]
    h_scr = refs[ns + 22]
    i = pl.program_id(0)

    def body(x, o):
        y = _ffn_compute(x, *ffn_w, h_scr)
        o[0][...] = y
        for ref, val in zip(o[1:], _gla_proj_compute(y, *proj_w)):
            ref[...] = val

    @pl.when(i < n_tiles)
    def _():
        body(_select(i, x_refs, x_firsts), outs)

    @pl.when(i == n_tiles)
    def _():
        body(xm_ref[...], outs_m)


def _ffn_gla_proj(x_segs, x_meta, ffn_w, proj_w):
    n_tiles = sum(s[2] for s in x_segs)
    n_main = n_tiles * TM
    n_mt = x_meta.shape[0]
    main_w = 2 * GLA_KEY + 2 * GLA_VAL
    widths = [(D_MODEL, F32), (GLA_KEY, BF16), (GLA_KEY, BF16), (GLA_VAL, BF16), (GLA_VAL, BF16),
              (GLA_KEY, F32)]
    in_specs = [_tiles((TM, D_MODEL), 0, f, c) for _, f, c in x_segs]
    in_specs += [_whole((n_mt, D_MODEL))] + _ffn_specs()
    in_specs += [_resident((1, D_MODEL)), _resident((D_MODEL, main_w)), _resident((D_MODEL, LANES)),
                 _resident((LANES, GLA_KEY)), _resident((1, GLA_KEY))]
    return pl.pallas_call(
        functools.partial(_ffn_gla_proj_kernel, x_firsts=tuple(s[1] for s in x_segs), n_tiles=n_tiles),
        grid=(n_tiles + 1,),
        in_specs=in_specs,
        out_specs=[_tiles((TM, w), 0, 0, n_tiles) for w, _ in widths]
        + [_whole((n_mt, w)) for w, _ in widths],
        out_shape=[jax.ShapeDtypeStruct((n_main, w), d) for w, d in widths]
        + [jax.ShapeDtypeStruct((n_mt, w), d) for w, d in widths],
        scratch_shapes=_ffn_scratch(),
        compiler_params=_params("arbitrary"),
        name="ffn_gla_proj",
    )(*[s[0] for s in x_segs], x_meta, *ffn_w, *proj_w)


def _gla_scan_kernel(*refs, chunk, n_chunks, n_steps, has_s0):
    q_ref, k_ref, v_ref, gl_ref = refs[:4]
    s0_ref = refs[4] if has_s0 else None
    o_ref, s_out_ref, st_scr = refs[4 + has_s0:]
    step = pl.program_id(1)

    @pl.when(step == 0)
    def _():
        for h in range(GLA_HEADS):
            if has_s0:
                st_scr[h] = s0_ref[0, h].T
            else:
                st_scr[h] = jnp.zeros((GLA_DV, GLA_DK), F32)

    row = lax.broadcasted_iota(jnp.int32, (chunk, chunk), 0)
    col = lax.broadcasted_iota(jnp.int32, (chunk, chunk), 1)
    causal = row >= col
    tri = jnp.where(causal, 1.0, 0.0).astype(BF16)
    scale = GLA_DK ** -0.5

    for c in range(n_chunks):
        rows = slice(c * chunk, (c + 1) * chunk)
        glog = gl_ref[rows, :]
        hi = glog.astype(BF16)
        lo = (glog - hi.astype(F32)).astype(BF16)
        gc = _dot(tri, hi) + _dot(tri, lo)
        glast = gc[chunk - 1:chunk, :]
        q = q_ref[rows, :].astype(F32) * scale
        k = k_ref[rows, :].astype(F32)
        qg = (q * jnp.exp(gc)).astype(BF16)
        kg = (k * jnp.exp(-gc)).astype(BF16)
        kd = (k * jnp.exp(glast - gc)).astype(BF16)
        decay = jnp.exp(glast)
        for h in range(GLA_HEADS):
            ks = slice(h * GLA_DK, (h + 1) * GLA_DK)
            vs = slice(h * GLA_DV, (h + 1) * GLA_DV)
            a = jnp.where(causal, _dot_nt(qg[:, ks], kg[:, ks]), 0.0).astype(BF16)
            vh = v_ref[rows, vs]
            st = st_scr[h]
            o_ref[rows, vs] = (_dot(a, vh) + _dot_nt(qg[:, ks], st.astype(BF16))).astype(BF16)
            st_scr[h] = st * decay[:, ks] + _dot_tn(vh, kd[:, ks])

    @pl.when(step == n_steps - 1)
    def _():
        for h in range(GLA_HEADS):
            s_out_ref[0, h] = st_scr[h].T


def _gla_scan(q, k, v, gl, s0, *, row0, n_batch, n_steps, chunk, n_chunks, name):
    rb = chunk * n_chunks
    base = row0 // rb

    def rmap(b, s):
        return (base + b * n_steps + s, 0)

    def omap(b, s):
        return (b * n_steps + s, 0)

    def smap(b, s):
        return (b, 0, 0, 0)

    ins = [q, k, v, gl]
    specs = [pl.BlockSpec((rb, GLA_KEY), rmap), pl.BlockSpec((rb, GLA_KEY), rmap),
             pl.BlockSpec((rb, GLA_VAL), rmap), pl.BlockSpec((rb, GLA_KEY), rmap)]
    state_block = (1, GLA_HEADS, GLA_DK, GLA_DV)
    if s0 is not None:
        ins.append(s0)
        specs.append(pl.BlockSpec(state_block, smap))
    kern = functools.partial(_gla_scan_kernel, chunk=chunk, n_chunks=n_chunks, n_steps=n_steps,
                             has_s0=s0 is not None)
    return pl.pallas_call(
        kern,
        grid=(n_batch, n_steps),
        in_specs=specs,
        out_specs=[pl.BlockSpec((rb, GLA_VAL), omap), pl.BlockSpec(state_block, smap)],
        out_shape=[jax.ShapeDtypeStruct((n_batch * n_steps * rb, GLA_VAL), BF16),
                   jax.ShapeDtypeStruct((n_batch,) + state_block[1:], F32)],
        scratch_shapes=[pltpu.VMEM((GLA_HEADS, GLA_DV, GLA_DK), F32)],
        compiler_params=_params("parallel", "arbitrary"),
        name=name,
    )(*ins)


def _gla_out_ffn_kernel(*refs, o_firsts, n_tiles):
    ns = len(o_firsts)
    o_refs = refs[:ns]
    r_ref, h_ref, om_ref, rm_ref, hm_ref, hn_ref, wo_ref = refs[ns:ns + 7]
    ffn_w = refs[ns + 7:ns + 11]
    y_ref, ym_ref, h_scr = refs[ns + 11:]
    i = pl.program_id(0)

    def body(o, r, h):
        return _ffn_compute(_gla_out_compute(o, r, h, hn_ref, wo_ref), *ffn_w, h_scr)

    @pl.when(i < n_tiles)
    def _():
        y_ref[...] = body(_select(i, o_refs, o_firsts), r_ref[...], h_ref[...])

    @pl.when(i == n_tiles)
    def _():
        ym_ref[...] = body(om_ref[...], rm_ref[...], hm_ref[...])


def _gla_out_ffn(o_segs, r, h, o_meta, r_meta, h_meta, head_norm, w_out, ffn_w):
    n_tiles = sum(s[2] for s in o_segs)
    n_mt = h_meta.shape[0]
    in_specs = [_tiles((TM, GLA_VAL), 0, f, c) for _, f, c in o_segs]
    in_specs += [_tiles((TM, GLA_VAL), 0, 0, n_tiles), _tiles((TM, D_MODEL), 0, 0, n_tiles),
                 _whole((n_mt, GLA_VAL)), _whole((n_mt, GLA_VAL)), _whole((n_mt, D_MODEL)),
                 _resident((1, GLA_DV)), _resident((GLA_VAL, D_MODEL))] + _ffn_specs()
    return pl.pallas_call(
        functools.partial(_gla_out_ffn_kernel, o_firsts=tuple(s[1] for s in o_segs), n_tiles=n_tiles),
        grid=(n_tiles + 1,),
        in_specs=in_specs,
        out_specs=[_tiles((TM, D_MODEL), 0, 0, n_tiles), _whole((n_mt, D_MODEL))],
        out_shape=[jax.ShapeDtypeStruct(h.shape, F32), jax.ShapeDtypeStruct(h_meta.shape, F32)],
        scratch_shapes=_ffn_scratch(),
        compiler_params=_params("arbitrary"),
        name="gla_out_ffn",
    )(*[s[0] for s in o_segs], r, h, o_meta, r_meta, h_meta, head_norm.reshape(1, GLA_DV), w_out, *ffn_w)


def _ffn_mla_proj_kernel(*refs, n_tiles):
    x_ref, cos_ref, sin_ref, xm_ref, cosm_ref, sinm_ref = refs[:6]
    ffn_w = refs[6:10]
    proj_w = refs[10:17]
    y_ref, q_out, kc_out, ckv_out, kr_out, vt_out = refs[17:23]
    ym_ref, qm_out, kcm_out, ckvm_out, krm_out = refs[23:28]
    h_scr = refs[28]
    i = pl.program_id(0)

    @pl.when(i < n_tiles)
    def _():
        y = _ffn_compute(x_ref[...], *ffn_w, h_scr)
        y_ref[...] = y
        _mla_proj_compute(y, cos_ref[...], sin_ref[...], *proj_w, q_out, kc_out, ckv_out, kr_out, vt_out)

    @pl.when(i == n_tiles)
    def _():
        y = _ffn_compute(xm_ref[...], *ffn_w, h_scr)
        ym_ref[...] = y
        _mla_proj_compute(y, cosm_ref[...], sinm_ref[...], *proj_w, qm_out, kcm_out, ckvm_out, krm_out,
                          None)


def _ffn_mla_proj(x, x_meta, cos_t, sin_t, cos_m, sin_m, ffn_w, proj_w):
    n_main = x.shape[0]
    n_tiles = n_main // TM
    n_mt = x_meta.shape[0]
    kt_per_tile = TM // TQ

    def rows(w):
        return _tiles((TM, w), 0, 0, n_tiles)

    in_specs = [rows(D_MODEL), rows(LANES), rows(LANES), _whole((n_mt, D_MODEL)), _whole((n_mt, LANES)),
                _whole((n_mt, LANES))] + _ffn_specs()
    in_specs += [_resident(w.shape) for w in proj_w]
    return pl.pallas_call(
        functools.partial(_ffn_mla_proj_kernel, n_tiles=n_tiles),
        grid=(n_tiles + 1,),
        in_specs=in_specs,
        out_specs=[rows(D_MODEL), _tiles((MLA_HEADS, TM, QK_W), 1, 0, n_tiles), rows(QK_W),
                   rows(MLA_KV_RANK), rows(MLA_ROPE),
                   _tiles((kt_per_tile, MLA_KV_RANK, TQ), 0, 0, n_tiles),
                   _whole((n_mt, D_MODEL)), _whole((MLA_HEADS, n_mt, QK_W)), _whole((n_mt, QK_W)),
                   _whole((n_mt, MLA_KV_RANK)), _whole((n_mt, MLA_ROPE))],
        out_shape=[jax.ShapeDtypeStruct((n_main, D_MODEL), F32),
                   jax.ShapeDtypeStruct((MLA_HEADS, n_main, QK_W), BF16),
                   jax.ShapeDtypeStruct((n_main, QK_W), BF16),
                   jax.ShapeDtypeStruct((n_main, MLA_KV_RANK), F32),
                   jax.ShapeDtypeStruct((n_main, MLA_ROPE), F32),
                   jax.ShapeDtypeStruct((n_main // TQ, MLA_KV_RANK, TQ), BF16),
                   jax.ShapeDtypeStruct((n_mt, D_MODEL), F32),
                   jax.ShapeDtypeStruct((MLA_HEADS, n_mt, QK_W), BF16),
                   jax.ShapeDtypeStruct((n_mt, QK_W), BF16),
                   jax.ShapeDtypeStruct((n_mt, MLA_KV_RANK), F32),
                   jax.ShapeDtypeStruct((n_mt, MLA_ROPE), F32)],
        scratch_shapes=_ffn_scratch(),
        compiler_params=_params("arbitrary"),
        name="ffn_mla_proj",
    )(x, cos_t, sin_t, x_meta, cos_m, sin_m, *ffn_w, *proj_w)


def _attn_prompt_kernel(q_ref, kf_ref, km_ref, vtf_ref, vtm_ref, o_ref,
                        s_scr, sm_scr, m_scr, l_scr, acc_scr):
    i = pl.program_id(1)
    cols = MLA_HEADS * TQ
    q = q_ref[...].reshape(cols, QK_W)

    def consume(parts):
        m_cur = None
        for s, _ in parts:
            mx = jnp.max(s, axis=0, keepdims=True)
            m_cur = mx if m_cur is None else jnp.maximum(m_cur, mx)
        m_old = m_scr[...]
        m_new = jnp.maximum(m_old, m_cur)
        alpha = jnp.exp2(m_old - m_new)
        p_sum = None
        pv = None
        for s, vt in parts:
            p = jnp.exp2(s - m_new)
            ps = jnp.sum(p, axis=0, keepdims=True)
            term = _dot(vt, p.astype(BF16))
            p_sum = ps if p_sum is None else p_sum + ps
            pv = term if pv is None else pv + term
        l_scr[...] = alpha * l_scr[...] + p_sum
        acc_scr[...] = alpha * acc_scr[...] + pv
        m_scr[...] = m_new

    m_scr[...] = jnp.full(m_scr.shape, -jnp.inf, F32)
    l_scr[...] = jnp.zeros(l_scr.shape, F32)
    acc_scr[...] = jnp.zeros(acc_scr.shape, F32)
    s_first = _dot_nt(jnp.concatenate([kf_ref[0], km_ref[...]], axis=0), q)
    s_scr[0] = s_first[:TQ]
    sm_scr[...] = s_first[TQ:]

    def pair(jj, carry):
        j = 2 * jj
        s_scr[1] = _dot_nt(kf_ref[j + 1], q)
        consume([(s_scr[0], vtf_ref[j])])
        s_scr[0] = _dot_nt(kf_ref[j + 2], q)
        consume([(s_scr[1], vtf_ref[j + 1])])
        return carry

    lax.fori_loop(0, lax.shift_right_logical(i, 1), pair, 0)

    @pl.when((i & 1) == 1)
    def _():
        s_scr[1] = _dot_nt(kf_ref[i], q)
        consume([(s_scr[0], vtf_ref[i - 1])])

    key = lax.broadcasted_iota(jnp.int32, (TQ, cols), 0)
    qry = lax.broadcasted_iota(jnp.int32, (TQ, cols), 1) % TQ
    visible = (key // CHUNK) <= (qry // CHUNK)
    s_own = jnp.where(visible, s_scr[i & 1], -jnp.inf)
    consume([(s_own, vtf_ref[i]), (sm_scr[...], vtm_ref[0])])

    for h in range(MLA_HEADS):
        cs = slice(h * TQ, (h + 1) * TQ)
        o_ref[h] = (acc_scr[:, cs] / l_scr[:, cs]).T.astype(BF16)


def _attn_prompt(q, k_tiles, kc_meta, vt_tiles, vt_meta, *, n_batch, seq):
    steps = seq // TQ
    cols = MLA_HEADS * TQ
    return pl.pallas_call(
        _attn_prompt_kernel,
        grid=(n_batch, steps),
        in_specs=[pl.BlockSpec((MLA_HEADS, TQ, QK_W), lambda b, i: (0, b * steps + i, 0)),
                  pl.BlockSpec((steps, TQ, QK_W), lambda b, i: (b, 0, 0)),
                  pl.BlockSpec((N_META, QK_W), lambda b, i: (b, 0)),
                  pl.BlockSpec((steps, MLA_KV_RANK, TQ), lambda b, i: (b, 0, 0)),
                  pl.BlockSpec((1, MLA_KV_RANK, N_META), lambda b, i: (b, 0, 0))],
        out_specs=pl.BlockSpec((MLA_HEADS, TQ, MLA_KV_RANK), lambda b, i: (0, b * steps + i, 0)),
        out_shape=jax.ShapeDtypeStruct((MLA_HEADS, n_batch * seq, MLA_KV_RANK), BF16),
        scratch_shapes=[pltpu.VMEM((2, TQ, cols), F32), pltpu.VMEM((N_META, cols), F32),
                        pltpu.VMEM((1, cols), F32),
                        pltpu.VMEM((1, cols), F32), pltpu.VMEM((MLA_KV_RANK, cols), F32)],
        compiler_params=_params("parallel", "arbitrary"),
        name="mla_attn_prompt",
    )(q, k_tiles, kc_meta, vt_tiles, vt_meta)


def _attn_full_kernel(*refs, n_q, has_past):
    if has_past:
        q_ref, kn_ref, pl_ref, pr_ref, o_ref = refs
    else:
        q_ref, kn_ref, o_ref = refs
    rows = MLA_HEADS * n_q
    q = q_ref[...].reshape(rows, QK_W)
    kn = kn_ref[...]
    s_n = _dot_nt(q, kn)
    m = jnp.max(s_n, axis=1, keepdims=True)
    if has_past:
        lat = pl_ref[0].astype(BF16)
        rp = pr_ref[0].astype(BF16)
        s_p = (_dot_nt(q[:, :MLA_KV_RANK], lat)
               + _dot_nt(q[:, MLA_KV_RANK:MLA_KV_RANK + MLA_ROPE], rp))
        m = jnp.maximum(m, jnp.max(s_p, axis=1, keepdims=True))
    p_n = jnp.exp2(s_n - m)
    l = jnp.sum(p_n, axis=1, keepdims=True)
    acc = _dot(p_n.astype(BF16), kn[:, :MLA_KV_RANK])
    if has_past:
        p_p = jnp.exp2(s_p - m)
        l = l + jnp.sum(p_p, axis=1, keepdims=True)
        acc = acc + _dot(p_p.astype(BF16), lat)
    o_ref[...] = (acc / l).astype(BF16).reshape(MLA_HEADS, n_q, MLA_KV_RANK)


def _attn_full(q, kc, past_lat, past_rope, *, row0, n_batch, n_q, name):
    base = row0 // n_q
    has_past = past_lat is not None
    ins = [q, kc]
    specs = [pl.BlockSpec((MLA_HEADS, n_q, QK_W), lambda b: (0, base + b, 0)),
             pl.BlockSpec((n_q, QK_W), lambda b: (base + b, 0))]
    if has_past:
        past = past_lat.shape[1]
        ins += [past_lat, past_rope]
        specs += [pl.BlockSpec((1, past, MLA_KV_RANK), lambda b: (b, 0, 0)),
                  pl.BlockSpec((1, past, MLA_ROPE), lambda b: (b, 0, 0))]
    return pl.pallas_call(
        functools.partial(_attn_full_kernel, n_q=n_q, has_past=has_past),
        grid=(n_batch,),
        in_specs=specs,
        out_specs=pl.BlockSpec((MLA_HEADS, n_q, MLA_KV_RANK), lambda b: (0, b, 0)),
        out_shape=jax.ShapeDtypeStruct((MLA_HEADS, n_batch * n_q, MLA_KV_RANK), BF16),
        compiler_params=_params("parallel"),
        name=name,
    )(*ins)


def _mla_out_ffn_final_kernel(*refs, firsts, n_tiles):
    ns = len(firsts)
    o_refs = refs[:ns]
    h_ref, om_ref, hm_ref, wuv_ref, wo_ref = refs[ns:ns + 5]
    ffn_w = refs[ns + 5:ns + 9]
    fw_ref = refs[ns + 9]
    y_refs = refs[ns + 10:2 * ns + 10]
    ym_ref, h_scr = refs[2 * ns + 10:]
    i = pl.program_id(0)

    def body(o_lat, h):
        y = _mla_out_compute(o_lat, h, wuv_ref, wo_ref)
        return _rms(_ffn_compute(y, *ffn_w, h_scr), fw_ref[...])

    @pl.when(i < n_tiles)
    def _():
        y = body(_select(i, o_refs, firsts), h_ref[...])
        bounds = list(firsts[1:]) + [n_tiles]
        for y_ref, lo, hi in zip(y_refs, firsts, bounds):
            @pl.when((i >= lo) & (i < hi))
            def _(y_ref=y_ref):
                y_ref[...] = y

    @pl.when(i == n_tiles)
    def _():
        ym_ref[...] = body(om_ref[...], hm_ref[...])


def _mla_out_ffn_final(o_segs, h, o_meta, h_meta, w_uv_t, w_out, ffn_w, final_w):
    n_tiles = sum(s[2] for s in o_segs)
    n_mt = h_meta.shape[0]
    firsts = tuple(s[1] for s in o_segs)
    in_specs = [_tiles((MLA_HEADS, TM, MLA_KV_RANK), 1, f, c) for _, f, c in o_segs]
    in_specs += [_tiles((TM, D_MODEL), 0, 0, n_tiles), _whole(o_meta.shape), _whole(h_meta.shape),
                 _resident(w_uv_t.shape), _resident(w_out.shape)]
    in_specs += _ffn_specs() + [_resident((1, D_MODEL))]
    return pl.pallas_call(
        functools.partial(_mla_out_ffn_final_kernel, firsts=firsts, n_tiles=n_tiles),
        grid=(n_tiles + 1,),
        in_specs=in_specs,
        out_specs=[_tiles((TM, D_MODEL), 0, f, c) for _, f, c in o_segs] + [_whole((n_mt, D_MODEL))],
        out_shape=[jax.ShapeDtypeStruct((c * TM, D_MODEL), F32) for _, _, c in o_segs]
        + [jax.ShapeDtypeStruct((n_mt, D_MODEL), F32)],
        scratch_shapes=_ffn_scratch(),
        compiler_params=_params("arbitrary"),
        name="mla_out_ffn_final",
    )(*[s[0] for s in o_segs], h, o_meta, h_meta, w_uv_t, w_out, *ffn_w, final_w.reshape(1, D_MODEL))


def _rope_tables(pos):
    half = MLA_ROPE // 2
    inv = ROPE_THETA ** (-jnp.arange(half, dtype=F32) / half)
    ang = pos[:, None] * inv[None, :]
    cos = jnp.cos(ang)
    sin = jnp.sin(ang)
    reps = LANES // MLA_ROPE
    return (jnp.tile(cos, (1, 2 * reps)), jnp.tile(jnp.concatenate([-sin, sin], axis=1), (1, reps)))


def kernel(x_prompt, x_sample, state_gla, cache_mla_latent, cache_mla_rope, meta_tokens, ffn1_norm, ffn1_w_gate, ffn1_w_up, ffn1_w_down, mix_norm, gla_w_in, gla_w_gate_up, gla_b_gate, gla_head_norm, gla_w_out, mla_w_down, mla_q_norm, mla_w_uq, mla_kv_norm, mla_w_uk, mla_w_uv, mla_w_out, ffn2_norm, ffn2_w_gate, ffn2_w_up, ffn2_w_down, final_norm):
    bp, seq, _ = x_prompt.shape
    bs, ls, _ = x_sample.shape
    past = cache_mla_latent.shape[2]
    n_fr = bp * seq
    n_sm = bs * ls
    n_mt = bp * N_META
    assert ffn1_norm.shape[0] == 2
    assert n_fr % TM == 0 and n_sm % TM == 0 and TM % TQ == 0 and seq % TQ == 0 and TQ % CHUNK == 0
    assert past % CHUNK == 0 and ls <= CHUNK and N_META <= CHUNK
    fr_tiles = n_fr // TM
    sm_tiles = n_sm // TM

    def ffn_weights(norm, wg, wu, wd, i):
        return (norm[i].reshape(1, D_MODEL), wg[i].astype(BF16), wu[i].astype(BF16), wd[i].astype(BF16))

    x_meta = jnp.tile(meta_tokens.astype(F32), (bp, 1))
    w_in = gla_w_in[0]
    main_w = 2 * GLA_KEY + 2 * GLA_VAL
    gla_proj_w = (mix_norm[0].reshape(1, D_MODEL), w_in[:, :main_w].astype(BF16),
                  jnp.pad(w_in[:, main_w:], ((0, 0), (0, LANES - GLA_RANK))).astype(BF16),
                  jnp.pad(gla_w_gate_up[0], ((0, LANES - GLA_RANK), (0, 0))).astype(BF16),
                  gla_b_gate[0].reshape(1, GLA_KEY))
    (h, q, k, v, r, gl, h_m, q_m, k_m, v_m, r_m, gl_m) = _ffn_gla_proj(
        [(x_prompt.reshape(n_fr, D_MODEL), 0, fr_tiles), (x_sample.reshape(n_sm, D_MODEL), fr_tiles, sm_tiles)],
        x_meta, ffn_weights(ffn1_norm, ffn1_w_gate, ffn1_w_up, ffn1_w_down, 0), gla_proj_w)
    o_m, s_meta = _gla_scan(q_m, k_m, v_m, gl_m, None, row0=0, n_batch=bp, n_steps=1,
                            chunk=N_META, n_chunks=1, name="gla_scan_meta")
    o_f, s_p = _gla_scan(q, k, v, gl, s_meta, row0=0, n_batch=bp, n_steps=seq // TQ,
                         chunk=CHUNK, n_chunks=TQ // CHUNK, name="gla_scan_frames")
    o_s, s_s = _gla_scan(q, k, v, gl, state_gla[0], row0=n_fr, n_batch=bs, n_steps=1,
                         chunk=ls, n_chunks=1, name="gla_scan_sample")
    h, h_m = _gla_out_ffn([(o_f, 0, fr_tiles), (o_s, fr_tiles, sm_tiles)], r, h, o_m, r_m, h_m,
                          gla_head_norm[0], gla_w_out[0].astype(BF16),
                          ffn_weights(ffn2_norm, ffn2_w_gate, ffn2_w_up, ffn2_w_down, 0))

    pos = jnp.concatenate([jnp.tile(N_META + jnp.arange(seq, dtype=F32), bp),
                           jnp.tile(past + jnp.arange(ls, dtype=F32), bs)])
    cos_t, sin_t = _rope_tables(pos)
    cos_m, sin_m = _rope_tables(jnp.tile(jnp.arange(N_META, dtype=F32), bp))
    w_down = mla_w_down[0]
    w_down = jnp.concatenate([w_down, w_down[:, -MLA_ROPE:]], axis=1).astype(BF16)
    w_uq = mla_w_uq[0].reshape(MLA_Q_RANK, MLA_HEADS, MLA_NOPE + MLA_ROPE)
    w_qn = w_uq[:, :, :MLA_NOPE].reshape(MLA_Q_RANK, MLA_HEADS * MLA_NOPE).astype(BF16)
    w_qr = w_uq[:, :, MLA_NOPE:].reshape(MLA_Q_RANK, MLA_HEADS * MLA_ROPE).astype(BF16)
    w_uk_t = jnp.transpose(mla_w_uk[0], (1, 2, 0)).astype(BF16)
    w_uv_t = jnp.transpose(mla_w_uv[0], (1, 0, 2)).astype(BF16)
    mla_proj_w = (mix_norm[1].reshape(1, D_MODEL), w_down, mla_q_norm[0].reshape(1, MLA_Q_RANK), w_qn, w_qr,
                  mla_kv_norm[0].reshape(1, MLA_KV_RANK), w_uk_t)
    (h, qa, kc, ckv, kr, vt, h_m, qa_m, kc_m, ckv_m, kr_m) = _ffn_mla_proj(
        h, h_m, cos_t, sin_t, cos_m, sin_m,
        ffn_weights(ffn1_norm, ffn1_w_gate, ffn1_w_up, ffn1_w_down, 1), mla_proj_w)
    k_tiles = kc.reshape((n_fr + n_sm) // TQ, TQ, QK_W)
    vt_meta = jnp.swapaxes(kc_m[:, :MLA_KV_RANK].reshape(bp, N_META, MLA_KV_RANK), 1, 2)
    ol_f = _attn_prompt(qa, k_tiles, kc_m, vt, vt_meta, n_batch=bp, seq=seq)
    ol_s = _attn_full(qa, kc, cache_mla_latent[0], cache_mla_rope[0], row0=n_fr, n_batch=bs, n_q=ls,
                      name="mla_attn_sample")
    ol_m = _attn_full(qa_m, kc_m, None, None, row0=0, n_batch=bp, n_q=N_META, name="mla_attn_meta")
    y_prompt, y_sample, _ = _mla_out_ffn_final(
        [(ol_f, 0, fr_tiles), (ol_s, fr_tiles, sm_tiles)], h, ol_m, h_m, w_uv_t, mla_w_out[0].astype(BF16),
        ffn_weights(ffn2_norm, ffn2_w_gate, ffn2_w_up, ffn2_w_down, 1), final_norm)

    def seq_major(t_main, t_meta, width):
        return jnp.concatenate([t_meta.reshape(bp, N_META, width), t_main[:n_fr].reshape(bp, seq, width)],
                               axis=1)

    return (y_prompt.reshape(bp, seq, D_MODEL), y_sample.reshape(bs, ls, D_MODEL),
            s_p[None], s_s[None],
            seq_major(ckv, ckv_m, MLA_KV_RANK)[None], seq_major(kr, kr_m, MLA_ROPE)[None],
            ckv[n_fr:].reshape(bs, ls, MLA_KV_RANK)[None], kr[n_fr:].reshape(bs, ls, MLA_ROPE)[None])
```

```python
import functools

import jax
import jax.numpy as jnp
from jax import lax
from jax.experimental import pallas as pl
from jax.experimental.pallas import tpu as pltpu

F32 = jnp.float32
BF16 = jnp.bfloat16

D_MODEL = 1024
D_FF = 2816
RMS_EPS = 1e-6
N_META = 16
CHUNK = 64

GLA_HEADS = 4
GLA_DK = 128
GLA_DV = 256
GLA_KEY = GLA_HEADS * GLA_DK
GLA_VAL = GLA_HEADS * GLA_DV
GLA_RANK = 16
GLA_GATE_NORMALIZER = 16.0

MLA_HEADS = 8
MLA_Q_RANK = 384
MLA_KV_RANK = 256
MLA_NOPE = 128
MLA_ROPE = 64
MLA_V = 128
ROPE_THETA = 10000.0
QK_W = MLA_KV_RANK + 128
LOG2_E = 1.4426950408889634

LANES = 128
VMEM_LIMIT = 56 * 1024 * 1024

TM = 512
FF_CHUNK = 256
TQ = 256

NT_DIMS = (((1,), (1,)), ((), ()))
TN_DIMS = (((0,), (0,)), ((), ()))


def _dot(a, b):
    return jnp.dot(a, b, preferred_element_type=F32)


def _dot_nt(a, b):
    return lax.dot_general(a, b, NT_DIMS, preferred_element_type=F32)


def _dot_tn(a, b):
    return lax.dot_general(a, b, TN_DIMS, preferred_element_type=F32)


def _rms(x, w):
    return x * lax.rsqrt(jnp.mean(x * x, axis=-1, keepdims=True) + RMS_EPS) * w


def _silu(x):
    return x * jax.nn.sigmoid(x)


def _params(*sem):
    return pltpu.CompilerParams(dimension_semantics=sem, vmem_limit_bytes=VMEM_LIMIT)


def _resident(shape):
    zeros = (0,) * len(shape)
    return pl.BlockSpec(shape, lambda *_: zeros, pipeline_mode=pl.Buffered(1))


def _whole(shape):
    zeros = (0,) * len(shape)
    return pl.BlockSpec(shape, lambda *_: zeros)


def _tiles(block, row_axis=0, first=0, count=None):
    nd = len(block)

    def imap(i):
        t = i - first
        if count is not None:
            t = jnp.clip(t, 0, count - 1)
        idx = [0] * nd
        idx[row_axis] = t
        return tuple(idx)

    return pl.BlockSpec(block, imap)


def _select(i, refs, firsts):
    val = refs[0][...]
    for ref, first in zip(refs[1:], firsts[1:]):
        val = jnp.where(i >= first, ref[...], val)
    return val


def _ffn_compute(x, nw_ref, wg_ref, wu_ref, wd_ref, h_scr):
    m = x.shape[0]
    xn = _rms(x, nw_ref[...]).astype(BF16)
    for c in range(D_FF // FF_CHUNK):
        sl = slice(c * FF_CHUNK, (c + 1) * FF_CHUNK)
        g = _dot(xn, wg_ref[:, sl])
        u = _dot(xn, wu_ref[:, sl])
        h_scr[:m, sl] = (_silu(g) * u).astype(BF16)
    return x + 0.5 * _dot(h_scr[:m, :], wd_ref[...])


def _gla_proj_compute(x, nw_ref, w_ref, wgu_ref, bg_ref):
    c_v = 2 * GLA_KEY
    c_r = c_v + GLA_VAL
    c_g = c_r + GLA_VAL
    xn = _rms(x, nw_ref[...]).astype(BF16)
    q = _dot(xn, w_ref[:, 0:GLA_KEY]).astype(BF16)
    k = _dot(xn, w_ref[:, GLA_KEY:c_v]).astype(BF16)
    v = _dot(xn, w_ref[:, c_v:c_r]).astype(BF16)
    r = _dot(xn, w_ref[:, c_r:c_g]).astype(BF16)
    gd = _dot(xn, w_ref[:, c_g:])
    z = _dot(gd.astype(BF16), wgu_ref[...]) + bg_ref[...]
    log_sig = jnp.minimum(z, 0.0) - jnp.log1p(jnp.exp(-jnp.abs(z)))
    return q, k, v, r, log_sig * (1.0 / GLA_GATE_NORMALIZER)


def _gla_out_compute(o, r, h, hn_ref, wo_ref):
    parts = []
    for hd in range(GLA_HEADS):
        vs = slice(hd * GLA_DV, (hd + 1) * GLA_DV)
        on = _rms(o[:, vs].astype(F32), hn_ref[...])
        parts.append((on * _silu(r[:, vs].astype(F32))).astype(BF16))
    return h + _dot(jnp.concatenate(parts, axis=1), wo_ref[...])


def _mla_proj_compute(x, cos_t, sin_t, nw_ref, wd_ref, qn_ref, wqn_ref, wqr_ref, kvn_ref, wuk_ref,
                      q_out, kc_out, ckv_out, kr_out, vt_out):
    m = x.shape[0]
    scale = (MLA_NOPE + MLA_ROPE) ** -0.5 * LOG2_E
    xn = _rms(x, nw_ref[...]).astype(BF16)
    xd = _dot(xn, wd_ref[...])
    cqn = _rms(xd[:, :MLA_Q_RANK], qn_ref[...]).astype(BF16)
    ckv = _rms(xd[:, MLA_Q_RANK:MLA_Q_RANK + MLA_KV_RANK], kvn_ref[...])
    lane = lax.broadcasted_iota(jnp.int32, (m, LANES), 1)
    first_half = (lane & (MLA_ROPE // 2)) == 0

    def rope(t):
        swapped = jnp.where(first_half, pltpu.roll(t, LANES - MLA_ROPE // 2, 1),
                            pltpu.roll(t, MLA_ROPE // 2, 1))
        return t * cos_t + swapped * sin_t

    kr2 = rope(xd[:, MLA_Q_RANK + MLA_KV_RANK:])
    ckv_out[...] = ckv
    kr_out[...] = kr2[:, :MLA_ROPE]
    kc_out[:, :MLA_KV_RANK] = ckv.astype(BF16)
    kc_out[:, MLA_KV_RANK:] = jnp.where(lane < MLA_ROPE, kr2, 0.0).astype(BF16)
    if vt_out is not None:
        for j in range(m // TQ):
            vt_out[j] = ckv[j * TQ:(j + 1) * TQ, :].T.astype(BF16)

    qn = _dot(cqn, wqn_ref[...])
    qr = _dot(cqn, wqr_ref[...])
    for c in range(MLA_HEADS // 2):
        rr = rope(qr[:, c * LANES:(c + 1) * LANES]) * scale
        for e in range(2):
            hd = 2 * c + e
            ql = _dot(qn[:, hd * MLA_NOPE:(hd + 1) * MLA_NOPE].astype(BF16), wuk_ref[hd]) * scale
            q_out[hd, :, :MLA_KV_RANK] = ql.astype(BF16)
            rot = rr if e == 0 else pltpu.roll(rr, MLA_ROPE, 1)
            q_out[hd, :, MLA_KV_RANK:] = rot.astype(BF16)


def _mla_out_compute(o_lat, h, wuv_ref, wo_ref):
    parts = [_dot(o_lat[hd], wuv_ref[hd]).astype(BF16) for hd in range(MLA_HEADS)]
    return h + _dot(jnp.concatenate(parts, axis=1), wo_ref[...])


def _layer(shape, layer):
    zeros = (0,) * len(shape)
    return pl.BlockSpec((None,) + tuple(shape), lambda *_: (layer,) + zeros, pipeline_mode=pl.Buffered(1))


def _layer_specs(params):
    return [_layer(a.shape[1:], layer) for a, layer in params]


def _arrays(params):
    return [a for a, _ in params]


def _ffn_scratch():
    return [pltpu.VMEM((TM, D_FF), BF16)]


def _ffn_gla_proj_kernel(*refs, x_firsts, n_tiles):
    ns = len(x_firsts)
    x_refs, xm_ref = refs[:ns], refs[ns]
    ffn_w = refs[ns + 1:ns + 5]
    proj_w = refs[ns + 5:ns + 9]
    outs = refs[ns + 9:ns + 15]
    outs_m = refs[ns + 15:ns + 21]
    h_scr = refs[ns + 21]
    i = pl.program_id(0)

    def body(x, o):
        y = _ffn_compute(x, *ffn_w, h_scr)
        o[0][...] = y
        for ref, val in zip(o[1:], _gla_proj_compute(y, *proj_w)):
            ref[...] = val

    @pl.when(i < n_tiles)
    def _():
        body(_select(i, x_refs, x_firsts), outs)

    @pl.when(i == n_tiles)
    def _():
        body(xm_ref[...], outs_m)


def _ffn_gla_proj(x_segs, x_meta, ffn, proj):
    n_tiles = sum(s[2] for s in x_segs)
    n_main = n_tiles * TM
    n_mt = x_meta.shape[0]
    widths = [(D_MODEL, F32), (GLA_KEY, BF16), (GLA_KEY, BF16), (GLA_VAL, BF16), (GLA_VAL, BF16),
              (GLA_KEY, F32)]
    in_specs = [_tiles((TM, D_MODEL), 0, f, c) for _, f, c in x_segs]
    in_specs += [_whole((n_mt, D_MODEL))] + _layer_specs(ffn) + _layer_specs(proj)
    return pl.pallas_call(
        functools.partial(_ffn_gla_proj_kernel, x_firsts=tuple(s[1] for s in x_segs), n_tiles=n_tiles),
        grid=(n_tiles + 1,),
        in_specs=in_specs,
        out_specs=[_tiles((TM, w), 0, 0, n_tiles) for w, _ in widths]
        + [_whole((n_mt, w)) for w, _ in widths],
        out_shape=[jax.ShapeDtypeStruct((n_main, w), d) for w, d in widths]
        + [jax.ShapeDtypeStruct((n_mt, w), d) for w, d in widths],
        scratch_shapes=_ffn_scratch(),
        compiler_params=_params("arbitrary"),
        name="ffn_gla_proj",
    )(*[s[0] for s in x_segs], x_meta, *_arrays(ffn), *_arrays(proj))


def _gla_scan_kernel(*refs, chunk, n_chunks, n_steps, has_s0):
    q_ref, k_ref, v_ref, gl_ref = refs[:4]
    s0_ref = refs[4] if has_s0 else None
    o_ref, s_out_ref, st_scr = refs[4 + has_s0:]
    step = pl.program_id(1)

    @pl.when(step == 0)
    def _():
        for h in range(GLA_HEADS):
            if has_s0:
                st_scr[h] = s0_ref[0, h].T
            else:
                st_scr[h] = jnp.zeros((GLA_DV, GLA_DK), F32)

    row = lax.broadcasted_iota(jnp.int32, (chunk, chunk), 0)
    col = lax.broadcasted_iota(jnp.int32, (chunk, chunk), 1)
    causal = row >= col
    tri = jnp.where(causal, 1.0, 0.0).astype(BF16)
    scale = GLA_DK ** -0.5
    state = [st_scr[h] for h in range(GLA_HEADS)]

    for c in range(n_chunks):
        rows = slice(c * chunk, (c + 1) * chunk)
        glog = gl_ref[rows, :]
        hi = glog.astype(BF16)
        lo = (glog - hi.astype(F32)).astype(BF16)
        gc = _dot(tri, hi) + _dot(tri, lo)
        glast = gc[chunk - 1:chunk, :]
        q = q_ref[rows, :].astype(F32) * scale
        k = k_ref[rows, :].astype(F32)
        qg = (q * jnp.exp(gc)).astype(BF16)
        kg = (k * jnp.exp(-gc)).astype(BF16)
        kd = (k * jnp.exp(glast - gc)).astype(BF16)
        decay = jnp.exp(glast)
        for h in range(GLA_HEADS):
            ks = slice(h * GLA_DK, (h + 1) * GLA_DK)
            vs = slice(h * GLA_DV, (h + 1) * GLA_DV)
            a = jnp.where(causal, _dot_nt(qg[:, ks], kg[:, ks]), 0.0).astype(BF16)
            vh = v_ref[rows, vs]
            o_ref[rows, vs] = (_dot(a, vh) + _dot_nt(qg[:, ks], state[h].astype(BF16))).astype(BF16)
            state[h] = state[h] * decay[:, ks] + _dot_tn(vh, kd[:, ks])

    for h in range(GLA_HEADS):
        st_scr[h] = state[h]

    @pl.when(step == n_steps - 1)
    def _():
        for h in range(GLA_HEADS):
            s_out_ref[0, h] = st_scr[h].T


def _gla_scan(q, k, v, gl, s0, *, row0, n_batch, n_steps, chunk, n_chunks, name):
    rb = chunk * n_chunks
    base = row0 // rb

    def rmap(b, s):
        return (base + b * n_steps + s, 0)

    def omap(b, s):
        return (b * n_steps + s, 0)

    def smap(b, s):
        return (b, 0, 0, 0)

    ins = [q, k, v, gl]
    specs = [pl.BlockSpec((rb, GLA_KEY), rmap), pl.BlockSpec((rb, GLA_KEY), rmap),
             pl.BlockSpec((rb, GLA_VAL), rmap), pl.BlockSpec((rb, GLA_KEY), rmap)]
    state_block = (1, GLA_HEADS, GLA_DK, GLA_DV)
    if s0 is not None:
        ins.append(s0)
        specs.append(pl.BlockSpec(state_block, smap))
    kern = functools.partial(_gla_scan_kernel, chunk=chunk, n_chunks=n_chunks, n_steps=n_steps,
                             has_s0=s0 is not None)
    return pl.pallas_call(
        kern,
        grid=(n_batch, n_steps),
        in_specs=specs,
        out_specs=[pl.BlockSpec((rb, GLA_VAL), omap), pl.BlockSpec(state_block, smap)],
        out_shape=[jax.ShapeDtypeStruct((n_batch * n_steps * rb, GLA_VAL), BF16),
                   jax.ShapeDtypeStruct((n_batch,) + state_block[1:], F32)],
        scratch_shapes=[pltpu.VMEM((GLA_HEADS, GLA_DV, GLA_DK), F32)],
        compiler_params=_params("parallel", "arbitrary"),
        name=name,
    )(*ins)


def _gla_out_ffn_kernel(*refs, o_firsts, n_tiles):
    ns = len(o_firsts)
    o_refs = refs[:ns]
    r_ref, h_ref, om_ref, rm_ref, hm_ref, hn_ref, wo_ref = refs[ns:ns + 7]
    ffn_w = refs[ns + 7:ns + 11]
    y_ref, ym_ref, h_scr = refs[ns + 11:]
    i = pl.program_id(0)

    def body(o, r, h):
        return _ffn_compute(_gla_out_compute(o, r, h, hn_ref, wo_ref), *ffn_w, h_scr)

    @pl.when(i < n_tiles)
    def _():
        y_ref[...] = body(_select(i, o_refs, o_firsts), r_ref[...], h_ref[...])

    @pl.when(i == n_tiles)
    def _():
        ym_ref[...] = body(om_ref[...], rm_ref[...], hm_ref[...])


def _gla_out_ffn(o_segs, r, h, o_meta, r_meta, h_meta, out, ffn):
    n_tiles = sum(s[2] for s in o_segs)
    n_mt = h_meta.shape[0]
    in_specs = [_tiles((TM, GLA_VAL), 0, f, c) for _, f, c in o_segs]
    in_specs += [_tiles((TM, GLA_VAL), 0, 0, n_tiles), _tiles((TM, D_MODEL), 0, 0, n_tiles),
                 _whole((n_mt, GLA_VAL)), _whole((n_mt, GLA_VAL)), _whole((n_mt, D_MODEL))]
    in_specs += _layer_specs(out) + _layer_specs(ffn)
    return pl.pallas_call(
        functools.partial(_gla_out_ffn_kernel, o_firsts=tuple(s[1] for s in o_segs), n_tiles=n_tiles),
        grid=(n_tiles + 1,),
        in_specs=in_specs,
        out_specs=[_tiles((TM, D_MODEL), 0, 0, n_tiles), _whole((n_mt, D_MODEL))],
        out_shape=[jax.ShapeDtypeStruct(h.shape, F32), jax.ShapeDtypeStruct(h_meta.shape, F32)],
        scratch_shapes=_ffn_scratch(),
        compiler_params=_params("arbitrary"),
        name="gla_out_ffn",
    )(*[s[0] for s in o_segs], r, h, o_meta, r_meta, h_meta, *_arrays(out), *_arrays(ffn))


def _ffn_mla_proj_kernel(*refs, n_tiles):
    x_ref, cos_ref, sin_ref, xm_ref, cosm_ref, sinm_ref = refs[:6]
    ffn_w = refs[6:10]
    proj_w = refs[10:17]
    y_ref, q_out, kc_out, ckv_out, kr_out, vt_out = refs[17:23]
    ym_ref, qm_out, kcm_out, ckvm_out, krm_out = refs[23:28]
    h_scr = refs[28]
    i = pl.program_id(0)

    @pl.when(i < n_tiles)
    def _():
        y = _ffn_compute(x_ref[...], *ffn_w, h_scr)
        y_ref[...] = y
        _mla_proj_compute(y, cos_ref[...], sin_ref[...], *proj_w, q_out, kc_out, ckv_out, kr_out, vt_out)

    @pl.when(i == n_tiles)
    def _():
        y = _ffn_compute(xm_ref[...], *ffn_w, h_scr)
        ym_ref[...] = y
        _mla_proj_compute(y, cosm_ref[...], sinm_ref[...], *proj_w, qm_out, kcm_out, ckvm_out, krm_out,
                          None)


def _ffn_mla_proj(x, x_meta, cos_t, sin_t, table_tile, cos_m, sin_m, ffn, proj_w):
    n_main = x.shape[0]
    n_tiles = n_main // TM
    n_mt = x_meta.shape[0]
    kt_per_tile = TM // TQ

    def rows(w):
        return _tiles((TM, w), 0, 0, n_tiles)

    table = pl.BlockSpec((TM, LANES), lambda i: (table_tile(jnp.minimum(i, n_tiles - 1)), 0))
    in_specs = [rows(D_MODEL), table, table, _whole((n_mt, D_MODEL)), _whole((n_mt, LANES)),
                _whole((n_mt, LANES))] + _layer_specs(ffn)
    in_specs += [_resident(w.shape) for w in proj_w]
    return pl.pallas_call(
        functools.partial(_ffn_mla_proj_kernel, n_tiles=n_tiles),
        grid=(n_tiles + 1,),
        in_specs=in_specs,
        out_specs=[rows(D_MODEL), _tiles((MLA_HEADS, TM, QK_W), 1, 0, n_tiles), rows(QK_W),
                   rows(MLA_KV_RANK), rows(MLA_ROPE),
                   _tiles((kt_per_tile, MLA_KV_RANK, TQ), 0, 0, n_tiles),
                   _whole((n_mt, D_MODEL)), _whole((MLA_HEADS, n_mt, QK_W)), _whole((n_mt, QK_W)),
                   _whole((n_mt, MLA_KV_RANK)), _whole((n_mt, MLA_ROPE))],
        out_shape=[jax.ShapeDtypeStruct((n_main, D_MODEL), F32),
                   jax.ShapeDtypeStruct((MLA_HEADS, n_main, QK_W), BF16),
                   jax.ShapeDtypeStruct((n_main, QK_W), BF16),
                   jax.ShapeDtypeStruct((n_main, MLA_KV_RANK), F32),
                   jax.ShapeDtypeStruct((n_main, MLA_ROPE), F32),
                   jax.ShapeDtypeStruct((n_main // TQ, MLA_KV_RANK, TQ), BF16),
                   jax.ShapeDtypeStruct((n_mt, D_MODEL), F32),
                   jax.ShapeDtypeStruct((MLA_HEADS, n_mt, QK_W), BF16),
                   jax.ShapeDtypeStruct((n_mt, QK_W), BF16),
                   jax.ShapeDtypeStruct((n_mt, MLA_KV_RANK), F32),
                   jax.ShapeDtypeStruct((n_mt, MLA_ROPE), F32)],
        scratch_shapes=_ffn_scratch(),
        compiler_params=_params("arbitrary"),
        name="ffn_mla_proj",
    )(x, cos_t, sin_t, x_meta, cos_m, sin_m, *_arrays(ffn), *proj_w)


def _attn_prompt_kernel(q_ref, kf_ref, km_ref, vtf_ref, vtm_ref, o_ref,
                        s_scr, sm_scr, m_scr, l_scr, acc_scr):
    i = pl.program_id(1)
    cols = MLA_HEADS * TQ
    q = q_ref[...].reshape(cols, QK_W)

    def consume(parts):
        m_cur = None
        for s, _ in parts:
            mx = jnp.max(s, axis=0, keepdims=True)
            m_cur = mx if m_cur is None else jnp.maximum(m_cur, mx)
        m_old = m_scr[...]
        m_new = jnp.maximum(m_old, m_cur)
        alpha = jnp.exp2(m_old - m_new)
        p_sum = None
        pv = None
        for s, vt in parts:
            p = jnp.exp2(s - m_new)
            ps = jnp.sum(p, axis=0, keepdims=True)
            term = _dot(vt, p.astype(BF16))
            p_sum = ps if p_sum is None else p_sum + ps
            pv = term if pv is None else pv + term
        l_scr[...] = alpha * l_scr[...] + p_sum
        acc_scr[...] = alpha * acc_scr[...] + pv
        m_scr[...] = m_new

    m_scr[...] = jnp.full(m_scr.shape, -jnp.inf, F32)
    l_scr[...] = jnp.zeros(l_scr.shape, F32)
    acc_scr[...] = jnp.zeros(acc_scr.shape, F32)
    s_first = _dot_nt(jnp.concatenate([kf_ref[0], km_ref[...]], axis=0), q)
    s_scr[0] = s_first[:TQ]
    sm_scr[...] = s_first[TQ:]

    def pair(jj, carry):
        j = 2 * jj
        s_scr[1] = _dot_nt(kf_ref[j + 1], q)
        consume([(s_scr[0], vtf_ref[j])])
        s_scr[0] = _dot_nt(kf_ref[j + 2], q)
        consume([(s_scr[1], vtf_ref[j + 1])])
        return carry

    lax.fori_loop(0, lax.shift_right_logical(i, 1), pair, 0)

    @pl.when((i & 1) == 1)
    def _():
        s_scr[1] = _dot_nt(kf_ref[i], q)
        consume([(s_scr[0], vtf_ref[i - 1])])

    key = lax.broadcasted_iota(jnp.int32, (TQ, cols), 0)
    qry = lax.broadcasted_iota(jnp.int32, (TQ, cols), 1) % TQ
    visible = (key // CHUNK) <= (qry // CHUNK)
    s_own = jnp.where(visible, s_scr[i & 1], -jnp.inf)
    consume([(s_own, vtf_ref[i]), (sm_scr[...], vtm_ref[0])])

    for h in range(MLA_HEADS):
        cs = slice(h * TQ, (h + 1) * TQ)
        o_ref[h] = (acc_scr[:, cs] / l_scr[:, cs]).T.astype(BF16)


def _attn_prompt(q, k_tiles, kc_meta, vt_tiles, vt_meta, *, n_batch, seq):
    steps = seq // TQ
    cols = MLA_HEADS * TQ
    return pl.pallas_call(
        _attn_prompt_kernel,
        grid=(n_batch, steps),
        in_specs=[pl.BlockSpec((MLA_HEADS, TQ, QK_W), lambda b, i: (0, b * steps + i, 0)),
                  pl.BlockSpec((steps, TQ, QK_W), lambda b, i: (b, 0, 0)),
                  pl.BlockSpec((N_META, QK_W), lambda b, i: (b, 0)),
                  pl.BlockSpec((steps, MLA_KV_RANK, TQ), lambda b, i: (b, 0, 0)),
                  pl.BlockSpec((1, MLA_KV_RANK, N_META), lambda b, i: (b, 0, 0))],
        out_specs=pl.BlockSpec((MLA_HEADS, TQ, MLA_KV_RANK), lambda b, i: (0, b * steps + i, 0)),
        out_shape=jax.ShapeDtypeStruct((MLA_HEADS, n_batch * seq, MLA_KV_RANK), BF16),
        scratch_shapes=[pltpu.VMEM((2, TQ, cols), F32), pltpu.VMEM((N_META, cols), F32),
                        pltpu.VMEM((1, cols), F32),
                        pltpu.VMEM((1, cols), F32), pltpu.VMEM((MLA_KV_RANK, cols), F32)],
        compiler_params=_params("parallel", "arbitrary"),
        name="mla_attn_prompt",
    )(q, k_tiles, kc_meta, vt_tiles, vt_meta)


def _attn_full_kernel(*refs, n_q, has_past):
    if has_past:
        q_ref, kn_ref, pl_ref, pr_ref, o_ref = refs
    else:
        q_ref, kn_ref, o_ref = refs
    rows = MLA_HEADS * n_q
    q = q_ref[...].reshape(rows, QK_W)
    kn = kn_ref[...]
    s_n = _dot_nt(q, kn)
    m = jnp.max(s_n, axis=1, keepdims=True)
    if has_past:
        lat = pl_ref[0].astype(BF16)
        rp_t = pr_ref[0].astype(BF16)
        s_p = (_dot_nt(q[:, :MLA_KV_RANK], lat)
               + _dot(q[:, MLA_KV_RANK:MLA_KV_RANK + MLA_ROPE], rp_t))
        m = jnp.maximum(m, jnp.max(s_p, axis=1, keepdims=True))
    p_n = jnp.exp2(s_n - m)
    l = jnp.sum(p_n, axis=1, keepdims=True)
    acc = _dot(p_n.astype(BF16), kn[:, :MLA_KV_RANK])
    if has_past:
        p_p = jnp.exp2(s_p - m)
        l = l + jnp.sum(p_p, axis=1, keepdims=True)
        acc = acc + _dot(p_p.astype(BF16), lat)
    o_ref[...] = (acc / l).astype(BF16).reshape(MLA_HEADS, n_q, MLA_KV_RANK)


def _attn_full(q, kc, past_lat, past_rope_t, *, row0, n_batch, n_q, name):
    base = row0 // n_q
    has_past = past_lat is not None
    ins = [q, kc]
    specs = [pl.BlockSpec((MLA_HEADS, n_q, QK_W), lambda b: (0, base + b, 0)),
             pl.BlockSpec((n_q, QK_W), lambda b: (base + b, 0))]
    if has_past:
        past = past_lat.shape[1]
        ins += [past_lat, past_rope_t]
        specs += [pl.BlockSpec((1, past, MLA_KV_RANK), lambda b: (b, 0, 0)),
                  pl.BlockSpec((1, MLA_ROPE, past), lambda b: (b, 0, 0))]
    return pl.pallas_call(
        functools.partial(_attn_full_kernel, n_q=n_q, has_past=has_past),
        grid=(n_batch,),
        in_specs=specs,
        out_specs=pl.BlockSpec((MLA_HEADS, n_q, MLA_KV_RANK), lambda b: (0, b, 0)),
        out_shape=jax.ShapeDtypeStruct((MLA_HEADS, n_batch * n_q, MLA_KV_RANK), BF16),
        compiler_params=_params("parallel"),
        name=name,
    )(*ins)


def _mla_out_ffn_final_kernel(*refs, firsts, n_tiles):
    ns = len(firsts)
    o_refs = refs[:ns]
    h_ref, om_ref, hm_ref, wuv_ref, wo_ref = refs[ns:ns + 5]
    ffn_w = refs[ns + 5:ns + 9]
    fw_ref = refs[ns + 9]
    y_refs = refs[ns + 10:2 * ns + 10]
    ym_ref, h_scr = refs[2 * ns + 10:]
    i = pl.program_id(0)

    def body(o_lat, h):
        y = _mla_out_compute(o_lat, h, wuv_ref, wo_ref)
        return _rms(_ffn_compute(y, *ffn_w, h_scr), fw_ref[...])

    @pl.when(i < n_tiles)
    def _():
        y = body(_select(i, o_refs, firsts), h_ref[...])
        bounds = list(firsts[1:]) + [n_tiles]
        for y_ref, lo, hi in zip(y_refs, firsts, bounds):
            @pl.when((i >= lo) & (i < hi))
            def _(y_ref=y_ref):
                y_ref[...] = y

    @pl.when(i == n_tiles)
    def _():
        ym_ref[...] = body(om_ref[...], hm_ref[...])


def _mla_out_ffn_final(o_segs, h, o_meta, h_meta, w_uv_t, w_out, ffn, final_w):
    n_tiles = sum(s[2] for s in o_segs)
    n_mt = h_meta.shape[0]
    firsts = tuple(s[1] for s in o_segs)
    in_specs = [_tiles((MLA_HEADS, TM, MLA_KV_RANK), 1, f, c) for _, f, c in o_segs]
    in_specs += [_tiles((TM, D_MODEL), 0, 0, n_tiles), _whole(o_meta.shape), _whole(h_meta.shape),
                 _resident(w_uv_t.shape), _resident(w_out.shape)]
    in_specs += _layer_specs(ffn) + [_resident((1, D_MODEL))]
    return pl.pallas_call(
        functools.partial(_mla_out_ffn_final_kernel, firsts=firsts, n_tiles=n_tiles),
        grid=(n_tiles + 1,),
        in_specs=in_specs,
        out_specs=[_tiles((TM, D_MODEL), 0, f, c) for _, f, c in o_segs] + [_whole((n_mt, D_MODEL))],
        out_shape=[jax.ShapeDtypeStruct((c * TM, D_MODEL), F32) for _, _, c in o_segs]
        + [jax.ShapeDtypeStruct((n_mt, D_MODEL), F32)],
        scratch_shapes=_ffn_scratch(),
        compiler_params=_params("arbitrary"),
        name="mla_out_ffn_final",
    )(*[s[0] for s in o_segs], h, o_meta, h_meta, w_uv_t, w_out, *_arrays(ffn), final_w.reshape(1, D_MODEL))


def _rope_tables(pos):
    half = MLA_ROPE // 2
    lane = jnp.arange(LANES)
    inv = ROPE_THETA ** (-(lane % half).astype(F32) / half)
    sign = jnp.where((lane // half) % 2 == 0, -1.0, 1.0).astype(F32)
    ang = pos[:, None] * inv[None, :]
    return jnp.cos(ang), jnp.sin(ang) * sign[None, :]


def kernel(x_prompt, x_sample, state_gla, cache_mla_latent, cache_mla_rope, meta_tokens, ffn1_norm, ffn1_w_gate, ffn1_w_up, ffn1_w_down, mix_norm, gla_w_in, gla_w_gate_up, gla_b_gate, gla_head_norm, gla_w_out, mla_w_down, mla_q_norm, mla_w_uq, mla_kv_norm, mla_w_uk, mla_w_uv, mla_w_out, ffn2_norm, ffn2_w_gate, ffn2_w_up, ffn2_w_down, final_norm):
    bp, seq, _ = x_prompt.shape
    bs, ls, _ = x_sample.shape
    past = cache_mla_latent.shape[2]
    n_fr = bp * seq
    n_sm = bs * ls
    n_mt = bp * N_META
    assert ffn1_norm.shape[0] == 2
    assert n_fr % TM == 0 and n_sm % TM == 0 and TM % TQ == 0 and seq % TQ == 0 and TQ % CHUNK == 0
    assert past % CHUNK == 0 and ls <= CHUNK and N_META <= CHUNK
    fr_tiles = n_fr // TM
    sm_tiles = n_sm // TM

    def vec(p):
        return p.reshape(p.shape[0], 1, p.shape[1])

    ffn1 = (vec(ffn1_norm), ffn1_w_gate.astype(BF16), ffn1_w_up.astype(BF16), ffn1_w_down.astype(BF16))
    ffn2 = (vec(ffn2_norm), ffn2_w_gate.astype(BF16), ffn2_w_up.astype(BF16), ffn2_w_down.astype(BF16))

    def at(params, layer):
        return [(p, layer) for p in params]

    x_meta = jnp.tile(meta_tokens.astype(F32), (bp, 1))
    gla_proj = [(vec(mix_norm), 0)] + at((gla_w_in.astype(BF16), gla_w_gate_up.astype(BF16),
                                          vec(gla_b_gate)), 0)
    (h, q, k, v, r, gl, h_m, q_m, k_m, v_m, r_m, gl_m) = _ffn_gla_proj(
        [(x_prompt.reshape(n_fr, D_MODEL), 0, fr_tiles), (x_sample.reshape(n_sm, D_MODEL), fr_tiles, sm_tiles)],
        x_meta, at(ffn1, 0), gla_proj)
    o_m, s_meta = _gla_scan(q_m, k_m, v_m, gl_m, None, row0=0, n_batch=bp, n_steps=1,
                            chunk=N_META, n_chunks=1, name="gla_scan_meta")
    o_f, s_p = _gla_scan(q, k, v, gl, s_meta, row0=0, n_batch=bp, n_steps=seq // TQ,
                         chunk=CHUNK, n_chunks=TQ // CHUNK, name="gla_scan_frames")
    o_s, s_s = _gla_scan(q, k, v, gl, state_gla.reshape(state_gla.shape[1:]), row0=n_fr, n_batch=bs,
                         n_steps=1, chunk=ls, n_chunks=1, name="gla_scan_sample")
    h, h_m = _gla_out_ffn([(o_f, 0, fr_tiles), (o_s, fr_tiles, sm_tiles)], r, h, o_m, r_m, h_m,
                          at((vec(gla_head_norm), gla_w_out.astype(BF16)), 0), at(ffn2, 0))

    seq_blocks = seq // TM
    cos_t, sin_t = _rope_tables(jnp.concatenate([N_META + jnp.arange(seq, dtype=F32),
                                                 jnp.tile(past + jnp.arange(ls, dtype=F32), TM // ls)]))
    cos_m, sin_m = _rope_tables(jnp.tile(jnp.arange(N_META, dtype=F32), bp))

    def table_tile(i):
        return jnp.where(i < fr_tiles, i % seq_blocks, seq_blocks)
    w_down = mla_w_down[0]
    w_down = jnp.concatenate([w_down, w_down[:, -MLA_ROPE:]], axis=1).astype(BF16)
    w_uq = mla_w_uq[0].reshape(MLA_Q_RANK, MLA_HEADS, MLA_NOPE + MLA_ROPE)
    w_qn = w_uq[:, :, :MLA_NOPE].reshape(MLA_Q_RANK, MLA_HEADS * MLA_NOPE).astype(BF16)
    w_qr = w_uq[:, :, MLA_NOPE:].reshape(MLA_Q_RANK, MLA_HEADS * MLA_ROPE).astype(BF16)
    w_uk_t = jnp.transpose(mla_w_uk[0], (1, 2, 0)).astype(BF16)
    w_uv_t = jnp.transpose(mla_w_uv[0], (1, 0, 2)).astype(BF16)
    mla_proj_w = (mix_norm[1].reshape(1, D_MODEL), w_down, mla_q_norm[0].reshape(1, MLA_Q_RANK), w_qn, w_qr,
                  mla_kv_norm[0].reshape(1, MLA_KV_RANK), w_uk_t)
    (h, qa, kc, ckv, kr, vt, h_m, qa_m, kc_m, ckv_m, kr_m) = _ffn_mla_proj(
        h, h_m, cos_t, sin_t, table_tile, cos_m, sin_m, at(ffn1, 1), mla_proj_w)
    k_tiles = kc.reshape((n_fr + n_sm) // TQ, TQ, QK_W)
    vt_meta = jnp.swapaxes(kc_m[:, :MLA_KV_RANK].reshape(bp, N_META, MLA_KV_RANK), 1, 2)
    ol_f = _attn_prompt(qa, k_tiles, kc_m, vt, vt_meta, n_batch=bp, seq=seq)
    past_rope_t = jnp.swapaxes(cache_mla_rope.reshape(bs, past, MLA_ROPE), 1, 2)
    ol_s = _attn_full(qa, kc, cache_mla_latent.reshape(bs, past, MLA_KV_RANK), past_rope_t,
                      row0=n_fr, n_batch=bs, n_q=ls, name="mla_attn_sample")
    ol_m = _attn_full(qa_m, kc_m, None, None, row0=0, n_batch=bp, n_q=N_META, name="mla_attn_meta")
    y_prompt, y_sample, _ = _mla_out_ffn_final(
        [(ol_f, 0, fr_tiles), (ol_s, fr_tiles, sm_tiles)], h, ol_m, h_m, w_uv_t, mla_w_out[0].astype(BF16),
        at(ffn2, 1), final_norm)

    def seq_major(t_main, t_meta, width):
        return jnp.concatenate([t_meta.reshape(bp, N_META, width), t_main[:n_fr].reshape(bp, seq, width)],
                               axis=1)

    return (y_prompt.reshape(bp, seq, D_MODEL), y_sample.reshape(bs, ls, D_MODEL),
            s_p[None], s_s[None],
            seq_major(ckv, ckv_m, MLA_KV_RANK)[None], seq_major(kr, kr_m, MLA_ROPE)[None],
            ckv[n_fr:].reshape(bs, ls, MLA_KV_RANK)[None], kr[n_fr:].reshape(bs, ls, MLA_ROPE)[None])
```

```python
import functools

import jax
import jax.numpy as jnp
from jax import lax
from jax.experimental import pallas as pl
from jax.experimental.pallas import tpu as pltpu

F32 = jnp.float32
BF16 = jnp.bfloat16

D_MODEL = 1024
D_FF = 2816
RMS_EPS = 1e-6
N_META = 16
CHUNK = 64

GLA_HEADS = 4
GLA_DK = 128
GLA_DV = 256
GLA_KEY = GLA_HEADS * GLA_DK
GLA_VAL = GLA_HEADS * GLA_DV
GLA_RANK = 16
GLA_GATE_NORMALIZER = 16.0

MLA_HEADS = 8
MLA_Q_RANK = 384
MLA_KV_RANK = 256
MLA_NOPE = 128
MLA_ROPE = 64
MLA_V = 128
ROPE_THETA = 10000.0
QK_W = MLA_KV_RANK + 128
LOG2_E = 1.4426950408889634

LANES = 128
VMEM_LIMIT = 56 * 1024 * 1024

TM = 512
FF_CHUNK = 256
TQ = 256

NT_DIMS = (((1,), (1,)), ((), ()))
TN_DIMS = (((0,), (0,)), ((), ()))


def _dot(a, b):
    return jnp.dot(a, b, preferred_element_type=F32)


def _dot_nt(a, b):
    return lax.dot_general(a, b, NT_DIMS, preferred_element_type=F32)


def _dot_tn(a, b):
    return lax.dot_general(a, b, TN_DIMS, preferred_element_type=F32)


def _rms(x, w):
    return x * lax.rsqrt(jnp.mean(x * x, axis=-1, keepdims=True) + RMS_EPS) * w


def _silu(x):
    return x * jax.nn.sigmoid(x)


def _params(*sem):
    return pltpu.CompilerParams(dimension_semantics=sem, vmem_limit_bytes=VMEM_LIMIT)


def _resident(shape):
    zeros = (0,) * len(shape)
    return pl.BlockSpec(shape, lambda *_: zeros, pipeline_mode=pl.Buffered(1))


def _whole(shape):
    zeros = (0,) * len(shape)
    return pl.BlockSpec(shape, lambda *_: zeros)


def _tiles(block, row_axis=0, first=0, count=None):
    nd = len(block)

    def imap(i):
        t = i - first
        if count is not None:
            t = jnp.clip(t, 0, count - 1)
        idx = [0] * nd
        idx[row_axis] = t
        return tuple(idx)

    return pl.BlockSpec(block, imap)


def _select(i, refs, firsts):
    val = refs[0][...]
    for ref, first in zip(refs[1:], firsts[1:]):
        val = jnp.where(i >= first, ref[...], val)
    return val


def _ffn_compute(x, nw_ref, wg_ref, wu_ref, wd_ref, h_scr):
    m = x.shape[0]
    xn = _rms(x, nw_ref[...]).astype(BF16)
    for c in range(D_FF // FF_CHUNK):
        sl = slice(c * FF_CHUNK, (c + 1) * FF_CHUNK)
        g = _dot(xn, wg_ref[:, sl])
        u = _dot(xn, wu_ref[:, sl])
        h_scr[:m, sl] = (_silu(g) * u).astype(BF16)
    return x + 0.5 * _dot(h_scr[:m, :], wd_ref[...])


def _gla_proj_compute(x, nw_ref, w_ref, wgu_ref, bg_ref):
    c_v = 2 * GLA_KEY
    c_r = c_v + GLA_VAL
    c_g = c_r + GLA_VAL
    xn = _rms(x, nw_ref[...]).astype(BF16)
    q = _dot(xn, w_ref[:, 0:GLA_KEY]).astype(BF16)
    k = _dot(xn, w_ref[:, GLA_KEY:c_v]).astype(BF16)
    v = _dot(xn, w_ref[:, c_v:c_r]).astype(BF16)
    r = _dot(xn, w_ref[:, c_r:c_g]).astype(BF16)
    gd = _dot(xn, w_ref[:, c_g:])
    z = _dot(gd.astype(BF16), wgu_ref[...]) + bg_ref[...]
    log_sig = jnp.minimum(z, 0.0) - jnp.log1p(jnp.exp(-jnp.abs(z)))
    return q, k, v, r, log_sig * (1.0 / GLA_GATE_NORMALIZER)


def _gla_out_compute(o, r, h, hn_ref, wo_ref):
    parts = []
    for hd in range(GLA_HEADS):
        vs = slice(hd * GLA_DV, (hd + 1) * GLA_DV)
        on = _rms(o[:, vs].astype(F32), hn_ref[...])
        parts.append((on * _silu(r[:, vs].astype(F32))).astype(BF16))
    return h + _dot(jnp.concatenate(parts, axis=1), wo_ref[...])


def _mla_proj_compute(x, cos_t, sin_t, nw_ref, wd_ref, qn_ref, wqn_ref, wqr_ref, kvn_ref, wuk_ref,
                      q_out, kc_out, ckv_out, kr_out, vt_out):
    m = x.shape[0]
    scale = (MLA_NOPE + MLA_ROPE) ** -0.5 * LOG2_E
    xn = _rms(x, nw_ref[...]).astype(BF16)
    xd = _dot(xn, wd_ref[...])
    cqn = _rms(xd[:, :MLA_Q_RANK], qn_ref[...]).astype(BF16)
    ckv = _rms(xd[:, MLA_Q_RANK:MLA_Q_RANK + MLA_KV_RANK], kvn_ref[...])
    lane = lax.broadcasted_iota(jnp.int32, (m, LANES), 1)
    first_half = (lane & (MLA_ROPE // 2)) == 0

    def rope(t):
        swapped = jnp.where(first_half, pltpu.roll(t, LANES - MLA_ROPE // 2, 1),
                            pltpu.roll(t, MLA_ROPE // 2, 1))
        return t * cos_t + swapped * sin_t

    kr2 = rope(xd[:, MLA_Q_RANK + MLA_KV_RANK:])
    ckv_out[...] = ckv
    kr_out[...] = kr2[:, :MLA_ROPE]
    kc_out[:, :MLA_KV_RANK] = ckv.astype(BF16)
    kc_out[:, MLA_KV_RANK:] = jnp.where(lane < MLA_ROPE, kr2, 0.0).astype(BF16)
    if vt_out is not None:
        for j in range(m // TQ):
            vt_out[j] = ckv[j * TQ:(j + 1) * TQ, :].T.astype(BF16)

    qn = _dot(cqn, wqn_ref[...])
    qr = _dot(cqn, wqr_ref[...])
    for c in range(MLA_HEADS // 2):
        rr = rope(qr[:, c * LANES:(c + 1) * LANES]) * scale
        for e in range(2):
            hd = 2 * c + e
            ql = _dot(qn[:, hd * MLA_NOPE:(hd + 1) * MLA_NOPE].astype(BF16), wuk_ref[hd]) * scale
            q_out[hd, :, :MLA_KV_RANK] = ql.astype(BF16)
            rot = rr if e == 0 else pltpu.roll(rr, MLA_ROPE, 1)
            q_out[hd, :, MLA_KV_RANK:] = rot.astype(BF16)


def _mla_out_compute(o_lat, h, wuv_ref, wo_ref):
    parts = [_dot(o_lat[hd], wuv_ref[hd]).astype(BF16) for hd in range(MLA_HEADS)]
    return h + _dot(jnp.concatenate(parts, axis=1), wo_ref[...])


def _layer(shape, layer):
    zeros = (0,) * len(shape)
    return pl.BlockSpec((None,) + tuple(shape), lambda *_: (layer,) + zeros, pipeline_mode=pl.Buffered(1))


def _layer_specs(params):
    return [_layer(a.shape[1:], layer) for a, layer in params]


def _arrays(params):
    return [a for a, _ in params]


def _ffn_scratch():
    return [pltpu.VMEM((TM, D_FF), BF16)]


def _ffn_gla_proj_kernel(*refs, x_firsts, n_tiles):
    ns = len(x_firsts)
    x_refs, xm_ref = refs[:ns], refs[ns]
    ffn_w = refs[ns + 1:ns + 5]
    proj_w = refs[ns + 5:ns + 9]
    outs = refs[ns + 9:ns + 15]
    outs_m = refs[ns + 15:ns + 21]
    h_scr = refs[ns + 21]
    i = pl.program_id(0)

    def body(x, o):
        y = _ffn_compute(x, *ffn_w, h_scr)
        o[0][...] = y
        for ref, val in zip(o[1:], _gla_proj_compute(y, *proj_w)):
            ref[...] = val

    @pl.when(i < n_tiles)
    def _():
        body(_select(i, x_refs, x_firsts), outs)

    @pl.when(i == n_tiles)
    def _():
        body(xm_ref[...], outs_m)


def _ffn_gla_proj(x_segs, x_meta, ffn, proj):
    n_tiles = sum(s[2] for s in x_segs)
    n_main = n_tiles * TM
    n_mt = x_meta.shape[0]
    widths = [(D_MODEL, F32), (GLA_KEY, BF16), (GLA_KEY, BF16), (GLA_VAL, BF16), (GLA_VAL, BF16),
              (GLA_KEY, F32)]
    in_specs = [_tiles((TM, D_MODEL), 0, f, c) for _, f, c in x_segs]
    in_specs += [_whole((n_mt, D_MODEL))] + _layer_specs(ffn) + _layer_specs(proj)
    return pl.pallas_call(
        functools.partial(_ffn_gla_proj_kernel, x_firsts=tuple(s[1] for s in x_segs), n_tiles=n_tiles),
        grid=(n_tiles + 1,),
        in_specs=in_specs,
        out_specs=[_tiles((TM, w), 0, 0, n_tiles) for w, _ in widths]
        + [_whole((n_mt, w)) for w, _ in widths],
        out_shape=[jax.ShapeDtypeStruct((n_main, w), d) for w, d in widths]
        + [jax.ShapeDtypeStruct((n_mt, w), d) for w, d in widths],
        scratch_shapes=_ffn_scratch(),
        compiler_params=_params("arbitrary"),
        name="ffn_gla_proj",
    )(*[s[0] for s in x_segs], x_meta, *_arrays(ffn), *_arrays(proj))


def _gla_scan_kernel(*refs, chunk, n_chunks, streams, n_steps, has_s0):
    q_ref, k_ref, v_ref, gl_ref = refs[:4]
    s0_ref = refs[4] if has_s0 else None
    o_ref, s_out_ref, st_scr = refs[4 + has_s0:]
    step = pl.program_id(1)

    @pl.when(step == 0)
    def _():
        for s in range(streams):
            for h in range(GLA_HEADS):
                if has_s0:
                    st_scr[s * GLA_HEADS + h] = s0_ref[s, h]
                else:
                    st_scr[s * GLA_HEADS + h] = jnp.zeros((GLA_DK, GLA_DV), F32)

    n_rows = chunk * n_chunks * streams
    row = lax.broadcasted_iota(jnp.int32, (n_rows, n_rows), 0)
    col = lax.broadcasted_iota(jnp.int32, (n_rows, n_rows), 1)
    causal = ((row // chunk) == (col // chunk)) & (row >= col)
    tri = jnp.where(causal, 1.0, 0.0).astype(BF16)
    scale = GLA_DK ** -0.5
    chunks = [slice(c * chunk, (c + 1) * chunk) for c in range(n_chunks * streams)]

    glog = gl_ref[...]
    hi = glog.astype(BF16)
    lo = (glog - hi.astype(F32)).astype(BF16)
    gc = _dot(tri, hi) + _dot(tri, lo)
    glast = [gc[rs.stop - 1:rs.stop, :] for rs in chunks]
    g_end = jnp.concatenate([jnp.broadcast_to(g, (chunk, GLA_KEY)) for g in glast], axis=0)
    q = q_ref[...].astype(F32) * scale
    k = k_ref[...].astype(F32)
    qg = (q * jnp.exp(gc)).astype(BF16)
    kg = (k * jnp.exp(-gc)).astype(BF16)
    kd = (k * jnp.exp(g_end - gc)).astype(BF16)
    sub = lax.broadcasted_iota(jnp.int32, (GLA_DK, GLA_KEY), 0)
    dec_rows = jnp.zeros((GLA_DK, GLA_KEY), F32)
    for c, g in enumerate(glast):
        dec_rows = jnp.where(sub == c, jnp.exp(g), dec_rows)

    key_slices = [slice(h * GLA_DK, (h + 1) * GLA_DK) for h in range(GLA_HEADS)]
    val_slices = [slice(h * GLA_DV, (h + 1) * GLA_DV) for h in range(GLA_HEADS)]
    values = [v_ref[:, vs] for vs in val_slices]
    scores = [_dot_nt(qg[:, ks], kg[:, ks]) for ks in key_slices]
    updates = [[_dot_tn(kd[rs, ks], vh[rs, :]) for rs in chunks] for ks, vh in zip(key_slices, values)]
    intra = [_dot(jnp.where(causal, a, 0.0).astype(BF16), vh) for a, vh in zip(scores, values)]
    for h, ks in enumerate(key_slices):
        dec_cols = dec_rows[:, ks].T
        states = []
        for s in range(streams):
            state = st_scr[s * GLA_HEADS + h]
            for c in range(s * n_chunks, (s + 1) * n_chunks):
                states.append(state.astype(BF16))
                state = state * dec_cols[:, c:c + 1] + updates[h][c]
            st_scr[s * GLA_HEADS + h] = state
        for c, rs in enumerate(chunks):
            o_ref[rs, val_slices[h]] = (intra[h][rs, :] + _dot(qg[rs, ks], states[c])).astype(BF16)

    @pl.when(step == n_steps - 1)
    def _():
        for s in range(streams):
            for h in range(GLA_HEADS):
                s_out_ref[s, h] = st_scr[s * GLA_HEADS + h]


def _gla_scan(q, k, v, gl, s0, *, row0, n_batch, n_steps, chunk, n_chunks, name, streams=1):
    assert n_batch % streams == 0 and (streams == 1 or n_steps == 1)
    rb = chunk * n_chunks * streams
    base = row0 // rb

    def rmap(b, s):
        return (base + b * n_steps + s, 0)

    def omap(b, s):
        return (b * n_steps + s, 0)

    def smap(b, s):
        return (b, 0, 0, 0)

    ins = [q, k, v, gl]
    specs = [pl.BlockSpec((rb, GLA_KEY), rmap), pl.BlockSpec((rb, GLA_KEY), rmap),
             pl.BlockSpec((rb, GLA_VAL), rmap), pl.BlockSpec((rb, GLA_KEY), rmap)]
    state_block = (streams, GLA_HEADS, GLA_DK, GLA_DV)
    if s0 is not None:
        ins.append(s0)
        specs.append(pl.BlockSpec(state_block, smap))
    kern = functools.partial(_gla_scan_kernel, chunk=chunk, n_chunks=n_chunks, streams=streams,
                             n_steps=n_steps, has_s0=s0 is not None)
    return pl.pallas_call(
        kern,
        grid=(n_batch // streams, n_steps),
        in_specs=specs,
        out_specs=[pl.BlockSpec((rb, GLA_VAL), omap), pl.BlockSpec(state_block, smap)],
        out_shape=[jax.ShapeDtypeStruct((n_batch * n_steps * chunk * n_chunks, GLA_VAL), BF16),
                   jax.ShapeDtypeStruct((n_batch,) + state_block[1:], F32)],
        scratch_shapes=[pltpu.VMEM((streams * GLA_HEADS, GLA_DK, GLA_DV), F32)],
        compiler_params=_params("parallel", "arbitrary"),
        name=name,
    )(*ins)


def _gla_out_ffn_kernel(*refs, o_firsts, n_tiles):
    ns = len(o_firsts)
    o_refs = refs[:ns]
    r_ref, h_ref, om_ref, rm_ref, hm_ref, hn_ref, wo_ref = refs[ns:ns + 7]
    ffn_w = refs[ns + 7:ns + 11]
    y_ref, ym_ref, h_scr = refs[ns + 11:]
    i = pl.program_id(0)

    def body(o, r, h):
        return _ffn_compute(_gla_out_compute(o, r, h, hn_ref, wo_ref), *ffn_w, h_scr)

    @pl.when(i < n_tiles)
    def _():
        y_ref[...] = body(_select(i, o_refs, o_firsts), r_ref[...], h_ref[...])

    @pl.when(i == n_tiles)
    def _():
        ym_ref[...] = body(om_ref[...], rm_ref[...], hm_ref[...])


def _gla_out_ffn(o_segs, r, h, o_meta, r_meta, h_meta, out, ffn):
    n_tiles = sum(s[2] for s in o_segs)
    n_mt = h_meta.shape[0]
    in_specs = [_tiles((TM, GLA_VAL), 0, f, c) for _, f, c in o_segs]
    in_specs += [_tiles((TM, GLA_VAL), 0, 0, n_tiles), _tiles((TM, D_MODEL), 0, 0, n_tiles),
                 _whole((n_mt, GLA_VAL)), _whole((n_mt, GLA_VAL)), _whole((n_mt, D_MODEL))]
    in_specs += _layer_specs(out) + _layer_specs(ffn)
    return pl.pallas_call(
        functools.partial(_gla_out_ffn_kernel, o_firsts=tuple(s[1] for s in o_segs), n_tiles=n_tiles),
        grid=(n_tiles + 1,),
        in_specs=in_specs,
        out_specs=[_tiles((TM, D_MODEL), 0, 0, n_tiles), _whole((n_mt, D_MODEL))],
        out_shape=[jax.ShapeDtypeStruct(h.shape, F32), jax.ShapeDtypeStruct(h_meta.shape, F32)],
        scratch_shapes=_ffn_scratch(),
        compiler_params=_params("arbitrary"),
        name="gla_out_ffn",
    )(*[s[0] for s in o_segs], r, h, o_meta, r_meta, h_meta, *_arrays(out), *_arrays(ffn))


def _ffn_mla_proj_kernel(*refs, n_tiles):
    x_ref, cos_ref, sin_ref, xm_ref, cosm_ref, sinm_ref = refs[:6]
    ffn_w = refs[6:10]
    proj_w = refs[10:17]
    y_ref, q_out, kc_out, ckv_out, kr_out, vt_out = refs[17:23]
    ym_ref, qm_out, kcm_out, ckvm_out, krm_out = refs[23:28]
    h_scr = refs[28]
    i = pl.program_id(0)

    @pl.when(i < n_tiles)
    def _():
        y = _ffn_compute(x_ref[...], *ffn_w, h_scr)
        y_ref[...] = y
        _mla_proj_compute(y, cos_ref[...], sin_ref[...], *proj_w, q_out, kc_out, ckv_out, kr_out, vt_out)

    @pl.when(i == n_tiles)
    def _():
        y = _ffn_compute(xm_ref[...], *ffn_w, h_scr)
        ym_ref[...] = y
        _mla_proj_compute(y, cosm_ref[...], sinm_ref[...], *proj_w, qm_out, kcm_out, ckvm_out, krm_out,
                          None)


def _ffn_mla_proj(x, x_meta, cos_t, sin_t, table_tile, cos_m, sin_m, ffn, proj_w):
    n_main = x.shape[0]
    n_tiles = n_main // TM
    n_mt = x_meta.shape[0]
    kt_per_tile = TM // TQ

    def rows(w):
        return _tiles((TM, w), 0, 0, n_tiles)

    table = pl.BlockSpec((TM, LANES), lambda i: (table_tile(jnp.minimum(i, n_tiles - 1)), 0))
    in_specs = [rows(D_MODEL), table, table, _whole((n_mt, D_MODEL)), _whole((n_mt, LANES)),
                _whole((n_mt, LANES))] + _layer_specs(ffn)
    in_specs += [_resident(w.shape) for w in proj_w]
    return pl.pallas_call(
        functools.partial(_ffn_mla_proj_kernel, n_tiles=n_tiles),
        grid=(n_tiles + 1,),
        in_specs=in_specs,
        out_specs=[rows(D_MODEL), _tiles((MLA_HEADS, TM, QK_W), 1, 0, n_tiles), rows(QK_W),
                   rows(MLA_KV_RANK), rows(MLA_ROPE),
                   _tiles((kt_per_tile, MLA_KV_RANK, TQ), 0, 0, n_tiles),
                   _whole((n_mt, D_MODEL)), _whole((MLA_HEADS, n_mt, QK_W)), _whole((n_mt, QK_W)),
                   _whole((n_mt, MLA_KV_RANK)), _whole((n_mt, MLA_ROPE))],
        out_shape=[jax.ShapeDtypeStruct((n_main, D_MODEL), F32),
                   jax.ShapeDtypeStruct((MLA_HEADS, n_main, QK_W), BF16),
                   jax.ShapeDtypeStruct((n_main, QK_W), BF16),
                   jax.ShapeDtypeStruct((n_main, MLA_KV_RANK), F32),
                   jax.ShapeDtypeStruct((n_main, MLA_ROPE), F32),
                   jax.ShapeDtypeStruct((n_main // TQ, MLA_KV_RANK, TQ), BF16),
                   jax.ShapeDtypeStruct((n_mt, D_MODEL), F32),
                   jax.ShapeDtypeStruct((MLA_HEADS, n_mt, QK_W), BF16),
                   jax.ShapeDtypeStruct((n_mt, QK_W), BF16),
                   jax.ShapeDtypeStruct((n_mt, MLA_KV_RANK), F32),
                   jax.ShapeDtypeStruct((n_mt, MLA_ROPE), F32)],
        scratch_shapes=_ffn_scratch(),
        compiler_params=_params("arbitrary"),
        name="ffn_mla_proj",
    )(x, cos_t, sin_t, x_meta, cos_m, sin_m, *_arrays(ffn), *proj_w)


def _attn_prompt_kernel(q_ref, kf_ref, km_ref, vtf_ref, vtm_ref, o_ref,
                        s_scr, sm_scr, m_scr, l_scr, acc_scr):
    i = pl.program_id(1)
    cols = MLA_HEADS * TQ
    q = q_ref[...].reshape(cols, QK_W)

    def consume(parts):
        m_cur = None
        for s, _ in parts:
            mx = jnp.max(s, axis=0, keepdims=True)
            m_cur = mx if m_cur is None else jnp.maximum(m_cur, mx)
        m_old = m_scr[...]
        m_new = jnp.maximum(m_old, m_cur)
        alpha = jnp.exp2(m_old - m_new)
        p_sum = None
        pv = None
        for s, vt in parts:
            p = jnp.exp2(s - m_new)
            ps = jnp.sum(p, axis=0, keepdims=True)
            term = _dot(vt, p.astype(BF16))
            p_sum = ps if p_sum is None else p_sum + ps
            pv = term if pv is None else pv + term
        l_scr[...] = alpha * l_scr[...] + p_sum
        acc_scr[...] = alpha * acc_scr[...] + pv
        m_scr[...] = m_new

    m_scr[...] = jnp.full(m_scr.shape, -jnp.inf, F32)
    l_scr[...] = jnp.zeros(l_scr.shape, F32)
    acc_scr[...] = jnp.zeros(acc_scr.shape, F32)
    s_first = _dot_nt(jnp.concatenate([kf_ref[0], km_ref[...]], axis=0), q)
    s_scr[0] = s_first[:TQ]
    sm_scr[...] = s_first[TQ:]

    def pair(jj, carry):
        j = 2 * jj
        s_scr[1] = _dot_nt(kf_ref[j + 1], q)
        consume([(s_scr[0], vtf_ref[j])])
        s_scr[0] = _dot_nt(kf_ref[j + 2], q)
        consume([(s_scr[1], vtf_ref[j + 1])])
        return carry

    lax.fori_loop(0, lax.shift_right_logical(i, 1), pair, 0)

    @pl.when((i & 1) == 1)
    def _():
        s_scr[1] = _dot_nt(kf_ref[i], q)
        consume([(s_scr[0], vtf_ref[i - 1])])

    key = lax.broadcasted_iota(jnp.int32, (TQ, cols), 0)
    qry = lax.broadcasted_iota(jnp.int32, (TQ, cols), 1) % TQ
    visible = (key // CHUNK) <= (qry // CHUNK)
    s_own = jnp.where(visible, s_scr[i & 1], -jnp.inf)
    consume([(s_own, vtf_ref[i]), (sm_scr[...], vtm_ref[0])])

    for h in range(MLA_HEADS):
        cs = slice(h * TQ, (h + 1) * TQ)
        o_ref[h] = (acc_scr[:, cs] / l_scr[:, cs]).T.astype(BF16)


def _attn_prompt(q, k_tiles, kc_meta, vt_tiles, vt_meta, *, n_batch, seq):
    steps = seq // TQ
    cols = MLA_HEADS * TQ
    return pl.pallas_call(
        _attn_prompt_kernel,
        grid=(n_batch, steps),
        in_specs=[pl.BlockSpec((MLA_HEADS, TQ, QK_W), lambda b, i: (0, b * steps + i, 0)),
                  pl.BlockSpec((steps, TQ, QK_W), lambda b, i: (b, 0, 0)),
                  pl.BlockSpec((N_META, QK_W), lambda b, i: (b, 0)),
                  pl.BlockSpec((steps, MLA_KV_RANK, TQ), lambda b, i: (b, 0, 0)),
                  pl.BlockSpec((1, MLA_KV_RANK, N_META), lambda b, i: (b, 0, 0))],
        out_specs=pl.BlockSpec((MLA_HEADS, TQ, MLA_KV_RANK), lambda b, i: (0, b * steps + i, 0)),
        out_shape=jax.ShapeDtypeStruct((MLA_HEADS, n_batch * seq, MLA_KV_RANK), BF16),
        scratch_shapes=[pltpu.VMEM((2, TQ, cols), F32), pltpu.VMEM((N_META, cols), F32),
                        pltpu.VMEM((1, cols), F32),
                        pltpu.VMEM((1, cols), F32), pltpu.VMEM((MLA_KV_RANK, cols), F32)],
        compiler_params=_params("parallel", "arbitrary"),
        name="mla_attn_prompt",
    )(q, k_tiles, kc_meta, vt_tiles, vt_meta)


def _attn_full_kernel(*refs, n_q, streams, has_past):
    if has_past:
        q_ref, kn_ref, pl_ref, pr_ref, o_ref = refs
    else:
        q_ref, kn_ref, o_ref = refs
    rows = MLA_HEADS * n_q
    blocks = [slice(s * n_q, (s + 1) * n_q) for s in range(streams)]
    qs = [q_ref[:, rs, :].reshape(rows, QK_W) for rs in blocks]
    kns = [kn_ref[rs, :] for rs in blocks]
    s_new = [_dot_nt(q, kn) for q, kn in zip(qs, kns)]
    if has_past:
        lats = [pl_ref[s].astype(BF16) for s in range(streams)]
        rp_ts = [pr_ref[s].astype(BF16) for s in range(streams)]
        s_past = [_dot_nt(q[:, :MLA_KV_RANK], lat) + _dot(q[:, MLA_KV_RANK:MLA_KV_RANK + MLA_ROPE], rp_t)
                  for q, lat, rp_t in zip(qs, lats, rp_ts)]
    for s, rs in enumerate(blocks):
        m = jnp.max(s_new[s], axis=1, keepdims=True)
        if has_past:
            m = jnp.maximum(m, jnp.max(s_past[s], axis=1, keepdims=True))
        p_n = jnp.exp2(s_new[s] - m)
        l = jnp.sum(p_n, axis=1, keepdims=True)
        acc = _dot(p_n.astype(BF16), kns[s][:, :MLA_KV_RANK])
        if has_past:
            p_p = jnp.exp2(s_past[s] - m)
            l = l + jnp.sum(p_p, axis=1, keepdims=True)
            acc = acc + _dot(p_p.astype(BF16), lats[s])
        o_ref[:, rs, :] = (acc / l).astype(BF16).reshape(MLA_HEADS, n_q, MLA_KV_RANK)


def _attn_full(q, kc, past_lat, past_rope_t, *, row0, n_batch, n_q, streams, name):
    assert n_batch % streams == 0
    nb = n_q * streams
    base = row0 // nb
    has_past = past_lat is not None
    ins = [q, kc]
    specs = [pl.BlockSpec((MLA_HEADS, nb, QK_W), lambda b: (0, base + b, 0)),
             pl.BlockSpec((nb, QK_W), lambda b: (base + b, 0))]
    if has_past:
        past = past_lat.shape[1]
        ins += [past_lat, past_rope_t]
        specs += [pl.BlockSpec((streams, past, MLA_KV_RANK), lambda b: (b, 0, 0)),
                  pl.BlockSpec((streams, MLA_ROPE, past), lambda b: (b, 0, 0))]
    return pl.pallas_call(
        functools.partial(_attn_full_kernel, n_q=n_q, streams=streams, has_past=has_past),
        grid=(n_batch // streams,),
        in_specs=specs,
        out_specs=pl.BlockSpec((MLA_HEADS, nb, MLA_KV_RANK), lambda b: (0, b, 0)),
        out_shape=jax.ShapeDtypeStruct((MLA_HEADS, n_batch * n_q, MLA_KV_RANK), BF16),
        compiler_params=_params("parallel"),
        name=name,
    )(*ins)


def _mla_out_ffn_final_kernel(*refs, firsts, n_tiles):
    ns = len(firsts)
    o_refs = refs[:ns]
    h_ref, om_ref, hm_ref, wuv_ref, wo_ref = refs[ns:ns + 5]
    ffn_w = refs[ns + 5:ns + 9]
    fw_ref = refs[ns + 9]
    y_refs = refs[ns + 10:2 * ns + 10]
    ym_ref, h_scr = refs[2 * ns + 10:]
    i = pl.program_id(0)

    def body(o_lat, h):
        y = _mla_out_compute(o_lat, h, wuv_ref, wo_ref)
        return _rms(_ffn_compute(y, *ffn_w, h_scr), fw_ref[...])

    @pl.when(i < n_tiles)
    def _():
        y = body(_select(i, o_refs, firsts), h_ref[...])
        bounds = list(firsts[1:]) + [n_tiles]
        for y_ref, lo, hi in zip(y_refs, firsts, bounds):
            @pl.when((i >= lo) & (i < hi))
            def _(y_ref=y_ref):
                y_ref[...] = y

    @pl.when(i == n_tiles)
    def _():
        ym_ref[...] = body(om_ref[...], hm_ref[...])


def _mla_out_ffn_final(o_segs, h, o_meta, h_meta, w_uv_t, w_out, ffn, final_w):
    n_tiles = sum(s[2] for s in o_segs)
    n_mt = h_meta.shape[0]
    firsts = tuple(s[1] for s in o_segs)
    in_specs = [_tiles((MLA_HEADS, TM, MLA_KV_RANK), 1, f, c) for _, f, c in o_segs]
    in_specs += [_tiles((TM, D_MODEL), 0, 0, n_tiles), _whole(o_meta.shape), _whole(h_meta.shape),
                 _resident(w_uv_t.shape), _resident(w_out.shape)]
    in_specs += _layer_specs(ffn) + [_resident((1, D_MODEL))]
    return pl.pallas_call(
        functools.partial(_mla_out_ffn_final_kernel, firsts=firsts, n_tiles=n_tiles),
        grid=(n_tiles + 1,),
        in_specs=in_specs,
        out_specs=[_tiles((TM, D_MODEL), 0, f, c) for _, f, c in o_segs] + [_whole((n_mt, D_MODEL))],
        out_shape=[jax.ShapeDtypeStruct((c * TM, D_MODEL), F32) for _, _, c in o_segs]
        + [jax.ShapeDtypeStruct((n_mt, D_MODEL), F32)],
        scratch_shapes=_ffn_scratch(),
        compiler_params=_params("arbitrary"),
        name="mla_out_ffn_final",
    )(*[s[0] for s in o_segs], h, o_meta, h_meta, w_uv_t, w_out, *_arrays(ffn), final_w.reshape(1, D_MODEL))


def _rope_tables(pos):
    half = MLA_ROPE // 2
    lane = jnp.arange(LANES)
    inv = ROPE_THETA ** (-(lane % half).astype(F32) / half)
    sign = jnp.where((lane // half) % 2 == 0, -1.0, 1.0).astype(F32)
    ang = pos[:, None] * inv[None, :]
    return jnp.cos(ang), jnp.sin(ang) * sign[None, :]


def kernel(x_prompt, x_sample, state_gla, cache_mla_latent, cache_mla_rope, meta_tokens, ffn1_norm, ffn1_w_gate, ffn1_w_up, ffn1_w_down, mix_norm, gla_w_in, gla_w_gate_up, gla_b_gate, gla_head_norm, gla_w_out, mla_w_down, mla_q_norm, mla_w_uq, mla_kv_norm, mla_w_uk, mla_w_uv, mla_w_out, ffn2_norm, ffn2_w_gate, ffn2_w_up, ffn2_w_down, final_norm):
    bp, seq, _ = x_prompt.shape
    bs, ls, _ = x_sample.shape
    past = cache_mla_latent.shape[2]
    n_fr = bp * seq
    n_sm = bs * ls
    n_mt = bp * N_META
    assert ffn1_norm.shape[0] == 2
    assert n_fr % TM == 0 and n_sm % TM == 0 and TM % TQ == 0 and seq % TQ == 0 and TQ % CHUNK == 0
    assert past % CHUNK == 0 and ls <= CHUNK and N_META <= CHUNK
    fr_tiles = n_fr // TM
    sm_tiles = n_sm // TM

    def vec(p):
        return p.reshape(p.shape[0], 1, p.shape[1])

    ffn1 = (vec(ffn1_norm), ffn1_w_gate.astype(BF16), ffn1_w_up.astype(BF16), ffn1_w_down.astype(BF16))
    ffn2 = (vec(ffn2_norm), ffn2_w_gate.astype(BF16), ffn2_w_up.astype(BF16), ffn2_w_down.astype(BF16))

    def at(params, layer):
        return [(p, layer) for p in params]

    x_meta = jnp.tile(meta_tokens.astype(F32), (bp, 1))
    gla_proj = [(vec(mix_norm), 0)] + at((gla_w_in.astype(BF16), gla_w_gate_up.astype(BF16),
                                          vec(gla_b_gate)), 0)
    (h, q, k, v, r, gl, h_m, q_m, k_m, v_m, r_m, gl_m) = _ffn_gla_proj(
        [(x_prompt.reshape(n_fr, D_MODEL), 0, fr_tiles), (x_sample.reshape(n_sm, D_MODEL), fr_tiles, sm_tiles)],
        x_meta, at(ffn1, 0), gla_proj)
    o_m, s_meta = _gla_scan(q_m, k_m, v_m, gl_m, None, row0=0, n_batch=bp, n_steps=1,
                            chunk=N_META, n_chunks=1, streams=bp, name="gla_scan_meta")
    o_f, s_p = _gla_scan(q, k, v, gl, s_meta, row0=0, n_batch=bp, n_steps=seq // TQ,
                         chunk=CHUNK, n_chunks=TQ // CHUNK, name="gla_scan_frames")
    o_s, s_s = _gla_scan(q, k, v, gl, state_gla.reshape(state_gla.shape[1:]), row0=n_fr, n_batch=bs,
                         n_steps=1, chunk=ls, n_chunks=1, streams=TQ // ls, name="gla_scan_sample")
    h, h_m = _gla_out_ffn([(o_f, 0, fr_tiles), (o_s, fr_tiles, sm_tiles)], r, h, o_m, r_m, h_m,
                          at((vec(gla_head_norm), gla_w_out.astype(BF16)), 0), at(ffn2, 0))

    seq_blocks = seq // TM
    cos_t, sin_t = _rope_tables(jnp.concatenate([N_META + jnp.arange(seq, dtype=F32),
                                                 jnp.tile(past + jnp.arange(ls, dtype=F32), TM // ls)]))
    cos_m, sin_m = _rope_tables(jnp.tile(jnp.arange(N_META, dtype=F32), bp))

    def table_tile(i):
        return jnp.where(i < fr_tiles, i % seq_blocks, seq_blocks)
    w_down = mla_w_down[0]
    w_down = jnp.concatenate([w_down, w_down[:, -MLA_ROPE:]], axis=1).astype(BF16)
    w_uq = mla_w_uq[0].reshape(MLA_Q_RANK, MLA_HEADS, MLA_NOPE + MLA_ROPE)
    w_qn = w_uq[:, :, :MLA_NOPE].reshape(MLA_Q_RANK, MLA_HEADS * MLA_NOPE).astype(BF16)
    w_qr = w_uq[:, :, MLA_NOPE:].reshape(MLA_Q_RANK, MLA_HEADS * MLA_ROPE).astype(BF16)
    w_uk_t = jnp.transpose(mla_w_uk[0], (1, 2, 0)).astype(BF16)
    w_uv_t = jnp.transpose(mla_w_uv[0], (1, 0, 2)).astype(BF16)
    mla_proj_w = (mix_norm[1].reshape(1, D_MODEL), w_down, mla_q_norm[0].reshape(1, MLA_Q_RANK), w_qn, w_qr,
                  mla_kv_norm[0].reshape(1, MLA_KV_RANK), w_uk_t)
    (h, qa, kc, ckv, kr, vt, h_m, qa_m, kc_m, ckv_m, kr_m) = _ffn_mla_proj(
        h, h_m, cos_t, sin_t, table_tile, cos_m, sin_m, at(ffn1, 1), mla_proj_w)
    k_tiles = kc.reshape((n_fr + n_sm) // TQ, TQ, QK_W)
    vt_meta = jnp.swapaxes(kc_m[:, :MLA_KV_RANK].reshape(bp, N_META, MLA_KV_RANK), 1, 2)
    ol_f = _attn_prompt(qa, k_tiles, kc_m, vt, vt_meta, n_batch=bp, seq=seq)
    past_rope_t = jnp.swapaxes(cache_mla_rope.reshape(bs, past, MLA_ROPE), 1, 2)
    ol_s = _attn_full(qa, kc, cache_mla_latent.reshape(bs, past, MLA_KV_RANK), past_rope_t,
                      row0=n_fr, n_batch=bs, n_q=ls, streams=2, name="mla_attn_sample")
    ol_m = _attn_full(qa_m, kc_m, None, None, row0=0, n_batch=bp, n_q=N_META, streams=bp,
                      name="mla_attn_meta")
    y_prompt, y_sample, _ = _mla_out_ffn_final(
        [(ol_f, 0, fr_tiles), (ol_s, fr_tiles, sm_tiles)], h, ol_m, h_m, w_uv_t, mla_w_out[0].astype(BF16),
        at(ffn2, 1), final_norm)

    def seq_major(t_main, t_meta, width):
        return jnp.concatenate([t_meta.reshape(bp, N_META, width), t_main[:n_fr].reshape(bp, seq, width)],
                               axis=1)

    return (y_prompt.reshape(bp, seq, D_MODEL), y_sample.reshape(bs, ls, D_MODEL),
            s_p[None], s_s[None],
            seq_major(ckv, ckv_m, MLA_KV_RANK)[None], seq_major(kr, kr_m, MLA_ROPE)[None],
            ckv[n_fr:].reshape(bs, ls, MLA_KV_RANK)[None], kr[n_fr:].reshape(bs, ls, MLA_ROPE)[None])
```

```python
import functools

import jax
import jax.numpy as jnp
from jax import lax
from jax.experimental import pallas as pl
from jax.experimental.pallas import tpu as pltpu

F32 = jnp.float32
BF16 = jnp.bfloat16

D_MODEL = 1024
D_FF = 2816
RMS_EPS = 1e-6
N_META = 16
CHUNK = 64

GLA_HEADS = 4
GLA_DK = 128
GLA_DV = 256
GLA_KEY = GLA_HEADS * GLA_DK
GLA_VAL = GLA_HEADS * GLA_DV
GLA_RANK = 16
GLA_GATE_NORMALIZER = 16.0

MLA_HEADS = 8
MLA_Q_RANK = 384
MLA_KV_RANK = 256
MLA_NOPE = 128
MLA_ROPE = 64
MLA_V = 128
ROPE_THETA = 10000.0
QK_W = MLA_KV_RANK + 128
LOG2_E = 1.4426950408889634

LANES = 128
VMEM_LIMIT = 58 * 1024 * 1024

TM = 512
FF_CHUNK = 256
TQ = 256
N_WSTEPS = D_FF // FF_CHUNK

NT_DIMS = (((1,), (1,)), ((), ()))
TN_DIMS = (((0,), (0,)), ((), ()))


def _dot(a, b):
    return jnp.dot(a, b, preferred_element_type=F32)


def _dot_nt(a, b):
    return lax.dot_general(a, b, NT_DIMS, preferred_element_type=F32)


def _dot_tn(a, b):
    return lax.dot_general(a, b, TN_DIMS, preferred_element_type=F32)


def _rms(x, w):
    return x * lax.rsqrt(jnp.mean(x * x, axis=-1, keepdims=True) + RMS_EPS) * w


def _silu(x):
    return x * jax.nn.sigmoid(x)


def _params(*sem):
    return pltpu.CompilerParams(dimension_semantics=sem, vmem_limit_bytes=VMEM_LIMIT)


def _resident(shape):
    zeros = (0,) * len(shape)
    return pl.BlockSpec(shape, lambda *_: zeros, pipeline_mode=pl.Buffered(1))


def _whole(shape):
    zeros = (0,) * len(shape)
    return pl.BlockSpec(shape, lambda *_: zeros)


def _tiles(block, row_axis=0, first=0, count=None):
    nd = len(block)

    def imap(i):
        t = i - N_WSTEPS - first
        if count is not None:
            t = jnp.clip(t, 0, count - 1)
        idx = [0] * nd
        idx[row_axis] = t
        return tuple(idx)

    return pl.BlockSpec(block, imap)


def _select(i, refs, firsts):
    val = refs[0][...]
    for ref, first in zip(refs[1:], firsts[1:]):
        val = jnp.where(i >= first, ref[...], val)
    return val


def _ffn_stage_weights(step, ffn_w, ffn_scr):
    _, wg_ref, wu_ref, wd_ref = ffn_w
    _, wg_scr, wu_scr, wd_scr = ffn_scr

    @pl.when(step < N_WSTEPS)
    def _():
        wg_scr[step] = wg_ref[...].astype(BF16)
        wu_scr[step] = wu_ref[...].astype(BF16)
        wd_scr[step] = wd_ref[...].astype(BF16)


def _ffn_compute(x, nw_ref, ffn_scr):
    h_scr, wg_scr, wu_scr, wd_scr = ffn_scr
    m = x.shape[0]
    xn = _rms(x, nw_ref[...]).astype(BF16)
    for c in range(N_WSTEPS):
        g = _dot(xn, wg_scr[c])
        u = _dot(xn, wu_scr[c])
        h_scr[:m, c * FF_CHUNK:(c + 1) * FF_CHUNK] = (_silu(g) * u).astype(BF16)
    return x + 0.5 * _dot(h_scr[:m, :], wd_scr[...].reshape(D_FF, D_MODEL))


def _gla_proj_compute(x, nw_ref, w_ref, wgu_ref, bg_ref):
    c_v = 2 * GLA_KEY
    c_r = c_v + GLA_VAL
    c_g = c_r + GLA_VAL
    xn = _rms(x, nw_ref[...]).astype(BF16)
    q = _dot(xn, w_ref[:, 0:GLA_KEY]).astype(BF16)
    k = _dot(xn, w_ref[:, GLA_KEY:c_v]).astype(BF16)
    v = _dot(xn, w_ref[:, c_v:c_r]).astype(BF16)
    r = _dot(xn, w_ref[:, c_r:c_g]).astype(BF16)
    gd = _dot(xn, w_ref[:, c_g:])
    z = _dot(gd.astype(BF16), wgu_ref[...]) + bg_ref[...]
    log_sig = jnp.minimum(z, 0.0) - jnp.log1p(jnp.exp(-jnp.abs(z)))
    return q, k, v, r, log_sig * (1.0 / GLA_GATE_NORMALIZER)


def _gla_out_compute(o, r, h, hn_ref, wo_ref):
    parts = []
    for hd in range(GLA_HEADS):
        vs = slice(hd * GLA_DV, (hd + 1) * GLA_DV)
        on = _rms(o[:, vs].astype(F32), hn_ref[...])
        parts.append((on * _silu(r[:, vs].astype(F32))).astype(BF16))
    return h + _dot(jnp.concatenate(parts, axis=1), wo_ref[...])


def _mla_proj_compute(x, cos_t, sin_t, nw_ref, wd_ref, qn_ref, wqn_ref, wqr_ref, kvn_ref, wuk_ref,
                      q_out, kc_out, ckv_out, kr_out, vt_out):
    m = x.shape[0]
    scale = (MLA_NOPE + MLA_ROPE) ** -0.5 * LOG2_E
    xn = _rms(x, nw_ref[...]).astype(BF16)
    xd = _dot(xn, wd_ref[...])
    cqn = _rms(xd[:, :MLA_Q_RANK], qn_ref[...]).astype(BF16)
    ckv = _rms(xd[:, MLA_Q_RANK:MLA_Q_RANK + MLA_KV_RANK], kvn_ref[...])
    lane = lax.broadcasted_iota(jnp.int32, (m, LANES), 1)
    first_half = (lane & (MLA_ROPE // 2)) == 0

    def rope(t):
        swapped = jnp.where(first_half, pltpu.roll(t, LANES - MLA_ROPE // 2, 1),
                            pltpu.roll(t, MLA_ROPE // 2, 1))
        return t * cos_t + swapped * sin_t

    kr2 = rope(xd[:, MLA_Q_RANK + MLA_KV_RANK:])
    ckv_out[...] = ckv
    kr_out[...] = kr2[:, :MLA_ROPE]
    kc_out[:, :MLA_KV_RANK] = ckv.astype(BF16)
    kc_out[:, MLA_KV_RANK:] = jnp.where(lane < MLA_ROPE, kr2, 0.0).astype(BF16)
    if vt_out is not None:
        for j in range(m // TQ):
            vt_out[j] = ckv[j * TQ:(j + 1) * TQ, :].T.astype(BF16)

    qn = _dot(cqn, wqn_ref[...])
    qr = _dot(cqn, wqr_ref[...])
    for c in range(MLA_HEADS // 2):
        rr = rope(qr[:, c * LANES:(c + 1) * LANES]) * scale
        for e in range(2):
            hd = 2 * c + e
            ql = _dot(qn[:, hd * MLA_NOPE:(hd + 1) * MLA_NOPE].astype(BF16), wuk_ref[hd]) * scale
            q_out[hd, :, :MLA_KV_RANK] = ql.astype(BF16)
            rot = rr if e == 0 else pltpu.roll(rr, MLA_ROPE, 1)
            q_out[hd, :, MLA_KV_RANK:] = rot.astype(BF16)


def _mla_out_compute(o_lat, h, wuv_ref, wo_ref):
    parts = [_dot(o_lat[hd], wuv_ref[hd]).astype(BF16) for hd in range(MLA_HEADS)]
    return h + _dot(jnp.concatenate(parts, axis=1), wo_ref[...])


def _layer(shape, layer):
    zeros = (0,) * len(shape)
    return pl.BlockSpec((None,) + tuple(shape), lambda *_: (layer,) + zeros, pipeline_mode=pl.Buffered(1))


def _layer_specs(params):
    return [_layer(a.shape[1:], layer) for a, layer in params]


def _arrays(params):
    return [a for a, _ in params]


def _ffn_specs(ffn):
    (nw, l_n), (_, l_g), (_, l_u), (_, l_d) = ffn

    def slab(i):
        return jnp.minimum(i, N_WSTEPS - 1)

    return [_layer(nw.shape[1:], l_n),
            pl.BlockSpec((None, D_MODEL, FF_CHUNK), lambda i: (l_g, 0, slab(i))),
            pl.BlockSpec((None, D_MODEL, FF_CHUNK), lambda i: (l_u, 0, slab(i))),
            pl.BlockSpec((None, FF_CHUNK, D_MODEL), lambda i: (l_d, slab(i), 0))]


def _ffn_scratch():
    return [pltpu.VMEM((TM, D_FF), BF16),
            pltpu.VMEM((N_WSTEPS, D_MODEL, FF_CHUNK), BF16), pltpu.VMEM((N_WSTEPS, D_MODEL, FF_CHUNK), BF16),
            pltpu.VMEM((N_WSTEPS, FF_CHUNK, D_MODEL), BF16)]


def _is_tile(i, n_tiles):
    return (i >= 0) & (i < n_tiles)


def _ffn_gla_proj_kernel(*refs, x_firsts, n_tiles):
    ns = len(x_firsts)
    x_refs, xm_ref = refs[:ns], refs[ns]
    ffn_w = refs[ns + 1:ns + 5]
    proj_w = refs[ns + 5:ns + 9]
    outs = refs[ns + 9:ns + 15]
    outs_m = refs[ns + 15:ns + 21]
    ffn_scr = refs[ns + 21:]
    _ffn_stage_weights(pl.program_id(0), ffn_w, ffn_scr)
    i = pl.program_id(0) - N_WSTEPS

    def body(x, o):
        y = _ffn_compute(x, ffn_w[0], ffn_scr)
        o[0][...] = y
        for ref, val in zip(o[1:], _gla_proj_compute(y, *proj_w)):
            ref[...] = val

    @pl.when(_is_tile(i, n_tiles))
    def _():
        body(_select(i, x_refs, x_firsts), outs)

    @pl.when(i == n_tiles)
    def _():
        body(xm_ref[...], outs_m)


def _ffn_gla_proj(x_segs, x_meta, ffn, proj):
    n_tiles = sum(s[2] for s in x_segs)
    n_main = n_tiles * TM
    n_mt = x_meta.shape[0]
    widths = [(D_MODEL, F32), (GLA_KEY, BF16), (GLA_KEY, BF16), (GLA_VAL, BF16), (GLA_VAL, BF16),
              (GLA_KEY, F32)]
    in_specs = [_tiles((TM, D_MODEL), 0, f, c) for _, f, c in x_segs]
    in_specs += [_whole((n_mt, D_MODEL))] + _ffn_specs(ffn) + _layer_specs(proj)
    return pl.pallas_call(
        functools.partial(_ffn_gla_proj_kernel, x_firsts=tuple(s[1] for s in x_segs), n_tiles=n_tiles),
        grid=(N_WSTEPS + n_tiles + 1,),
        in_specs=in_specs,
        out_specs=[_tiles((TM, w), 0, 0, n_tiles) for w, _ in widths]
        + [_whole((n_mt, w)) for w, _ in widths],
        out_shape=[jax.ShapeDtypeStruct((n_main, w), d) for w, d in widths]
        + [jax.ShapeDtypeStruct((n_mt, w), d) for w, d in widths],
        scratch_shapes=_ffn_scratch(),
        compiler_params=_params("arbitrary"),
        name="ffn_gla_proj",
    )(*[s[0] for s in x_segs], x_meta, *_arrays(ffn), *_arrays(proj))


def _gla_scan_kernel(*refs, chunk, n_chunks, streams, n_steps, has_s0):
    q_ref, k_ref, v_ref, gl_ref = refs[:4]
    s0_ref = refs[4] if has_s0 else None
    o_ref, s_out_ref, st_scr = refs[4 + has_s0:]
    step = pl.program_id(1)

    @pl.when(step == 0)
    def _():
        for s in range(streams):
            for h in range(GLA_HEADS):
                if has_s0:
                    st_scr[s * GLA_HEADS + h] = s0_ref[s, h]
                else:
                    st_scr[s * GLA_HEADS + h] = jnp.zeros((GLA_DK, GLA_DV), F32)

    n_rows = chunk * n_chunks * streams
    row = lax.broadcasted_iota(jnp.int32, (n_rows, n_rows), 0)
    col = lax.broadcasted_iota(jnp.int32, (n_rows, n_rows), 1)
    causal = ((row // chunk) == (col // chunk)) & (row >= col)
    tri = jnp.where(causal, 1.0, 0.0).astype(BF16)
    scale = GLA_DK ** -0.5
    chunks = [slice(c * chunk, (c + 1) * chunk) for c in range(n_chunks * streams)]

    glog = gl_ref[...]
    hi = glog.astype(BF16)
    lo = (glog - hi.astype(F32)).astype(BF16)
    gc = _dot(tri, hi) + _dot(tri, lo)
    glast = [gc[rs.stop - 1:rs.stop, :] for rs in chunks]
    g_end = jnp.concatenate([jnp.broadcast_to(g, (chunk, GLA_KEY)) for g in glast], axis=0)
    q = q_ref[...].astype(F32) * scale
    k = k_ref[...].astype(F32)
    qg = (q * jnp.exp(gc)).astype(BF16)
    kg = (k * jnp.exp(-gc)).astype(BF16)
    kd = (k * jnp.exp(g_end - gc)).astype(BF16)
    sub = lax.broadcasted_iota(jnp.int32, (GLA_DK, GLA_KEY), 0)
    dec_rows = jnp.zeros((GLA_DK, GLA_KEY), F32)
    for c, g in enumerate(glast):
        dec_rows = jnp.where(sub == c, jnp.exp(g), dec_rows)

    key_slices = [slice(h * GLA_DK, (h + 1) * GLA_DK) for h in range(GLA_HEADS)]
    val_slices = [slice(h * GLA_DV, (h + 1) * GLA_DV) for h in range(GLA_HEADS)]
    values = [v_ref[:, vs] for vs in val_slices]
    scores = [_dot_nt(qg[:, ks], kg[:, ks]) for ks in key_slices]
    updates = [[_dot_tn(kd[rs, ks], vh[rs, :]) for rs in chunks] for ks, vh in zip(key_slices, values)]
    intra = [_dot(jnp.where(causal, a, 0.0).astype(BF16), vh) for a, vh in zip(scores, values)]
    for h, ks in enumerate(key_slices):
        dec_cols = dec_rows[:, ks].T
        states = []
        for s in range(streams):
            state = st_scr[s * GLA_HEADS + h]
            for c in range(s * n_chunks, (s + 1) * n_chunks):
                states.append(state.astype(BF16))
                state = state * dec_cols[:, c:c + 1] + updates[h][c]
            st_scr[s * GLA_HEADS + h] = state
        for c, rs in enumerate(chunks):
            o_ref[rs, val_slices[h]] = (intra[h][rs, :] + _dot(qg[rs, ks], states[c])).astype(BF16)

    @pl.when(step == n_steps - 1)
    def _():
        for s in range(streams):
            for h in range(GLA_HEADS):
                s_out_ref[s, h] = st_scr[s * GLA_HEADS + h]


def _gla_scan(q, k, v, gl, s0, *, row0, n_batch, n_steps, chunk, n_chunks, name, streams=1):
    assert n_batch % streams == 0 and (streams == 1 or n_steps == 1)
    rb = chunk * n_chunks * streams
    base = row0 // rb

    def rmap(b, s):
        return (base + b * n_steps + s, 0)

    def omap(b, s):
        return (b * n_steps + s, 0)

    def smap(b, s):
        return (b, 0, 0, 0)

    ins = [q, k, v, gl]
    specs = [pl.BlockSpec((rb, GLA_KEY), rmap), pl.BlockSpec((rb, GLA_KEY), rmap),
             pl.BlockSpec((rb, GLA_VAL), rmap), pl.BlockSpec((rb, GLA_KEY), rmap)]
    state_block = (streams, GLA_HEADS, GLA_DK, GLA_DV)
    if s0 is not None:
        ins.append(s0)
        specs.append(pl.BlockSpec(state_block, smap))
    kern = functools.partial(_gla_scan_kernel, chunk=chunk, n_chunks=n_chunks, streams=streams,
                             n_steps=n_steps, has_s0=s0 is not None)
    return pl.pallas_call(
        kern,
        grid=(n_batch // streams, n_steps),
        in_specs=specs,
        out_specs=[pl.BlockSpec((rb, GLA_VAL), omap), pl.BlockSpec(state_block, smap)],
        out_shape=[jax.ShapeDtypeStruct((n_batch * n_steps * chunk * n_chunks, GLA_VAL), BF16),
                   jax.ShapeDtypeStruct((n_batch,) + state_block[1:], F32)],
        scratch_shapes=[pltpu.VMEM((streams * GLA_HEADS, GLA_DK, GLA_DV), F32)],
        compiler_params=_params("parallel", "arbitrary"),
        name=name,
    )(*ins)


def _gla_out_ffn_kernel(*refs, o_firsts, n_tiles):
    ns = len(o_firsts)
    o_refs = refs[:ns]
    r_ref, h_ref, om_ref, rm_ref, hm_ref, hn_ref, wo_ref = refs[ns:ns + 7]
    ffn_w = refs[ns + 7:ns + 11]
    y_ref, ym_ref = refs[ns + 11:ns + 13]
    ffn_scr = refs[ns + 13:]
    _ffn_stage_weights(pl.program_id(0), ffn_w, ffn_scr)
    i = pl.program_id(0) - N_WSTEPS

    def body(o, r, h):
        return _ffn_compute(_gla_out_compute(o, r, h, hn_ref, wo_ref), ffn_w[0], ffn_scr)

    @pl.when(_is_tile(i, n_tiles))
    def _():
        y_ref[...] = body(_select(i, o_refs, o_firsts), r_ref[...], h_ref[...])

    @pl.when(i == n_tiles)
    def _():
        ym_ref[...] = body(om_ref[...], rm_ref[...], hm_ref[...])


def _gla_out_ffn(o_segs, r, h, o_meta, r_meta, h_meta, out, ffn):
    n_tiles = sum(s[2] for s in o_segs)
    n_mt = h_meta.shape[0]
    in_specs = [_tiles((TM, GLA_VAL), 0, f, c) for _, f, c in o_segs]
    in_specs += [_tiles((TM, GLA_VAL), 0, 0, n_tiles), _tiles((TM, D_MODEL), 0, 0, n_tiles),
                 _whole((n_mt, GLA_VAL)), _whole((n_mt, GLA_VAL)), _whole((n_mt, D_MODEL))]
    in_specs += _layer_specs(out) + _ffn_specs(ffn)
    return pl.pallas_call(
        functools.partial(_gla_out_ffn_kernel, o_firsts=tuple(s[1] for s in o_segs), n_tiles=n_tiles),
        grid=(N_WSTEPS + n_tiles + 1,),
        in_specs=in_specs,
        out_specs=[_tiles((TM, D_MODEL), 0, 0, n_tiles), _whole((n_mt, D_MODEL))],
        out_shape=[jax.ShapeDtypeStruct(h.shape, F32), jax.ShapeDtypeStruct(h_meta.shape, F32)],
        scratch_shapes=_ffn_scratch(),
        compiler_params=_params("arbitrary"),
        name="gla_out_ffn",
    )(*[s[0] for s in o_segs], r, h, o_meta, r_meta, h_meta, *_arrays(out), *_arrays(ffn))


def _ffn_mla_proj_kernel(*refs, n_tiles):
    x_ref, cos_ref, sin_ref, xm_ref, cosm_ref, sinm_ref = refs[:6]
    ffn_w = refs[6:10]
    proj_w = refs[10:17]
    y_ref, q_out, kc_out, ckv_out, kr_out, vt_out = refs[17:23]
    ym_ref, qm_out, kcm_out, ckvm_out, krm_out = refs[23:28]
    ffn_scr = refs[28:]
    _ffn_stage_weights(pl.program_id(0), ffn_w, ffn_scr)
    i = pl.program_id(0) - N_WSTEPS

    @pl.when(_is_tile(i, n_tiles))
    def _():
        y = _ffn_compute(x_ref[...], ffn_w[0], ffn_scr)
        y_ref[...] = y
        _mla_proj_compute(y, cos_ref[...], sin_ref[...], *proj_w, q_out, kc_out, ckv_out, kr_out, vt_out)

    @pl.when(i == n_tiles)
    def _():
        y = _ffn_compute(xm_ref[...], ffn_w[0], ffn_scr)
        ym_ref[...] = y
        _mla_proj_compute(y, cosm_ref[...], sinm_ref[...], *proj_w, qm_out, kcm_out, ckvm_out, krm_out,
                          None)


def _ffn_mla_proj(x, x_meta, cos_t, sin_t, table_tile, cos_m, sin_m, ffn, proj_w):
    n_main = x.shape[0]
    n_tiles = n_main // TM
    n_mt = x_meta.shape[0]
    kt_per_tile = TM // TQ

    def rows(w):
        return _tiles((TM, w), 0, 0, n_tiles)

    table = pl.BlockSpec((TM, LANES), lambda i: (table_tile(jnp.clip(i - N_WSTEPS, 0, n_tiles - 1)), 0))
    in_specs = [rows(D_MODEL), table, table, _whole((n_mt, D_MODEL)), _whole((n_mt, LANES)),
                _whole((n_mt, LANES))] + _ffn_specs(ffn)
    in_specs += [_resident(w.shape) for w in proj_w]
    return pl.pallas_call(
        functools.partial(_ffn_mla_proj_kernel, n_tiles=n_tiles),
        grid=(N_WSTEPS + n_tiles + 1,),
        in_specs=in_specs,
        out_specs=[rows(D_MODEL), _tiles((MLA_HEADS, TM, QK_W), 1, 0, n_tiles), rows(QK_W),
                   rows(MLA_KV_RANK), rows(MLA_ROPE),
                   _tiles((kt_per_tile, MLA_KV_RANK, TQ), 0, 0, n_tiles),
                   _whole((n_mt, D_MODEL)), _whole((MLA_HEADS, n_mt, QK_W)), _whole((n_mt, QK_W)),
                   _whole((n_mt, MLA_KV_RANK)), _whole((n_mt, MLA_ROPE))],
        out_shape=[jax.ShapeDtypeStruct((n_main, D_MODEL), F32),
                   jax.ShapeDtypeStruct((MLA_HEADS, n_main, QK_W), BF16),
                   jax.ShapeDtypeStruct((n_main, QK_W), BF16),
                   jax.ShapeDtypeStruct((n_main, MLA_KV_RANK), F32),
                   jax.ShapeDtypeStruct((n_main, MLA_ROPE), F32),
                   jax.ShapeDtypeStruct((n_main // TQ, MLA_KV_RANK, TQ), BF16),
                   jax.ShapeDtypeStruct((n_mt, D_MODEL), F32),
                   jax.ShapeDtypeStruct((MLA_HEADS, n_mt, QK_W), BF16),
                   jax.ShapeDtypeStruct((n_mt, QK_W), BF16),
                   jax.ShapeDtypeStruct((n_mt, MLA_KV_RANK), F32),
                   jax.ShapeDtypeStruct((n_mt, MLA_ROPE), F32)],
        scratch_shapes=_ffn_scratch(),
        compiler_params=_params("arbitrary"),
        name="ffn_mla_proj",
    )(x, cos_t, sin_t, x_meta, cos_m, sin_m, *_arrays(ffn), *proj_w)


def _attn_prompt_kernel(q_ref, kf_ref, km_ref, vtf_ref, vtm_ref, o_ref,
                        s_scr, sm_scr, m_scr, l_scr, acc_scr):
    i = pl.program_id(1)
    cols = MLA_HEADS * TQ
    q = q_ref[...].reshape(cols, QK_W)

    def consume(parts):
        m_cur = None
        for s, _ in parts:
            mx = jnp.max(s, axis=0, keepdims=True)
            m_cur = mx if m_cur is None else jnp.maximum(m_cur, mx)
        m_old = m_scr[...]
        m_new = jnp.maximum(m_old, m_cur)
        alpha = jnp.exp2(m_old - m_new)
        p_sum = None
        pv = None
        for s, vt in parts:
            p = jnp.exp2(s - m_new)
            ps = jnp.sum(p, axis=0, keepdims=True)
            term = _dot(vt, p.astype(BF16))
            p_sum = ps if p_sum is None else p_sum + ps
            pv = term if pv is None else pv + term
        l_scr[...] = alpha * l_scr[...] + p_sum
        acc_scr[...] = alpha * acc_scr[...] + pv
        m_scr[...] = m_new

    m_scr[...] = jnp.full(m_scr.shape, -jnp.inf, F32)
    l_scr[...] = jnp.zeros(l_scr.shape, F32)
    acc_scr[...] = jnp.zeros(acc_scr.shape, F32)
    s_first = _dot_nt(jnp.concatenate([kf_ref[0], km_ref[...]], axis=0), q)
    s_scr[0] = s_first[:TQ]
    sm_scr[...] = s_first[TQ:]

    def pair(jj, carry):
        j = 2 * jj
        s_scr[1] = _dot_nt(kf_ref[j + 1], q)
        consume([(s_scr[0], vtf_ref[j])])
        s_scr[0] = _dot_nt(kf_ref[j + 2], q)
        consume([(s_scr[1], vtf_ref[j + 1])])
        return carry

    lax.fori_loop(0, lax.shift_right_logical(i, 1), pair, 0)

    @pl.when((i & 1) == 1)
    def _():
        s_scr[1] = _dot_nt(kf_ref[i], q)
        consume([(s_scr[0], vtf_ref[i - 1])])

    key = lax.broadcasted_iota(jnp.int32, (TQ, cols), 0)
    qry = lax.broadcasted_iota(jnp.int32, (TQ, cols), 1) % TQ
    visible = (key // CHUNK) <= (qry // CHUNK)
    s_own = jnp.where(visible, s_scr[i & 1], -jnp.inf)
    consume([(s_own, vtf_ref[i]), (sm_scr[...], vtm_ref[0])])

    for h in range(MLA_HEADS):
        cs = slice(h * TQ, (h + 1) * TQ)
        o_ref[h] = (acc_scr[:, cs] / l_scr[:, cs]).T.astype(BF16)


def _attn_prompt(q, k_tiles, kc_meta, vt_tiles, vt_meta, *, n_batch, seq):
    steps = seq // TQ
    cols = MLA_HEADS * TQ
    return pl.pallas_call(
        _attn_prompt_kernel,
        grid=(n_batch, steps),
        in_specs=[pl.BlockSpec((MLA_HEADS, TQ, QK_W), lambda b, i: (0, b * steps + i, 0)),
                  pl.BlockSpec((steps, TQ, QK_W), lambda b, i: (b, 0, 0)),
                  pl.BlockSpec((N_META, QK_W), lambda b, i: (b, 0)),
                  pl.BlockSpec((steps, MLA_KV_RANK, TQ), lambda b, i: (b, 0, 0)),
                  pl.BlockSpec((1, MLA_KV_RANK, N_META), lambda b, i: (b, 0, 0))],
        out_specs=pl.BlockSpec((MLA_HEADS, TQ, MLA_KV_RANK), lambda b, i: (0, b * steps + i, 0)),
        out_shape=jax.ShapeDtypeStruct((MLA_HEADS, n_batch * seq, MLA_KV_RANK), BF16),
        scratch_shapes=[pltpu.VMEM((2, TQ, cols), F32), pltpu.VMEM((N_META, cols), F32),
                        pltpu.VMEM((1, cols), F32),
                        pltpu.VMEM((1, cols), F32), pltpu.VMEM((MLA_KV_RANK, cols), F32)],
        compiler_params=_params("parallel", "arbitrary"),
        name="mla_attn_prompt",
    )(q, k_tiles, kc_meta, vt_tiles, vt_meta)


def _attn_full_kernel(*refs, n_q, streams, has_past):
    if has_past:
        q_ref, kn_ref, pl_ref, pr_ref, o_ref = refs
    else:
        q_ref, kn_ref, o_ref = refs
    rows = MLA_HEADS * n_q
    blocks = [slice(s * n_q, (s + 1) * n_q) for s in range(streams)]
    qs = [q_ref[:, rs, :].reshape(rows, QK_W) for rs in blocks]
    kns = [kn_ref[rs, :] for rs in blocks]
    s_new = [_dot_nt(q, kn) for q, kn in zip(qs, kns)]
    if has_past:
        lats = [pl_ref[s].astype(BF16) for s in range(streams)]
        rp_ts = [pr_ref[s].astype(BF16) for s in range(streams)]
        s_past = [_dot_nt(q[:, :MLA_KV_RANK], lat) + _dot(q[:, MLA_KV_RANK:MLA_KV_RANK + MLA_ROPE], rp_t)
                  for q, lat, rp_t in zip(qs, lats, rp_ts)]
    for s, rs in enumerate(blocks):
        m = jnp.max(s_new[s], axis=1, keepdims=True)
        if has_past:
            m = jnp.maximum(m, jnp.max(s_past[s], axis=1, keepdims=True))
        p_n = jnp.exp2(s_new[s] - m)
        l = jnp.sum(p_n, axis=1, keepdims=True)
        acc = _dot(p_n.astype(BF16), kns[s][:, :MLA_KV_RANK])
        if has_past:
            p_p = jnp.exp2(s_past[s] - m)
            l = l + jnp.sum(p_p, axis=1, keepdims=True)
            acc = acc + _dot(p_p.astype(BF16), lats[s])
        o_ref[:, rs, :] = (acc / l).astype(BF16).reshape(MLA_HEADS, n_q, MLA_KV_RANK)


def _attn_full(q, kc, past_lat, past_rope_t, *, row0, n_batch, n_q, streams, name):
    assert n_batch % streams == 0
    nb = n_q * streams
    base = row0 // nb
    has_past = past_lat is not None
    ins = [q, kc]
    specs = [pl.BlockSpec((MLA_HEADS, nb, QK_W), lambda b: (0, base + b, 0)),
             pl.BlockSpec((nb, QK_W), lambda b: (base + b, 0))]
    if has_past:
        past = past_lat.shape[1]
        ins += [past_lat, past_rope_t]
        specs += [pl.BlockSpec((streams, past, MLA_KV_RANK), lambda b: (b, 0, 0)),
                  pl.BlockSpec((streams, MLA_ROPE, past), lambda b: (b, 0, 0))]
    return pl.pallas_call(
        functools.partial(_attn_full_kernel, n_q=n_q, streams=streams, has_past=has_past),
        grid=(n_batch // streams,),
        in_specs=specs,
        out_specs=pl.BlockSpec((MLA_HEADS, nb, MLA_KV_RANK), lambda b: (0, b, 0)),
        out_shape=jax.ShapeDtypeStruct((MLA_HEADS, n_batch * n_q, MLA_KV_RANK), BF16),
        compiler_params=_params("parallel"),
        name=name,
    )(*ins)


def _mla_out_ffn_final_kernel(*refs, firsts, n_tiles):
    ns = len(firsts)
    o_refs = refs[:ns]
    h_ref, om_ref, hm_ref, wuv_ref, wo_ref = refs[ns:ns + 5]
    ffn_w = refs[ns + 5:ns + 9]
    fw_ref = refs[ns + 9]
    y_refs = refs[ns + 10:2 * ns + 10]
    ym_ref = refs[2 * ns + 10]
    ffn_scr = refs[2 * ns + 11:]
    _ffn_stage_weights(pl.program_id(0), ffn_w, ffn_scr)
    i = pl.program_id(0) - N_WSTEPS

    def body(o_lat, h):
        y = _mla_out_compute(o_lat, h, wuv_ref, wo_ref)
        return _rms(_ffn_compute(y, ffn_w[0], ffn_scr), fw_ref[...])

    @pl.when(_is_tile(i, n_tiles))
    def _():
        y = body(_select(i, o_refs, firsts), h_ref[...])
        bounds = list(firsts[1:]) + [n_tiles]
        for y_ref, lo, hi in zip(y_refs, firsts, bounds):
            @pl.when((i >= lo) & (i < hi))
            def _(y_ref=y_ref):
                y_ref[...] = y

    @pl.when(i == n_tiles)
    def _():
        ym_ref[...] = body(om_ref[...], hm_ref[...])


def _mla_out_ffn_final(o_segs, h, o_meta, h_meta, w_uv_t, w_out, ffn, final_w):
    n_tiles = sum(s[2] for s in o_segs)
    n_mt = h_meta.shape[0]
    firsts = tuple(s[1] for s in o_segs)
    in_specs = [_tiles((MLA_HEADS, TM, MLA_KV_RANK), 1, f, c) for _, f, c in o_segs]
    in_specs += [_tiles((TM, D_MODEL), 0, 0, n_tiles), _whole(o_meta.shape), _whole(h_meta.shape),
                 _resident(w_uv_t.shape), _resident(w_out.shape)]
    in_specs += _ffn_specs(ffn) + [_resident((1, D_MODEL))]
    return pl.pallas_call(
        functools.partial(_mla_out_ffn_final_kernel, firsts=firsts, n_tiles=n_tiles),
        grid=(N_WSTEPS + n_tiles + 1,),
        in_specs=in_specs,
        out_specs=[_tiles((TM, D_MODEL), 0, f, c) for _, f, c in o_segs] + [_whole((n_mt, D_MODEL))],
        out_shape=[jax.ShapeDtypeStruct((c * TM, D_MODEL), F32) for _, _, c in o_segs]
        + [jax.ShapeDtypeStruct((n_mt, D_MODEL), F32)],
        scratch_shapes=_ffn_scratch(),
        compiler_params=_params("arbitrary"),
        name="mla_out_ffn_final",
    )(*[s[0] for s in o_segs], h, o_meta, h_meta, w_uv_t, w_out, *_arrays(ffn), final_w.reshape(1, D_MODEL))


def _rope_tables(pos):
    half = MLA_ROPE // 2
    lane = jnp.arange(LANES)
    inv = ROPE_THETA ** (-(lane % half).astype(F32) / half)
    sign = jnp.where((lane // half) % 2 == 0, -1.0, 1.0).astype(F32)
    ang = pos[:, None] * inv[None, :]
    return jnp.cos(ang), jnp.sin(ang) * sign[None, :]


def kernel(x_prompt, x_sample, state_gla, cache_mla_latent, cache_mla_rope, meta_tokens, ffn1_norm, ffn1_w_gate, ffn1_w_up, ffn1_w_down, mix_norm, gla_w_in, gla_w_gate_up, gla_b_gate, gla_head_norm, gla_w_out, mla_w_down, mla_q_norm, mla_w_uq, mla_kv_norm, mla_w_uk, mla_w_uv, mla_w_out, ffn2_norm, ffn2_w_gate, ffn2_w_up, ffn2_w_down, final_norm):
    bp, seq, _ = x_prompt.shape
    bs, ls, _ = x_sample.shape
    past = cache_mla_latent.shape[2]
    n_fr = bp * seq
    n_sm = bs * ls
    n_mt = bp * N_META
    assert ffn1_norm.shape[0] == 2
    assert n_fr % TM == 0 and n_sm % TM == 0 and TM % TQ == 0 and seq % TQ == 0 and TQ % CHUNK == 0
    assert past % CHUNK == 0 and ls <= CHUNK and N_META <= CHUNK
    fr_tiles = n_fr // TM
    sm_tiles = n_sm // TM

    def vec(p):
        return p.reshape(p.shape[0], 1, p.shape[1])

    ffn1 = (vec(ffn1_norm), ffn1_w_gate, ffn1_w_up, ffn1_w_down)
    ffn2 = (vec(ffn2_norm), ffn2_w_gate, ffn2_w_up, ffn2_w_down)

    def at(params, layer):
        return [(p, layer) for p in params]

    x_meta = jnp.tile(meta_tokens.astype(F32), (bp, 1))
    gla_proj = [(vec(mix_norm), 0)] + at((gla_w_in.astype(BF16), gla_w_gate_up.astype(BF16),
                                          vec(gla_b_gate)), 0)
    (h, q, k, v, r, gl, h_m, q_m, k_m, v_m, r_m, gl_m) = _ffn_gla_proj(
        [(x_prompt.reshape(n_fr, D_MODEL), 0, fr_tiles), (x_sample.reshape(n_sm, D_MODEL), fr_tiles, sm_tiles)],
        x_meta, at(ffn1, 0), gla_proj)
    o_m, s_meta = _gla_scan(q_m, k_m, v_m, gl_m, None, row0=0, n_batch=bp, n_steps=1,
                            chunk=N_META, n_chunks=1, streams=bp, name="gla_scan_meta")
    o_f, s_p = _gla_scan(q, k, v, gl, s_meta, row0=0, n_batch=bp, n_steps=seq // TQ,
                         chunk=CHUNK, n_chunks=TQ // CHUNK, name="gla_scan_frames")
    o_s, s_s = _gla_scan(q, k, v, gl, state_gla.reshape(state_gla.shape[1:]), row0=n_fr, n_batch=bs,
                         n_steps=1, chunk=ls, n_chunks=1, streams=TQ // ls, name="gla_scan_sample")
    h, h_m = _gla_out_ffn([(o_f, 0, fr_tiles), (o_s, fr_tiles, sm_tiles)], r, h, o_m, r_m, h_m,
                          at((vec(gla_head_norm), gla_w_out.astype(BF16)), 0), at(ffn2, 0))

    seq_blocks = seq // TM
    cos_t, sin_t = _rope_tables(jnp.concatenate([N_META + jnp.arange(seq, dtype=F32),
                                                 jnp.tile(past + jnp.arange(ls, dtype=F32), TM // ls)]))
    cos_m, sin_m = _rope_tables(jnp.tile(jnp.arange(N_META, dtype=F32), bp))

    def table_tile(i):
        return jnp.where(i < fr_tiles, i % seq_blocks, seq_blocks)
    w_down = mla_w_down[0]
    w_down = jnp.concatenate([w_down, w_down[:, -MLA_ROPE:]], axis=1).astype(BF16)
    w_uq = mla_w_uq[0].reshape(MLA_Q_RANK, MLA_HEADS, MLA_NOPE + MLA_ROPE)
    w_qn = w_uq[:, :, :MLA_NOPE].reshape(MLA_Q_RANK, MLA_HEADS * MLA_NOPE).astype(BF16)
    w_qr = w_uq[:, :, MLA_NOPE:].reshape(MLA_Q_RANK, MLA_HEADS * MLA_ROPE).astype(BF16)
    w_uk_t = jnp.transpose(mla_w_uk[0], (1, 2, 0)).astype(BF16)
    w_uv_t = jnp.transpose(mla_w_uv[0], (1, 0, 2)).astype(BF16)
    mla_proj_w = (mix_norm[1].reshape(1, D_MODEL), w_down, mla_q_norm[0].reshape(1, MLA_Q_RANK), w_qn, w_qr,
                  mla_kv_norm[0].reshape(1, MLA_KV_RANK), w_uk_t)
    (h, qa, kc, ckv, kr, vt, h_m, qa_m, kc_m, ckv_m, kr_m) = _ffn_mla_proj(
        h, h_m, cos_t, sin_t, table_tile, cos_m, sin_m, at(ffn1, 1), mla_proj_w)
    k_tiles = kc.reshape((n_fr + n_sm) // TQ, TQ, QK_W)
    vt_meta = jnp.swapaxes(kc_m[:, :MLA_KV_RANK].reshape(bp, N_META, MLA_KV_RANK), 1, 2)
    ol_f = _attn_prompt(qa, k_tiles, kc_m, vt, vt_meta, n_batch=bp, seq=seq)
    past_rope_t = jnp.swapaxes(cache_mla_rope.reshape(bs, past, MLA_ROPE), 1, 2)
    ol_s = _attn_full(qa, kc, cache_mla_latent.reshape(bs, past, MLA_KV_RANK), past_rope_t,
                      row0=n_fr, n_batch=bs, n_q=ls, streams=2, name="mla_attn_sample")
    ol_m = _attn_full(qa_m, kc_m, None, None, row0=0, n_batch=bp, n_q=N_META, streams=bp,
                      name="mla_attn_meta")
    y_prompt, y_sample, _ = _mla_out_ffn_final(
        [(ol_f, 0, fr_tiles), (ol_s, fr_tiles, sm_tiles)], h, ol_m, h_m, w_uv_t, mla_w_out[0].astype(BF16),
        at(ffn2, 1), final_norm)

    def seq_major(t_main, t_meta, width):
        return jnp.concatenate([t_meta.reshape(bp, N_META, width), t_main[:n_fr].reshape(bp, seq, width)],
                               axis=1)

    return (y_prompt.reshape(bp, seq, D_MODEL), y_sample.reshape(bs, ls, D_MODEL),
            s_p[None], s_s[None],
            seq_major(ckv, ckv_m, MLA_KV_RANK)[None], seq_major(kr, kr_m, MLA_ROPE)[None],
            ckv[n_fr:].reshape(bs, ls, MLA_KV_RANK)[None], kr[n_fr:].reshape(bs, ls, MLA_ROPE)[None])
```

```python
import functools

import jax
import jax.numpy as jnp
from jax import lax
from jax.experimental import pallas as pl
from jax.experimental.pallas import tpu as pltpu

F32 = jnp.float32
BF16 = jnp.bfloat16

D_MODEL = 1024
D_FF = 2816
RMS_EPS = 1e-6
N_META = 16
CHUNK = 64

GLA_HEADS = 4
GLA_DK = 128
GLA_DV = 256
GLA_KEY = GLA_HEADS * GLA_DK
GLA_VAL = GLA_HEADS * GLA_DV
GLA_RANK = 16
GLA_GATE_NORMALIZER = 16.0

MLA_HEADS = 8
MLA_Q_RANK = 384
MLA_KV_RANK = 256
MLA_NOPE = 128
MLA_ROPE = 64
MLA_V = 128
ROPE_THETA = 10000.0
QK_W = MLA_KV_RANK + 128
LOG2_E = 1.4426950408889634
VT_ROWS = MLA_KV_RANK + 16

LANES = 128
VMEM_LIMIT = 58 * 1024 * 1024

TM = 512
FF_CHUNK = 256
TQ = 256
N_WSTEPS = D_FF // FF_CHUNK

NT_DIMS = (((1,), (1,)), ((), ()))
TN_DIMS = (((0,), (0,)), ((), ()))


def _dot(a, b):
    return jnp.dot(a, b, preferred_element_type=F32)


def _dot_nt(a, b):
    return lax.dot_general(a, b, NT_DIMS, preferred_element_type=F32)


def _dot_tn(a, b):
    return lax.dot_general(a, b, TN_DIMS, preferred_element_type=F32)


def _rms(x, w):
    return x * lax.rsqrt(jnp.mean(x * x, axis=-1, keepdims=True) + RMS_EPS) * w


def _silu(x):
    return x * jax.nn.sigmoid(x)


def _params(*sem):
    return pltpu.CompilerParams(dimension_semantics=sem, vmem_limit_bytes=VMEM_LIMIT)


def _resident(shape):
    zeros = (0,) * len(shape)
    return pl.BlockSpec(shape, lambda *_: zeros, pipeline_mode=pl.Buffered(1))


def _whole(shape):
    zeros = (0,) * len(shape)
    return pl.BlockSpec(shape, lambda *_: zeros)


def _tiles(block, row_axis=0, first=0, count=None):
    nd = len(block)

    def imap(i):
        t = i - N_WSTEPS - first
        if count is not None:
            t = jnp.clip(t, 0, count - 1)
        idx = [0] * nd
        idx[row_axis] = t
        return tuple(idx)

    return pl.BlockSpec(block, imap)


def _select(i, refs, firsts):
    val = refs[0][...]
    for ref, first in zip(refs[1:], firsts[1:]):
        val = jnp.where(i >= first, ref[...], val)
    return val


def _ffn_stage_weights(step, ffn_w, ffn_scr):
    _, wg_ref, wu_ref, wd_ref = ffn_w
    _, wg_scr, wu_scr, wd_scr = ffn_scr

    @pl.when(step < N_WSTEPS)
    def _():
        wg_scr[step] = wg_ref[...].astype(BF16)
        wu_scr[step] = wu_ref[...].astype(BF16)
        wd_scr[step] = wd_ref[...].astype(BF16)


def _ffn_compute(x, nw_ref, ffn_scr):
    h_scr, wg_scr, wu_scr, wd_scr = ffn_scr
    m = x.shape[0]
    xn = _rms(x, nw_ref[...]).astype(BF16)
    for c in range(N_WSTEPS):
        g = _dot(xn, wg_scr[c])
        u = _dot(xn, wu_scr[c])
        h_scr[:m, c * FF_CHUNK:(c + 1) * FF_CHUNK] = (_silu(g) * u).astype(BF16)
    return x + 0.5 * _dot(h_scr[:m, :], wd_scr[...].reshape(D_FF, D_MODEL))


def _gla_proj_compute(x, nw_ref, w_ref, wgu_ref, bg_ref):
    c_v = 2 * GLA_KEY
    c_r = c_v + GLA_VAL
    c_g = c_r + GLA_VAL
    xn = _rms(x, nw_ref[...]).astype(BF16)
    q = _dot(xn, w_ref[:, 0:GLA_KEY]).astype(BF16)
    k = _dot(xn, w_ref[:, GLA_KEY:c_v]).astype(BF16)
    v = _dot(xn, w_ref[:, c_v:c_r]).astype(BF16)
    r = _dot(xn, w_ref[:, c_r:c_g]).astype(BF16)
    gd = _dot(xn, w_ref[:, c_g:])
    z = _dot(gd.astype(BF16), wgu_ref[...]) + bg_ref[...]
    log_sig = jnp.minimum(z, 0.0) - jnp.log1p(jnp.exp(-jnp.abs(z)))
    return q, k, v, r, log_sig * (1.0 / GLA_GATE_NORMALIZER)


def _gla_out_compute(o, r, h, hn_ref, wo_ref):
    parts = []
    for hd in range(GLA_HEADS):
        vs = slice(hd * GLA_DV, (hd + 1) * GLA_DV)
        on = _rms(o[:, vs].astype(F32), hn_ref[...])
        parts.append((on * _silu(r[:, vs].astype(F32))).astype(BF16))
    return h + _dot(jnp.concatenate(parts, axis=1), wo_ref[...])


def _mla_proj_compute(x, cos_t, sin_t, nw_ref, wd_ref, qn_ref, wqn_ref, wqr_ref, kvn_ref, wuk_ref,
                      q_out, kc_out, vt_out):
    m = x.shape[0]
    scale = (MLA_NOPE + MLA_ROPE) ** -0.5 * LOG2_E
    xn = _rms(x, nw_ref[...]).astype(BF16)
    xd = _dot(xn, wd_ref[...])
    cqn = _rms(xd[:, :MLA_Q_RANK], qn_ref[...]).astype(BF16)
    ckv = _rms(xd[:, MLA_Q_RANK:MLA_Q_RANK + MLA_KV_RANK], kvn_ref[...])
    lane = lax.broadcasted_iota(jnp.int32, (m, LANES), 1)
    first_half = (lane & (MLA_ROPE // 2)) == 0

    def rope(t):
        swapped = jnp.where(first_half, pltpu.roll(t, LANES - MLA_ROPE // 2, 1),
                            pltpu.roll(t, MLA_ROPE // 2, 1))
        return t * cos_t + swapped * sin_t

    kr2 = rope(xd[:, MLA_Q_RANK + MLA_KV_RANK:])
    kc_out[:, :MLA_KV_RANK] = ckv.astype(BF16)
    kc_out[:, MLA_KV_RANK:] = jnp.where(lane < MLA_ROPE, kr2, 0.0).astype(BF16)
    if vt_out is not None:
        for j in range(m // TQ):
            vt_out[j, :MLA_KV_RANK, :] = ckv[j * TQ:(j + 1) * TQ, :].T.astype(BF16)
            vt_out[j, MLA_KV_RANK:, :] = jnp.ones((VT_ROWS - MLA_KV_RANK, TQ), BF16)

    qn = _dot(cqn, wqn_ref[...])
    qr = _dot(cqn, wqr_ref[...])
    for c in range(MLA_HEADS // 2):
        rr = rope(qr[:, c * LANES:(c + 1) * LANES]) * scale
        for e in range(2):
            hd = 2 * c + e
            ql = _dot(qn[:, hd * MLA_NOPE:(hd + 1) * MLA_NOPE].astype(BF16), wuk_ref[hd]) * scale
            q_out[hd, :, :MLA_KV_RANK] = ql.astype(BF16)
            rot = rr if e == 0 else pltpu.roll(rr, MLA_ROPE, 1)
            q_out[hd, :, MLA_KV_RANK:] = rot.astype(BF16)
    return ckv, kr2[:, :MLA_ROPE]


def _mla_out_compute(o_lat, h, wuv_ref, wo_ref):
    parts = [_dot(o_lat[hd], wuv_ref[hd]).astype(BF16) for hd in range(MLA_HEADS)]
    return h + _dot(jnp.concatenate(parts, axis=1), wo_ref[...])


def _layer(shape, layer):
    zeros = (0,) * len(shape)
    return pl.BlockSpec((None,) + tuple(shape), lambda *_: (layer,) + zeros, pipeline_mode=pl.Buffered(1))


def _layer_specs(params):
    return [_layer(a.shape[1:], layer) for a, layer in params]


def _arrays(params):
    return [a for a, _ in params]


def _ffn_specs(ffn):
    (nw, l_n), (_, l_g), (_, l_u), (_, l_d) = ffn

    def slab(i):
        return jnp.minimum(i, N_WSTEPS - 1)

    return [_layer(nw.shape[1:], l_n),
            pl.BlockSpec((None, D_MODEL, FF_CHUNK), lambda i: (l_g, 0, slab(i))),
            pl.BlockSpec((None, D_MODEL, FF_CHUNK), lambda i: (l_u, 0, slab(i))),
            pl.BlockSpec((None, FF_CHUNK, D_MODEL), lambda i: (l_d, slab(i), 0))]


def _ffn_scratch():
    return [pltpu.VMEM((TM, D_FF), BF16),
            pltpu.VMEM((N_WSTEPS, D_MODEL, FF_CHUNK), BF16), pltpu.VMEM((N_WSTEPS, D_MODEL, FF_CHUNK), BF16),
            pltpu.VMEM((N_WSTEPS, FF_CHUNK, D_MODEL), BF16)]


def _is_tile(i, n_tiles):
    return (i >= 0) & (i < n_tiles)


def _ffn_gla_proj_kernel(*refs, x_firsts, n_tiles):
    ns = len(x_firsts)
    x_refs, xm_ref = refs[:ns], refs[ns]
    ffn_w = refs[ns + 1:ns + 5]
    proj_w = refs[ns + 5:ns + 9]
    outs = refs[ns + 9:ns + 15]
    outs_m = refs[ns + 15:ns + 21]
    ffn_scr = refs[ns + 21:]
    _ffn_stage_weights(pl.program_id(0), ffn_w, ffn_scr)
    i = pl.program_id(0) - N_WSTEPS

    def body(x, o):
        y = _ffn_compute(x, ffn_w[0], ffn_scr)
        o[0][...] = y
        for ref, val in zip(o[1:], _gla_proj_compute(y, *proj_w)):
            ref[...] = val

    @pl.when(_is_tile(i, n_tiles))
    def _():
        body(_select(i, x_refs, x_firsts), outs)

    @pl.when(i == n_tiles)
    def _():
        body(xm_ref[...], outs_m)


def _ffn_gla_proj(x_segs, x_meta, ffn, proj):
    n_tiles = sum(s[2] for s in x_segs)
    n_main = n_tiles * TM
    n_mt = x_meta.shape[0]
    widths = [(D_MODEL, F32), (GLA_KEY, BF16), (GLA_KEY, BF16), (GLA_VAL, BF16), (GLA_VAL, BF16),
              (GLA_KEY, F32)]
    in_specs = [_tiles((TM, D_MODEL), 0, f, c) for _, f, c in x_segs]
    in_specs += [_whole((n_mt, D_MODEL))] + _ffn_specs(ffn) + _layer_specs(proj)
    return pl.pallas_call(
        functools.partial(_ffn_gla_proj_kernel, x_firsts=tuple(s[1] for s in x_segs), n_tiles=n_tiles),
        grid=(N_WSTEPS + n_tiles + 1,),
        in_specs=in_specs,
        out_specs=[_tiles((TM, w), 0, 0, n_tiles) for w, _ in widths]
        + [_whole((n_mt, w)) for w, _ in widths],
        out_shape=[jax.ShapeDtypeStruct((n_main, w), d) for w, d in widths]
        + [jax.ShapeDtypeStruct((n_mt, w), d) for w, d in widths],
        scratch_shapes=_ffn_scratch(),
        compiler_params=_params("arbitrary"),
        name="ffn_gla_proj",
    )(*[s[0] for s in x_segs], x_meta, *_arrays(ffn), *_arrays(proj))


def _gla_scan_kernel(*refs, chunk, n_chunks, streams, n_steps, has_s0):
    q_ref, k_ref, v_ref, gl_ref = refs[:4]
    s0_ref = refs[4] if has_s0 else None
    o_ref, s_out_ref, st_scr = refs[4 + has_s0:]
    step = pl.program_id(1)

    @pl.when(step == 0)
    def _():
        for s in range(streams):
            for h in range(GLA_HEADS):
                if has_s0:
                    st_scr[s * GLA_HEADS + h] = s0_ref[s, h]
                else:
                    st_scr[s * GLA_HEADS + h] = jnp.zeros((GLA_DK, GLA_DV), F32)

    n_rows = chunk * n_chunks * streams
    row = lax.broadcasted_iota(jnp.int32, (n_rows, n_rows), 0)
    col = lax.broadcasted_iota(jnp.int32, (n_rows, n_rows), 1)
    causal = ((row // chunk) == (col // chunk)) & (row >= col)
    tri = jnp.where(causal, 1.0, 0.0).astype(BF16)
    scale = GLA_DK ** -0.5
    chunks = [slice(c * chunk, (c + 1) * chunk) for c in range(n_chunks * streams)]

    glog = gl_ref[...]
    hi = glog.astype(BF16)
    lo = (glog - hi.astype(F32)).astype(BF16)
    gc = _dot(tri, hi) + _dot(tri, lo)
    glast = [gc[rs.stop - 1:rs.stop, :] for rs in chunks]
    g_end = jnp.concatenate([jnp.broadcast_to(g, (chunk, GLA_KEY)) for g in glast], axis=0)
    q = q_ref[...].astype(F32) * scale
    k = k_ref[...].astype(F32)
    qg = (q * jnp.exp(gc)).astype(BF16)
    kg = (k * jnp.exp(-gc)).astype(BF16)
    kd = (k * jnp.exp(g_end - gc)).astype(BF16)
    sub = lax.broadcasted_iota(jnp.int32, (GLA_DK, GLA_KEY), 0)
    dec_rows = jnp.zeros((GLA_DK, GLA_KEY), F32)
    for c, g in enumerate(glast):
        dec_rows = jnp.where(sub == c, jnp.exp(g), dec_rows)

    key_slices = [slice(h * GLA_DK, (h + 1) * GLA_DK) for h in range(GLA_HEADS)]
    val_slices = [slice(h * GLA_DV, (h + 1) * GLA_DV) for h in range(GLA_HEADS)]
    values = [v_ref[:, vs] for vs in val_slices]
    scores = [_dot_nt(qg[:, ks], kg[:, ks]) for ks in key_slices]
    updates = [[_dot_tn(kd[rs, ks], vh[rs, :]) for rs in chunks] for ks, vh in zip(key_slices, values)]
    intra = [_dot(jnp.where(causal, a, 0.0).astype(BF16), vh) for a, vh in zip(scores, values)]
    for h, ks in enumerate(key_slices):
        dec_cols = dec_rows[:, ks].T
        states = []
        for s in range(streams):
            state = st_scr[s * GLA_HEADS + h]
            for c in range(s * n_chunks, (s + 1) * n_chunks):
                states.append(state.astype(BF16))
                state = state * dec_cols[:, c:c + 1] + updates[h][c]
            st_scr[s * GLA_HEADS + h] = state
        for c, rs in enumerate(chunks):
            o_ref[rs, val_slices[h]] = (intra[h][rs, :] + _dot(qg[rs, ks], states[c])).astype(BF16)

    @pl.when(step == n_steps - 1)
    def _():
        for s in range(streams):
            for h in range(GLA_HEADS):
                s_out_ref[s, h] = st_scr[s * GLA_HEADS + h]


def _gla_scan(q, k, v, gl, s0, *, row0, n_batch, n_steps, chunk, n_chunks, name, streams=1):
    assert n_batch % streams == 0 and (streams == 1 or n_steps == 1)
    rb = chunk * n_chunks * streams
    base = row0 // rb

    def rmap(b, s):
        return (base + b * n_steps + s, 0)

    def omap(b, s):
        return (b * n_steps + s, 0)

    def smap(b, s):
        return (b, 0, 0, 0)

    ins = [q, k, v, gl]
    specs = [pl.BlockSpec((rb, GLA_KEY), rmap), pl.BlockSpec((rb, GLA_KEY), rmap),
             pl.BlockSpec((rb, GLA_VAL), rmap), pl.BlockSpec((rb, GLA_KEY), rmap)]
    state_block = (streams, GLA_HEADS, GLA_DK, GLA_DV)
    if s0 is not None:
        ins.append(s0)
        specs.append(pl.BlockSpec(state_block, smap))
    kern = functools.partial(_gla_scan_kernel, chunk=chunk, n_chunks=n_chunks, streams=streams,
                             n_steps=n_steps, has_s0=s0 is not None)
    return pl.pallas_call(
        kern,
        grid=(n_batch // streams, n_steps),
        in_specs=specs,
        out_specs=[pl.BlockSpec((rb, GLA_VAL), omap), pl.BlockSpec(state_block, smap)],
        out_shape=[jax.ShapeDtypeStruct((n_batch * n_steps * chunk * n_chunks, GLA_VAL), BF16),
                   jax.ShapeDtypeStruct((n_batch,) + state_block[1:], F32)],
        scratch_shapes=[pltpu.VMEM((streams * GLA_HEADS, GLA_DK, GLA_DV), F32)],
        compiler_params=_params("parallel", "arbitrary"),
        name=name,
    )(*ins)


def _gla_out_ffn_kernel(*refs, o_firsts, n_tiles):
    ns = len(o_firsts)
    o_refs = refs[:ns]
    r_ref, h_ref, om_ref, rm_ref, hm_ref, hn_ref, wo_ref = refs[ns:ns + 7]
    ffn_w = refs[ns + 7:ns + 11]
    y_ref, ym_ref = refs[ns + 11:ns + 13]
    ffn_scr = refs[ns + 13:]
    _ffn_stage_weights(pl.program_id(0), ffn_w, ffn_scr)
    i = pl.program_id(0) - N_WSTEPS

    def body(o, r, h):
        return _ffn_compute(_gla_out_compute(o, r, h, hn_ref, wo_ref), ffn_w[0], ffn_scr)

    @pl.when(_is_tile(i, n_tiles))
    def _():
        y_ref[...] = body(_select(i, o_refs, o_firsts), r_ref[...], h_ref[...])

    @pl.when(i == n_tiles)
    def _():
        ym_ref[...] = body(om_ref[...], rm_ref[...], hm_ref[...])


def _gla_out_ffn(o_segs, r, h, o_meta, r_meta, h_meta, out, ffn):
    n_tiles = sum(s[2] for s in o_segs)
    n_mt = h_meta.shape[0]
    in_specs = [_tiles((TM, GLA_VAL), 0, f, c) for _, f, c in o_segs]
    in_specs += [_tiles((TM, GLA_VAL), 0, 0, n_tiles), _tiles((TM, D_MODEL), 0, 0, n_tiles),
                 _whole((n_mt, GLA_VAL)), _whole((n_mt, GLA_VAL)), _whole((n_mt, D_MODEL))]
    in_specs += _layer_specs(out) + _ffn_specs(ffn)
    return pl.pallas_call(
        functools.partial(_gla_out_ffn_kernel, o_firsts=tuple(s[1] for s in o_segs), n_tiles=n_tiles),
        grid=(N_WSTEPS + n_tiles + 1,),
        in_specs=in_specs,
        out_specs=[_tiles((TM, D_MODEL), 0, 0, n_tiles), _whole((n_mt, D_MODEL))],
        out_shape=[jax.ShapeDtypeStruct(h.shape, F32), jax.ShapeDtypeStruct(h_meta.shape, F32)],
        scratch_shapes=_ffn_scratch(),
        compiler_params=_params("arbitrary"),
        name="gla_out_ffn",
    )(*[s[0] for s in o_segs], r, h, o_meta, r_meta, h_meta, *_arrays(out), *_arrays(ffn))


def _ffn_mla_proj_kernel(*refs, n_tiles, first_tiles):
    x_ref, cos_ref, sin_ref, xm_ref, cosm_ref, sinm_ref = refs[:6]
    ffn_w = refs[6:10]
    proj_w = refs[10:17]
    y_ref, q_out, kc_out, ckv_a, kr_a, ckv_b, kr_b, vt_out = refs[17:25]
    ym_ref, qm_out, kcm_out, ckvm_out, krm_out = refs[25:30]
    ffn_scr = refs[30:]
    _ffn_stage_weights(pl.program_id(0), ffn_w, ffn_scr)
    i = pl.program_id(0) - N_WSTEPS

    @pl.when(_is_tile(i, n_tiles))
    def _():
        y = _ffn_compute(x_ref[...], ffn_w[0], ffn_scr)
        y_ref[...] = y
        ckv, kr = _mla_proj_compute(y, cos_ref[...], sin_ref[...], *proj_w, q_out, kc_out, vt_out)

        @pl.when(i < first_tiles)
        def _():
            ckv_a[...] = ckv
            kr_a[...] = kr

        @pl.when(i >= first_tiles)
        def _():
            ckv_b[...] = ckv
            kr_b[...] = kr

    @pl.when(i == n_tiles)
    def _():
        y = _ffn_compute(xm_ref[...], ffn_w[0], ffn_scr)
        ym_ref[...] = y
        ckv, kr = _mla_proj_compute(y, cosm_ref[...], sinm_ref[...], *proj_w, qm_out, kcm_out, None)
        ckvm_out[...] = ckv
        krm_out[...] = kr


def _ffn_mla_proj(x, x_meta, cos_t, sin_t, table_tile, cos_m, sin_m, ffn, proj_w, first_tiles):
    n_main = x.shape[0]
    n_tiles = n_main // TM
    n_mt = x_meta.shape[0]
    kt_per_tile = TM // TQ
    rest_tiles = n_tiles - first_tiles

    def rows(w):
        return _tiles((TM, w), 0, 0, n_tiles)

    table = pl.BlockSpec((TM, LANES), lambda i: (table_tile(jnp.clip(i - N_WSTEPS, 0, n_tiles - 1)), 0))
    in_specs = [rows(D_MODEL), table, table, _whole((n_mt, D_MODEL)), _whole((n_mt, LANES)),
                _whole((n_mt, LANES))] + _ffn_specs(ffn)
    in_specs += [_resident(w.shape) for w in proj_w]
    return pl.pallas_call(
        functools.partial(_ffn_mla_proj_kernel, n_tiles=n_tiles, first_tiles=first_tiles),
        grid=(N_WSTEPS + n_tiles + 1,),
        in_specs=in_specs,
        out_specs=[rows(D_MODEL), _tiles((MLA_HEADS, TM, QK_W), 1, 0, n_tiles), rows(QK_W),
                   _tiles((TM, MLA_KV_RANK), 0, 0, first_tiles), _tiles((TM, MLA_ROPE), 0, 0, first_tiles),
                   _tiles((TM, MLA_KV_RANK), 0, first_tiles, rest_tiles),
                   _tiles((TM, MLA_ROPE), 0, first_tiles, rest_tiles),
                   _tiles((kt_per_tile, VT_ROWS, TQ), 0, 0, n_tiles),
                   _whole((n_mt, D_MODEL)), _whole((MLA_HEADS, n_mt, QK_W)), _whole((n_mt, QK_W)),
                   _whole((n_mt, MLA_KV_RANK)), _whole((n_mt, MLA_ROPE))],
        out_shape=[jax.ShapeDtypeStruct((n_main, D_MODEL), F32),
                   jax.ShapeDtypeStruct((MLA_HEADS, n_main, QK_W), BF16),
                   jax.ShapeDtypeStruct((n_main, QK_W), BF16),
                   jax.ShapeDtypeStruct((first_tiles * TM, MLA_KV_RANK), F32),
                   jax.ShapeDtypeStruct((first_tiles * TM, MLA_ROPE), F32),
                   jax.ShapeDtypeStruct((rest_tiles * TM, MLA_KV_RANK), F32),
                   jax.ShapeDtypeStruct((rest_tiles * TM, MLA_ROPE), F32),
                   jax.ShapeDtypeStruct((n_main // TQ, VT_ROWS, TQ), BF16),
                   jax.ShapeDtypeStruct((n_mt, D_MODEL), F32),
                   jax.ShapeDtypeStruct((MLA_HEADS, n_mt, QK_W), BF16),
                   jax.ShapeDtypeStruct((n_mt, QK_W), BF16),
                   jax.ShapeDtypeStruct((n_mt, MLA_KV_RANK), F32),
                   jax.ShapeDtypeStruct((n_mt, MLA_ROPE), F32)],
        scratch_shapes=_ffn_scratch(),
        compiler_params=_params("arbitrary"),
        name="ffn_mla_proj",
    )(x, cos_t, sin_t, x_meta, cos_m, sin_m, *_arrays(ffn), *proj_w)


def _attn_prompt_kernel(q_ref, kf_ref, km_ref, vtf_ref, vtm_ref, o_ref,
                        s_scr, sm_scr, m_scr, acc_scr):
    i = pl.program_id(1)
    cols = MLA_HEADS * TQ
    q = q_ref[...].reshape(cols, QK_W)

    def consume(parts):
        m_cur = None
        for s, _ in parts:
            mx = jnp.max(s, axis=0, keepdims=True)
            m_cur = mx if m_cur is None else jnp.maximum(m_cur, mx)
        m_old = m_scr[...]
        m_new = jnp.maximum(m_old, m_cur)
        alpha = jnp.exp2(m_old - m_new)
        pv = None
        for s, vt in parts:
            term = _dot(vt, jnp.exp2(s - m_new).astype(BF16))
            pv = term if pv is None else pv + term
        acc_scr[...] = alpha * acc_scr[...] + pv
        m_scr[...] = m_new

    m_scr[...] = jnp.full(m_scr.shape, -jnp.inf, F32)
    acc_scr[...] = jnp.zeros(acc_scr.shape, F32)
    s_first = _dot_nt(jnp.concatenate([kf_ref[0], km_ref[...]], axis=0), q)
    s_scr[0] = s_first[:TQ]
    sm_scr[...] = s_first[TQ:]

    def pair(jj, carry):
        j = 2 * jj
        s_scr[1] = _dot_nt(kf_ref[j + 1], q)
        consume([(s_scr[0], vtf_ref[j])])
        s_scr[0] = _dot_nt(kf_ref[j + 2], q)
        consume([(s_scr[1], vtf_ref[j + 1])])
        return carry

    lax.fori_loop(0, lax.shift_right_logical(i, 1), pair, 0)

    @pl.when((i & 1) == 1)
    def _():
        s_scr[1] = _dot_nt(kf_ref[i], q)
        consume([(s_scr[0], vtf_ref[i - 1])])

    key = lax.broadcasted_iota(jnp.int32, (TQ, cols), 0)
    qry = lax.broadcasted_iota(jnp.int32, (TQ, cols), 1) % TQ
    visible = (key // CHUNK) <= (qry // CHUNK)
    s_own = jnp.where(visible, s_scr[i & 1], -jnp.inf)
    consume([(s_own, vtf_ref[i]), (sm_scr[...], vtm_ref[0])])

    for h in range(MLA_HEADS):
        cs = slice(h * TQ, (h + 1) * TQ)
        total = acc_scr[MLA_KV_RANK:MLA_KV_RANK + 1, cs]
        o_ref[h] = (acc_scr[:MLA_KV_RANK, cs] / total).T.astype(BF16)


def _attn_prompt(q, k_tiles, kc_meta, vt_tiles, vt_meta, *, n_batch, seq):
    steps = seq // TQ
    cols = MLA_HEADS * TQ
    return pl.pallas_call(
        _attn_prompt_kernel,
        grid=(n_batch, steps),
        in_specs=[pl.BlockSpec((MLA_HEADS, TQ, QK_W), lambda b, i: (0, b * steps + i, 0)),
                  pl.BlockSpec((steps, TQ, QK_W), lambda b, i: (b, 0, 0)),
                  pl.BlockSpec((N_META, QK_W), lambda b, i: (b, 0)),
                  pl.BlockSpec((steps, VT_ROWS, TQ), lambda b, i: (b, 0, 0)),
                  pl.BlockSpec((1, VT_ROWS, N_META), lambda b, i: (b, 0, 0))],
        out_specs=pl.BlockSpec((MLA_HEADS, TQ, MLA_KV_RANK), lambda b, i: (0, b * steps + i, 0)),
        out_shape=jax.ShapeDtypeStruct((MLA_HEADS, n_batch * seq, MLA_KV_RANK), BF16),
        scratch_shapes=[pltpu.VMEM((2, TQ, cols), F32), pltpu.VMEM((N_META, cols), F32),
                        pltpu.VMEM((1, cols), F32), pltpu.VMEM((VT_ROWS, cols), F32)],
        compiler_params=_params("parallel", "arbitrary"),
        name="mla_attn_prompt",
    )(q, k_tiles, kc_meta, vt_tiles, vt_meta)


def _attn_full_kernel(*refs, n_q, streams, has_past):
    if has_past:
        q_ref, kn_ref, pl_ref, pr_ref, o_ref = refs
    else:
        q_ref, kn_ref, o_ref = refs
    rows = MLA_HEADS * n_q
    blocks = [slice(s * n_q, (s + 1) * n_q) for s in range(streams)]
    qs = [q_ref[:, rs, :].reshape(rows, QK_W) for rs in blocks]
    kns = [kn_ref[rs, :] for rs in blocks]
    s_new = [_dot_nt(q, kn) for q, kn in zip(qs, kns)]
    if has_past:
        lats = [pl_ref[s].astype(BF16) for s in range(streams)]
        rp_ts = [pr_ref[s].astype(BF16) for s in range(streams)]
        s_past = [_dot_nt(q[:, :MLA_KV_RANK], lat) + _dot(q[:, MLA_KV_RANK:MLA_KV_RANK + MLA_ROPE], rp_t)
                  for q, lat, rp_t in zip(qs, lats, rp_ts)]
    for s, rs in enumerate(blocks):
        m = jnp.max(s_new[s], axis=1, keepdims=True)
        if has_past:
            m = jnp.maximum(m, jnp.max(s_past[s], axis=1, keepdims=True))
        p_n = jnp.exp2(s_new[s] - m)
        l = jnp.sum(p_n, axis=1, keepdims=True)
        acc = _dot(p_n.astype(BF16), kns[s][:, :MLA_KV_RANK])
        if has_past:
            p_p = jnp.exp2(s_past[s] - m)
            l = l + jnp.sum(p_p, axis=1, keepdims=True)
            acc = acc + _dot(p_p.astype(BF16), lats[s])
        o_ref[:, rs, :] = (acc / l).astype(BF16).reshape(MLA_HEADS, n_q, MLA_KV_RANK)


def _attn_full(q, kc, past_lat, past_rope_t, *, row0, n_batch, n_q, streams, name):
    assert n_batch % streams == 0
    nb = n_q * streams
    base = row0 // nb
    has_past = past_lat is not None
    ins = [q, kc]
    specs = [pl.BlockSpec((MLA_HEADS, nb, QK_W), lambda b: (0, base + b, 0)),
             pl.BlockSpec((nb, QK_W), lambda b: (base + b, 0))]
    if has_past:
        past = past_lat.shape[1]
        ins += [past_lat, past_rope_t]
        specs += [pl.BlockSpec((streams, past, MLA_KV_RANK), lambda b: (b, 0, 0)),
                  pl.BlockSpec((streams, MLA_ROPE, past), lambda b: (b, 0, 0))]
    return pl.pallas_call(
        functools.partial(_attn_full_kernel, n_q=n_q, streams=streams, has_past=has_past),
        grid=(n_batch // streams,),
        in_specs=specs,
        out_specs=pl.BlockSpec((MLA_HEADS, nb, MLA_KV_RANK), lambda b: (0, b, 0)),
        out_shape=jax.ShapeDtypeStruct((MLA_HEADS, n_batch * n_q, MLA_KV_RANK), BF16),
        compiler_params=_params("parallel"),
        name=name,
    )(*ins)


def _mla_out_ffn_final_kernel(*refs, firsts, n_tiles):
    ns = len(firsts)
    o_refs = refs[:ns]
    h_ref, om_ref, hm_ref, wuv_ref, wo_ref = refs[ns:ns + 5]
    ffn_w = refs[ns + 5:ns + 9]
    fw_ref = refs[ns + 9]
    y_refs = refs[ns + 10:2 * ns + 10]
    ym_ref = refs[2 * ns + 10]
    ffn_scr = refs[2 * ns + 11:]
    _ffn_stage_weights(pl.program_id(0), ffn_w, ffn_scr)
    i = pl.program_id(0) - N_WSTEPS

    def body(o_lat, h):
        y = _mla_out_compute(o_lat, h, wuv_ref, wo_ref)
        return _rms(_ffn_compute(y, ffn_w[0], ffn_scr), fw_ref[...])

    @pl.when(_is_tile(i, n_tiles))
    def _():
        y = body(_select(i, o_refs, firsts), h_ref[...])
        bounds = list(firsts[1:]) + [n_tiles]
        for y_ref, lo, hi in zip(y_refs, firsts, bounds):
            @pl.when((i >= lo) & (i < hi))
            def _(y_ref=y_ref):
                y_ref[...] = y

    @pl.when(i == n_tiles)
    def _():
        ym_ref[...] = body(om_ref[...], hm_ref[...])


def _mla_out_ffn_final(o_segs, h, o_meta, h_meta, w_uv_t, w_out, ffn, final_w):
    n_tiles = sum(s[2] for s in o_segs)
    n_mt = h_meta.shape[0]
    firsts = tuple(s[1] for s in o_segs)
    in_specs = [_tiles((MLA_HEADS, TM, MLA_KV_RANK), 1, f, c) for _, f, c in o_segs]
    in_specs += [_tiles((TM, D_MODEL), 0, 0, n_tiles), _whole(o_meta.shape), _whole(h_meta.shape),
                 _resident(w_uv_t.shape), _resident(w_out.shape)]
    in_specs += _ffn_specs(ffn) + [_resident((1, D_MODEL))]
    return pl.pallas_call(
        functools.partial(_mla_out_ffn_final_kernel, firsts=firsts, n_tiles=n_tiles),
        grid=(N_WSTEPS + n_tiles + 1,),
        in_specs=in_specs,
        out_specs=[_tiles((TM, D_MODEL), 0, f, c) for _, f, c in o_segs] + [_whole((n_mt, D_MODEL))],
        out_shape=[jax.ShapeDtypeStruct((c * TM, D_MODEL), F32) for _, _, c in o_segs]
        + [jax.ShapeDtypeStruct((n_mt, D_MODEL), F32)],
        scratch_shapes=_ffn_scratch(),
        compiler_params=_params("arbitrary"),
        name="mla_out_ffn_final",
    )(*[s[0] for s in o_segs], h, o_meta, h_meta, w_uv_t, w_out, *_arrays(ffn), final_w.reshape(1, D_MODEL))


def _rope_tables(pos):
    half = MLA_ROPE // 2
    lane = jnp.arange(LANES)
    inv = ROPE_THETA ** (-(lane % half).astype(F32) / half)
    sign = jnp.where((lane // half) % 2 == 0, -1.0, 1.0).astype(F32)
    ang = pos[:, None] * inv[None, :]
    return jnp.cos(ang), jnp.sin(ang) * sign[None, :]


def kernel(x_prompt, x_sample, state_gla, cache_mla_latent, cache_mla_rope, meta_tokens, ffn1_norm, ffn1_w_gate, ffn1_w_up, ffn1_w_down, mix_norm, gla_w_in, gla_w_gate_up, gla_b_gate, gla_head_norm, gla_w_out, mla_w_down, mla_q_norm, mla_w_uq, mla_kv_norm, mla_w_uk, mla_w_uv, mla_w_out, ffn2_norm, ffn2_w_gate, ffn2_w_up, ffn2_w_down, final_norm):
    bp, seq, _ = x_prompt.shape
    bs, ls, _ = x_sample.shape
    past = cache_mla_latent.shape[2]
    n_fr = bp * seq
    n_sm = bs * ls
    n_mt = bp * N_META
    assert ffn1_norm.shape[0] == 2
    assert n_fr % TM == 0 and n_sm % TM == 0 and TM % TQ == 0 and seq % TQ == 0 and TQ % CHUNK == 0
    assert past % CHUNK == 0 and ls <= CHUNK and N_META <= CHUNK
    fr_tiles = n_fr // TM
    sm_tiles = n_sm // TM

    def vec(p):
        return p.reshape(p.shape[0], 1, p.shape[1])

    ffn1 = (vec(ffn1_norm), ffn1_w_gate, ffn1_w_up, ffn1_w_down)
    ffn2 = (vec(ffn2_norm), ffn2_w_gate, ffn2_w_up, ffn2_w_down)

    def at(params, layer):
        return [(p, layer) for p in params]

    x_meta = jnp.tile(meta_tokens.astype(F32), (bp, 1))
    gla_proj = [(vec(mix_norm), 0)] + at((gla_w_in.astype(BF16), gla_w_gate_up.astype(BF16),
                                          vec(gla_b_gate)), 0)
    (h, q, k, v, r, gl, h_m, q_m, k_m, v_m, r_m, gl_m) = _ffn_gla_proj(
        [(x_prompt.reshape(n_fr, D_MODEL), 0, fr_tiles), (x_sample.reshape(n_sm, D_MODEL), fr_tiles, sm_tiles)],
        x_meta, at(ffn1, 0), gla_proj)
    o_m, s_meta = _gla_scan(q_m, k_m, v_m, gl_m, None, row0=0, n_batch=bp, n_steps=1,
                            chunk=N_META, n_chunks=1, streams=bp, name="gla_scan_meta")
    o_f, s_p = _gla_scan(q, k, v, gl, s_meta, row0=0, n_batch=bp, n_steps=seq // TQ,
                         chunk=CHUNK, n_chunks=TQ // CHUNK, name="gla_scan_frames")
    o_s, s_s = _gla_scan(q, k, v, gl, state_gla.reshape(state_gla.shape[1:]), row0=n_fr, n_batch=bs,
                         n_steps=1, chunk=ls, n_chunks=1, streams=TQ // ls, name="gla_scan_sample")
    h, h_m = _gla_out_ffn([(o_f, 0, fr_tiles), (o_s, fr_tiles, sm_tiles)], r, h, o_m, r_m, h_m,
                          at((vec(gla_head_norm), gla_w_out.astype(BF16)), 0), at(ffn2, 0))

    seq_blocks = seq // TM
    cos_t, sin_t = _rope_tables(jnp.concatenate([N_META + jnp.arange(seq, dtype=F32),
                                                 jnp.tile(past + jnp.arange(ls, dtype=F32), TM // ls)]))
    cos_m, sin_m = _rope_tables(jnp.tile(jnp.arange(N_META, dtype=F32), bp))

    def table_tile(i):
        return jnp.where(i < fr_tiles, i % seq_blocks, seq_blocks)
    w_down = mla_w_down[0]
    w_down = jnp.concatenate([w_down, w_down[:, -MLA_ROPE:]], axis=1).astype(BF16)
    w_uq = mla_w_uq[0].reshape(MLA_Q_RANK, MLA_HEADS, MLA_NOPE + MLA_ROPE)
    w_qn = w_uq[:, :, :MLA_NOPE].reshape(MLA_Q_RANK, MLA_HEADS * MLA_NOPE).astype(BF16)
    w_qr = w_uq[:, :, MLA_NOPE:].reshape(MLA_Q_RANK, MLA_HEADS * MLA_ROPE).astype(BF16)
    w_uk_t = jnp.transpose(mla_w_uk[0], (1, 2, 0)).astype(BF16)
    w_uv_t = jnp.transpose(mla_w_uv[0], (1, 0, 2)).astype(BF16)
    mla_proj_w = (mix_norm[1].reshape(1, D_MODEL), w_down, mla_q_norm[0].reshape(1, MLA_Q_RANK), w_qn, w_qr,
                  mla_kv_norm[0].reshape(1, MLA_KV_RANK), w_uk_t)
    (h, qa, kc, ckv_f, kr_f, ckv_s, kr_s, vt, h_m, qa_m, kc_m, ckv_m, kr_m) = _ffn_mla_proj(
        h, h_m, cos_t, sin_t, table_tile, cos_m, sin_m, at(ffn1, 1), mla_proj_w, fr_tiles)
    k_tiles = kc.reshape((n_fr + n_sm) // TQ, TQ, QK_W)
    vt_meta = jnp.swapaxes(kc_m[:, :MLA_KV_RANK].reshape(bp, N_META, MLA_KV_RANK), 1, 2)
    vt_meta = jnp.concatenate([vt_meta, jnp.ones((bp, VT_ROWS - MLA_KV_RANK, N_META), BF16)], axis=1)
    ol_f = _attn_prompt(qa, k_tiles, kc_m, vt, vt_meta, n_batch=bp, seq=seq)
    past_rope_t = jnp.swapaxes(cache_mla_rope.reshape(bs, past, MLA_ROPE), 1, 2)
    ol_s = _attn_full(qa, kc, cache_mla_latent.reshape(bs, past, MLA_KV_RANK), past_rope_t,
                      row0=n_fr, n_batch=bs, n_q=ls, streams=2, name="mla_attn_sample")
    ol_m = _attn_full(qa_m, kc_m, None, None, row0=0, n_batch=bp, n_q=N_META, streams=bp,
                      name="mla_attn_meta")
    y_prompt, y_sample, _ = _mla_out_ffn_final(
        [(ol_f, 0, fr_tiles), (ol_s, fr_tiles, sm_tiles)], h, ol_m, h_m, w_uv_t, mla_w_out[0].astype(BF16),
        at(ffn2, 1), final_norm)

    def seq_major(t_frames, t_meta, width):
        return jnp.concatenate([t_meta.reshape(bp, N_META, width), t_frames.reshape(bp, seq, width)], axis=1)

    return (y_prompt.reshape(bp, seq, D_MODEL), y_sample.reshape(bs, ls, D_MODEL),
            s_p[None], s_s[None],
            seq_major(ckv_f, ckv_m, MLA_KV_RANK)[None], seq_major(kr_f, kr_m, MLA_ROPE)[None],
            ckv_s.reshape(1, bs, ls, MLA_KV_RANK), kr_s.reshape(1, bs, ls, MLA_ROPE))
```

```python
import functools

import jax
import jax.numpy as jnp
from jax import lax
from jax.experimental import pallas as pl
from jax.experimental.pallas import tpu as pltpu

F32 = jnp.float32
BF16 = jnp.bfloat16

D_MODEL = 1024
D_FF = 2816
RMS_EPS = 1e-6
N_META = 16
CHUNK = 64

GLA_HEADS = 4
GLA_DK = 128
GLA_DV = 256
GLA_KEY = GLA_HEADS * GLA_DK
GLA_VAL = GLA_HEADS * GLA_DV
GLA_RANK = 16
GLA_GATE_NORMALIZER = 16.0

MLA_HEADS = 8
MLA_Q_RANK = 384
MLA_KV_RANK = 256
MLA_NOPE = 128
MLA_ROPE = 64
MLA_V = 128
ROPE_THETA = 10000.0
QK_W = MLA_KV_RANK + 128
LOG2_E = 1.4426950408889634
VT_ROWS = MLA_KV_RANK + 16

LANES = 128
VMEM_LIMIT = 58 * 1024 * 1024

TM = 512
FF_CHUNK = 256
TQ = 256
GLA_GROUP_ROWS = 256
GLA_STEP_ROWS = 512
N_WSTEPS = D_FF // FF_CHUNK

NT_DIMS = (((1,), (1,)), ((), ()))
TN_DIMS = (((0,), (0,)), ((), ()))


def _dot(a, b):
    return jnp.dot(a, b, preferred_element_type=F32)


def _dot_nt(a, b):
    return lax.dot_general(a, b, NT_DIMS, preferred_element_type=F32)


def _dot_tn(a, b):
    return lax.dot_general(a, b, TN_DIMS, preferred_element_type=F32)


def _rms(x, w):
    return x * lax.rsqrt(jnp.mean(x * x, axis=-1, keepdims=True) + RMS_EPS) * w


def _silu(x):
    return x * jax.nn.sigmoid(x)


def _params(*sem):
    return pltpu.CompilerParams(dimension_semantics=sem, vmem_limit_bytes=VMEM_LIMIT)


def _resident(shape):
    zeros = (0,) * len(shape)
    return pl.BlockSpec(shape, lambda *_: zeros, pipeline_mode=pl.Buffered(1))


def _whole(shape):
    zeros = (0,) * len(shape)
    return pl.BlockSpec(shape, lambda *_: zeros)


def _tiles(block, row_axis=0, first=0, count=None):
    nd = len(block)

    def imap(i):
        t = i - N_WSTEPS - first
        if count is not None:
            t = jnp.clip(t, 0, count - 1)
        idx = [0] * nd
        idx[row_axis] = t
        return tuple(idx)

    return pl.BlockSpec(block, imap)


def _select(i, refs, firsts):
    val = refs[0][...]
    for ref, first in zip(refs[1:], firsts[1:]):
        val = jnp.where(i >= first, ref[...], val)
    return val


def _ffn_stage_weights(step, ffn_w, ffn_scr):
    _, wg_ref, wu_ref, wd_ref = ffn_w
    _, wg_scr, wu_scr, wd_scr = ffn_scr

    @pl.when(step < N_WSTEPS)
    def _():
        wg_scr[step] = wg_ref[...].astype(BF16)
        wu_scr[step] = wu_ref[...].astype(BF16)
        wd_scr[step] = wd_ref[...].astype(BF16)


def _ffn_compute(x, nw_ref, ffn_scr):
    h_scr, wg_scr, wu_scr, wd_scr = ffn_scr
    m = x.shape[0]
    xn = _rms(x, nw_ref[...]).astype(BF16)
    for c in range(N_WSTEPS):
        g = _dot(xn, wg_scr[c])
        u = _dot(xn, wu_scr[c])
        h_scr[:m, c * FF_CHUNK:(c + 1) * FF_CHUNK] = (_silu(g) * u).astype(BF16)
    return x + 0.5 * _dot(h_scr[:m, :], wd_scr[...].reshape(D_FF, D_MODEL))


def _gla_proj_compute(x, nw_ref, w_ref, wgu_ref, bg_ref):
    c_v = 2 * GLA_KEY
    c_r = c_v + GLA_VAL
    c_g = c_r + GLA_VAL
    xn = _rms(x, nw_ref[...]).astype(BF16)
    q = _dot_nt(xn, w_ref[0:GLA_KEY, :]).astype(BF16)
    k = _dot_nt(xn, w_ref[GLA_KEY:c_v, :]).astype(BF16)
    v = _dot_nt(xn, w_ref[c_v:c_r, :]).astype(BF16)
    r = _dot_nt(xn, w_ref[c_r:c_g, :]).astype(BF16)
    gd = _dot_nt(xn, w_ref[c_g:, :])
    z = _dot(gd.astype(BF16), wgu_ref[...]) + bg_ref[...]
    log_sig = jnp.minimum(z, 0.0) - jnp.log1p(jnp.exp(-jnp.abs(z)))
    return q, k, v, r, log_sig * (1.0 / GLA_GATE_NORMALIZER)


def _gla_out_compute(o, r, h, hn_ref, wo_ref):
    parts = []
    for hd in range(GLA_HEADS):
        vs = slice(hd * GLA_DV, (hd + 1) * GLA_DV)
        on = _rms(o[:, vs].astype(F32), hn_ref[...])
        parts.append((on * _silu(r[:, vs].astype(F32))).astype(BF16))
    return h + _dot(jnp.concatenate(parts, axis=1), wo_ref[...])


def _mla_proj_compute(x, cos_t, sin_t, nw_ref, wd_ref, qn_ref, wqn_ref, wqr_ref, kvn_ref, wuk_ref,
                      q_out, kc_out, vt_out):
    m = x.shape[0]
    scale = (MLA_NOPE + MLA_ROPE) ** -0.5 * LOG2_E
    xn = _rms(x, nw_ref[...]).astype(BF16)
    xd = _dot(xn, wd_ref[...])
    cqn = _rms(xd[:, :MLA_Q_RANK], qn_ref[...]).astype(BF16)
    ckv = _rms(xd[:, MLA_Q_RANK:MLA_Q_RANK + MLA_KV_RANK], kvn_ref[...])
    lane = lax.broadcasted_iota(jnp.int32, (m, LANES), 1)
    first_half = (lane & (MLA_ROPE // 2)) == 0

    def rope(t):
        swapped = jnp.where(first_half, pltpu.roll(t, LANES - MLA_ROPE // 2, 1),
                            pltpu.roll(t, MLA_ROPE // 2, 1))
        return t * cos_t + swapped * sin_t

    kr2 = rope(xd[:, MLA_Q_RANK + MLA_KV_RANK:])
    kc_out[:, :MLA_KV_RANK] = ckv.astype(BF16)
    kc_out[:, MLA_KV_RANK:] = jnp.where(lane < MLA_ROPE, kr2, 0.0).astype(BF16)
    if vt_out is not None:
        for j in range(m // TQ):
            vt_out[j, :MLA_KV_RANK, :] = ckv[j * TQ:(j + 1) * TQ, :].T.astype(BF16)
            vt_out[j, MLA_KV_RANK:, :] = jnp.ones((VT_ROWS - MLA_KV_RANK, TQ), BF16)

    qn = _dot(cqn, wqn_ref[...])
    qr = _dot(cqn, wqr_ref[...])
    for c in range(MLA_HEADS // 2):
        rr = rope(qr[:, c * LANES:(c + 1) * LANES]) * scale
        for e in range(2):
            hd = 2 * c + e
            ql = _dot(qn[:, hd * MLA_NOPE:(hd + 1) * MLA_NOPE].astype(BF16), wuk_ref[hd]) * scale
            q_out[hd, :, :MLA_KV_RANK] = ql.astype(BF16)
            rot = rr if e == 0 else pltpu.roll(rr, MLA_ROPE, 1)
            q_out[hd, :, MLA_KV_RANK:] = rot.astype(BF16)
    return ckv, kr2[:, :MLA_ROPE]


def _mla_out_compute(o_lat, h, wuv_ref, wo_ref):
    parts = [_dot(o_lat[hd], wuv_ref[hd]).astype(BF16) for hd in range(MLA_HEADS)]
    return h + _dot(jnp.concatenate(parts, axis=1), wo_ref[...])


def _layer(shape, layer):
    zeros = (0,) * len(shape)
    return pl.BlockSpec((None,) + tuple(shape), lambda *_: (layer,) + zeros, pipeline_mode=pl.Buffered(1))


def _layer_specs(params):
    return [_layer(a.shape[1:], layer) for a, layer in params]


def _arrays(params):
    return [a for a, _ in params]


def _ffn_specs(ffn):
    (nw, l_n), (_, l_g), (_, l_u), (_, l_d) = ffn

    def slab(i):
        return jnp.minimum(i, N_WSTEPS - 1)

    return [_layer(nw.shape[1:], l_n),
            pl.BlockSpec((None, D_MODEL, FF_CHUNK), lambda i: (l_g, 0, slab(i))),
            pl.BlockSpec((None, D_MODEL, FF_CHUNK), lambda i: (l_u, 0, slab(i))),
            pl.BlockSpec((None, FF_CHUNK, D_MODEL), lambda i: (l_d, slab(i), 0))]


def _ffn_scratch():
    return [pltpu.VMEM((TM, D_FF), BF16),
            pltpu.VMEM((N_WSTEPS, D_MODEL, FF_CHUNK), BF16), pltpu.VMEM((N_WSTEPS, D_MODEL, FF_CHUNK), BF16),
            pltpu.VMEM((N_WSTEPS, FF_CHUNK, D_MODEL), BF16)]


def _is_tile(i, n_tiles):
    return (i >= 0) & (i < n_tiles)


def _ffn_gla_proj_kernel(*refs, x_firsts, n_tiles):
    ns = len(x_firsts)
    x_refs, xm_ref = refs[:ns], refs[ns]
    ffn_w = refs[ns + 1:ns + 5]
    proj_w = refs[ns + 5:ns + 9]
    outs = refs[ns + 9:ns + 15]
    outs_m = refs[ns + 15:ns + 21]
    ffn_scr = refs[ns + 21:]
    _ffn_stage_weights(pl.program_id(0), ffn_w, ffn_scr)
    i = pl.program_id(0) - N_WSTEPS

    def body(x, o):
        y = _ffn_compute(x, ffn_w[0], ffn_scr)
        o[0][...] = y
        for ref, val in zip(o[1:], _gla_proj_compute(y, *proj_w)):
            ref[...] = val

    @pl.when(_is_tile(i, n_tiles))
    def _():
        body(_select(i, x_refs, x_firsts), outs)

    @pl.when(i == n_tiles)
    def _():
        body(xm_ref[...], outs_m)


def _ffn_gla_proj(x_segs, x_meta, ffn, proj):
    n_tiles = sum(s[2] for s in x_segs)
    n_main = n_tiles * TM
    n_mt = x_meta.shape[0]
    widths = [(D_MODEL, F32), (GLA_KEY, BF16), (GLA_KEY, BF16), (GLA_VAL, BF16), (GLA_VAL, BF16),
              (GLA_KEY, F32)]
    in_specs = [_tiles((TM, D_MODEL), 0, f, c) for _, f, c in x_segs]
    in_specs += [_whole((n_mt, D_MODEL))] + _ffn_specs(ffn) + _layer_specs(proj)
    return pl.pallas_call(
        functools.partial(_ffn_gla_proj_kernel, x_firsts=tuple(s[1] for s in x_segs), n_tiles=n_tiles),
        grid=(N_WSTEPS + n_tiles + 1,),
        in_specs=in_specs,
        out_specs=[_tiles((TM, w), 0, 0, n_tiles) for w, _ in widths]
        + [_whole((n_mt, w)) for w, _ in widths],
        out_shape=[jax.ShapeDtypeStruct((n_main, w), d) for w, d in widths]
        + [jax.ShapeDtypeStruct((n_mt, w), d) for w, d in widths],
        scratch_shapes=_ffn_scratch(),
        compiler_params=_params("arbitrary"),
        name="ffn_gla_proj",
    )(*[s[0] for s in x_segs], x_meta, *_arrays(ffn), *_arrays(proj))


def _gla_scan_kernel(*refs, chunk, n_chunks, streams, group, n_steps, has_s0):
    q_ref, k_ref, v_ref, gl_ref = refs[:4]
    s0_ref = refs[4] if has_s0 else None
    o_ref, s_out_ref, st_scr = refs[4 + has_s0:]
    step = pl.program_id(1)

    @pl.when(step == 0)
    def _():
        for s in range(streams):
            for h in range(GLA_HEADS):
                if has_s0:
                    st_scr[s * GLA_HEADS + h] = s0_ref[s, h]
                else:
                    st_scr[s * GLA_HEADS + h] = jnp.zeros((GLA_DK, GLA_DV), F32)

    total = n_chunks * streams
    g_rows = chunk * group
    row = lax.broadcasted_iota(jnp.int32, (g_rows, g_rows), 0)
    col = lax.broadcasted_iota(jnp.int32, (g_rows, g_rows), 1)
    causal = ((row // chunk) == (col // chunk)) & (row >= col)
    tri = jnp.where(causal, 1.0, 0.0).astype(BF16)
    scale = GLA_DK ** -0.5
    groups = [slice(g * g_rows, (g + 1) * g_rows) for g in range(total // group)]
    in_group = [slice(c * chunk, (c + 1) * chunk) for c in range(group)]
    key_slices = [slice(h * GLA_DK, (h + 1) * GLA_DK) for h in range(GLA_HEADS)]
    val_slices = [slice(h * GLA_DV, (h + 1) * GLA_DV) for h in range(GLA_HEADS)]

    gcs = []
    for rows in groups:
        glog = gl_ref[rows, :]
        hi = glog.astype(BF16)
        lo = (glog - hi.astype(F32)).astype(BF16)
        gcs.append(_dot(tri, hi) + _dot(tri, lo))
    sub = lax.broadcasted_iota(jnp.int32, (GLA_DK, GLA_KEY), 0)
    dec_rows = jnp.zeros((GLA_DK, GLA_KEY), F32)
    qgs, kgs, kds = [], [], []
    for g, (rows, gc) in enumerate(zip(groups, gcs)):
        glast = [gc[rs.stop - 1:rs.stop, :] for rs in in_group]
        g_end = jnp.concatenate([jnp.broadcast_to(t, (chunk, GLA_KEY)) for t in glast], axis=0)
        for c, t in enumerate(glast):
            dec_rows = jnp.where(sub == g * group + c, jnp.exp(t), dec_rows)
        q = q_ref[rows, :].astype(F32) * scale
        k = k_ref[rows, :].astype(F32)
        qgs.append((q * jnp.exp(gc)).astype(BF16))
        kgs.append((k * jnp.exp(-gc)).astype(BF16))
        kds.append((k * jnp.exp(g_end - gc)).astype(BF16))

    values = [[v_ref[rows, vs] for vs in val_slices] for rows in groups]
    scores = [[_dot_nt(qg[:, ks], kg[:, ks]) for ks in key_slices] for qg, kg in zip(qgs, kgs)]
    updates = [[_dot_tn(kds[g][rs, ks], values[g][h][rs, :]) for g in range(len(groups)) for rs in in_group]
               for h, ks in enumerate(key_slices)]
    intra = [[_dot(jnp.where(causal, a, 0.0).astype(BF16), vh) for a, vh in zip(sg, vg)]
             for sg, vg in zip(scores, values)]
    for h, ks in enumerate(key_slices):
        dec_cols = dec_rows[:, ks].T
        states = []
        for s in range(streams):
            state = st_scr[s * GLA_HEADS + h]
            for c in range(s * n_chunks, (s + 1) * n_chunks):
                states.append(state.astype(BF16))
                state = state * dec_cols[:, c:c + 1] + updates[h][c]
            st_scr[s * GLA_HEADS + h] = state
        for g, rows in enumerate(groups):
            for c, rs in enumerate(in_group):
                out_rows = slice(rows.start + rs.start, rows.start + rs.stop)
                o_ref[out_rows, val_slices[h]] = (
                    intra[g][h][rs, :] + _dot(qgs[g][rs, ks], states[g * group + c])).astype(BF16)

    @pl.when(step == n_steps - 1)
    def _():
        for s in range(streams):
            for h in range(GLA_HEADS):
                s_out_ref[s, h] = st_scr[s * GLA_HEADS + h]


def _gla_scan(q, k, v, gl, s0, *, row0, n_batch, n_steps, chunk, n_chunks, name, streams=1):
    assert n_batch % streams == 0 and (streams == 1 or n_steps == 1)
    rb = chunk * n_chunks * streams
    base = row0 // rb

    def rmap(b, s):
        return (base + b * n_steps + s, 0)

    def omap(b, s):
        return (b * n_steps + s, 0)

    def smap(b, s):
        return (b, 0, 0, 0)

    ins = [q, k, v, gl]
    specs = [pl.BlockSpec((rb, GLA_KEY), rmap), pl.BlockSpec((rb, GLA_KEY), rmap),
             pl.BlockSpec((rb, GLA_VAL), rmap), pl.BlockSpec((rb, GLA_KEY), rmap)]
    state_block = (streams, GLA_HEADS, GLA_DK, GLA_DV)
    if s0 is not None:
        ins.append(s0)
        specs.append(pl.BlockSpec(state_block, smap))
    group = min(n_chunks * streams, GLA_GROUP_ROWS // chunk)
    assert (n_chunks * streams) % group == 0
    kern = functools.partial(_gla_scan_kernel, chunk=chunk, n_chunks=n_chunks, streams=streams,
                             group=group, n_steps=n_steps, has_s0=s0 is not None)
    return pl.pallas_call(
        kern,
        grid=(n_batch // streams, n_steps),
        in_specs=specs,
        out_specs=[pl.BlockSpec((rb, GLA_VAL), omap), pl.BlockSpec(state_block, smap)],
        out_shape=[jax.ShapeDtypeStruct((n_batch * n_steps * chunk * n_chunks, GLA_VAL), BF16),
                   jax.ShapeDtypeStruct((n_batch,) + state_block[1:], F32)],
        scratch_shapes=[pltpu.VMEM((streams * GLA_HEADS, GLA_DK, GLA_DV), F32)],
        compiler_params=_params("parallel", "arbitrary"),
        name=name,
    )(*ins)


def _gla_out_ffn_kernel(*refs, o_firsts, n_tiles):
    ns = len(o_firsts)
    o_refs = refs[:ns]
    r_ref, h_ref, om_ref, rm_ref, hm_ref, hn_ref, wo_ref = refs[ns:ns + 7]
    ffn_w = refs[ns + 7:ns + 11]
    y_ref, ym_ref = refs[ns + 11:ns + 13]
    ffn_scr = refs[ns + 13:]
    _ffn_stage_weights(pl.program_id(0), ffn_w, ffn_scr)
    i = pl.program_id(0) - N_WSTEPS

    def body(o, r, h):
        return _ffn_compute(_gla_out_compute(o, r, h, hn_ref, wo_ref), ffn_w[0], ffn_scr)

    @pl.when(_is_tile(i, n_tiles))
    def _():
        y_ref[...] = body(_select(i, o_refs, o_firsts), r_ref[...], h_ref[...])

    @pl.when(i == n_tiles)
    def _():
        ym_ref[...] = body(om_ref[...], rm_ref[...], hm_ref[...])


def _gla_out_ffn(o_segs, r, h, o_meta, r_meta, h_meta, out, ffn):
    n_tiles = sum(s[2] for s in o_segs)
    n_mt = h_meta.shape[0]
    in_specs = [_tiles((TM, GLA_VAL), 0, f, c) for _, f, c in o_segs]
    in_specs += [_tiles((TM, GLA_VAL), 0, 0, n_tiles), _tiles((TM, D_MODEL), 0, 0, n_tiles),
                 _whole((n_mt, GLA_VAL)), _whole((n_mt, GLA_VAL)), _whole((n_mt, D_MODEL))]
    in_specs += _layer_specs(out) + _ffn_specs(ffn)
    return pl.pallas_call(
        functools.partial(_gla_out_ffn_kernel, o_firsts=tuple(s[1] for s in o_segs), n_tiles=n_tiles),
        grid=(N_WSTEPS + n_tiles + 1,),
        in_specs=in_specs,
        out_specs=[_tiles((TM, D_MODEL), 0, 0, n_tiles), _whole((n_mt, D_MODEL))],
        out_shape=[jax.ShapeDtypeStruct(h.shape, F32), jax.ShapeDtypeStruct(h_meta.shape, F32)],
        scratch_shapes=_ffn_scratch(),
        compiler_params=_params("arbitrary"),
        name="gla_out_ffn",
    )(*[s[0] for s in o_segs], r, h, o_meta, r_meta, h_meta, *_arrays(out), *_arrays(ffn))


def _ffn_mla_proj_kernel(*refs, n_tiles, first_tiles):
    x_ref, cos_ref, sin_ref, xm_ref, cosm_ref, sinm_ref = refs[:6]
    ffn_w = refs[6:10]
    proj_w = refs[10:17]
    y_ref, q_out, kc_out, ckv_a, kr_a, ckv_b, kr_b, vt_out = refs[17:25]
    ym_ref, qm_out, kcm_out, ckvm_out, krm_out = refs[25:30]
    ffn_scr = refs[30:]
    _ffn_stage_weights(pl.program_id(0), ffn_w, ffn_scr)
    i = pl.program_id(0) - N_WSTEPS

    @pl.when(_is_tile(i, n_tiles))
    def _():
        y = _ffn_compute(x_ref[...], ffn_w[0], ffn_scr)
        y_ref[...] = y
        ckv, kr = _mla_proj_compute(y, cos_ref[...], sin_ref[...], *proj_w, q_out, kc_out, vt_out)

        @pl.when(i < first_tiles)
        def _():
            ckv_a[...] = ckv
            kr_a[...] = kr

        @pl.when(i >= first_tiles)
        def _():
            ckv_b[...] = ckv
            kr_b[...] = kr

    @pl.when(i == n_tiles)
    def _():
        y = _ffn_compute(xm_ref[...], ffn_w[0], ffn_scr)
        ym_ref[...] = y
        ckv, kr = _mla_proj_compute(y, cosm_ref[...], sinm_ref[...], *proj_w, qm_out, kcm_out, None)
        ckvm_out[...] = ckv
        krm_out[...] = kr


def _ffn_mla_proj(x, x_meta, cos_t, sin_t, table_tile, cos_m, sin_m, ffn, proj_w, first_tiles):
    n_main = x.shape[0]
    n_tiles = n_main // TM
    n_mt = x_meta.shape[0]
    kt_per_tile = TM // TQ
    rest_tiles = n_tiles - first_tiles

    def rows(w):
        return _tiles((TM, w), 0, 0, n_tiles)

    table = pl.BlockSpec((TM, LANES), lambda i: (table_tile(jnp.clip(i - N_WSTEPS, 0, n_tiles - 1)), 0))
    in_specs = [rows(D_MODEL), table, table, _whole((n_mt, D_MODEL)), _whole((n_mt, LANES)),
                _whole((n_mt, LANES))] + _ffn_specs(ffn)
    in_specs += [_resident(w.shape) for w in proj_w]
    return pl.pallas_call(
        functools.partial(_ffn_mla_proj_kernel, n_tiles=n_tiles, first_tiles=first_tiles),
        grid=(N_WSTEPS + n_tiles + 1,),
        in_specs=in_specs,
        out_specs=[rows(D_MODEL), _tiles((MLA_HEADS, TM, QK_W), 1, 0, n_tiles), rows(QK_W),
                   _tiles((TM, MLA_KV_RANK), 0, 0, first_tiles), _tiles((TM, MLA_ROPE), 0, 0, first_tiles),
                   _tiles((TM, MLA_KV_RANK), 0, first_tiles, rest_tiles),
                   _tiles((TM, MLA_ROPE), 0, first_tiles, rest_tiles),
                   _tiles((kt_per_tile, VT_ROWS, TQ), 0, 0, n_tiles),
                   _whole((n_mt, D_MODEL)), _whole((MLA_HEADS, n_mt, QK_W)), _whole((n_mt, QK_W)),
                   _whole((n_mt, MLA_KV_RANK)), _whole((n_mt, MLA_ROPE))],
        out_shape=[jax.ShapeDtypeStruct((n_main, D_MODEL), F32),
                   jax.ShapeDtypeStruct((MLA_HEADS, n_main, QK_W), BF16),
                   jax.ShapeDtypeStruct((n_main, QK_W), BF16),
                   jax.ShapeDtypeStruct((first_tiles * TM, MLA_KV_RANK), F32),
                   jax.ShapeDtypeStruct((first_tiles * TM, MLA_ROPE), F32),
                   jax.ShapeDtypeStruct((rest_tiles * TM, MLA_KV_RANK), F32),
                   jax.ShapeDtypeStruct((rest_tiles * TM, MLA_ROPE), F32),
                   jax.ShapeDtypeStruct((n_main // TQ, VT_ROWS, TQ), BF16),
                   jax.ShapeDtypeStruct((n_mt, D_MODEL), F32),
                   jax.ShapeDtypeStruct((MLA_HEADS, n_mt, QK_W), BF16),
                   jax.ShapeDtypeStruct((n_mt, QK_W), BF16),
                   jax.ShapeDtypeStruct((n_mt, MLA_KV_RANK), F32),
                   jax.ShapeDtypeStruct((n_mt, MLA_ROPE), F32)],
        scratch_shapes=_ffn_scratch(),
        compiler_params=_params("arbitrary"),
        name="ffn_mla_proj",
    )(x, cos_t, sin_t, x_meta, cos_m, sin_m, *_arrays(ffn), *proj_w)


def _attn_prompt_kernel(q_ref, kf_ref, km_ref, vtf_ref, vtm_ref, o_ref,
                        s_scr, sm_scr, m_scr, acc_scr):
    i = pl.program_id(1)
    cols = MLA_HEADS * TQ
    q = q_ref[...].reshape(cols, QK_W)

    def consume(parts):
        m_cur = None
        for s, _ in parts:
            mx = jnp.max(s, axis=0, keepdims=True)
            m_cur = mx if m_cur is None else jnp.maximum(m_cur, mx)
        m_old = m_scr[...]
        m_new = jnp.maximum(m_old, m_cur)
        alpha = jnp.exp2(m_old - m_new)
        pv = None
        for s, vt in parts:
            term = _dot(vt, jnp.exp2(s - m_new).astype(BF16))
            pv = term if pv is None else pv + term
        acc_scr[...] = alpha * acc_scr[...] + pv
        m_scr[...] = m_new

    m_scr[...] = jnp.full(m_scr.shape, -jnp.inf, F32)
    acc_scr[...] = jnp.zeros(acc_scr.shape, F32)
    s_first = _dot_nt(jnp.concatenate([kf_ref[0], km_ref[...]], axis=0), q)
    s_scr[0] = s_first[:TQ]
    sm_scr[...] = s_first[TQ:]

    def pair(jj, carry):
        j = 2 * jj
        s_scr[1] = _dot_nt(kf_ref[j + 1], q)
        consume([(s_scr[0], vtf_ref[j])])
        s_scr[0] = _dot_nt(kf_ref[j + 2], q)
        consume([(s_scr[1], vtf_ref[j + 1])])
        return carry

    lax.fori_loop(0, lax.shift_right_logical(i, 1), pair, 0)

    @pl.when((i & 1) == 1)
    def _():
        s_scr[1] = _dot_nt(kf_ref[i], q)
        consume([(s_scr[0], vtf_ref[i - 1])])

    key = lax.broadcasted_iota(jnp.int32, (TQ, cols), 0)
    qry = lax.broadcasted_iota(jnp.int32, (TQ, cols), 1) % TQ
    visible = (key // CHUNK) <= (qry // CHUNK)
    s_own = jnp.where(visible, s_scr[i & 1], -jnp.inf)
    consume([(s_own, vtf_ref[i]), (sm_scr[...], vtm_ref[0])])

    for h in range(MLA_HEADS):
        cs = slice(h * TQ, (h + 1) * TQ)
        total = acc_scr[MLA_KV_RANK:MLA_KV_RANK + 1, cs]
        o_ref[h] = (acc_scr[:MLA_KV_RANK, cs] / total).T.astype(BF16)


def _attn_prompt(q, k_tiles, kc_meta, vt_tiles, vt_meta, *, n_batch, seq):
    steps = seq // TQ
    cols = MLA_HEADS * TQ
    return pl.pallas_call(
        _attn_prompt_kernel,
        grid=(n_batch, steps),
        in_specs=[pl.BlockSpec((MLA_HEADS, TQ, QK_W), lambda b, i: (0, b * steps + i, 0)),
                  pl.BlockSpec((steps, TQ, QK_W), lambda b, i: (b, 0, 0)),
                  pl.BlockSpec((N_META, QK_W), lambda b, i: (b, 0)),
                  pl.BlockSpec((steps, VT_ROWS, TQ), lambda b, i: (b, 0, 0)),
                  pl.BlockSpec((1, VT_ROWS, N_META), lambda b, i: (b, 0, 0))],
        out_specs=pl.BlockSpec((MLA_HEADS, TQ, MLA_KV_RANK), lambda b, i: (0, b * steps + i, 0)),
        out_shape=jax.ShapeDtypeStruct((MLA_HEADS, n_batch * seq, MLA_KV_RANK), BF16),
        scratch_shapes=[pltpu.VMEM((2, TQ, cols), F32), pltpu.VMEM((N_META, cols), F32),
                        pltpu.VMEM((1, cols), F32), pltpu.VMEM((VT_ROWS, cols), F32)],
        compiler_params=_params("parallel", "arbitrary"),
        name="mla_attn_prompt",
    )(q, k_tiles, kc_meta, vt_tiles, vt_meta)


def _attn_full_kernel(*refs, n_q, streams, has_past):
    if has_past:
        q_ref, kn_ref, pl_ref, pr_ref, o_ref = refs
    else:
        q_ref, kn_ref, o_ref = refs
    rows = MLA_HEADS * n_q
    blocks = [slice(s * n_q, (s + 1) * n_q) for s in range(streams)]
    qs = [q_ref[:, rs, :].reshape(rows, QK_W) for rs in blocks]
    kns = [kn_ref[rs, :] for rs in blocks]
    s_new = [_dot_nt(q, kn) for q, kn in zip(qs, kns)]
    if has_past:
        lats = [pl_ref[s].astype(BF16) for s in range(streams)]
        rp_ts = [pr_ref[s].astype(BF16) for s in range(streams)]
        s_past = [_dot_nt(q[:, :MLA_KV_RANK], lat) + _dot(q[:, MLA_KV_RANK:MLA_KV_RANK + MLA_ROPE], rp_t)
                  for q, lat, rp_t in zip(qs, lats, rp_ts)]
    for s, rs in enumerate(blocks):
        m = jnp.max(s_new[s], axis=1, keepdims=True)
        if has_past:
            m = jnp.maximum(m, jnp.max(s_past[s], axis=1, keepdims=True))
        p_n = jnp.exp2(s_new[s] - m)
        l = jnp.sum(p_n, axis=1, keepdims=True)
        acc = _dot(p_n.astype(BF16), kns[s][:, :MLA_KV_RANK])
        if has_past:
            p_p = jnp.exp2(s_past[s] - m)
            l = l + jnp.sum(p_p, axis=1, keepdims=True)
            acc = acc + _dot(p_p.astype(BF16), lats[s])
        o_ref[:, rs, :] = (acc / l).astype(BF16).reshape(MLA_HEADS, n_q, MLA_KV_RANK)


def _attn_full(q, kc, past_lat, past_rope_t, *, row0, n_batch, n_q, streams, name):
    assert n_batch % streams == 0
    nb = n_q * streams
    base = row0 // nb
    has_past = past_lat is not None
    ins = [q, kc]
    specs = [pl.BlockSpec((MLA_HEADS, nb, QK_W), lambda b: (0, base + b, 0)),
             pl.BlockSpec((nb, QK_W), lambda b: (base + b, 0))]
    if has_past:
        past = past_lat.shape[1]
        ins += [past_lat, past_rope_t]
        specs += [pl.BlockSpec((streams, past, MLA_KV_RANK), lambda b: (b, 0, 0)),
                  pl.BlockSpec((streams, MLA_ROPE, past), lambda b: (b, 0, 0))]
    return pl.pallas_call(
        functools.partial(_attn_full_kernel, n_q=n_q, streams=streams, has_past=has_past),
        grid=(n_batch // streams,),
        in_specs=specs,
        out_specs=pl.BlockSpec((MLA_HEADS, nb, MLA_KV_RANK), lambda b: (0, b, 0)),
        out_shape=jax.ShapeDtypeStruct((MLA_HEADS, n_batch * n_q, MLA_KV_RANK), BF16),
        compiler_params=_params("parallel"),
        name=name,
    )(*ins)


def _mla_out_ffn_final_kernel(*refs, firsts, n_tiles):
    ns = len(firsts)
    o_refs = refs[:ns]
    h_ref, om_ref, hm_ref, wuv_ref, wo_ref = refs[ns:ns + 5]
    ffn_w = refs[ns + 5:ns + 9]
    fw_ref = refs[ns + 9]
    y_refs = refs[ns + 10:2 * ns + 10]
    ym_ref = refs[2 * ns + 10]
    ffn_scr = refs[2 * ns + 11:]
    _ffn_stage_weights(pl.program_id(0), ffn_w, ffn_scr)
    i = pl.program_id(0) - N_WSTEPS

    def body(o_lat, h):
        y = _mla_out_compute(o_lat, h, wuv_ref, wo_ref)
        return _rms(_ffn_compute(y, ffn_w[0], ffn_scr), fw_ref[...])

    @pl.when(_is_tile(i, n_tiles))
    def _():
        y = body(_select(i, o_refs, firsts), h_ref[...])
        bounds = list(firsts[1:]) + [n_tiles]
        for y_ref, lo, hi in zip(y_refs, firsts, bounds):
            @pl.when((i >= lo) & (i < hi))
            def _(y_ref=y_ref):
                y_ref[...] = y

    @pl.when(i == n_tiles)
    def _():
        ym_ref[...] = body(om_ref[...], hm_ref[...])


def _mla_out_ffn_final(o_segs, h, o_meta, h_meta, w_uv_t, w_out, ffn, final_w):
    n_tiles = sum(s[2] for s in o_segs)
    n_mt = h_meta.shape[0]
    firsts = tuple(s[1] for s in o_segs)
    in_specs = [_tiles((MLA_HEADS, TM, MLA_KV_RANK), 1, f, c) for _, f, c in o_segs]
    in_specs += [_tiles((TM, D_MODEL), 0, 0, n_tiles), _whole(o_meta.shape), _whole(h_meta.shape),
                 _resident(w_uv_t.shape), _resident(w_out.shape)]
    in_specs += _ffn_specs(ffn) + [_resident((1, D_MODEL))]
    return pl.pallas_call(
        functools.partial(_mla_out_ffn_final_kernel, firsts=firsts, n_tiles=n_tiles),
        grid=(N_WSTEPS + n_tiles + 1,),
        in_specs=in_specs,
        out_specs=[_tiles((TM, D_MODEL), 0, f, c) for _, f, c in o_segs] + [_whole((n_mt, D_MODEL))],
        out_shape=[jax.ShapeDtypeStruct((c * TM, D_MODEL), F32) for _, _, c in o_segs]
        + [jax.ShapeDtypeStruct((n_mt, D_MODEL), F32)],
        scratch_shapes=_ffn_scratch(),
        compiler_params=_params("arbitrary"),
        name="mla_out_ffn_final",
    )(*[s[0] for s in o_segs], h, o_meta, h_meta, w_uv_t, w_out, *_arrays(ffn), final_w.reshape(1, D_MODEL))


def _rope_tables(pos):
    half = MLA_ROPE // 2
    lane = jnp.arange(LANES)
    inv = ROPE_THETA ** (-(lane % half).astype(F32) / half)
    sign = jnp.where((lane // half) % 2 == 0, -1.0, 1.0).astype(F32)
    ang = pos[:, None] * inv[None, :]
    return jnp.cos(ang), jnp.sin(ang) * sign[None, :]


def kernel(x_prompt, x_sample, state_gla, cache_mla_latent, cache_mla_rope, meta_tokens, ffn1_norm, ffn1_w_gate, ffn1_w_up, ffn1_w_down, mix_norm, gla_w_in, gla_w_gate_up, gla_b_gate, gla_head_norm, gla_w_out, mla_w_down, mla_q_norm, mla_w_uq, mla_kv_norm, mla_w_uk, mla_w_uv, mla_w_out, ffn2_norm, ffn2_w_gate, ffn2_w_up, ffn2_w_down, final_norm):
    bp, seq, _ = x_prompt.shape
    bs, ls, _ = x_sample.shape
    past = cache_mla_latent.shape[2]
    n_fr = bp * seq
    n_sm = bs * ls
    n_mt = bp * N_META
    assert ffn1_norm.shape[0] == 2
    assert n_fr % TM == 0 and n_sm % TM == 0 and TM % TQ == 0 and seq % TQ == 0 and TQ % CHUNK == 0
    assert past % CHUNK == 0 and ls <= CHUNK and N_META <= CHUNK
    fr_tiles = n_fr // TM
    sm_tiles = n_sm // TM

    def vec(p):
        return p.reshape(p.shape[0], 1, p.shape[1])

    ffn1 = (vec(ffn1_norm), ffn1_w_gate, ffn1_w_up, ffn1_w_down)
    ffn2 = (vec(ffn2_norm), ffn2_w_gate, ffn2_w_up, ffn2_w_down)

    def at(params, layer):
        return [(p, layer) for p in params]

    x_meta = jnp.tile(meta_tokens.astype(F32), (bp, 1))
    gla_proj = [(vec(mix_norm), 0)] + at((jnp.swapaxes(gla_w_in, 1, 2).astype(BF16), gla_w_gate_up.astype(BF16),
                                          vec(gla_b_gate)), 0)
    (h, q, k, v, r, gl, h_m, q_m, k_m, v_m, r_m, gl_m) = _ffn_gla_proj(
        [(x_prompt.reshape(n_fr, D_MODEL), 0, fr_tiles), (x_sample.reshape(n_sm, D_MODEL), fr_tiles, sm_tiles)],
        x_meta, at(ffn1, 0), gla_proj)
    o_m, s_meta = _gla_scan(q_m, k_m, v_m, gl_m, None, row0=0, n_batch=bp, n_steps=1,
                            chunk=N_META, n_chunks=1, streams=bp, name="gla_scan_meta")
    o_f, s_p = _gla_scan(q, k, v, gl, s_meta, row0=0, n_batch=bp, n_steps=seq // GLA_STEP_ROWS,
                         chunk=CHUNK, n_chunks=GLA_STEP_ROWS // CHUNK, name="gla_scan_frames")
    o_s, s_s = _gla_scan(q, k, v, gl, state_gla.reshape(state_gla.shape[1:]), row0=n_fr, n_batch=bs,
                         n_steps=1, chunk=ls, n_chunks=1, streams=TQ // ls, name="gla_scan_sample")
    h, h_m = _gla_out_ffn([(o_f, 0, fr_tiles), (o_s, fr_tiles, sm_tiles)], r, h, o_m, r_m, h_m,
                          at((vec(gla_head_norm), gla_w_out.astype(BF16)), 0), at(ffn2, 0))

    seq_blocks = seq // TM
    cos_t, sin_t = _rope_tables(jnp.concatenate([N_META + jnp.arange(seq, dtype=F32),
                                                 jnp.tile(past + jnp.arange(ls, dtype=F32), TM // ls)]))
    cos_m, sin_m = _rope_tables(jnp.tile(jnp.arange(N_META, dtype=F32), bp))

    def table_tile(i):
        return jnp.where(i < fr_tiles, i % seq_blocks, seq_blocks)
    w_down = mla_w_down[0]
    w_down = jnp.concatenate([w_down, w_down[:, -MLA_ROPE:]], axis=1).astype(BF16)
    w_uq = mla_w_uq[0].reshape(MLA_Q_RANK, MLA_HEADS, MLA_NOPE + MLA_ROPE)
    w_qn = w_uq[:, :, :MLA_NOPE].reshape(MLA_Q_RANK, MLA_HEADS * MLA_NOPE).astype(BF16)
    w_qr = w_uq[:, :, MLA_NOPE:].reshape(MLA_Q_RANK, MLA_HEADS * MLA_ROPE).astype(BF16)
    w_uk_t = jnp.transpose(mla_w_uk[0], (1, 2, 0)).astype(BF16)
    w_uv_t = jnp.transpose(mla_w_uv[0], (1, 0, 2)).astype(BF16)
    mla_proj_w = (mix_norm[1].reshape(1, D_MODEL), w_down, mla_q_norm[0].reshape(1, MLA_Q_RANK), w_qn, w_qr,
                  mla_kv_norm[0].reshape(1, MLA_KV_RANK), w_uk_t)
    (h, qa, kc, ckv_f, kr_f, ckv_s, kr_s, vt, h_m, qa_m, kc_m, ckv_m, kr_m) = _ffn_mla_proj(
        h, h_m, cos_t, sin_t, table_tile, cos_m, sin_m, at(ffn1, 1), mla_proj_w, fr_tiles)
    k_tiles = kc.reshape((n_fr + n_sm) // TQ, TQ, QK_W)
    vt_meta = jnp.swapaxes(kc_m[:, :MLA_KV_RANK].reshape(bp, N_META, MLA_KV_RANK), 1, 2)
    vt_meta = jnp.concatenate([vt_meta, jnp.ones((bp, VT_ROWS - MLA_KV_RANK, N_META), BF16)], axis=1)
    ol_f = _attn_prompt(qa, k_tiles, kc_m, vt, vt_meta, n_batch=bp, seq=seq)
    past_rope_t = jnp.swapaxes(cache_mla_rope.reshape(bs, past, MLA_ROPE), 1, 2)
    ol_s = _attn_full(qa, kc, cache_mla_latent.reshape(bs, past, MLA_KV_RANK), past_rope_t,
                      row0=n_fr, n_batch=bs, n_q=ls, streams=2, name="mla_attn_sample")
    ol_m = _attn_full(qa_m, kc_m, None, None, row0=0, n_batch=bp, n_q=N_META, streams=bp,
                      name="mla_attn_meta")
    y_prompt, y_sample, _ = _mla_out_ffn_final(
        [(ol_f, 0, fr_tiles), (ol_s, fr_tiles, sm_tiles)], h, ol_m, h_m, w_uv_t, mla_w_out[0].astype(BF16),
        at(ffn2, 1), final_norm)

    def seq_major(t_frames, t_meta, width):
        return jnp.concatenate([t_meta.reshape(bp, N_META, width), t_frames.reshape(bp, seq, width)], axis=1)

    return (y_prompt.reshape(bp, seq, D_MODEL), y_sample.reshape(bs, ls, D_MODEL),
            s_p[None], s_s[None],
            seq_major(ckv_f, ckv_m, MLA_KV_RANK)[None], seq_major(kr_f, kr_m, MLA_ROPE)[None],
            ckv_s.reshape(1, bs, ls, MLA_KV_RANK), kr_s.reshape(1, bs, ls, MLA_ROPE))
```

```python
import functools

import jax
import jax.numpy as jnp
from jax import lax
from jax.experimental import pallas as pl
from jax.experimental.pallas import tpu as pltpu

F32 = jnp.float32
BF16 = jnp.bfloat16

D_MODEL = 1024
D_FF = 2816
RMS_EPS = 1e-6
N_META = 16
CHUNK = 64

GLA_HEADS = 4
GLA_DK = 128
GLA_DV = 256
GLA_KEY = GLA_HEADS * GLA_DK
GLA_VAL = GLA_HEADS * GLA_DV
GLA_RANK = 16
GLA_GATE_NORMALIZER = 16.0

MLA_HEADS = 8
MLA_Q_RANK = 384
MLA_KV_RANK = 256
MLA_NOPE = 128
MLA_ROPE = 64
ROPE_THETA = 10000.0
LOG2_E = 1.4426950408889634

LANES = 128
BF16_ROWS = 16
VMEM_LIMIT = 58 * 1024 * 1024
QK_W = MLA_KV_RANK + LANES
VT_ROWS = MLA_KV_RANK + BF16_ROWS

TM = 512
FF_CHUNK = 256
TQ = 256
GLA_GROUP_ROWS = 256
GLA_STEP_ROWS = 512
N_WSTEPS = D_FF // FF_CHUNK

NT_DIMS = (((1,), (1,)), ((), ()))
TN_DIMS = (((0,), (0,)), ((), ()))


def _dot(a, b):
    return jnp.dot(a, b, preferred_element_type=F32)


def _dot_nt(a, b):
    return lax.dot_general(a, b, NT_DIMS, preferred_element_type=F32)


def _dot_tn(a, b):
    return lax.dot_general(a, b, TN_DIMS, preferred_element_type=F32)


def _rms(x, w):
    return x * lax.rsqrt(jnp.mean(x * x, axis=-1, keepdims=True) + RMS_EPS) * w


def _silu(x):
    return x * jax.nn.sigmoid(x)


def _params(*sem):
    return pltpu.CompilerParams(dimension_semantics=sem, vmem_limit_bytes=VMEM_LIMIT)


def _resident(shape):
    zeros = (0,) * len(shape)
    return pl.BlockSpec(shape, lambda *_: zeros, pipeline_mode=pl.Buffered(1))


def _whole(shape):
    zeros = (0,) * len(shape)
    return pl.BlockSpec(shape, lambda *_: zeros)


def _tiles(block, row_axis=0, first=0, count=None):
    nd = len(block)

    def imap(i):
        t = i - N_WSTEPS - first
        if count is not None:
            t = jnp.clip(t, 0, count - 1)
        idx = [0] * nd
        idx[row_axis] = t
        return tuple(idx)

    return pl.BlockSpec(block, imap)


def _select(i, refs, firsts):
    val = refs[0][...]
    for ref, first in zip(refs[1:], firsts[1:]):
        val = jnp.where(i >= first, ref[...], val)
    return val


def _ffn_stage_weights(step, ffn_w, ffn_scr):
    _, wg_ref, wu_ref, wd_ref = ffn_w
    _, wg_scr, wu_scr, wd_scr = ffn_scr

    @pl.when(step < N_WSTEPS)
    def _():
        wg_scr[step] = wg_ref[...].astype(BF16)
        wu_scr[step] = wu_ref[...].astype(BF16)
        wd_scr[step] = wd_ref[...].astype(BF16)


def _ffn_compute(x, nw_ref, ffn_scr):
    h_scr, wg_scr, wu_scr, wd_scr = ffn_scr
    m = x.shape[0]
    xn = _rms(x, nw_ref[...]).astype(BF16)
    for c in range(N_WSTEPS):
        g = _dot(xn, wg_scr[c])
        u = _dot(xn, wu_scr[c])
        h_scr[:m, c * FF_CHUNK:(c + 1) * FF_CHUNK] = (_silu(g) * u).astype(BF16)
    return x + 0.5 * _dot(h_scr[:m, :], wd_scr[...].reshape(D_FF, D_MODEL))


def _gla_proj_compute(x, nw_ref, w_ref, wgu_ref, bg_ref):
    c_v = 2 * GLA_KEY
    c_r = c_v + GLA_VAL
    c_g = c_r + GLA_VAL
    xn = _rms(x, nw_ref[...]).astype(BF16)
    q = _dot_nt(xn, w_ref[0:GLA_KEY, :]).astype(BF16)
    k = _dot_nt(xn, w_ref[GLA_KEY:c_v, :]).astype(BF16)
    v = _dot_nt(xn, w_ref[c_v:c_r, :]).astype(BF16)
    r = _dot_nt(xn, w_ref[c_r:c_g, :]).astype(BF16)
    gd = _dot_nt(xn, w_ref[c_g:, :])
    z = _dot(gd.astype(BF16), wgu_ref[...]) + bg_ref[...]
    log_sig = jnp.minimum(z, 0.0) - jnp.log1p(jnp.exp(-jnp.abs(z)))
    return q, k, v, r, log_sig * (1.0 / GLA_GATE_NORMALIZER)


def _gla_out_compute(o, r, h, hn_ref, wo_ref):
    parts = []
    for hd in range(GLA_HEADS):
        vs = slice(hd * GLA_DV, (hd + 1) * GLA_DV)
        on = _rms(o[:, vs].astype(F32), hn_ref[...])
        parts.append((on * _silu(r[:, vs].astype(F32))).astype(BF16))
    return h + _dot(jnp.concatenate(parts, axis=1), wo_ref[...])


def _mla_proj_compute(x, cos_t, sin_t, nw_ref, wd_ref, qn_ref, wqn_ref, wqr_ref, kvn_ref, wuk_ref,
                      q_out, kc_out, vt_out):
    m = x.shape[0]
    scale = (MLA_NOPE + MLA_ROPE) ** -0.5 * LOG2_E
    xn = _rms(x, nw_ref[...]).astype(BF16)
    xd = _dot(xn, wd_ref[...])
    cqn = _rms(xd[:, :MLA_Q_RANK], qn_ref[...]).astype(BF16)
    ckv = _rms(xd[:, MLA_Q_RANK:MLA_Q_RANK + MLA_KV_RANK], kvn_ref[...])
    lane = lax.broadcasted_iota(jnp.int32, (m, LANES), 1)
    first_half = (lane & (MLA_ROPE // 2)) == 0

    def rope(t):
        swapped = jnp.where(first_half, pltpu.roll(t, LANES - MLA_ROPE // 2, 1),
                            pltpu.roll(t, MLA_ROPE // 2, 1))
        return t * cos_t + swapped * sin_t

    kr2 = rope(xd[:, MLA_Q_RANK + MLA_KV_RANK:])
    kc_out[:, :MLA_KV_RANK] = ckv.astype(BF16)
    kc_out[:, MLA_KV_RANK:] = jnp.where(lane < MLA_ROPE, kr2, 0.0).astype(BF16)
    if vt_out is not None:
        for j in range(m // TQ):
            vt_out[j, :MLA_KV_RANK, :] = ckv[j * TQ:(j + 1) * TQ, :].T.astype(BF16)
            vt_out[j, MLA_KV_RANK:, :] = jnp.ones((VT_ROWS - MLA_KV_RANK, TQ), BF16)

    qn = _dot(cqn, wqn_ref[...])
    qr = _dot(cqn, wqr_ref[...])
    for c in range(MLA_HEADS // 2):
        rr = rope(qr[:, c * LANES:(c + 1) * LANES]) * scale
        for e in range(2):
            hd = 2 * c + e
            ql = _dot(qn[:, hd * MLA_NOPE:(hd + 1) * MLA_NOPE].astype(BF16), wuk_ref[hd]) * scale
            q_out[hd, :, :MLA_KV_RANK] = ql.astype(BF16)
            rot = rr if e == 0 else pltpu.roll(rr, MLA_ROPE, 1)
            q_out[hd, :, MLA_KV_RANK:] = rot.astype(BF16)
    return ckv, kr2[:, :MLA_ROPE]


def _mla_out_compute(o_lat, h, wuv_ref, wo_ref):
    parts = [_dot(o_lat[hd], wuv_ref[hd]).astype(BF16) for hd in range(MLA_HEADS)]
    return h + _dot(jnp.concatenate(parts, axis=1), wo_ref[...])


def _layer(shape, layer):
    zeros = (0,) * len(shape)
    return pl.BlockSpec((None,) + tuple(shape), lambda *_: (layer,) + zeros, pipeline_mode=pl.Buffered(1))


def _layer_specs(params):
    return [_layer(a.shape[1:], layer) for a, layer in params]


def _arrays(params):
    return [a for a, _ in params]


def _ffn_specs(ffn):
    (nw, l_n), (_, l_g), (_, l_u), (_, l_d) = ffn

    def slab(i):
        return jnp.minimum(i, N_WSTEPS - 1)

    return [_layer(nw.shape[1:], l_n),
            pl.BlockSpec((None, D_MODEL, FF_CHUNK), lambda i: (l_g, 0, slab(i))),
            pl.BlockSpec((None, D_MODEL, FF_CHUNK), lambda i: (l_u, 0, slab(i))),
            pl.BlockSpec((None, FF_CHUNK, D_MODEL), lambda i: (l_d, slab(i), 0))]


def _ffn_scratch():
    return [pltpu.VMEM((TM, D_FF), BF16),
            pltpu.VMEM((N_WSTEPS, D_MODEL, FF_CHUNK), BF16), pltpu.VMEM((N_WSTEPS, D_MODEL, FF_CHUNK), BF16),
            pltpu.VMEM((N_WSTEPS, FF_CHUNK, D_MODEL), BF16)]


def _is_tile(i, n_tiles):
    return (i >= 0) & (i < n_tiles)


def _ffn_gla_proj_kernel(*refs, x_firsts, n_tiles):
    ns = len(x_firsts)
    x_refs, xm_ref = refs[:ns], refs[ns]
    ffn_w = refs[ns + 1:ns + 5]
    proj_w = refs[ns + 5:ns + 9]
    outs = refs[ns + 9:ns + 15]
    outs_m = refs[ns + 15:ns + 21]
    ffn_scr = refs[ns + 21:]
    _ffn_stage_weights(pl.program_id(0), ffn_w, ffn_scr)
    i = pl.program_id(0) - N_WSTEPS

    def body(x, o):
        y = _ffn_compute(x, ffn_w[0], ffn_scr)
        o[0][...] = y
        for ref, val in zip(o[1:], _gla_proj_compute(y, *proj_w)):
            ref[...] = val

    @pl.when(_is_tile(i, n_tiles))
    def _():
        body(_select(i, x_refs, x_firsts), outs)

    @pl.when(i == n_tiles)
    def _():
        body(xm_ref[...], outs_m)


def _ffn_gla_proj(x_segs, x_meta, ffn, proj):
    n_tiles = sum(s[2] for s in x_segs)
    n_main = n_tiles * TM
    n_mt = x_meta.shape[0]
    widths = [(D_MODEL, F32), (GLA_KEY, BF16), (GLA_KEY, BF16), (GLA_VAL, BF16), (GLA_VAL, BF16),
              (GLA_KEY, F32)]
    in_specs = [_tiles((TM, D_MODEL), 0, f, c) for _, f, c in x_segs]
    in_specs += [_whole((n_mt, D_MODEL))] + _ffn_specs(ffn) + _layer_specs(proj)
    return pl.pallas_call(
        functools.partial(_ffn_gla_proj_kernel, x_firsts=tuple(s[1] for s in x_segs), n_tiles=n_tiles),
        grid=(N_WSTEPS + n_tiles + 1,),
        in_specs=in_specs,
        out_specs=[_tiles((TM, w), 0, 0, n_tiles) for w, _ in widths]
        + [_whole((n_mt, w)) for w, _ in widths],
        out_shape=[jax.ShapeDtypeStruct((n_main, w), d) for w, d in widths]
        + [jax.ShapeDtypeStruct((n_mt, w), d) for w, d in widths],
        scratch_shapes=_ffn_scratch(),
        compiler_params=_params("arbitrary"),
        name="ffn_gla_proj",
    )(*[s[0] for s in x_segs], x_meta, *_arrays(ffn), *_arrays(proj))


def _gla_scan_kernel(*refs, chunk, n_chunks, streams, group, n_steps, has_s0):
    q_ref, k_ref, v_ref, gl_ref = refs[:4]
    s0_ref = refs[4] if has_s0 else None
    o_ref, s_out_ref, st_scr = refs[4 + has_s0:]
    step = pl.program_id(1)

    @pl.when(step == 0)
    def _():
        for s in range(streams):
            for h in range(GLA_HEADS):
                if has_s0:
                    st_scr[s * GLA_HEADS + h] = s0_ref[s, h]
                else:
                    st_scr[s * GLA_HEADS + h] = jnp.zeros((GLA_DK, GLA_DV), F32)

    total = n_chunks * streams
    g_rows = chunk * group
    row = lax.broadcasted_iota(jnp.int32, (g_rows, g_rows), 0)
    col = lax.broadcasted_iota(jnp.int32, (g_rows, g_rows), 1)
    causal = ((row // chunk) == (col // chunk)) & (row >= col)
    tri = jnp.where(causal, 1.0, 0.0).astype(BF16)
    scale = GLA_DK ** -0.5
    groups = [slice(g * g_rows, (g + 1) * g_rows) for g in range(total // group)]
    in_group = [slice(c * chunk, (c + 1) * chunk) for c in range(group)]
    key_slices = [slice(h * GLA_DK, (h + 1) * GLA_DK) for h in range(GLA_HEADS)]
    val_slices = [slice(h * GLA_DV, (h + 1) * GLA_DV) for h in range(GLA_HEADS)]

    gcs = []
    for rows in groups:
        glog = gl_ref[rows, :]
        hi = glog.astype(BF16)
        lo = (glog - hi.astype(F32)).astype(BF16)
        gcs.append(_dot(tri, hi) + _dot(tri, lo))
    sub = lax.broadcasted_iota(jnp.int32, (GLA_DK, GLA_KEY), 0)
    dec_rows = jnp.zeros((GLA_DK, GLA_KEY), F32)
    qgs, kgs, kds = [], [], []
    for g, (rows, gc) in enumerate(zip(groups, gcs)):
        glast = [gc[rs.stop - 1:rs.stop, :] for rs in in_group]
        g_end = jnp.concatenate([jnp.broadcast_to(t, (chunk, GLA_KEY)) for t in glast], axis=0)
        for c, t in enumerate(glast):
            dec_rows = jnp.where(sub == g * group + c, jnp.exp(t), dec_rows)
        q = q_ref[rows, :].astype(F32) * scale
        k = k_ref[rows, :].astype(F32)
        qgs.append((q * jnp.exp(gc)).astype(BF16))
        kgs.append((k * jnp.exp(-gc)).astype(BF16))
        kds.append((k * jnp.exp(g_end - gc)).astype(BF16))

    values = [[v_ref[rows, vs] for vs in val_slices] for rows in groups]
    scores = [[_dot_nt(qg[:, ks], kg[:, ks]) for ks in key_slices] for qg, kg in zip(qgs, kgs)]
    updates = [[_dot_tn(kds[g][rs, ks], values[g][h][rs, :]) for g in range(len(groups)) for rs in in_group]
               for h, ks in enumerate(key_slices)]
    intra = [[_dot(jnp.where(causal, a, 0.0).astype(BF16), vh) for a, vh in zip(sg, vg)]
             for sg, vg in zip(scores, values)]
    for h, ks in enumerate(key_slices):
        dec_cols = dec_rows[:, ks].T
        states = []
        for s in range(streams):
            state = st_scr[s * GLA_HEADS + h]
            for c in range(s * n_chunks, (s + 1) * n_chunks):
                states.append(state.astype(BF16))
                state = state * dec_cols[:, c:c + 1] + updates[h][c]
            st_scr[s * GLA_HEADS + h] = state
        for g, rows in enumerate(groups):
            for c, rs in enumerate(in_group):
                out_rows = slice(rows.start + rs.start, rows.start + rs.stop)
                o_ref[out_rows, val_slices[h]] = (
                    intra[g][h][rs, :] + _dot(qgs[g][rs, ks], states[g * group + c])).astype(BF16)

    @pl.when(step == n_steps - 1)
    def _():
        for s in range(streams):
            for h in range(GLA_HEADS):
                s_out_ref[s, h] = st_scr[s * GLA_HEADS + h]


def _gla_scan(q, k, v, gl, s0, *, row0, n_batch, n_steps, chunk, n_chunks, name, streams=1):
    assert n_batch % streams == 0 and (streams == 1 or n_steps == 1)
    rb = chunk * n_chunks * streams
    base = row0 // rb

    def rmap(b, s):
        return (base + b * n_steps + s, 0)

    def omap(b, s):
        return (b * n_steps + s, 0)

    def smap(b, s):
        return (b, 0, 0, 0)

    ins = [q, k, v, gl]
    specs = [pl.BlockSpec((rb, GLA_KEY), rmap), pl.BlockSpec((rb, GLA_KEY), rmap),
             pl.BlockSpec((rb, GLA_VAL), rmap), pl.BlockSpec((rb, GLA_KEY), rmap)]
    state_block = (streams, GLA_HEADS, GLA_DK, GLA_DV)
    if s0 is not None:
        ins.append(s0)
        specs.append(pl.BlockSpec(state_block, smap))
    group = min(n_chunks * streams, GLA_GROUP_ROWS // chunk)
    assert (n_chunks * streams) % group == 0
    kern = functools.partial(_gla_scan_kernel, chunk=chunk, n_chunks=n_chunks, streams=streams,
                             group=group, n_steps=n_steps, has_s0=s0 is not None)
    return pl.pallas_call(
        kern,
        grid=(n_batch // streams, n_steps),
        in_specs=specs,
        out_specs=[pl.BlockSpec((rb, GLA_VAL), omap), pl.BlockSpec(state_block, smap)],
        out_shape=[jax.ShapeDtypeStruct((n_batch * n_steps * chunk * n_chunks, GLA_VAL), BF16),
                   jax.ShapeDtypeStruct((n_batch,) + state_block[1:], F32)],
        scratch_shapes=[pltpu.VMEM((streams * GLA_HEADS, GLA_DK, GLA_DV), F32)],
        compiler_params=_params("parallel", "arbitrary"),
        name=name,
    )(*ins)


def _gla_out_ffn_kernel(*refs, o_firsts, n_tiles):
    ns = len(o_firsts)
    o_refs = refs[:ns]
    r_ref, h_ref, om_ref, rm_ref, hm_ref, hn_ref, wo_ref = refs[ns:ns + 7]
    ffn_w = refs[ns + 7:ns + 11]
    y_ref, ym_ref = refs[ns + 11:ns + 13]
    ffn_scr = refs[ns + 13:]
    _ffn_stage_weights(pl.program_id(0), ffn_w, ffn_scr)
    i = pl.program_id(0) - N_WSTEPS

    def body(o, r, h):
        return _ffn_compute(_gla_out_compute(o, r, h, hn_ref, wo_ref), ffn_w[0], ffn_scr)

    @pl.when(_is_tile(i, n_tiles))
    def _():
        y_ref[...] = body(_select(i, o_refs, o_firsts), r_ref[...], h_ref[...])

    @pl.when(i == n_tiles)
    def _():
        ym_ref[...] = body(om_ref[...], rm_ref[...], hm_ref[...])


def _gla_out_ffn(o_segs, r, h, o_meta, r_meta, h_meta, out, ffn):
    n_tiles = sum(s[2] for s in o_segs)
    n_mt = h_meta.shape[0]
    in_specs = [_tiles((TM, GLA_VAL), 0, f, c) for _, f, c in o_segs]
    in_specs += [_tiles((TM, GLA_VAL), 0, 0, n_tiles), _tiles((TM, D_MODEL), 0, 0, n_tiles),
                 _whole((n_mt, GLA_VAL)), _whole((n_mt, GLA_VAL)), _whole((n_mt, D_MODEL))]
    in_specs += _layer_specs(out) + _ffn_specs(ffn)
    return pl.pallas_call(
        functools.partial(_gla_out_ffn_kernel, o_firsts=tuple(s[1] for s in o_segs), n_tiles=n_tiles),
        grid=(N_WSTEPS + n_tiles + 1,),
        in_specs=in_specs,
        out_specs=[_tiles((TM, D_MODEL), 0, 0, n_tiles), _whole((n_mt, D_MODEL))],
        out_shape=[jax.ShapeDtypeStruct(h.shape, F32), jax.ShapeDtypeStruct(h_meta.shape, F32)],
        scratch_shapes=_ffn_scratch(),
        compiler_params=_params("arbitrary"),
        name="gla_out_ffn",
    )(*[s[0] for s in o_segs], r, h, o_meta, r_meta, h_meta, *_arrays(out), *_arrays(ffn))


def _ffn_mla_proj_kernel(*refs, n_tiles, first_tiles):
    x_ref, cos_ref, sin_ref, xm_ref, cosm_ref, sinm_ref = refs[:6]
    ffn_w = refs[6:10]
    proj_w = refs[10:17]
    y_ref, q_out, kc_out, ckv_a, kr_a, ckv_b, kr_b, vt_out = refs[17:25]
    ym_ref, qm_out, kcm_out, ckvm_out, krm_out = refs[25:30]
    ffn_scr = refs[30:]
    _ffn_stage_weights(pl.program_id(0), ffn_w, ffn_scr)
    i = pl.program_id(0) - N_WSTEPS

    @pl.when(_is_tile(i, n_tiles))
    def _():
        y = _ffn_compute(x_ref[...], ffn_w[0], ffn_scr)
        y_ref[...] = y
        ckv, kr = _mla_proj_compute(y, cos_ref[...], sin_ref[...], *proj_w, q_out, kc_out, vt_out)

        @pl.when(i < first_tiles)
        def _():
            ckv_a[...] = ckv
            kr_a[...] = kr

        @pl.when(i >= first_tiles)
        def _():
            ckv_b[...] = ckv
            kr_b[...] = kr

    @pl.when(i == n_tiles)
    def _():
        y = _ffn_compute(xm_ref[...], ffn_w[0], ffn_scr)
        ym_ref[...] = y
        ckv, kr = _mla_proj_compute(y, cosm_ref[...], sinm_ref[...], *proj_w, qm_out, kcm_out, None)
        ckvm_out[...] = ckv
        krm_out[...] = kr


def _ffn_mla_proj(x, x_meta, cos_t, sin_t, table_tile, cos_m, sin_m, ffn, proj_w, first_tiles):
    n_main = x.shape[0]
    n_tiles = n_main // TM
    n_mt = x_meta.shape[0]
    kt_per_tile = TM // TQ
    rest_tiles = n_tiles - first_tiles

    def rows(w):
        return _tiles((TM, w), 0, 0, n_tiles)

    table = pl.BlockSpec((TM, LANES), lambda i: (table_tile(jnp.clip(i - N_WSTEPS, 0, n_tiles - 1)), 0))
    in_specs = [rows(D_MODEL), table, table, _whole((n_mt, D_MODEL)), _whole((n_mt, LANES)),
                _whole((n_mt, LANES))] + _ffn_specs(ffn)
    in_specs += [_resident(w.shape) for w in proj_w]
    return pl.pallas_call(
        functools.partial(_ffn_mla_proj_kernel, n_tiles=n_tiles, first_tiles=first_tiles),
        grid=(N_WSTEPS + n_tiles + 1,),
        in_specs=in_specs,
        out_specs=[rows(D_MODEL), _tiles((MLA_HEADS, TM, QK_W), 1, 0, n_tiles), rows(QK_W),
                   _tiles((TM, MLA_KV_RANK), 0, 0, first_tiles), _tiles((TM, MLA_ROPE), 0, 0, first_tiles),
                   _tiles((TM, MLA_KV_RANK), 0, first_tiles, rest_tiles),
                   _tiles((TM, MLA_ROPE), 0, first_tiles, rest_tiles),
                   _tiles((kt_per_tile, VT_ROWS, TQ), 0, 0, n_tiles),
                   _whole((n_mt, D_MODEL)), _whole((MLA_HEADS, n_mt, QK_W)), _whole((n_mt, QK_W)),
                   _whole((n_mt, MLA_KV_RANK)), _whole((n_mt, MLA_ROPE))],
        out_shape=[jax.ShapeDtypeStruct((n_main, D_MODEL), F32),
                   jax.ShapeDtypeStruct((MLA_HEADS, n_main, QK_W), BF16),
                   jax.ShapeDtypeStruct((n_main, QK_W), BF16),
                   jax.ShapeDtypeStruct((first_tiles * TM, MLA_KV_RANK), F32),
                   jax.ShapeDtypeStruct((first_tiles * TM, MLA_ROPE), F32),
                   jax.ShapeDtypeStruct((rest_tiles * TM, MLA_KV_RANK), F32),
                   jax.ShapeDtypeStruct((rest_tiles * TM, MLA_ROPE), F32),
                   jax.ShapeDtypeStruct((n_main // TQ, VT_ROWS, TQ), BF16),
                   jax.ShapeDtypeStruct((n_mt, D_MODEL), F32),
                   jax.ShapeDtypeStruct((MLA_HEADS, n_mt, QK_W), BF16),
                   jax.ShapeDtypeStruct((n_mt, QK_W), BF16),
                   jax.ShapeDtypeStruct((n_mt, MLA_KV_RANK), F32),
                   jax.ShapeDtypeStruct((n_mt, MLA_ROPE), F32)],
        scratch_shapes=_ffn_scratch(),
        compiler_params=_params("arbitrary"),
        name="ffn_mla_proj",
    )(x, cos_t, sin_t, x_meta, cos_m, sin_m, *_arrays(ffn), *proj_w)


def _attn_prompt_kernel(q_ref, kf_ref, km_ref, vtf_ref, vtm_ref, o_ref,
                        s_scr, sm_scr, m_scr, acc_scr):
    i = pl.program_id(1)
    cols = MLA_HEADS * TQ
    q = q_ref[...].reshape(cols, QK_W)

    def consume(parts):
        m_cur = None
        for s, _ in parts:
            mx = jnp.max(s, axis=0, keepdims=True)
            m_cur = mx if m_cur is None else jnp.maximum(m_cur, mx)
        m_old = m_scr[...]
        m_new = jnp.maximum(m_old, m_cur)
        alpha = jnp.exp2(m_old - m_new)
        pv = None
        for s, vt in parts:
            term = _dot(vt, jnp.exp2(s - m_new).astype(BF16))
            pv = term if pv is None else pv + term
        acc_scr[...] = alpha * acc_scr[...] + pv
        m_scr[...] = m_new

    m_scr[...] = jnp.full(m_scr.shape, -jnp.inf, F32)
    acc_scr[...] = jnp.zeros(acc_scr.shape, F32)
    s_first = _dot_nt(jnp.concatenate([kf_ref[0], km_ref[...]], axis=0), q)
    s_scr[0] = s_first[:TQ]
    sm_scr[...] = s_first[TQ:]

    def pair(jj, carry):
        j = 2 * jj
        s_scr[1] = _dot_nt(kf_ref[j + 1], q)
        consume([(s_scr[0], vtf_ref[j])])
        s_scr[0] = _dot_nt(kf_ref[j + 2], q)
        consume([(s_scr[1], vtf_ref[j + 1])])
        return carry

    lax.fori_loop(0, lax.shift_right_logical(i, 1), pair, 0)

    @pl.when((i & 1) == 1)
    def _():
        s_scr[1] = _dot_nt(kf_ref[i], q)
        consume([(s_scr[0], vtf_ref[i - 1])])

    key = lax.broadcasted_iota(jnp.int32, (TQ, cols), 0)
    qry = lax.broadcasted_iota(jnp.int32, (TQ, cols), 1) % TQ
    visible = (key // CHUNK) <= (qry // CHUNK)
    s_own = jnp.where(visible, s_scr[i & 1], -jnp.inf)
    consume([(s_own, vtf_ref[i]), (sm_scr[...], vtm_ref[0])])

    for h in range(MLA_HEADS):
        cs = slice(h * TQ, (h + 1) * TQ)
        total = acc_scr[MLA_KV_RANK:MLA_KV_RANK + 1, cs]
        o_ref[h] = (acc_scr[:MLA_KV_RANK, cs] / total).T.astype(BF16)


def _attn_prompt(q, k_tiles, kc_meta, vt_tiles, vt_meta, *, n_batch, seq):
    steps = seq // TQ
    cols = MLA_HEADS * TQ
    return pl.pallas_call(
        _attn_prompt_kernel,
        grid=(n_batch, steps),
        in_specs=[pl.BlockSpec((MLA_HEADS, TQ, QK_W), lambda b, i: (0, b * steps + i, 0)),
                  pl.BlockSpec((steps, TQ, QK_W), lambda b, i: (b, 0, 0)),
                  pl.BlockSpec((N_META, QK_W), lambda b, i: (b, 0)),
                  pl.BlockSpec((steps, VT_ROWS, TQ), lambda b, i: (b, 0, 0)),
                  pl.BlockSpec((1, VT_ROWS, N_META), lambda b, i: (b, 0, 0))],
        out_specs=pl.BlockSpec((MLA_HEADS, TQ, MLA_KV_RANK), lambda b, i: (0, b * steps + i, 0)),
        out_shape=jax.ShapeDtypeStruct((MLA_HEADS, n_batch * seq, MLA_KV_RANK), BF16),
        scratch_shapes=[pltpu.VMEM((2, TQ, cols), F32), pltpu.VMEM((N_META, cols), F32),
                        pltpu.VMEM((1, cols), F32), pltpu.VMEM((VT_ROWS, cols), F32)],
        compiler_params=_params("parallel", "arbitrary"),
        name="mla_attn_prompt",
    )(q, k_tiles, kc_meta, vt_tiles, vt_meta)


def _attn_full_kernel(*refs, n_q, streams, has_past):
    if has_past:
        q_ref, kn_ref, pl_ref, pr_ref, o_ref = refs
    else:
        q_ref, kn_ref, o_ref = refs
    rows = MLA_HEADS * n_q
    blocks = [slice(s * n_q, (s + 1) * n_q) for s in range(streams)]
    qs = [q_ref[:, rs, :].reshape(rows, QK_W) for rs in blocks]
    kns = [kn_ref[rs, :] for rs in blocks]
    s_new = [_dot_nt(q, kn) for q, kn in zip(qs, kns)]
    if has_past:
        lats = [pl_ref[s].astype(BF16) for s in range(streams)]
        rp_ts = [pr_ref[s].astype(BF16) for s in range(streams)]
        s_past = [_dot_nt(q[:, :MLA_KV_RANK], lat) + _dot(q[:, MLA_KV_RANK:MLA_KV_RANK + MLA_ROPE], rp_t)
                  for q, lat, rp_t in zip(qs, lats, rp_ts)]
    for s, rs in enumerate(blocks):
        m = jnp.max(s_new[s], axis=1, keepdims=True)
        if has_past:
            m = jnp.maximum(m, jnp.max(s_past[s], axis=1, keepdims=True))
        p_n = jnp.exp2(s_new[s] - m)
        l = jnp.sum(p_n, axis=1, keepdims=True)
        acc = _dot(p_n.astype(BF16), kns[s][:, :MLA_KV_RANK])
        if has_past:
            p_p = jnp.exp2(s_past[s] - m)
            l = l + jnp.sum(p_p, axis=1, keepdims=True)
            acc = acc + _dot(p_p.astype(BF16), lats[s])
        o_ref[:, rs, :] = (acc / l).astype(BF16).reshape(MLA_HEADS, n_q, MLA_KV_RANK)


def _attn_full(q, kc, past_lat, past_rope_t, *, row0, n_batch, n_q, streams, name):
    assert n_batch % streams == 0
    nb = n_q * streams
    base = row0 // nb
    has_past = past_lat is not None
    ins = [q, kc]
    specs = [pl.BlockSpec((MLA_HEADS, nb, QK_W), lambda b: (0, base + b, 0)),
             pl.BlockSpec((nb, QK_W), lambda b: (base + b, 0))]
    if has_past:
        past = past_lat.shape[1]
        ins += [past_lat, past_rope_t]
        specs += [pl.BlockSpec((streams, past, MLA_KV_RANK), lambda b: (b, 0, 0)),
                  pl.BlockSpec((streams, MLA_ROPE, past), lambda b: (b, 0, 0))]
    return pl.pallas_call(
        functools.partial(_attn_full_kernel, n_q=n_q, streams=streams, has_past=has_past),
        grid=(n_batch // streams,),
        in_specs=specs,
        out_specs=pl.BlockSpec((MLA_HEADS, nb, MLA_KV_RANK), lambda b: (0, b, 0)),
        out_shape=jax.ShapeDtypeStruct((MLA_HEADS, n_batch * n_q, MLA_KV_RANK), BF16),
        compiler_params=_params("parallel"),
        name=name,
    )(*ins)


def _mla_out_ffn_final_kernel(*refs, firsts, n_tiles):
    ns = len(firsts)
    o_refs = refs[:ns]
    h_ref, om_ref, hm_ref, wuv_ref, wo_ref = refs[ns:ns + 5]
    ffn_w = refs[ns + 5:ns + 9]
    fw_ref = refs[ns + 9]
    y_refs = refs[ns + 10:2 * ns + 10]
    ym_ref = refs[2 * ns + 10]
    ffn_scr = refs[2 * ns + 11:]
    _ffn_stage_weights(pl.program_id(0), ffn_w, ffn_scr)
    i = pl.program_id(0) - N_WSTEPS

    def body(o_lat, h):
        y = _mla_out_compute(o_lat, h, wuv_ref, wo_ref)
        return _rms(_ffn_compute(y, ffn_w[0], ffn_scr), fw_ref[...])

    @pl.when(_is_tile(i, n_tiles))
    def _():
        y = body(_select(i, o_refs, firsts), h_ref[...])
        bounds = list(firsts[1:]) + [n_tiles]
        for y_ref, lo, hi in zip(y_refs, firsts, bounds):
            @pl.when((i >= lo) & (i < hi))
            def _(y_ref=y_ref):
                y_ref[...] = y

    @pl.when(i == n_tiles)
    def _():
        ym_ref[...] = body(om_ref[...], hm_ref[...])


def _mla_out_ffn_final(o_segs, h, o_meta, h_meta, w_uv_t, w_out, ffn, final_w):
    n_tiles = sum(s[2] for s in o_segs)
    n_mt = h_meta.shape[0]
    firsts = tuple(s[1] for s in o_segs)
    in_specs = [_tiles((MLA_HEADS, TM, MLA_KV_RANK), 1, f, c) for _, f, c in o_segs]
    in_specs += [_tiles((TM, D_MODEL), 0, 0, n_tiles), _whole(o_meta.shape), _whole(h_meta.shape),
                 _resident(w_uv_t.shape), _resident(w_out.shape)]
    in_specs += _ffn_specs(ffn) + [_resident((1, D_MODEL))]
    return pl.pallas_call(
        functools.partial(_mla_out_ffn_final_kernel, firsts=firsts, n_tiles=n_tiles),
        grid=(N_WSTEPS + n_tiles + 1,),
        in_specs=in_specs,
        out_specs=[_tiles((TM, D_MODEL), 0, f, c) for _, f, c in o_segs] + [_whole((n_mt, D_MODEL))],
        out_shape=[jax.ShapeDtypeStruct((c * TM, D_MODEL), F32) for _, _, c in o_segs]
        + [jax.ShapeDtypeStruct((n_mt, D_MODEL), F32)],
        scratch_shapes=_ffn_scratch(),
        compiler_params=_params("arbitrary"),
        name="mla_out_ffn_final",
    )(*[s[0] for s in o_segs], h, o_meta, h_meta, w_uv_t, w_out, *_arrays(ffn), final_w.reshape(1, D_MODEL))


def _rope_tables(pos):
    half = MLA_ROPE // 2
    lane = jnp.arange(LANES)
    inv = ROPE_THETA ** (-(lane % half).astype(F32) / half)
    sign = jnp.where((lane // half) % 2 == 0, -1.0, 1.0).astype(F32)
    ang = pos[:, None] * inv[None, :]
    return jnp.cos(ang), jnp.sin(ang) * sign[None, :]


def kernel(x_prompt, x_sample, state_gla, cache_mla_latent, cache_mla_rope, meta_tokens, ffn1_norm, ffn1_w_gate, ffn1_w_up, ffn1_w_down, mix_norm, gla_w_in, gla_w_gate_up, gla_b_gate, gla_head_norm, gla_w_out, mla_w_down, mla_q_norm, mla_w_uq, mla_kv_norm, mla_w_uk, mla_w_uv, mla_w_out, ffn2_norm, ffn2_w_gate, ffn2_w_up, ffn2_w_down, final_norm):
    bp, seq, _ = x_prompt.shape
    bs, ls, _ = x_sample.shape
    past = cache_mla_latent.shape[2]
    n_fr = bp * seq
    n_sm = bs * ls
    n_mt = bp * N_META
    assert ffn1_norm.shape[0] == 2
    assert n_fr % TM == 0 and n_sm % TM == 0 and TM % TQ == 0 and seq % TQ == 0 and TQ % CHUNK == 0
    assert past % CHUNK == 0 and ls <= CHUNK and N_META <= CHUNK
    fr_tiles = n_fr // TM
    sm_tiles = n_sm // TM

    def vec(p):
        return p.reshape(p.shape[0], 1, p.shape[1])

    ffn1 = (vec(ffn1_norm), ffn1_w_gate, ffn1_w_up, ffn1_w_down)
    ffn2 = (vec(ffn2_norm), ffn2_w_gate, ffn2_w_up, ffn2_w_down)

    def at(params, layer):
        return [(p, layer) for p in params]

    x_meta = jnp.tile(meta_tokens.astype(F32), (bp, 1))
    gla_proj = [(vec(mix_norm), 0)] + at((jnp.swapaxes(gla_w_in, 1, 2).astype(BF16), gla_w_gate_up.astype(BF16),
                                          vec(gla_b_gate)), 0)
    (h, q, k, v, r, gl, h_m, q_m, k_m, v_m, r_m, gl_m) = _ffn_gla_proj(
        [(x_prompt.reshape(n_fr, D_MODEL), 0, fr_tiles), (x_sample.reshape(n_sm, D_MODEL), fr_tiles, sm_tiles)],
        x_meta, at(ffn1, 0), gla_proj)
    o_m, s_meta = _gla_scan(q_m, k_m, v_m, gl_m, None, row0=0, n_batch=bp, n_steps=1,
                            chunk=N_META, n_chunks=1, streams=bp, name="gla_scan_meta")
    o_f, s_p = _gla_scan(q, k, v, gl, s_meta, row0=0, n_batch=bp, n_steps=seq // GLA_STEP_ROWS,
                         chunk=CHUNK, n_chunks=GLA_STEP_ROWS // CHUNK, name="gla_scan_frames")
    o_s, s_s = _gla_scan(q, k, v, gl, state_gla.reshape(state_gla.shape[1:]), row0=n_fr, n_batch=bs,
                         n_steps=1, chunk=ls, n_chunks=1, streams=TQ // ls, name="gla_scan_sample")
    h, h_m = _gla_out_ffn([(o_f, 0, fr_tiles), (o_s, fr_tiles, sm_tiles)], r, h, o_m, r_m, h_m,
                          at((vec(gla_head_norm), gla_w_out.astype(BF16)), 0), at(ffn2, 0))

    seq_blocks = seq // TM
    cos_t, sin_t = _rope_tables(jnp.concatenate([N_META + jnp.arange(seq, dtype=F32),
                                                 jnp.tile(past + jnp.arange(ls, dtype=F32), TM // ls)]))
    cos_m, sin_m = _rope_tables(jnp.tile(jnp.arange(N_META, dtype=F32), bp))

    def table_tile(i):
        return jnp.where(i < fr_tiles, i % seq_blocks, seq_blocks)
    w_down = mla_w_down[0]
    w_down = jnp.concatenate([w_down, w_down[:, -MLA_ROPE:]], axis=1).astype(BF16)
    w_uq = mla_w_uq[0].reshape(MLA_Q_RANK, MLA_HEADS, MLA_NOPE + MLA_ROPE)
    w_qn = w_uq[:, :, :MLA_NOPE].reshape(MLA_Q_RANK, MLA_HEADS * MLA_NOPE).astype(BF16)
    w_qr = w_uq[:, :, MLA_NOPE:].reshape(MLA_Q_RANK, MLA_HEADS * MLA_ROPE).astype(BF16)
    w_uk_t = jnp.transpose(mla_w_uk[0], (1, 2, 0)).astype(BF16)
    w_uv_t = jnp.transpose(mla_w_uv[0], (1, 0, 2)).astype(BF16)
    mla_proj_w = (mix_norm[1].reshape(1, D_MODEL), w_down, mla_q_norm[0].reshape(1, MLA_Q_RANK), w_qn, w_qr,
                  mla_kv_norm[0].reshape(1, MLA_KV_RANK), w_uk_t)
    (h, qa, kc, ckv_f, kr_f, ckv_s, kr_s, vt, h_m, qa_m, kc_m, ckv_m, kr_m) = _ffn_mla_proj(
        h, h_m, cos_t, sin_t, table_tile, cos_m, sin_m, at(ffn1, 1), mla_proj_w, fr_tiles)
    k_tiles = kc.reshape((n_fr + n_sm) // TQ, TQ, QK_W)
    vt_meta = jnp.swapaxes(kc_m[:, :MLA_KV_RANK].reshape(bp, N_META, MLA_KV_RANK), 1, 2)
    vt_meta = jnp.concatenate([vt_meta, jnp.ones((bp, VT_ROWS - MLA_KV_RANK, N_META), BF16)], axis=1)
    ol_f = _attn_prompt(qa, k_tiles, kc_m, vt, vt_meta, n_batch=bp, seq=seq)
    past_rope_t = jnp.swapaxes(cache_mla_rope.reshape(bs, past, MLA_ROPE), 1, 2)
    ol_s = _attn_full(qa, kc, cache_mla_latent.reshape(bs, past, MLA_KV_RANK), past_rope_t,
                      row0=n_fr, n_batch=bs, n_q=ls, streams=4, name="mla_attn_sample")
    ol_m = _attn_full(qa_m, kc_m, None, None, row0=0, n_batch=bp, n_q=N_META, streams=bp,
                      name="mla_attn_meta")
    y_prompt, y_sample, _ = _mla_out_ffn_final(
        [(ol_f, 0, fr_tiles), (ol_s, fr_tiles, sm_tiles)], h, ol_m, h_m, w_uv_t, mla_w_out[0].astype(BF16),
        at(ffn2, 1), final_norm)

    def seq_major(t_frames, t_meta, width):
        return jnp.concatenate([t_meta.reshape(bp, N_META, width), t_frames.reshape(bp, seq, width)], axis=1)

    return (y_prompt.reshape(bp, seq, D_MODEL), y_sample.reshape(bs, ls, D_MODEL),
            s_p[None], s_s[None],
            seq_major(ckv_f, ckv_m, MLA_KV_RANK)[None], seq_major(kr_f, kr_m, MLA_ROPE)[None],
            ckv_s.reshape(1, bs, ls, MLA_KV_RANK), kr_s.reshape(1, bs, ls, MLA_ROPE))
```

```python
import functools

import jax
import jax.numpy as jnp
from jax import lax
from jax.experimental import pallas as pl
from jax.experimental.pallas import tpu as pltpu

F32 = jnp.float32
BF16 = jnp.bfloat16

D_MODEL = 1024
D_FF = 2816
RMS_EPS = 1e-6
N_META = 16
CHUNK = 64

GLA_HEADS = 4
GLA_DK = 128
GLA_DV = 256
GLA_KEY = GLA_HEADS * GLA_DK
GLA_VAL = GLA_HEADS * GLA_DV
GLA_RANK = 16
GLA_GATE_NORMALIZER = 16.0

MLA_HEADS = 8
MLA_Q_RANK = 384
MLA_KV_RANK = 256
MLA_NOPE = 128
MLA_ROPE = 64
ROPE_THETA = 10000.0
LOG2_E = 1.4426950408889634

LANES = 128
BF16_ROWS = 16
VMEM_LIMIT = 58 * 1024 * 1024
QK_W = MLA_KV_RANK + LANES
VT_ROWS = MLA_KV_RANK + BF16_ROWS

TM = 512
FF_CHUNK = 256
TQ = 256
GLA_GROUP_ROWS = 256
GLA_STEP_ROWS = 512
N_WSTEPS = D_FF // FF_CHUNK

NT_DIMS = (((1,), (1,)), ((), ()))
TN_DIMS = (((0,), (0,)), ((), ()))


def _dot(a, b):
    return jnp.dot(a, b, preferred_element_type=F32)


def _dot_nt(a, b):
    return lax.dot_general(a, b, NT_DIMS, preferred_element_type=F32)


def _dot_tn(a, b):
    return lax.dot_general(a, b, TN_DIMS, preferred_element_type=F32)


def _rms(x, w):
    return x * lax.rsqrt(jnp.mean(x * x, axis=-1, keepdims=True) + RMS_EPS) * w


def _silu(x):
    return x * jax.nn.sigmoid(x)


def _params(*sem):
    return pltpu.CompilerParams(dimension_semantics=sem, vmem_limit_bytes=VMEM_LIMIT)


def _resident(shape):
    zeros = (0,) * len(shape)
    return pl.BlockSpec(shape, lambda *_: zeros, pipeline_mode=pl.Buffered(1))


def _whole(shape):
    zeros = (0,) * len(shape)
    return pl.BlockSpec(shape, lambda *_: zeros)


def _tiles(block, row_axis=0, first=0, count=None):
    nd = len(block)

    def imap(i):
        t = i - N_WSTEPS - first
        if count is not None:
            t = jnp.clip(t, 0, count - 1)
        idx = [0] * nd
        idx[row_axis] = t
        return tuple(idx)

    return pl.BlockSpec(block, imap)


def _select(i, refs, firsts):
    val = refs[0][...]
    for ref, first in zip(refs[1:], firsts[1:]):
        val = jnp.where(i >= first, ref[...], val)
    return val


def _ffn_stage_weights(step, ffn_w, ffn_scr):
    _, wg_ref, wu_ref, wd_ref = ffn_w
    _, wg_scr, wu_scr, wd_scr = ffn_scr

    @pl.when(step < N_WSTEPS)
    def _():
        wg_scr[step] = wg_ref[...].astype(BF16)
        wu_scr[step] = wu_ref[...].astype(BF16)
        wd_scr[step] = wd_ref[...].astype(BF16)


def _ffn_compute(x, nw_ref, ffn_scr):
    h_scr, wg_scr, wu_scr, wd_scr = ffn_scr
    m = x.shape[0]
    xn = _rms(x, nw_ref[...]).astype(BF16)
    for c in range(N_WSTEPS):
        g = _dot(xn, wg_scr[c])
        u = _dot(xn, wu_scr[c])
        h_scr[:m, c * FF_CHUNK:(c + 1) * FF_CHUNK] = (_silu(g) * u).astype(BF16)
    return x + 0.5 * _dot(h_scr[:m, :], wd_scr[...].reshape(D_FF, D_MODEL))


def _gla_proj_compute(x, nw_ref, w_ref, wgu_ref, bg_ref):
    c_v = 2 * GLA_KEY
    c_r = c_v + GLA_VAL
    c_g = c_r + GLA_VAL
    xn = _rms(x, nw_ref[...]).astype(BF16)
    q = _dot_nt(xn, w_ref[0:GLA_KEY, :]).astype(BF16)
    k = _dot_nt(xn, w_ref[GLA_KEY:c_v, :]).astype(BF16)
    v = _dot_nt(xn, w_ref[c_v:c_r, :]).astype(BF16)
    r = _dot_nt(xn, w_ref[c_r:c_g, :]).astype(BF16)
    gd = _dot_nt(xn, w_ref[c_g:, :])
    z = _dot(gd.astype(BF16), wgu_ref[...]) + bg_ref[...]
    log_sig = jnp.minimum(z, 0.0) - jnp.log1p(jnp.exp(-jnp.abs(z)))
    return q, k, v, r, log_sig * (1.0 / GLA_GATE_NORMALIZER)


def _gla_out_compute(o, r, h, hn_ref, wo_ref):
    parts = []
    for hd in range(GLA_HEADS):
        vs = slice(hd * GLA_DV, (hd + 1) * GLA_DV)
        on = _rms(o[:, vs].astype(F32), hn_ref[...])
        parts.append((on * _silu(r[:, vs].astype(F32))).astype(BF16))
    return h + _dot(jnp.concatenate(parts, axis=1), wo_ref[...])


def _mla_proj_compute(x, cos_t, sin_t, nw_ref, wd_ref, qn_ref, wqn_ref, wqr_ref, kvn_ref, wuk_ref,
                      q_out, kc_out, vt_out):
    m = x.shape[0]
    scale = (MLA_NOPE + MLA_ROPE) ** -0.5 * LOG2_E
    xn = _rms(x, nw_ref[...]).astype(BF16)
    xd = _dot(xn, wd_ref[...])
    cqn = _rms(xd[:, :MLA_Q_RANK], qn_ref[...]).astype(BF16)
    ckv = _rms(xd[:, MLA_Q_RANK:MLA_Q_RANK + MLA_KV_RANK], kvn_ref[...])
    lane = lax.broadcasted_iota(jnp.int32, (m, LANES), 1)
    first_half = (lane & (MLA_ROPE // 2)) == 0

    def rope(t):
        swapped = jnp.where(first_half, pltpu.roll(t, LANES - MLA_ROPE // 2, 1),
                            pltpu.roll(t, MLA_ROPE // 2, 1))
        return t * cos_t + swapped * sin_t

    kr2 = rope(xd[:, MLA_Q_RANK + MLA_KV_RANK:])
    kc_out[:, :MLA_KV_RANK] = ckv.astype(BF16)
    kc_out[:, MLA_KV_RANK:] = jnp.where(lane < MLA_ROPE, kr2, 0.0).astype(BF16)
    if vt_out is not None:
        for j in range(m // TQ):
            vt_out[j, :MLA_KV_RANK, :] = ckv[j * TQ:(j + 1) * TQ, :].T.astype(BF16)
            vt_out[j, MLA_KV_RANK:, :] = jnp.ones((VT_ROWS - MLA_KV_RANK, TQ), BF16)

    qn = _dot(cqn, wqn_ref[...])
    qr = _dot(cqn, wqr_ref[...])
    for c in range(MLA_HEADS // 2):
        rr = rope(qr[:, c * LANES:(c + 1) * LANES]) * scale
        for e in range(2):
            hd = 2 * c + e
            ql = _dot(qn[:, hd * MLA_NOPE:(hd + 1) * MLA_NOPE].astype(BF16), wuk_ref[hd]) * scale
            q_out[hd, :, :MLA_KV_RANK] = ql.astype(BF16)
            rot = rr if e == 0 else pltpu.roll(rr, MLA_ROPE, 1)
            q_out[hd, :, MLA_KV_RANK:] = rot.astype(BF16)
    return ckv, kr2[:, :MLA_ROPE]


def _mla_out_compute(o_lat, h, wuv_ref, wo_ref):
    parts = [_dot(o_lat[hd], wuv_ref[hd]).astype(BF16) for hd in range(MLA_HEADS)]
    return h + _dot(jnp.concatenate(parts, axis=1), wo_ref[...])


def _layer(shape, layer):
    zeros = (0,) * len(shape)
    return pl.BlockSpec((None,) + tuple(shape), lambda *_: (layer,) + zeros, pipeline_mode=pl.Buffered(1))


def _layer_specs(params):
    return [_layer(a.shape[1:], layer) for a, layer in params]


def _arrays(params):
    return [a for a, _ in params]


def _ffn_specs(ffn):
    (nw, l_n), (_, l_g), (_, l_u), (_, l_d) = ffn

    def slab(i):
        return jnp.minimum(i, N_WSTEPS - 1)

    return [_layer(nw.shape[1:], l_n),
            pl.BlockSpec((None, D_MODEL, FF_CHUNK), lambda i: (l_g, 0, slab(i))),
            pl.BlockSpec((None, D_MODEL, FF_CHUNK), lambda i: (l_u, 0, slab(i))),
            pl.BlockSpec((None, FF_CHUNK, D_MODEL), lambda i: (l_d, slab(i), 0))]


def _ffn_scratch():
    return [pltpu.VMEM((TM, D_FF), BF16),
            pltpu.VMEM((N_WSTEPS, D_MODEL, FF_CHUNK), BF16), pltpu.VMEM((N_WSTEPS, D_MODEL, FF_CHUNK), BF16),
            pltpu.VMEM((N_WSTEPS, FF_CHUNK, D_MODEL), BF16)]


def _is_tile(i, n_tiles):
    return (i >= 0) & (i < n_tiles)


def _ffn_gla_proj_kernel(*refs, x_firsts, n_tiles):
    ns = len(x_firsts)
    x_refs, xm_ref = refs[:ns], refs[ns]
    ffn_w = refs[ns + 1:ns + 5]
    proj_w = refs[ns + 5:ns + 9]
    outs = refs[ns + 9:ns + 15]
    outs_m = refs[ns + 15:ns + 21]
    ffn_scr = refs[ns + 21:]
    _ffn_stage_weights(pl.program_id(0), ffn_w, ffn_scr)
    i = pl.program_id(0) - N_WSTEPS

    def body(x, o):
        y = _ffn_compute(x, ffn_w[0], ffn_scr)
        o[0][...] = y
        for ref, val in zip(o[1:], _gla_proj_compute(y, *proj_w)):
            ref[...] = val

    @pl.when(_is_tile(i, n_tiles))
    def _():
        body(_select(i, x_refs, x_firsts), outs)

    @pl.when(i == n_tiles)
    def _():
        body(xm_ref[...], outs_m)


def _ffn_gla_proj(x_segs, x_meta, ffn, proj):
    n_tiles = sum(s[2] for s in x_segs)
    n_main = n_tiles * TM
    n_mt = x_meta.shape[0]
    widths = [(D_MODEL, F32), (GLA_KEY, BF16), (GLA_KEY, BF16), (GLA_VAL, BF16), (GLA_VAL, BF16),
              (GLA_KEY, F32)]
    in_specs = [_tiles((TM, D_MODEL), 0, f, c) for _, f, c in x_segs]
    in_specs += [_whole((n_mt, D_MODEL))] + _ffn_specs(ffn) + _layer_specs(proj)
    return pl.pallas_call(
        functools.partial(_ffn_gla_proj_kernel, x_firsts=tuple(s[1] for s in x_segs), n_tiles=n_tiles),
        grid=(N_WSTEPS + n_tiles + 1,),
        in_specs=in_specs,
        out_specs=[_tiles((TM, w), 0, 0, n_tiles) for w, _ in widths]
        + [_whole((n_mt, w)) for w, _ in widths],
        out_shape=[jax.ShapeDtypeStruct((n_main, w), d) for w, d in widths]
        + [jax.ShapeDtypeStruct((n_mt, w), d) for w, d in widths],
        scratch_shapes=_ffn_scratch(),
        compiler_params=_params("arbitrary"),
        name="ffn_gla_proj",
    )(*[s[0] for s in x_segs], x_meta, *_arrays(ffn), *_arrays(proj))


def _gla_scan_kernel(*refs, chunk, n_chunks, streams, group, n_steps, has_s0):
    q_ref, k_ref, v_ref, gl_ref = refs[:4]
    s0_ref = refs[4] if has_s0 else None
    o_ref, s_out_ref, st_scr = refs[4 + has_s0:]
    step = pl.program_id(1)

    @pl.when(step == 0)
    def _():
        for s in range(streams):
            for h in range(GLA_HEADS):
                if has_s0:
                    st_scr[s * GLA_HEADS + h] = s0_ref[s, h]
                else:
                    st_scr[s * GLA_HEADS + h] = jnp.zeros((GLA_DK, GLA_DV), F32)

    total = n_chunks * streams
    g_rows = chunk * group
    row = lax.broadcasted_iota(jnp.int32, (g_rows, g_rows), 0)
    col = lax.broadcasted_iota(jnp.int32, (g_rows, g_rows), 1)
    causal = ((row // chunk) == (col // chunk)) & (row >= col)
    tri = jnp.where(causal, 1.0, 0.0).astype(BF16)
    scale = GLA_DK ** -0.5
    groups = [slice(g * g_rows, (g + 1) * g_rows) for g in range(total // group)]
    in_group = [slice(c * chunk, (c + 1) * chunk) for c in range(group)]
    key_slices = [slice(h * GLA_DK, (h + 1) * GLA_DK) for h in range(GLA_HEADS)]
    val_slices = [slice(h * GLA_DV, (h + 1) * GLA_DV) for h in range(GLA_HEADS)]

    gcs = []
    for rows in groups:
        glog = gl_ref[rows, :]
        hi = glog.astype(BF16)
        lo = (glog - hi.astype(F32)).astype(BF16)
        gcs.append(_dot(tri, hi) + _dot(tri, lo))
    sub = lax.broadcasted_iota(jnp.int32, (GLA_DK, GLA_KEY), 0)
    dec_rows = jnp.zeros((GLA_DK, GLA_KEY), F32)
    qgs, kgs, kds = [], [], []
    for g, (rows, gc) in enumerate(zip(groups, gcs)):
        glast = [gc[rs.stop - 1:rs.stop, :] for rs in in_group]
        g_end = jnp.concatenate([jnp.broadcast_to(t, (chunk, GLA_KEY)) for t in glast], axis=0)
        for c, t in enumerate(glast):
            dec_rows = jnp.where(sub == g * group + c, jnp.exp(t), dec_rows)
        q = q_ref[rows, :].astype(F32) * scale
        k = k_ref[rows, :].astype(F32)
        qgs.append((q * jnp.exp(gc)).astype(BF16))
        kgs.append((k * jnp.exp(-gc)).astype(BF16))
        kds.append((k * jnp.exp(g_end - gc)).astype(BF16))

    values = [[v_ref[rows, vs] for vs in val_slices] for rows in groups]
    scores = [[_dot_nt(qg[:, ks], kg[:, ks]) for ks in key_slices] for qg, kg in zip(qgs, kgs)]
    updates = [[_dot_tn(kds[g][rs, ks], values[g][h][rs, :]) for g in range(len(groups)) for rs in in_group]
               for h, ks in enumerate(key_slices)]
    intra = [[_dot(jnp.where(causal, a, 0.0).astype(BF16), vh) for a, vh in zip(sg, vg)]
             for sg, vg in zip(scores, values)]
    for h, ks in enumerate(key_slices):
        dec_cols = dec_rows[:, ks].T
        states = []
        for s in range(streams):
            state = st_scr[s * GLA_HEADS + h]
            for c in range(s * n_chunks, (s + 1) * n_chunks):
                states.append(state.astype(BF16))
                state = state * dec_cols[:, c:c + 1] + updates[h][c]
            st_scr[s * GLA_HEADS + h] = state
        for g, rows in enumerate(groups):
            for c, rs in enumerate(in_group):
                out_rows = slice(rows.start + rs.start, rows.start + rs.stop)
                o_ref[out_rows, val_slices[h]] = (
                    intra[g][h][rs, :] + _dot(qgs[g][rs, ks], states[g * group + c])).astype(BF16)

    @pl.when(step == n_steps - 1)
    def _():
        for s in range(streams):
            for h in range(GLA_HEADS):
                s_out_ref[s, h] = st_scr[s * GLA_HEADS + h]


def _gla_scan(q, k, v, gl, s0, *, row0, n_batch, n_steps, chunk, n_chunks, name, streams=1):
    assert n_batch % streams == 0 and (streams == 1 or n_steps == 1)
    rb = chunk * n_chunks * streams
    base = row0 // rb

    def rmap(b, s):
        return (base + b * n_steps + s, 0)

    def omap(b, s):
        return (b * n_steps + s, 0)

    def smap(b, s):
        return (b, 0, 0, 0)

    ins = [q, k, v, gl]
    specs = [pl.BlockSpec((rb, GLA_KEY), rmap), pl.BlockSpec((rb, GLA_KEY), rmap),
             pl.BlockSpec((rb, GLA_VAL), rmap), pl.BlockSpec((rb, GLA_KEY), rmap)]
    state_block = (streams, GLA_HEADS, GLA_DK, GLA_DV)
    if s0 is not None:
        ins.append(s0)
        specs.append(pl.BlockSpec(state_block, smap))
    group = min(n_chunks * streams, GLA_GROUP_ROWS // chunk)
    assert (n_chunks * streams) % group == 0
    kern = functools.partial(_gla_scan_kernel, chunk=chunk, n_chunks=n_chunks, streams=streams,
                             group=group, n_steps=n_steps, has_s0=s0 is not None)
    return pl.pallas_call(
        kern,
        grid=(n_batch // streams, n_steps),
        in_specs=specs,
        out_specs=[pl.BlockSpec((rb, GLA_VAL), omap), pl.BlockSpec(state_block, smap)],
        out_shape=[jax.ShapeDtypeStruct((n_batch * n_steps * chunk * n_chunks, GLA_VAL), BF16),
                   jax.ShapeDtypeStruct((n_batch,) + state_block[1:], F32)],
        scratch_shapes=[pltpu.VMEM((streams * GLA_HEADS, GLA_DK, GLA_DV), F32)],
        compiler_params=_params("parallel", "arbitrary"),
        name=name,
    )(*ins)


def _gla_out_ffn_kernel(*refs, o_firsts, n_tiles):
    ns = len(o_firsts)
    o_refs = refs[:ns]
    r_ref, h_ref, om_ref, rm_ref, hm_ref, hn_ref, wo_ref = refs[ns:ns + 7]
    ffn_w = refs[ns + 7:ns + 11]
    y_ref, ym_ref = refs[ns + 11:ns + 13]
    ffn_scr = refs[ns + 13:]
    _ffn_stage_weights(pl.program_id(0), ffn_w, ffn_scr)
    i = pl.program_id(0) - N_WSTEPS

    def body(o, r, h):
        return _ffn_compute(_gla_out_compute(o, r, h, hn_ref, wo_ref), ffn_w[0], ffn_scr)

    @pl.when(_is_tile(i, n_tiles))
    def _():
        y_ref[...] = body(_select(i, o_refs, o_firsts), r_ref[...], h_ref[...])

    @pl.when(i == n_tiles)
    def _():
        ym_ref[...] = body(om_ref[...], rm_ref[...], hm_ref[...])


def _gla_out_ffn(o_segs, r, h, o_meta, r_meta, h_meta, out, ffn):
    n_tiles = sum(s[2] for s in o_segs)
    n_mt = h_meta.shape[0]
    in_specs = [_tiles((TM, GLA_VAL), 0, f, c) for _, f, c in o_segs]
    in_specs += [_tiles((TM, GLA_VAL), 0, 0, n_tiles), _tiles((TM, D_MODEL), 0, 0, n_tiles),
                 _whole((n_mt, GLA_VAL)), _whole((n_mt, GLA_VAL)), _whole((n_mt, D_MODEL))]
    in_specs += _layer_specs(out) + _ffn_specs(ffn)
    return pl.pallas_call(
        functools.partial(_gla_out_ffn_kernel, o_firsts=tuple(s[1] for s in o_segs), n_tiles=n_tiles),
        grid=(N_WSTEPS + n_tiles + 1,),
        in_specs=in_specs,
        out_specs=[_tiles((TM, D_MODEL), 0, 0, n_tiles), _whole((n_mt, D_MODEL))],
        out_shape=[jax.ShapeDtypeStruct(h.shape, F32), jax.ShapeDtypeStruct(h_meta.shape, F32)],
        scratch_shapes=_ffn_scratch(),
        compiler_params=_params("arbitrary"),
        name="gla_out_ffn",
    )(*[s[0] for s in o_segs], r, h, o_meta, r_meta, h_meta, *_arrays(out), *_arrays(ffn))


def _ffn_mla_proj_kernel(*refs, n_tiles, n_prompt, stream_tiles):
    x_ref, cos_ref, sin_ref, xm_ref, cosm_ref, sinm_ref = refs[:6]
    ffn_w = refs[6:10]
    proj_w = refs[10:17]
    y_ref, q_out, kc_out, ckv_p, kr_p, ckv_b, kr_b, vt_out = refs[17:25]
    ym_ref, qm_out, kcm_out = refs[25:28]
    ffn_scr = refs[28:32]
    ckv_buf, kr_buf, ckvm_buf, krm_buf, sems, meta_sems = refs[32:]
    first_tiles = n_prompt * stream_tiles
    _ffn_stage_weights(pl.program_id(0), ffn_w, ffn_scr)
    i = pl.program_id(0) - N_WSTEPS

    def tile_copies(t):
        slot = t % 2
        stream = t // stream_tiles
        rows = pl.ds(N_META + (t % stream_tiles) * TM, TM)
        return (pltpu.make_async_copy(ckv_buf.at[slot], ckv_p.at[stream, rows, :], sems.at[slot, 0]),
                pltpu.make_async_copy(kr_buf.at[slot], kr_p.at[stream, rows, :], sems.at[slot, 1]))

    def meta_copies():
        n_meta = xm_ref.shape[0] // n_prompt
        out = []
        for b in range(n_prompt):
            src = pl.ds(b * n_meta, n_meta)
            out.append(pltpu.make_async_copy(ckvm_buf.at[src], ckv_p.at[b, pl.ds(0, n_meta), :],
                                             meta_sems.at[0]))
            out.append(pltpu.make_async_copy(krm_buf.at[src], kr_p.at[b, pl.ds(0, n_meta), :],
                                             meta_sems.at[1]))
        return out

    @pl.when(_is_tile(i, n_tiles))
    def _():
        y = _ffn_compute(x_ref[...], ffn_w[0], ffn_scr)
        y_ref[...] = y
        ckv, kr = _mla_proj_compute(y, cos_ref[...], sin_ref[...], *proj_w, q_out, kc_out, vt_out)

        @pl.when(i < first_tiles)
        def _():
            @pl.when(i >= 2)
            def _():
                for cp in tile_copies(i - 2):
                    cp.wait()

            slot = i % 2
            ckv_buf[slot] = ckv
            kr_buf[slot] = kr
            for cp in tile_copies(i):
                cp.start()

        @pl.when(i >= first_tiles)
        def _():
            ckv_b[...] = ckv
            kr_b[...] = kr

    @pl.when(i == n_tiles)
    def _():
        for t in (first_tiles - 2, first_tiles - 1):
            for cp in tile_copies(t):
                cp.wait()
        y = _ffn_compute(xm_ref[...], ffn_w[0], ffn_scr)
        ym_ref[...] = y
        ckv, kr = _mla_proj_compute(y, cosm_ref[...], sinm_ref[...], *proj_w, qm_out, kcm_out, None)
        ckvm_buf[...] = ckv
        krm_buf[...] = kr
        for cp in meta_copies():
            cp.start()
        for cp in meta_copies():
            cp.wait()


def _ffn_mla_proj(x, x_meta, cos_t, sin_t, table_tile, cos_m, sin_m, ffn, proj_w, n_prompt, seq):
    n_main = x.shape[0]
    n_tiles = n_main // TM
    n_mt = x_meta.shape[0]
    kt_per_tile = TM // TQ
    stream_tiles = seq // TM
    first_tiles = n_prompt * stream_tiles
    rest_tiles = n_tiles - first_tiles
    assert first_tiles >= 2 and n_mt == n_prompt * N_META
    any_space = pl.BlockSpec(memory_space=pl.ANY)

    def rows(w):
        return _tiles((TM, w), 0, 0, n_tiles)

    table = pl.BlockSpec((TM, LANES), lambda i: (table_tile(jnp.clip(i - N_WSTEPS, 0, n_tiles - 1)), 0))
    in_specs = [rows(D_MODEL), table, table, _whole((n_mt, D_MODEL)), _whole((n_mt, LANES)),
                _whole((n_mt, LANES))] + _ffn_specs(ffn)
    in_specs += [_resident(w.shape) for w in proj_w]
    return pl.pallas_call(
        functools.partial(_ffn_mla_proj_kernel, n_tiles=n_tiles, n_prompt=n_prompt,
                          stream_tiles=stream_tiles),
        grid=(N_WSTEPS + n_tiles + 1,),
        in_specs=in_specs,
        out_specs=[rows(D_MODEL), _tiles((MLA_HEADS, TM, QK_W), 1, 0, n_tiles), rows(QK_W),
                   any_space, any_space,
                   _tiles((TM, MLA_KV_RANK), 0, first_tiles, rest_tiles),
                   _tiles((TM, MLA_ROPE), 0, first_tiles, rest_tiles),
                   _tiles((kt_per_tile, VT_ROWS, TQ), 0, 0, n_tiles),
                   _whole((n_mt, D_MODEL)), _whole((MLA_HEADS, n_mt, QK_W)), _whole((n_mt, QK_W))],
        out_shape=[jax.ShapeDtypeStruct((n_main, D_MODEL), F32),
                   jax.ShapeDtypeStruct((MLA_HEADS, n_main, QK_W), BF16),
                   jax.ShapeDtypeStruct((n_main, QK_W), BF16),
                   jax.ShapeDtypeStruct((n_prompt, N_META + seq, MLA_KV_RANK), F32),
                   jax.ShapeDtypeStruct((n_prompt, N_META + seq, MLA_ROPE), F32),
                   jax.ShapeDtypeStruct((rest_tiles * TM, MLA_KV_RANK), F32),
                   jax.ShapeDtypeStruct((rest_tiles * TM, MLA_ROPE), F32),
                   jax.ShapeDtypeStruct((n_main // TQ, VT_ROWS, TQ), BF16),
                   jax.ShapeDtypeStruct((n_mt, D_MODEL), F32),
                   jax.ShapeDtypeStruct((MLA_HEADS, n_mt, QK_W), BF16),
                   jax.ShapeDtypeStruct((n_mt, QK_W), BF16)],
        scratch_shapes=_ffn_scratch() + [
            pltpu.VMEM((2, TM, MLA_KV_RANK), F32), pltpu.VMEM((2, TM, MLA_ROPE), F32),
            pltpu.VMEM((n_mt, MLA_KV_RANK), F32), pltpu.VMEM((n_mt, MLA_ROPE), F32),
            pltpu.SemaphoreType.DMA((2, 2)), pltpu.SemaphoreType.DMA((2,))],
        compiler_params=_params("arbitrary"),
        name="ffn_mla_proj",
    )(x, cos_t, sin_t, x_meta, cos_m, sin_m, *_arrays(ffn), *proj_w)


def _attn_prompt_kernel(q_ref, kf_ref, km_ref, vtf_ref, vtm_ref, o_ref,
                        s_scr, sm_scr, m_scr, acc_scr):
    i = pl.program_id(1)
    cols = MLA_HEADS * TQ
    q = q_ref[...].reshape(cols, QK_W)

    def consume(parts):
        m_cur = None
        for s, _ in parts:
            mx = jnp.max(s, axis=0, keepdims=True)
            m_cur = mx if m_cur is None else jnp.maximum(m_cur, mx)
        m_old = m_scr[...]
        m_new = jnp.maximum(m_old, m_cur)
        alpha = jnp.exp2(m_old - m_new)
        pv = None
        for s, vt in parts:
            term = _dot(vt, jnp.exp2(s - m_new).astype(BF16))
            pv = term if pv is None else pv + term
        acc_scr[...] = alpha * acc_scr[...] + pv
        m_scr[...] = m_new

    m_scr[...] = jnp.full(m_scr.shape, -jnp.inf, F32)
    acc_scr[...] = jnp.zeros(acc_scr.shape, F32)
    s_first = _dot_nt(jnp.concatenate([kf_ref[0], km_ref[...]], axis=0), q)
    s_scr[0] = s_first[:TQ]
    sm_scr[...] = s_first[TQ:]

    def pair(jj, carry):
        j = 2 * jj
        s_scr[1] = _dot_nt(kf_ref[j + 1], q)
        consume([(s_scr[0], vtf_ref[j])])
        s_scr[0] = _dot_nt(kf_ref[j + 2], q)
        consume([(s_scr[1], vtf_ref[j + 1])])
        return carry

    lax.fori_loop(0, lax.shift_right_logical(i, 1), pair, 0)

    @pl.when((i & 1) == 1)
    def _():
        s_scr[1] = _dot_nt(kf_ref[i], q)
        consume([(s_scr[0], vtf_ref[i - 1])])

    key = lax.broadcasted_iota(jnp.int32, (TQ, cols), 0)
    qry = lax.broadcasted_iota(jnp.int32, (TQ, cols), 1) % TQ
    visible = (key // CHUNK) <= (qry // CHUNK)
    s_own = jnp.where(visible, s_scr[i & 1], -jnp.inf)
    consume([(s_own, vtf_ref[i]), (sm_scr[...], vtm_ref[0])])

    for h in range(MLA_HEADS):
        cs = slice(h * TQ, (h + 1) * TQ)
        total = acc_scr[MLA_KV_RANK:MLA_KV_RANK + 1, cs]
        o_ref[h] = (acc_scr[:MLA_KV_RANK, cs] / total).T.astype(BF16)


def _attn_prompt(q, k_tiles, kc_meta, vt_tiles, vt_meta, *, n_batch, seq):
    steps = seq // TQ
    cols = MLA_HEADS * TQ
    return pl.pallas_call(
        _attn_prompt_kernel,
        grid=(n_batch, steps),
        in_specs=[pl.BlockSpec((MLA_HEADS, TQ, QK_W), lambda b, i: (0, b * steps + i, 0)),
                  pl.BlockSpec((steps, TQ, QK_W), lambda b, i: (b, 0, 0)),
                  pl.BlockSpec((N_META, QK_W), lambda b, i: (b, 0)),
                  pl.BlockSpec((steps, VT_ROWS, TQ), lambda b, i: (b, 0, 0)),
                  pl.BlockSpec((1, VT_ROWS, N_META), lambda b, i: (b, 0, 0))],
        out_specs=pl.BlockSpec((MLA_HEADS, TQ, MLA_KV_RANK), lambda b, i: (0, b * steps + i, 0)),
        out_shape=jax.ShapeDtypeStruct((MLA_HEADS, n_batch * seq, MLA_KV_RANK), BF16),
        scratch_shapes=[pltpu.VMEM((2, TQ, cols), F32), pltpu.VMEM((N_META, cols), F32),
                        pltpu.VMEM((1, cols), F32), pltpu.VMEM((VT_ROWS, cols), F32)],
        compiler_params=_params("parallel", "arbitrary"),
        name="mla_attn_prompt",
    )(q, k_tiles, kc_meta, vt_tiles, vt_meta)


def _attn_full_kernel(*refs, n_q, streams, has_past):
    if has_past:
        q_ref, kn_ref, pl_ref, pr_ref, o_ref = refs
    else:
        q_ref, kn_ref, o_ref = refs
    rows = MLA_HEADS * n_q
    blocks = [slice(s * n_q, (s + 1) * n_q) for s in range(streams)]
    qs = [q_ref[:, rs, :].reshape(rows, QK_W) for rs in blocks]
    kns = [kn_ref[rs, :] for rs in blocks]
    s_new = [_dot_nt(q, kn) for q, kn in zip(qs, kns)]
    if has_past:
        lats = [pl_ref[s].astype(BF16) for s in range(streams)]
        rp_ts = [pr_ref[s].astype(BF16) for s in range(streams)]
        s_past = [_dot_nt(q[:, :MLA_KV_RANK], lat) + _dot(q[:, MLA_KV_RANK:MLA_KV_RANK + MLA_ROPE], rp_t)
                  for q, lat, rp_t in zip(qs, lats, rp_ts)]
    for s, rs in enumerate(blocks):
        m = jnp.max(s_new[s], axis=1, keepdims=True)
        if has_past:
            m = jnp.maximum(m, jnp.max(s_past[s], axis=1, keepdims=True))
        p_n = jnp.exp2(s_new[s] - m)
        l = jnp.sum(p_n, axis=1, keepdims=True)
        acc = _dot(p_n.astype(BF16), kns[s][:, :MLA_KV_RANK])
        if has_past:
            p_p = jnp.exp2(s_past[s] - m)
            l = l + jnp.sum(p_p, axis=1, keepdims=True)
            acc = acc + _dot(p_p.astype(BF16), lats[s])
        o_ref[:, rs, :] = (acc / l).astype(BF16).reshape(MLA_HEADS, n_q, MLA_KV_RANK)


def _attn_full(q, kc, past_lat, past_rope_t, *, row0, n_batch, n_q, streams, name):
    assert n_batch % streams == 0
    nb = n_q * streams
    base = row0 // nb
    has_past = past_lat is not None
    ins = [q, kc]
    specs = [pl.BlockSpec((MLA_HEADS, nb, QK_W), lambda b: (0, base + b, 0)),
             pl.BlockSpec((nb, QK_W), lambda b: (base + b, 0))]
    if has_past:
        past = past_lat.shape[1]
        ins += [past_lat, past_rope_t]
        specs += [pl.BlockSpec((streams, past, MLA_KV_RANK), lambda b: (b, 0, 0)),
                  pl.BlockSpec((streams, MLA_ROPE, past), lambda b: (b, 0, 0))]
    return pl.pallas_call(
        functools.partial(_attn_full_kernel, n_q=n_q, streams=streams, has_past=has_past),
        grid=(n_batch // streams,),
        in_specs=specs,
        out_specs=pl.BlockSpec((MLA_HEADS, nb, MLA_KV_RANK), lambda b: (0, b, 0)),
        out_shape=jax.ShapeDtypeStruct((MLA_HEADS, n_batch * n_q, MLA_KV_RANK), BF16),
        compiler_params=_params("parallel"),
        name=name,
    )(*ins)


def _mla_out_ffn_final_kernel(*refs, firsts, n_tiles):
    ns = len(firsts)
    o_refs = refs[:ns]
    h_ref, om_ref, hm_ref, wuv_ref, wo_ref = refs[ns:ns + 5]
    ffn_w = refs[ns + 5:ns + 9]
    fw_ref = refs[ns + 9]
    y_refs = refs[ns + 10:2 * ns + 10]
    ym_ref = refs[2 * ns + 10]
    ffn_scr = refs[2 * ns + 11:]
    _ffn_stage_weights(pl.program_id(0), ffn_w, ffn_scr)
    i = pl.program_id(0) - N_WSTEPS

    def body(o_lat, h):
        y = _mla_out_compute(o_lat, h, wuv_ref, wo_ref)
        return _rms(_ffn_compute(y, ffn_w[0], ffn_scr), fw_ref[...])

    @pl.when(_is_tile(i, n_tiles))
    def _():
        y = body(_select(i, o_refs, firsts), h_ref[...])
        bounds = list(firsts[1:]) + [n_tiles]
        for y_ref, lo, hi in zip(y_refs, firsts, bounds):
            @pl.when((i >= lo) & (i < hi))
            def _(y_ref=y_ref):
                y_ref[...] = y

    @pl.when(i == n_tiles)
    def _():
        ym_ref[...] = body(om_ref[...], hm_ref[...])


def _mla_out_ffn_final(o_segs, h, o_meta, h_meta, w_uv_t, w_out, ffn, final_w):
    n_tiles = sum(s[2] for s in o_segs)
    n_mt = h_meta.shape[0]
    firsts = tuple(s[1] for s in o_segs)
    in_specs = [_tiles((MLA_HEADS, TM, MLA_KV_RANK), 1, f, c) for _, f, c in o_segs]
    in_specs += [_tiles((TM, D_MODEL), 0, 0, n_tiles), _whole(o_meta.shape), _whole(h_meta.shape),
                 _resident(w_uv_t.shape), _resident(w_out.shape)]
    in_specs += _ffn_specs(ffn) + [_resident((1, D_MODEL))]
    return pl.pallas_call(
        functools.partial(_mla_out_ffn_final_kernel, firsts=firsts, n_tiles=n_tiles),
        grid=(N_WSTEPS + n_tiles + 1,),
        in_specs=in_specs,
        out_specs=[_tiles((TM, D_MODEL), 0, f, c) for _, f, c in o_segs] + [_whole((n_mt, D_MODEL))],
        out_shape=[jax.ShapeDtypeStruct((c * TM, D_MODEL), F32) for _, _, c in o_segs]
        + [jax.ShapeDtypeStruct((n_mt, D_MODEL), F32)],
        scratch_shapes=_ffn_scratch(),
        compiler_params=_params("arbitrary"),
        name="mla_out_ffn_final",
    )(*[s[0] for s in o_segs], h, o_meta, h_meta, w_uv_t, w_out, *_arrays(ffn), final_w.reshape(1, D_MODEL))


def _rope_tables(pos):
    half = MLA_ROPE // 2
    lane = jnp.arange(LANES)
    inv = ROPE_THETA ** (-(lane % half).astype(F32) / half)
    sign = jnp.where((lane // half) % 2 == 0, -1.0, 1.0).astype(F32)
    ang = pos[:, None] * inv[None, :]
    return jnp.cos(ang), jnp.sin(ang) * sign[None, :]


def kernel(x_prompt, x_sample, state_gla, cache_mla_latent, cache_mla_rope, meta_tokens, ffn1_norm, ffn1_w_gate, ffn1_w_up, ffn1_w_down, mix_norm, gla_w_in, gla_w_gate_up, gla_b_gate, gla_head_norm, gla_w_out, mla_w_down, mla_q_norm, mla_w_uq, mla_kv_norm, mla_w_uk, mla_w_uv, mla_w_out, ffn2_norm, ffn2_w_gate, ffn2_w_up, ffn2_w_down, final_norm):
    bp, seq, _ = x_prompt.shape
    bs, ls, _ = x_sample.shape
    past = cache_mla_latent.shape[2]
    n_fr = bp * seq
    n_sm = bs * ls
    n_mt = bp * N_META
    assert ffn1_norm.shape[0] == 2
    assert n_fr % TM == 0 and n_sm % TM == 0 and TM % TQ == 0 and seq % TQ == 0 and TQ % CHUNK == 0
    assert past % CHUNK == 0 and ls <= CHUNK and N_META <= CHUNK
    fr_tiles = n_fr // TM
    sm_tiles = n_sm // TM

    def vec(p):
        return p.reshape(p.shape[0], 1, p.shape[1])

    ffn1 = (vec(ffn1_norm), ffn1_w_gate, ffn1_w_up, ffn1_w_down)
    ffn2 = (vec(ffn2_norm), ffn2_w_gate, ffn2_w_up, ffn2_w_down)

    def at(params, layer):
        return [(p, layer) for p in params]

    x_meta = jnp.tile(meta_tokens.astype(F32), (bp, 1))
    gla_proj = [(vec(mix_norm), 0)] + at((jnp.swapaxes(gla_w_in, 1, 2).astype(BF16), gla_w_gate_up.astype(BF16),
                                          vec(gla_b_gate)), 0)
    (h, q, k, v, r, gl, h_m, q_m, k_m, v_m, r_m, gl_m) = _ffn_gla_proj(
        [(x_prompt.reshape(n_fr, D_MODEL), 0, fr_tiles), (x_sample.reshape(n_sm, D_MODEL), fr_tiles, sm_tiles)],
        x_meta, at(ffn1, 0), gla_proj)
    o_m, s_meta = _gla_scan(q_m, k_m, v_m, gl_m, None, row0=0, n_batch=bp, n_steps=1,
                            chunk=N_META, n_chunks=1, streams=bp, name="gla_scan_meta")
    o_f, s_p = _gla_scan(q, k, v, gl, s_meta, row0=0, n_batch=bp, n_steps=seq // GLA_STEP_ROWS,
                         chunk=CHUNK, n_chunks=GLA_STEP_ROWS // CHUNK, name="gla_scan_frames")
    o_s, s_s = _gla_scan(q, k, v, gl, state_gla.reshape(state_gla.shape[1:]), row0=n_fr, n_batch=bs,
                         n_steps=1, chunk=ls, n_chunks=1, streams=TQ // ls, name="gla_scan_sample")
    h, h_m = _gla_out_ffn([(o_f, 0, fr_tiles), (o_s, fr_tiles, sm_tiles)], r, h, o_m, r_m, h_m,
                          at((vec(gla_head_norm), gla_w_out.astype(BF16)), 0), at(ffn2, 0))

    seq_blocks = seq // TM
    cos_t, sin_t = _rope_tables(jnp.concatenate([N_META + jnp.arange(seq, dtype=F32),
                                                 jnp.tile(past + jnp.arange(ls, dtype=F32), TM // ls)]))
    cos_m, sin_m = _rope_tables(jnp.tile(jnp.arange(N_META, dtype=F32), bp))

    def table_tile(i):
        return jnp.where(i < fr_tiles, i % seq_blocks, seq_blocks)
    w_down = mla_w_down[0]
    w_down = jnp.concatenate([w_down, w_down[:, -MLA_ROPE:]], axis=1).astype(BF16)
    w_uq = mla_w_uq[0].reshape(MLA_Q_RANK, MLA_HEADS, MLA_NOPE + MLA_ROPE)
    w_qn = w_uq[:, :, :MLA_NOPE].reshape(MLA_Q_RANK, MLA_HEADS * MLA_NOPE).astype(BF16)
    w_qr = w_uq[:, :, MLA_NOPE:].reshape(MLA_Q_RANK, MLA_HEADS * MLA_ROPE).astype(BF16)
    w_uk_t = jnp.transpose(mla_w_uk[0], (1, 2, 0)).astype(BF16)
    w_uv_t = jnp.transpose(mla_w_uv[0], (1, 0, 2)).astype(BF16)
    mla_proj_w = (mix_norm[1].reshape(1, D_MODEL), w_down, mla_q_norm[0].reshape(1, MLA_Q_RANK), w_qn, w_qr,
                  mla_kv_norm[0].reshape(1, MLA_KV_RANK), w_uk_t)
    (h, qa, kc, ckv_p, kr_p, ckv_s, kr_s, vt, h_m, qa_m, kc_m) = _ffn_mla_proj(
        h, h_m, cos_t, sin_t, table_tile, cos_m, sin_m, at(ffn1, 1), mla_proj_w, bp, seq)
    k_tiles = kc.reshape((n_fr + n_sm) // TQ, TQ, QK_W)
    vt_meta = jnp.swapaxes(kc_m[:, :MLA_KV_RANK].reshape(bp, N_META, MLA_KV_RANK), 1, 2)
    vt_meta = jnp.concatenate([vt_meta, jnp.ones((bp, VT_ROWS - MLA_KV_RANK, N_META), BF16)], axis=1)
    ol_f = _attn_prompt(qa, k_tiles, kc_m, vt, vt_meta, n_batch=bp, seq=seq)
    past_rope_t = jnp.swapaxes(cache_mla_rope.reshape(bs, past, MLA_ROPE), 1, 2)
    ol_s = _attn_full(qa, kc, cache_mla_latent.reshape(bs, past, MLA_KV_RANK), past_rope_t,
                      row0=n_fr, n_batch=bs, n_q=ls, streams=4, name="mla_attn_sample")
    ol_m = _attn_full(qa_m, kc_m, None, None, row0=0, n_batch=bp, n_q=N_META, streams=bp,
                      name="mla_attn_meta")
    y_prompt, y_sample, _ = _mla_out_ffn_final(
        [(ol_f, 0, fr_tiles), (ol_s, fr_tiles, sm_tiles)], h, ol_m, h_m, w_uv_t, mla_w_out[0].astype(BF16),
        at(ffn2, 1), final_norm)

    return (y_prompt.reshape(bp, seq, D_MODEL), y_sample.reshape(bs, ls, D_MODEL),
            s_p[None], s_s[None], ckv_p[None], kr_p[None],
            ckv_s.reshape(1, bs, ls, MLA_KV_RANK), kr_s.reshape(1, bs, ls, MLA_ROPE))
```

```python
import functools

import jax
import jax.numpy as jnp
from jax import lax
from jax.experimental import pallas as pl
from jax.experimental.pallas import tpu as pltpu

F32 = jnp.float32
BF16 = jnp.bfloat16

D_MODEL = 1024
D_FF = 2816
RMS_EPS = 1e-6
N_META = 16
CHUNK = 64

GLA_HEADS = 4
GLA_DK = 128
GLA_DV = 256
GLA_KEY = GLA_HEADS * GLA_DK
GLA_VAL = GLA_HEADS * GLA_DV
GLA_RANK = 16
GLA_GATE_NORMALIZER = 16.0

MLA_HEADS = 8
MLA_Q_RANK = 384
MLA_KV_RANK = 256
MLA_NOPE = 128
MLA_ROPE = 64
ROPE_THETA = 10000.0
LOG2_E = 1.4426950408889634

LANES = 128
BF16_ROWS = 16
VMEM_LIMIT = 58 * 1024 * 1024
QK_W = MLA_KV_RANK + LANES
VT_ROWS = MLA_KV_RANK + BF16_ROWS

TM = 512
FF_CHUNK = 256
TQ = 256
ATT_STREAMS = 2
GLA_GROUP_ROWS = 256
GLA_STEP_ROWS = 512
N_WSTEPS = D_FF // FF_CHUNK

NT_DIMS = (((1,), (1,)), ((), ()))
TN_DIMS = (((0,), (0,)), ((), ()))


def _dot(a, b):
    return jnp.dot(a, b, preferred_element_type=F32)


def _dot_nt(a, b):
    return lax.dot_general(a, b, NT_DIMS, preferred_element_type=F32)


def _dot_tn(a, b):
    return lax.dot_general(a, b, TN_DIMS, preferred_element_type=F32)


def _rms(x, w):
    return x * lax.rsqrt(jnp.mean(x * x, axis=-1, keepdims=True) + RMS_EPS) * w


def _silu(x):
    return x * jax.nn.sigmoid(x)


def _params(*sem):
    return pltpu.CompilerParams(dimension_semantics=sem, vmem_limit_bytes=VMEM_LIMIT)


def _resident(shape):
    zeros = (0,) * len(shape)
    return pl.BlockSpec(shape, lambda *_: zeros, pipeline_mode=pl.Buffered(1))


def _whole(shape):
    zeros = (0,) * len(shape)
    return pl.BlockSpec(shape, lambda *_: zeros)


def _tiles(block, row_axis=0, first=0, count=None):
    nd = len(block)

    def imap(i):
        t = i - N_WSTEPS - first
        if count is not None:
            t = jnp.clip(t, 0, count - 1)
        idx = [0] * nd
        idx[row_axis] = t
        return tuple(idx)

    return pl.BlockSpec(block, imap)


def _select(i, refs, firsts):
    val = refs[0][...]
    for ref, first in zip(refs[1:], firsts[1:]):
        val = jnp.where(i >= first, ref[...], val)
    return val


def _ffn_stage_weights(step, ffn_w, ffn_scr):
    _, wg_ref, wu_ref, wd_ref = ffn_w
    _, wg_scr, wu_scr, wd_scr = ffn_scr

    @pl.when(step < N_WSTEPS)
    def _():
        wg_scr[step] = wg_ref[...].astype(BF16)
        wu_scr[step] = wu_ref[...].astype(BF16)
        wd_scr[step] = wd_ref[...].astype(BF16)


def _ffn_compute(x, nw_ref, ffn_scr):
    h_scr, wg_scr, wu_scr, wd_scr = ffn_scr
    m = x.shape[0]
    xn = _rms(x, nw_ref[...]).astype(BF16)
    for c in range(N_WSTEPS):
        g = _dot(xn, wg_scr[c])
        u = _dot(xn, wu_scr[c])
        h_scr[:m, c * FF_CHUNK:(c + 1) * FF_CHUNK] = (_silu(g) * u).astype(BF16)
    return x + 0.5 * _dot(h_scr[:m, :], wd_scr[...].reshape(D_FF, D_MODEL))


def _gla_proj_compute(x, nw_ref, w_ref, wgu_ref, bg_ref):
    c_v = 2 * GLA_KEY
    c_r = c_v + GLA_VAL
    c_g = c_r + GLA_VAL
    xn = _rms(x, nw_ref[...]).astype(BF16)
    q = _dot_nt(xn, w_ref[0:GLA_KEY, :]).astype(BF16)
    k = _dot_nt(xn, w_ref[GLA_KEY:c_v, :]).astype(BF16)
    v = _dot_nt(xn, w_ref[c_v:c_r, :]).astype(BF16)
    r = _dot_nt(xn, w_ref[c_r:c_g, :]).astype(BF16)
    gd = _dot_nt(xn, w_ref[c_g:, :])
    z = _dot(gd.astype(BF16), wgu_ref[...]) + bg_ref[...]
    log_sig = jnp.minimum(z, 0.0) - jnp.log1p(jnp.exp(-jnp.abs(z)))
    return q, k, v, r, log_sig * (1.0 / GLA_GATE_NORMALIZER)


def _gla_out_compute(o, r, h, hn_ref, wo_ref):
    parts = []
    for hd in range(GLA_HEADS):
        vs = slice(hd * GLA_DV, (hd + 1) * GLA_DV)
        on = _rms(o[:, vs].astype(F32), hn_ref[...])
        parts.append((on * _silu(r[:, vs].astype(F32))).astype(BF16))
    return h + _dot(jnp.concatenate(parts, axis=1), wo_ref[...])


def _mla_proj_compute(x, cos_t, sin_t, nw_ref, wd_ref, qn_ref, wqn_ref, wqr_ref, kvn_ref, wuk_ref,
                      q_out, kc_out, vt_out):
    m = x.shape[0]
    scale = (MLA_NOPE + MLA_ROPE) ** -0.5 * LOG2_E
    xn = _rms(x, nw_ref[...]).astype(BF16)
    xd = _dot(xn, wd_ref[...])
    cqn = _rms(xd[:, :MLA_Q_RANK], qn_ref[...]).astype(BF16)
    ckv = _rms(xd[:, MLA_Q_RANK:MLA_Q_RANK + MLA_KV_RANK], kvn_ref[...])
    lane = lax.broadcasted_iota(jnp.int32, (m, LANES), 1)
    first_half = (lane & (MLA_ROPE // 2)) == 0

    def rope(t):
        swapped = jnp.where(first_half, pltpu.roll(t, LANES - MLA_ROPE // 2, 1),
                            pltpu.roll(t, MLA_ROPE // 2, 1))
        return t * cos_t + swapped * sin_t

    kr2 = rope(xd[:, MLA_Q_RANK + MLA_KV_RANK:])
    kc_out[:, :MLA_KV_RANK] = ckv.astype(BF16)
    kc_out[:, MLA_KV_RANK:] = jnp.where(lane < MLA_ROPE, kr2, 0.0).astype(BF16)
    if vt_out is not None:
        for j in range(m // TQ):
            vt_out[j, :MLA_KV_RANK, :] = ckv[j * TQ:(j + 1) * TQ, :].T.astype(BF16)
            vt_out[j, MLA_KV_RANK:, :] = jnp.ones((VT_ROWS - MLA_KV_RANK, TQ), BF16)

    qn = _dot(cqn, wqn_ref[...])
    qr = _dot(cqn, wqr_ref[...])
    for c in range(MLA_HEADS // 2):
        rr = rope(qr[:, c * LANES:(c + 1) * LANES]) * scale
        for e in range(2):
            hd = 2 * c + e
            ql = _dot(qn[:, hd * MLA_NOPE:(hd + 1) * MLA_NOPE].astype(BF16), wuk_ref[hd]) * scale
            q_out[hd, :, :MLA_KV_RANK] = ql.astype(BF16)
            rot = rr if e == 0 else pltpu.roll(rr, MLA_ROPE, 1)
            q_out[hd, :, MLA_KV_RANK:] = rot.astype(BF16)
    return ckv, kr2[:, :MLA_ROPE]


def _mla_out_compute(o_lat, h, wuv_ref, wo_ref):
    parts = [_dot(o_lat[hd], wuv_ref[hd]).astype(BF16) for hd in range(MLA_HEADS)]
    return h + _dot(jnp.concatenate(parts, axis=1), wo_ref[...])


def _layer(shape, layer):
    zeros = (0,) * len(shape)
    return pl.BlockSpec((None,) + tuple(shape), lambda *_: (layer,) + zeros, pipeline_mode=pl.Buffered(1))


def _layer_specs(params):
    return [_layer(a.shape[1:], layer) for a, layer in params]


def _arrays(params):
    return [a for a, _ in params]


def _ffn_specs(ffn):
    (nw, l_n), (_, l_g), (_, l_u), (_, l_d) = ffn

    def slab(i):
        return jnp.minimum(i, N_WSTEPS - 1)

    return [_layer(nw.shape[1:], l_n),
            pl.BlockSpec((None, D_MODEL, FF_CHUNK), lambda i: (l_g, 0, slab(i))),
            pl.BlockSpec((None, D_MODEL, FF_CHUNK), lambda i: (l_u, 0, slab(i))),
            pl.BlockSpec((None, FF_CHUNK, D_MODEL), lambda i: (l_d, slab(i), 0))]


def _ffn_scratch():
    return [pltpu.VMEM((TM, D_FF), BF16),
            pltpu.VMEM((N_WSTEPS, D_MODEL, FF_CHUNK), BF16), pltpu.VMEM((N_WSTEPS, D_MODEL, FF_CHUNK), BF16),
            pltpu.VMEM((N_WSTEPS, FF_CHUNK, D_MODEL), BF16)]


def _is_tile(i, n_tiles):
    return (i >= 0) & (i < n_tiles)


def _ffn_gla_proj_kernel(*refs, x_firsts, n_tiles):
    ns = len(x_firsts)
    x_refs, xm_ref = refs[:ns], refs[ns]
    ffn_w = refs[ns + 1:ns + 5]
    proj_w = refs[ns + 5:ns + 9]
    outs = refs[ns + 9:ns + 15]
    outs_m = refs[ns + 15:ns + 21]
    ffn_scr = refs[ns + 21:]
    _ffn_stage_weights(pl.program_id(0), ffn_w, ffn_scr)
    i = pl.program_id(0) - N_WSTEPS

    def body(x, o):
        y = _ffn_compute(x, ffn_w[0], ffn_scr)
        o[0][...] = y
        for ref, val in zip(o[1:], _gla_proj_compute(y, *proj_w)):
            ref[...] = val

    @pl.when(_is_tile(i, n_tiles))
    def _():
        body(_select(i, x_refs, x_firsts), outs)

    @pl.when(i == n_tiles)
    def _():
        body(xm_ref[...], outs_m)


def _ffn_gla_proj(x_segs, x_meta, ffn, proj):
    n_tiles = sum(s[2] for s in x_segs)
    n_main = n_tiles * TM
    n_mt = x_meta.shape[0]
    widths = [(D_MODEL, F32), (GLA_KEY, BF16), (GLA_KEY, BF16), (GLA_VAL, BF16), (GLA_VAL, BF16),
              (GLA_KEY, F32)]
    in_specs = [_tiles((TM, D_MODEL), 0, f, c) for _, f, c in x_segs]
    in_specs += [_whole((n_mt, D_MODEL))] + _ffn_specs(ffn) + _layer_specs(proj)
    return pl.pallas_call(
        functools.partial(_ffn_gla_proj_kernel, x_firsts=tuple(s[1] for s in x_segs), n_tiles=n_tiles),
        grid=(N_WSTEPS + n_tiles + 1,),
        in_specs=in_specs,
        out_specs=[_tiles((TM, w), 0, 0, n_tiles) for w, _ in widths]
        + [_whole((n_mt, w)) for w, _ in widths],
        out_shape=[jax.ShapeDtypeStruct((n_main, w), d) for w, d in widths]
        + [jax.ShapeDtypeStruct((n_mt, w), d) for w, d in widths],
        scratch_shapes=_ffn_scratch(),
        compiler_params=_params("arbitrary"),
        name="ffn_gla_proj",
    )(*[s[0] for s in x_segs], x_meta, *_arrays(ffn), *_arrays(proj))


def _gla_scan_kernel(*refs, chunk, n_chunks, streams, group, n_steps, has_s0):
    q_ref, k_ref, v_ref, gl_ref = refs[:4]
    s0_ref = refs[4] if has_s0 else None
    o_ref, s_out_ref, st_scr = refs[4 + has_s0:]
    step = pl.program_id(1)

    @pl.when(step == 0)
    def _():
        for s in range(streams):
            for h in range(GLA_HEADS):
                if has_s0:
                    st_scr[s * GLA_HEADS + h] = s0_ref[s, h]
                else:
                    st_scr[s * GLA_HEADS + h] = jnp.zeros((GLA_DK, GLA_DV), F32)

    total = n_chunks * streams
    g_rows = chunk * group
    row = lax.broadcasted_iota(jnp.int32, (g_rows, g_rows), 0)
    col = lax.broadcasted_iota(jnp.int32, (g_rows, g_rows), 1)
    causal = ((row // chunk) == (col // chunk)) & (row >= col)
    tri = jnp.where(causal, 1.0, 0.0).astype(BF16)
    scale = GLA_DK ** -0.5
    groups = [slice(g * g_rows, (g + 1) * g_rows) for g in range(total // group)]
    in_group = [slice(c * chunk, (c + 1) * chunk) for c in range(group)]
    key_slices = [slice(h * GLA_DK, (h + 1) * GLA_DK) for h in range(GLA_HEADS)]
    val_slices = [slice(h * GLA_DV, (h + 1) * GLA_DV) for h in range(GLA_HEADS)]

    gcs = []
    for rows in groups:
        glog = gl_ref[rows, :]
        hi = glog.astype(BF16)
        lo = (glog - hi.astype(F32)).astype(BF16)
        gcs.append(_dot(tri, hi) + _dot(tri, lo))
    sub = lax.broadcasted_iota(jnp.int32, (GLA_DK, GLA_KEY), 0)
    dec_rows = jnp.zeros((GLA_DK, GLA_KEY), F32)
    qgs, kgs, kds = [], [], []
    for g, (rows, gc) in enumerate(zip(groups, gcs)):
        glast = [gc[rs.stop - 1:rs.stop, :] for rs in in_group]
        g_end = jnp.concatenate([jnp.broadcast_to(t, (chunk, GLA_KEY)) for t in glast], axis=0)
        for c, t in enumerate(glast):
            dec_rows = jnp.where(sub == g * group + c, jnp.exp(t), dec_rows)
        q = q_ref[rows, :].astype(F32) * scale
        k = k_ref[rows, :].astype(F32)
        qgs.append((q * jnp.exp(gc)).astype(BF16))
        kgs.append((k * jnp.exp(-gc)).astype(BF16))
        kds.append((k * jnp.exp(g_end - gc)).astype(BF16))

    values = [[v_ref[rows, vs] for vs in val_slices] for rows in groups]
    scores = [[_dot_nt(qg[:, ks], kg[:, ks]) for ks in key_slices] for qg, kg in zip(qgs, kgs)]
    updates = [[_dot_tn(kds[g][rs, ks], values[g][h][rs, :]) for g in range(len(groups)) for rs in in_group]
               for h, ks in enumerate(key_slices)]
    intra = [[_dot(jnp.where(causal, a, 0.0).astype(BF16), vh) for a, vh in zip(sg, vg)]
             for sg, vg in zip(scores, values)]
    for h, ks in enumerate(key_slices):
        dec_cols = dec_rows[:, ks].T
        states = []
        for s in range(streams):
            state = st_scr[s * GLA_HEADS + h]
            for c in range(s * n_chunks, (s + 1) * n_chunks):
                states.append(state.astype(BF16))
                state = state * dec_cols[:, c:c + 1] + updates[h][c]
            st_scr[s * GLA_HEADS + h] = state
        for g, rows in enumerate(groups):
            for c, rs in enumerate(in_group):
                out_rows = slice(rows.start + rs.start, rows.start + rs.stop)
                o_ref[out_rows, val_slices[h]] = (
                    intra[g][h][rs, :] + _dot(qgs[g][rs, ks], states[g * group + c])).astype(BF16)

    @pl.when(step == n_steps - 1)
    def _():
        for s in range(streams):
            for h in range(GLA_HEADS):
                s_out_ref[s, h] = st_scr[s * GLA_HEADS + h]


def _gla_scan(q, k, v, gl, s0, *, row0, n_batch, n_steps, chunk, n_chunks, name, streams=1):
    assert n_batch % streams == 0 and (streams == 1 or n_steps == 1)
    rb = chunk * n_chunks * streams
    base = row0 // rb

    def rmap(b, s):
        return (base + b * n_steps + s, 0)

    def omap(b, s):
        return (b * n_steps + s, 0)

    def smap(b, s):
        return (b, 0, 0, 0)

    ins = [q, k, v, gl]
    specs = [pl.BlockSpec((rb, GLA_KEY), rmap), pl.BlockSpec((rb, GLA_KEY), rmap),
             pl.BlockSpec((rb, GLA_VAL), rmap), pl.BlockSpec((rb, GLA_KEY), rmap)]
    state_block = (streams, GLA_HEADS, GLA_DK, GLA_DV)
    if s0 is not None:
        ins.append(s0)
        specs.append(pl.BlockSpec(state_block, smap))
    group = min(n_chunks * streams, GLA_GROUP_ROWS // chunk)
    assert (n_chunks * streams) % group == 0
    kern = functools.partial(_gla_scan_kernel, chunk=chunk, n_chunks=n_chunks, streams=streams,
                             group=group, n_steps=n_steps, has_s0=s0 is not None)
    return pl.pallas_call(
        kern,
        grid=(n_batch // streams, n_steps),
        in_specs=specs,
        out_specs=[pl.BlockSpec((rb, GLA_VAL), omap), pl.BlockSpec(state_block, smap)],
        out_shape=[jax.ShapeDtypeStruct((n_batch * n_steps * chunk * n_chunks, GLA_VAL), BF16),
                   jax.ShapeDtypeStruct((n_batch,) + state_block[1:], F32)],
        scratch_shapes=[pltpu.VMEM((streams * GLA_HEADS, GLA_DK, GLA_DV), F32)],
        compiler_params=_params("parallel", "arbitrary"),
        name=name,
    )(*ins)


def _gla_out_ffn_kernel(*refs, o_firsts, n_tiles):
    ns = len(o_firsts)
    o_refs = refs[:ns]
    r_ref, h_ref, om_ref, rm_ref, hm_ref, hn_ref, wo_ref = refs[ns:ns + 7]
    ffn_w = refs[ns + 7:ns + 11]
    y_ref, ym_ref = refs[ns + 11:ns + 13]
    ffn_scr = refs[ns + 13:]
    _ffn_stage_weights(pl.program_id(0), ffn_w, ffn_scr)
    i = pl.program_id(0) - N_WSTEPS

    def body(o, r, h):
        return _ffn_compute(_gla_out_compute(o, r, h, hn_ref, wo_ref), ffn_w[0], ffn_scr)

    @pl.when(_is_tile(i, n_tiles))
    def _():
        y_ref[...] = body(_select(i, o_refs, o_firsts), r_ref[...], h_ref[...])

    @pl.when(i == n_tiles)
    def _():
        ym_ref[...] = body(om_ref[...], rm_ref[...], hm_ref[...])


def _gla_out_ffn(o_segs, r, h, o_meta, r_meta, h_meta, out, ffn):
    n_tiles = sum(s[2] for s in o_segs)
    n_mt = h_meta.shape[0]
    in_specs = [_tiles((TM, GLA_VAL), 0, f, c) for _, f, c in o_segs]
    in_specs += [_tiles((TM, GLA_VAL), 0, 0, n_tiles), _tiles((TM, D_MODEL), 0, 0, n_tiles),
                 _whole((n_mt, GLA_VAL)), _whole((n_mt, GLA_VAL)), _whole((n_mt, D_MODEL))]
    in_specs += _layer_specs(out) + _ffn_specs(ffn)
    return pl.pallas_call(
        functools.partial(_gla_out_ffn_kernel, o_firsts=tuple(s[1] for s in o_segs), n_tiles=n_tiles),
        grid=(N_WSTEPS + n_tiles + 1,),
        in_specs=in_specs,
        out_specs=[_tiles((TM, D_MODEL), 0, 0, n_tiles), _whole((n_mt, D_MODEL))],
        out_shape=[jax.ShapeDtypeStruct(h.shape, F32), jax.ShapeDtypeStruct(h_meta.shape, F32)],
        scratch_shapes=_ffn_scratch(),
        compiler_params=_params("arbitrary"),
        name="gla_out_ffn",
    )(*[s[0] for s in o_segs], r, h, o_meta, r_meta, h_meta, *_arrays(out), *_arrays(ffn))


def _ffn_mla_proj_kernel(*refs, n_tiles, n_prompt, stream_tiles):
    x_ref, cos_ref, sin_ref, xm_ref, cosm_ref, sinm_ref = refs[:6]
    ffn_w = refs[6:10]
    proj_w = refs[10:17]
    y_ref, q_out, kc_out, ckv_p, kr_p, ckv_b, kr_b, vt_out = refs[17:25]
    ym_ref, qm_out, kcm_out = refs[25:28]
    ffn_scr = refs[28:32]
    ckv_buf, kr_buf, ckvm_buf, krm_buf, sems, meta_sems = refs[32:]
    first_tiles = n_prompt * stream_tiles
    _ffn_stage_weights(pl.program_id(0), ffn_w, ffn_scr)
    i = pl.program_id(0) - N_WSTEPS

    def tile_copies(t):
        slot = t % 2
        stream = t // stream_tiles
        rows = pl.ds(N_META + (t % stream_tiles) * TM, TM)
        return (pltpu.make_async_copy(ckv_buf.at[slot], ckv_p.at[stream, rows, :], sems.at[slot, 0]),
                pltpu.make_async_copy(kr_buf.at[slot], kr_p.at[stream, rows, :], sems.at[slot, 1]))

    def meta_copies():
        n_meta = xm_ref.shape[0] // n_prompt
        out = []
        for b in range(n_prompt):
            src = pl.ds(b * n_meta, n_meta)
            out.append(pltpu.make_async_copy(ckvm_buf.at[src], ckv_p.at[b, pl.ds(0, n_meta), :],
                                             meta_sems.at[0]))
            out.append(pltpu.make_async_copy(krm_buf.at[src], kr_p.at[b, pl.ds(0, n_meta), :],
                                             meta_sems.at[1]))
        return out

    @pl.when(_is_tile(i, n_tiles))
    def _():
        y = _ffn_compute(x_ref[...], ffn_w[0], ffn_scr)
        y_ref[...] = y
        ckv, kr = _mla_proj_compute(y, cos_ref[...], sin_ref[...], *proj_w, q_out, kc_out, vt_out)

        @pl.when(i < first_tiles)
        def _():
            @pl.when(i >= 2)
            def _():
                for cp in tile_copies(i - 2):
                    cp.wait()

            slot = i % 2
            ckv_buf[slot] = ckv
            kr_buf[slot] = kr
            for cp in tile_copies(i):
                cp.start()

        @pl.when(i >= first_tiles)
        def _():
            ckv_b[...] = ckv
            kr_b[...] = kr

    @pl.when(i == n_tiles)
    def _():
        for t in (first_tiles - 2, first_tiles - 1):
            for cp in tile_copies(t):
                cp.wait()
        y = _ffn_compute(xm_ref[...], ffn_w[0], ffn_scr)
        ym_ref[...] = y
        ckv, kr = _mla_proj_compute(y, cosm_ref[...], sinm_ref[...], *proj_w, qm_out, kcm_out, None)
        ckvm_buf[...] = ckv
        krm_buf[...] = kr
        for cp in meta_copies():
            cp.start()
        for cp in meta_copies():
            cp.wait()


def _ffn_mla_proj(x, x_meta, cos_t, sin_t, table_tile, cos_m, sin_m, ffn, proj_w, n_prompt, seq):
    n_main = x.shape[0]
    n_tiles = n_main // TM
    n_mt = x_meta.shape[0]
    kt_per_tile = TM // TQ
    stream_tiles = seq // TM
    first_tiles = n_prompt * stream_tiles
    rest_tiles = n_tiles - first_tiles
    assert first_tiles >= 2 and n_mt == n_prompt * N_META
    any_space = pl.BlockSpec(memory_space=pl.ANY)

    def rows(w):
        return _tiles((TM, w), 0, 0, n_tiles)

    table = pl.BlockSpec((TM, LANES), lambda i: (table_tile(jnp.clip(i - N_WSTEPS, 0, n_tiles - 1)), 0))
    in_specs = [rows(D_MODEL), table, table, _whole((n_mt, D_MODEL)), _whole((n_mt, LANES)),
                _whole((n_mt, LANES))] + _ffn_specs(ffn)
    in_specs += [_resident(w.shape) for w in proj_w]
    return pl.pallas_call(
        functools.partial(_ffn_mla_proj_kernel, n_tiles=n_tiles, n_prompt=n_prompt,
                          stream_tiles=stream_tiles),
        grid=(N_WSTEPS + n_tiles + 1,),
        in_specs=in_specs,
        out_specs=[rows(D_MODEL), _tiles((MLA_HEADS, TM, QK_W), 1, 0, n_tiles), rows(QK_W),
                   any_space, any_space,
                   _tiles((TM, MLA_KV_RANK), 0, first_tiles, rest_tiles),
                   _tiles((TM, MLA_ROPE), 0, first_tiles, rest_tiles),
                   _tiles((kt_per_tile, VT_ROWS, TQ), 0, 0, n_tiles),
                   _whole((n_mt, D_MODEL)), _whole((MLA_HEADS, n_mt, QK_W)), _whole((n_mt, QK_W))],
        out_shape=[jax.ShapeDtypeStruct((n_main, D_MODEL), F32),
                   jax.ShapeDtypeStruct((MLA_HEADS, n_main, QK_W), BF16),
                   jax.ShapeDtypeStruct((n_main, QK_W), BF16),
                   jax.ShapeDtypeStruct((n_prompt, N_META + seq, MLA_KV_RANK), F32),
                   jax.ShapeDtypeStruct((n_prompt, N_META + seq, MLA_ROPE), F32),
                   jax.ShapeDtypeStruct((rest_tiles * TM, MLA_KV_RANK), F32),
                   jax.ShapeDtypeStruct((rest_tiles * TM, MLA_ROPE), F32),
                   jax.ShapeDtypeStruct((n_main // TQ, VT_ROWS, TQ), BF16),
                   jax.ShapeDtypeStruct((n_mt, D_MODEL), F32),
                   jax.ShapeDtypeStruct((MLA_HEADS, n_mt, QK_W), BF16),
                   jax.ShapeDtypeStruct((n_mt, QK_W), BF16)],
        scratch_shapes=_ffn_scratch() + [
            pltpu.VMEM((2, TM, MLA_KV_RANK), F32), pltpu.VMEM((2, TM, MLA_ROPE), F32),
            pltpu.VMEM((n_mt, MLA_KV_RANK), F32), pltpu.VMEM((n_mt, MLA_ROPE), F32),
            pltpu.SemaphoreType.DMA((2, 2)), pltpu.SemaphoreType.DMA((2,))],
        compiler_params=_params("arbitrary"),
        name="ffn_mla_proj",
    )(x, cos_t, sin_t, x_meta, cos_m, sin_m, *_arrays(ffn), *proj_w)


def _attn_prompt_kernel(*refs, n_prob):
    ins = [refs[5 * p:5 * p + 5] for p in range(n_prob)]
    o_refs = refs[5 * n_prob:6 * n_prob]
    scr = [refs[6 * n_prob + 4 * p:6 * n_prob + 4 * p + 4] for p in range(n_prob)]
    i = pl.program_id(1)
    cols = MLA_HEADS * TQ
    qs = [q_ref[...].reshape(cols, QK_W) for q_ref, *_ in ins]

    def scores(p, t):
        return _dot_nt(ins[p][1][t], qs[p])

    def consume(p, parts):
        _, _, m_scr, acc_scr = scr[p]
        m_cur = None
        for s, _ in parts:
            mx = jnp.max(s, axis=0, keepdims=True)
            m_cur = mx if m_cur is None else jnp.maximum(m_cur, mx)
        m_old = m_scr[...]
        m_new = jnp.maximum(m_old, m_cur)
        alpha = jnp.exp2(m_old - m_new)
        pv = None
        for s, vt in parts:
            term = _dot(vt, jnp.exp2(s - m_new).astype(BF16))
            pv = term if pv is None else pv + term
        acc_scr[...] = alpha * acc_scr[...] + pv
        m_scr[...] = m_new

    for p in range(n_prob):
        s_scr, sm_scr, m_scr, acc_scr = scr[p]
        m_scr[...] = jnp.full(m_scr.shape, -jnp.inf, F32)
        acc_scr[...] = jnp.zeros(acc_scr.shape, F32)
        s_first = _dot_nt(jnp.concatenate([ins[p][1][0], ins[p][2][...]], axis=0), qs[p])
        s_scr[0] = s_first[:TQ]
        sm_scr[...] = s_first[TQ:]

    def pair(jj, carry):
        j = 2 * jj
        for cur, nxt, t in ((0, 1, j), (1, 0, j + 1)):
            for p in range(n_prob):
                scr[p][0][nxt] = scores(p, t + 1)
            for p in range(n_prob):
                consume(p, [(scr[p][0][cur], ins[p][3][t])])
        return carry

    lax.fori_loop(0, lax.shift_right_logical(i, 1), pair, 0)

    @pl.when((i & 1) == 1)
    def _():
        for p in range(n_prob):
            scr[p][0][1] = scores(p, i)
        for p in range(n_prob):
            consume(p, [(scr[p][0][0], ins[p][3][i - 1])])

    key = lax.broadcasted_iota(jnp.int32, (TQ, cols), 0)
    qry = lax.broadcasted_iota(jnp.int32, (TQ, cols), 1) % TQ
    visible = (key // CHUNK) <= (qry // CHUNK)
    for p in range(n_prob):
        s_own = jnp.where(visible, scr[p][0][i & 1], -jnp.inf)
        consume(p, [(s_own, ins[p][3][i]), (scr[p][1][...], ins[p][4][0])])

    for p in range(n_prob):
        acc_scr = scr[p][3]
        for h in range(MLA_HEADS):
            cs = slice(h * TQ, (h + 1) * TQ)
            total = acc_scr[MLA_KV_RANK:MLA_KV_RANK + 1, cs]
            o_refs[p][h] = (acc_scr[:MLA_KV_RANK, cs] / total).T.astype(BF16)


def _attn_prompt(q, k_tiles, kc_meta, vt_tiles, vt_meta, *, n_batch, seq, n_prob):
    steps = seq // TQ
    cols = MLA_HEADS * TQ
    per = n_batch // n_prob
    assert per * n_prob == n_batch
    in_specs = []
    for p in range(n_prob):
        def stream(b, p=p):
            return p * per + b
        in_specs += [pl.BlockSpec((MLA_HEADS, TQ, QK_W), lambda b, i, f=stream: (0, f(b) * steps + i, 0)),
                     pl.BlockSpec((steps, TQ, QK_W), lambda b, i, f=stream: (f(b), 0, 0)),
                     pl.BlockSpec((N_META, QK_W), lambda b, i, f=stream: (f(b), 0)),
                     pl.BlockSpec((steps, VT_ROWS, TQ), lambda b, i, f=stream: (f(b), 0, 0)),
                     pl.BlockSpec((1, VT_ROWS, N_META), lambda b, i, f=stream: (f(b), 0, 0))]
    scratch = [pltpu.VMEM((2, TQ, cols), F32), pltpu.VMEM((N_META, cols), F32),
               pltpu.VMEM((1, cols), F32), pltpu.VMEM((VT_ROWS, cols), F32)]
    return pl.pallas_call(
        functools.partial(_attn_prompt_kernel, n_prob=n_prob),
        grid=(per, steps),
        in_specs=in_specs,
        out_specs=[pl.BlockSpec((MLA_HEADS, TQ, MLA_KV_RANK), lambda b, i: (0, b * steps + i, 0))] * n_prob,
        out_shape=[jax.ShapeDtypeStruct((MLA_HEADS, per * seq, MLA_KV_RANK), BF16)] * n_prob,
        scratch_shapes=scratch * n_prob,
        compiler_params=_params("parallel", "arbitrary"),
        name="mla_attn_prompt",
    )(*([q, k_tiles, kc_meta, vt_tiles, vt_meta] * n_prob))


def _attn_full_kernel(*refs, n_q, streams, has_past):
    if has_past:
        q_ref, kn_ref, pl_ref, pr_ref, o_ref = refs
    else:
        q_ref, kn_ref, o_ref = refs
    rows = MLA_HEADS * n_q
    blocks = [slice(s * n_q, (s + 1) * n_q) for s in range(streams)]
    qs = [q_ref[:, rs, :].reshape(rows, QK_W) for rs in blocks]
    kns = [kn_ref[rs, :] for rs in blocks]
    s_new = [_dot_nt(q, kn) for q, kn in zip(qs, kns)]
    if has_past:
        lats = [pl_ref[s].astype(BF16) for s in range(streams)]
        rp_ts = [pr_ref[s].astype(BF16) for s in range(streams)]
        s_past = [_dot_nt(q[:, :MLA_KV_RANK], lat) + _dot(q[:, MLA_KV_RANK:MLA_KV_RANK + MLA_ROPE], rp_t)
                  for q, lat, rp_t in zip(qs, lats, rp_ts)]
    for s, rs in enumerate(blocks):
        m = jnp.max(s_new[s], axis=1, keepdims=True)
        if has_past:
            m = jnp.maximum(m, jnp.max(s_past[s], axis=1, keepdims=True))
        p_n = jnp.exp2(s_new[s] - m)
        l = jnp.sum(p_n, axis=1, keepdims=True)
        acc = _dot(p_n.astype(BF16), kns[s][:, :MLA_KV_RANK])
        if has_past:
            p_p = jnp.exp2(s_past[s] - m)
            l = l + jnp.sum(p_p, axis=1, keepdims=True)
            acc = acc + _dot(p_p.astype(BF16), lats[s])
        o_ref[:, rs, :] = (acc / l).astype(BF16).reshape(MLA_HEADS, n_q, MLA_KV_RANK)


def _attn_full(q, kc, past_lat, past_rope_t, *, row0, n_batch, n_q, streams, name):
    assert n_batch % streams == 0
    nb = n_q * streams
    base = row0 // nb
    has_past = past_lat is not None
    ins = [q, kc]
    specs = [pl.BlockSpec((MLA_HEADS, nb, QK_W), lambda b: (0, base + b, 0)),
             pl.BlockSpec((nb, QK_W), lambda b: (base + b, 0))]
    if has_past:
        past = past_lat.shape[1]
        ins += [past_lat, past_rope_t]
        specs += [pl.BlockSpec((streams, past, MLA_KV_RANK), lambda b: (b, 0, 0)),
                  pl.BlockSpec((streams, MLA_ROPE, past), lambda b: (b, 0, 0))]
    return pl.pallas_call(
        functools.partial(_attn_full_kernel, n_q=n_q, streams=streams, has_past=has_past),
        grid=(n_batch // streams,),
        in_specs=specs,
        out_specs=pl.BlockSpec((MLA_HEADS, nb, MLA_KV_RANK), lambda b: (0, b, 0)),
        out_shape=jax.ShapeDtypeStruct((MLA_HEADS, n_batch * n_q, MLA_KV_RANK), BF16),
        compiler_params=_params("parallel"),
        name=name,
    )(*ins)


def _mla_out_ffn_final_kernel(*refs, o_firsts, y_firsts, n_tiles):
    ns, ny = len(o_firsts), len(y_firsts)
    o_refs = refs[:ns]
    h_ref, om_ref, hm_ref, wuv_ref, wo_ref = refs[ns:ns + 5]
    ffn_w = refs[ns + 5:ns + 9]
    fw_ref = refs[ns + 9]
    y_refs = refs[ns + 10:ns + 10 + ny]
    ym_ref = refs[ns + 10 + ny]
    ffn_scr = refs[ns + 11 + ny:]
    _ffn_stage_weights(pl.program_id(0), ffn_w, ffn_scr)
    i = pl.program_id(0) - N_WSTEPS

    def body(o_lat, h):
        y = _mla_out_compute(o_lat, h, wuv_ref, wo_ref)
        return _rms(_ffn_compute(y, ffn_w[0], ffn_scr), fw_ref[...])

    @pl.when(_is_tile(i, n_tiles))
    def _():
        y = body(_select(i, o_refs, o_firsts), h_ref[...])
        bounds = list(y_firsts[1:]) + [n_tiles]
        for y_ref, lo, hi in zip(y_refs, y_firsts, bounds):
            @pl.when((i >= lo) & (i < hi))
            def _(y_ref=y_ref):
                y_ref[...] = y

    @pl.when(i == n_tiles)
    def _():
        ym_ref[...] = body(om_ref[...], hm_ref[...])


def _mla_out_ffn_final(o_segs, y_segs, h, o_meta, h_meta, w_uv_t, w_out, ffn, final_w):
    n_tiles = sum(s[2] for s in o_segs)
    n_mt = h_meta.shape[0]
    in_specs = [_tiles((MLA_HEADS, TM, MLA_KV_RANK), 1, f, c) for _, f, c in o_segs]
    in_specs += [_tiles((TM, D_MODEL), 0, 0, n_tiles), _whole(o_meta.shape), _whole(h_meta.shape),
                 _resident(w_uv_t.shape), _resident(w_out.shape)]
    in_specs += _ffn_specs(ffn) + [_resident((1, D_MODEL))]
    return pl.pallas_call(
        functools.partial(_mla_out_ffn_final_kernel, o_firsts=tuple(s[1] for s in o_segs),
                          y_firsts=tuple(f for f, _ in y_segs), n_tiles=n_tiles),
        grid=(N_WSTEPS + n_tiles + 1,),
        in_specs=in_specs,
        out_specs=[_tiles((TM, D_MODEL), 0, f, c) for f, c in y_segs] + [_whole((n_mt, D_MODEL))],
        out_shape=[jax.ShapeDtypeStruct((c * TM, D_MODEL), F32) for _, c in y_segs]
        + [jax.ShapeDtypeStruct((n_mt, D_MODEL), F32)],
        scratch_shapes=_ffn_scratch(),
        compiler_params=_params("arbitrary"),
        name="mla_out_ffn_final",
    )(*[s[0] for s in o_segs], h, o_meta, h_meta, w_uv_t, w_out, *_arrays(ffn), final_w.reshape(1, D_MODEL))


def _rope_tables(pos):
    half = MLA_ROPE // 2
    lane = jnp.arange(LANES)
    inv = ROPE_THETA ** (-(lane % half).astype(F32) / half)
    sign = jnp.where((lane // half) % 2 == 0, -1.0, 1.0).astype(F32)
    ang = pos[:, None] * inv[None, :]
    return jnp.cos(ang), jnp.sin(ang) * sign[None, :]


def kernel(x_prompt, x_sample, state_gla, cache_mla_latent, cache_mla_rope, meta_tokens, ffn1_norm, ffn1_w_gate, ffn1_w_up, ffn1_w_down, mix_norm, gla_w_in, gla_w_gate_up, gla_b_gate, gla_head_norm, gla_w_out, mla_w_down, mla_q_norm, mla_w_uq, mla_kv_norm, mla_w_uk, mla_w_uv, mla_w_out, ffn2_norm, ffn2_w_gate, ffn2_w_up, ffn2_w_down, final_norm):
    bp, seq, _ = x_prompt.shape
    bs, ls, _ = x_sample.shape
    past = cache_mla_latent.shape[2]
    n_fr = bp * seq
    n_sm = bs * ls
    n_mt = bp * N_META
    assert ffn1_norm.shape[0] == 2
    assert n_fr % TM == 0 and n_sm % TM == 0 and TM % TQ == 0 and seq % TQ == 0 and TQ % CHUNK == 0
    assert past % CHUNK == 0 and ls <= CHUNK and N_META <= CHUNK
    fr_tiles = n_fr // TM
    sm_tiles = n_sm // TM

    def vec(p):
        return p.reshape(p.shape[0], 1, p.shape[1])

    ffn1 = (vec(ffn1_norm), ffn1_w_gate, ffn1_w_up, ffn1_w_down)
    ffn2 = (vec(ffn2_norm), ffn2_w_gate, ffn2_w_up, ffn2_w_down)

    def at(params, layer):
        return [(p, layer) for p in params]

    x_meta = jnp.tile(meta_tokens.astype(F32), (bp, 1))
    gla_proj = [(vec(mix_norm), 0)] + at((jnp.swapaxes(gla_w_in, 1, 2).astype(BF16), gla_w_gate_up.astype(BF16),
                                          vec(gla_b_gate)), 0)
    (h, q, k, v, r, gl, h_m, q_m, k_m, v_m, r_m, gl_m) = _ffn_gla_proj(
        [(x_prompt.reshape(n_fr, D_MODEL), 0, fr_tiles), (x_sample.reshape(n_sm, D_MODEL), fr_tiles, sm_tiles)],
        x_meta, at(ffn1, 0), gla_proj)
    o_m, s_meta = _gla_scan(q_m, k_m, v_m, gl_m, None, row0=0, n_batch=bp, n_steps=1,
                            chunk=N_META, n_chunks=1, streams=bp, name="gla_scan_meta")
    o_f, s_p = _gla_scan(q, k, v, gl, s_meta, row0=0, n_batch=bp, n_steps=seq // GLA_STEP_ROWS,
                         chunk=CHUNK, n_chunks=GLA_STEP_ROWS // CHUNK, name="gla_scan_frames")
    o_s, s_s = _gla_scan(q, k, v, gl, state_gla.reshape(state_gla.shape[1:]), row0=n_fr, n_batch=bs,
                         n_steps=1, chunk=ls, n_chunks=1, streams=TQ // ls, name="gla_scan_sample")
    h, h_m = _gla_out_ffn([(o_f, 0, fr_tiles), (o_s, fr_tiles, sm_tiles)], r, h, o_m, r_m, h_m,
                          at((vec(gla_head_norm), gla_w_out.astype(BF16)), 0), at(ffn2, 0))

    seq_blocks = seq // TM
    cos_t, sin_t = _rope_tables(jnp.concatenate([N_META + jnp.arange(seq, dtype=F32),
                                                 jnp.tile(past + jnp.arange(ls, dtype=F32), TM // ls)]))
    cos_m, sin_m = _rope_tables(jnp.tile(jnp.arange(N_META, dtype=F32), bp))

    def table_tile(i):
        return jnp.where(i < fr_tiles, i % seq_blocks, seq_blocks)
    w_down = mla_w_down[0]
    w_down = jnp.concatenate([w_down, w_down[:, -MLA_ROPE:]], axis=1).astype(BF16)
    w_uq = mla_w_uq[0].reshape(MLA_Q_RANK, MLA_HEADS, MLA_NOPE + MLA_ROPE)
    w_qn = w_uq[:, :, :MLA_NOPE].reshape(MLA_Q_RANK, MLA_HEADS * MLA_NOPE).astype(BF16)
    w_qr = w_uq[:, :, MLA_NOPE:].reshape(MLA_Q_RANK, MLA_HEADS * MLA_ROPE).astype(BF16)
    w_uk_t = jnp.transpose(mla_w_uk[0], (1, 2, 0)).astype(BF16)
    w_uv_t = jnp.transpose(mla_w_uv[0], (1, 0, 2)).astype(BF16)
    mla_proj_w = (mix_norm[1].reshape(1, D_MODEL), w_down, mla_q_norm[0].reshape(1, MLA_Q_RANK), w_qn, w_qr,
                  mla_kv_norm[0].reshape(1, MLA_KV_RANK), w_uk_t)
    (h, qa, kc, ckv_p, kr_p, ckv_s, kr_s, vt, h_m, qa_m, kc_m) = _ffn_mla_proj(
        h, h_m, cos_t, sin_t, table_tile, cos_m, sin_m, at(ffn1, 1), mla_proj_w, bp, seq)
    k_tiles = kc.reshape((n_fr + n_sm) // TQ, TQ, QK_W)
    vt_meta = jnp.swapaxes(kc_m[:, :MLA_KV_RANK].reshape(bp, N_META, MLA_KV_RANK), 1, 2)
    vt_meta = jnp.concatenate([vt_meta, jnp.ones((bp, VT_ROWS - MLA_KV_RANK, N_META), BF16)], axis=1)
    ol_f = _attn_prompt(qa, k_tiles, kc_m, vt, vt_meta, n_batch=bp, seq=seq, n_prob=ATT_STREAMS)
    part_tiles = fr_tiles // ATT_STREAMS
    past_rope_t = jnp.swapaxes(cache_mla_rope.reshape(bs, past, MLA_ROPE), 1, 2)
    ol_s = _attn_full(qa, kc, cache_mla_latent.reshape(bs, past, MLA_KV_RANK), past_rope_t,
                      row0=n_fr, n_batch=bs, n_q=ls, streams=4, name="mla_attn_sample")
    ol_m = _attn_full(qa_m, kc_m, None, None, row0=0, n_batch=bp, n_q=N_META, streams=bp,
                      name="mla_attn_meta")
    y_prompt, y_sample, _ = _mla_out_ffn_final(
        [(o, p * part_tiles, part_tiles) for p, o in enumerate(ol_f)] + [(ol_s, fr_tiles, sm_tiles)],
        [(0, fr_tiles), (fr_tiles, sm_tiles)], h, ol_m, h_m, w_uv_t, mla_w_out[0].astype(BF16),
        at(ffn2, 1), final_norm)

    return (y_prompt.reshape(bp, seq, D_MODEL), y_sample.reshape(bs, ls, D_MODEL),
            s_p[None], s_s[None], ckv_p[None], kr_p[None],
            ckv_s.reshape(1, bs, ls, MLA_KV_RANK), kr_s.reshape(1, bs, ls, MLA_ROPE))
```

```python
import functools

import jax
import jax.numpy as jnp
from jax import lax
from jax.experimental import pallas as pl
from jax.experimental.pallas import tpu as pltpu

F32 = jnp.float32
BF16 = jnp.bfloat16

D_MODEL = 1024
D_FF = 2816
RMS_EPS = 1e-6
N_META = 16
CHUNK = 64

GLA_HEADS = 4
GLA_DK = 128
GLA_DV = 256
GLA_KEY = GLA_HEADS * GLA_DK
GLA_VAL = GLA_HEADS * GLA_DV
GLA_RANK = 16
GLA_GATE_NORMALIZER = 16.0

MLA_HEADS = 8
MLA_Q_RANK = 384
MLA_KV_RANK = 256
MLA_NOPE = 128
MLA_ROPE = 64
ROPE_THETA = 10000.0
LOG2_E = 1.4426950408889634

LANES = 128
BF16_ROWS = 16
VMEM_LIMIT = 58 * 1024 * 1024
QK_W = MLA_KV_RANK + LANES
VT_ROWS = MLA_KV_RANK + BF16_ROWS

TM = 512
FF_CHUNK = 256
TQ = 256
ATT_STREAMS = 2
GLA_GROUP_ROWS = 256
GLA_STEP_ROWS = 512
N_SLABS = D_FF // FF_CHUNK

NT_DIMS = (((1,), (1,)), ((), ()))
TN_DIMS = (((0,), (0,)), ((), ()))


def _dot(a, b):
    return jnp.dot(a, b, preferred_element_type=F32)


def _dot_nt(a, b):
    return lax.dot_general(a, b, NT_DIMS, preferred_element_type=F32)


def _dot_tn(a, b):
    return lax.dot_general(a, b, TN_DIMS, preferred_element_type=F32)


def _rms(x, w):
    return x * lax.rsqrt(jnp.mean(x * x, axis=-1, keepdims=True) + RMS_EPS) * w


def _silu(x):
    return x * jax.nn.sigmoid(x)


def _params(*sem):
    return pltpu.CompilerParams(dimension_semantics=sem, vmem_limit_bytes=VMEM_LIMIT)


def _resident(shape):
    zeros = (0,) * len(shape)
    return pl.BlockSpec(shape, lambda *_: zeros, pipeline_mode=pl.Buffered(1))


def _whole(shape):
    zeros = (0,) * len(shape)
    return pl.BlockSpec(shape, lambda *_: zeros)


def _tiles(block, row_axis=0, first=0, count=None):
    nd = len(block)

    def imap(i):
        t = i - first
        if count is not None:
            t = jnp.clip(t, 0, count - 1)
        idx = [0] * nd
        idx[row_axis] = t
        return tuple(idx)

    return pl.BlockSpec(block, imap)


def _select(i, refs, firsts):
    val = refs[0][...]
    for ref, first in zip(refs[1:], firsts[1:]):
        val = jnp.where(i >= first, ref[...], val)
    return val


def _ffn_compute(x, ffn_w, ffn_scr, layers, first):
    nw_ref, wg_hbm, wu_hbm, wd_hbm = ffn_w
    h_scr, wg_scr, wu_scr, wd_scr, land_g, land_u, land_d, sems = ffn_scr
    m = x.shape[0]
    xn = _rms(x, nw_ref[...]).astype(BF16)

    def slab_copy(kind, c):
        slot = c % 2
        cols = pl.ds(c * FF_CHUNK, FF_CHUNK)
        if kind == 0:
            src, dst = wg_hbm.at[layers[0], :, cols], land_g.at[slot]
        elif kind == 1:
            src, dst = wu_hbm.at[layers[1], :, cols], land_u.at[slot]
        else:
            src, dst = wd_hbm.at[layers[2], cols, :], land_d.at[slot]
        return pltpu.make_async_copy(src, dst, sems.at[kind, slot])

    def hidden(c):
        g = _dot(xn, wg_scr[c])
        u = _dot(xn, wu_scr[c])
        h_scr[:m, c * FF_CHUNK:(c + 1) * FF_CHUNK] = (_silu(g) * u).astype(BF16)

    def fill():
        for c in range(N_SLABS):
            hidden(c)

    def fill_streaming():
        for c in range(2):
            for kind in range(3):
                slab_copy(kind, c).start()
        for c in range(N_SLABS):
            for kind, (land, scr) in enumerate(((land_g, wg_scr), (land_u, wu_scr), (land_d, wd_scr))):
                slab_copy(kind, c).wait()
                scr[c] = land[c % 2].astype(BF16)
                if c + 2 < N_SLABS:
                    slab_copy(kind, c + 2).start()
            hidden(c)

    if first is None:
        fill()
    else:
        pl.when(first)(fill_streaming)
        pl.when(jnp.logical_not(first))(fill)
    return x + 0.5 * _dot(h_scr[:m, :], wd_scr[...].reshape(D_FF, D_MODEL))


def _gla_proj_compute(x, nw_ref, w_ref, wgu_ref, bg_ref):
    c_v = 2 * GLA_KEY
    c_r = c_v + GLA_VAL
    c_g = c_r + GLA_VAL
    xn = _rms(x, nw_ref[...]).astype(BF16)
    q = _dot_nt(xn, w_ref[0:GLA_KEY, :]).astype(BF16)
    k = _dot_nt(xn, w_ref[GLA_KEY:c_v, :]).astype(BF16)
    v = _dot_nt(xn, w_ref[c_v:c_r, :]).astype(BF16)
    r = _dot_nt(xn, w_ref[c_r:c_g, :]).astype(BF16)
    gd = _dot_nt(xn, w_ref[c_g:, :])
    z = _dot(gd.astype(BF16), wgu_ref[...]) + bg_ref[...]
    log_sig = jnp.minimum(z, 0.0) - jnp.log1p(jnp.exp(-jnp.abs(z)))
    return q, k, v, r, log_sig * (1.0 / GLA_GATE_NORMALIZER)


def _gla_out_compute(o, r, h, hn_ref, wo_ref):
    parts = []
    for hd in range(GLA_HEADS):
        vs = slice(hd * GLA_DV, (hd + 1) * GLA_DV)
        on = _rms(o[:, vs].astype(F32), hn_ref[...])
        parts.append((on * _silu(r[:, vs].astype(F32))).astype(BF16))
    return h + _dot(jnp.concatenate(parts, axis=1), wo_ref[...])


def _mla_proj_compute(x, cos_t, sin_t, nw_ref, wd_ref, qn_ref, wqn_ref, wqr_ref, kvn_ref, wuk_ref,
                      q_out, kc_out, vt_out):
    m = x.shape[0]
    scale = (MLA_NOPE + MLA_ROPE) ** -0.5 * LOG2_E
    xn = _rms(x, nw_ref[...]).astype(BF16)
    xd = _dot(xn, wd_ref[...])
    cqn = _rms(xd[:, :MLA_Q_RANK], qn_ref[...]).astype(BF16)
    ckv = _rms(xd[:, MLA_Q_RANK:MLA_Q_RANK + MLA_KV_RANK], kvn_ref[...])
    lane = lax.broadcasted_iota(jnp.int32, (m, LANES), 1)
    first_half = (lane & (MLA_ROPE // 2)) == 0

    def rope(t):
        swapped = jnp.where(first_half, pltpu.roll(t, LANES - MLA_ROPE // 2, 1),
                            pltpu.roll(t, MLA_ROPE // 2, 1))
        return t * cos_t + swapped * sin_t

    kr2 = rope(xd[:, MLA_Q_RANK + MLA_KV_RANK:])
    kc_out[:, :MLA_KV_RANK] = ckv.astype(BF16)
    kc_out[:, MLA_KV_RANK:] = jnp.where(lane < MLA_ROPE, kr2, 0.0).astype(BF16)
    if vt_out is not None:
        for j in range(m // TQ):
            vt_out[j, :MLA_KV_RANK, :] = ckv[j * TQ:(j + 1) * TQ, :].T.astype(BF16)
            vt_out[j, MLA_KV_RANK:, :] = jnp.ones((VT_ROWS - MLA_KV_RANK, TQ), BF16)

    qn = _dot(cqn, wqn_ref[...])
    qr = _dot(cqn, wqr_ref[...])
    for c in range(MLA_HEADS // 2):
        rr = rope(qr[:, c * LANES:(c + 1) * LANES]) * scale
        for e in range(2):
            hd = 2 * c + e
            ql = _dot(qn[:, hd * MLA_NOPE:(hd + 1) * MLA_NOPE].astype(BF16), wuk_ref[hd]) * scale
            q_out[hd, :, :MLA_KV_RANK] = ql.astype(BF16)
            rot = rr if e == 0 else pltpu.roll(rr, MLA_ROPE, 1)
            q_out[hd, :, MLA_KV_RANK:] = rot.astype(BF16)
    return ckv, kr2[:, :MLA_ROPE]


def _mla_out_compute(o_lat, h, wuv_ref, wo_ref):
    parts = [_dot(o_lat[hd], wuv_ref[hd]).astype(BF16) for hd in range(MLA_HEADS)]
    return h + _dot(jnp.concatenate(parts, axis=1), wo_ref[...])


def _layer(shape, layer):
    zeros = (0,) * len(shape)
    return pl.BlockSpec((None,) + tuple(shape), lambda *_: (layer,) + zeros, pipeline_mode=pl.Buffered(1))


def _layer_specs(params):
    return [_layer(a.shape[1:], layer) for a, layer in params]


def _arrays(params):
    return [a for a, _ in params]


def _ffn_specs(ffn):
    nw, l_n = ffn[0]
    return [_layer(nw.shape[1:], l_n)] + [pl.BlockSpec(memory_space=pl.ANY)] * 3


def _ffn_layers(ffn):
    return tuple(layer for _, layer in ffn[1:])


def _ffn_scratch():
    return [pltpu.VMEM((TM, D_FF), BF16),
            pltpu.VMEM((N_SLABS, D_MODEL, FF_CHUNK), BF16), pltpu.VMEM((N_SLABS, D_MODEL, FF_CHUNK), BF16),
            pltpu.VMEM((N_SLABS, FF_CHUNK, D_MODEL), BF16),
            pltpu.VMEM((2, D_MODEL, FF_CHUNK), F32), pltpu.VMEM((2, D_MODEL, FF_CHUNK), F32),
            pltpu.VMEM((2, FF_CHUNK, D_MODEL), F32), pltpu.SemaphoreType.DMA((3, 2))]


N_FFN_SCRATCH = 8


def _ffn_gla_proj_kernel(*refs, x_firsts, n_tiles, ffn_layers):
    ns = len(x_firsts)
    x_refs, xm_ref = refs[:ns], refs[ns]
    ffn_w = refs[ns + 1:ns + 5]
    proj_w = refs[ns + 5:ns + 9]
    outs = refs[ns + 9:ns + 15]
    outs_m = refs[ns + 15:ns + 21]
    ffn_scr = refs[ns + 21:]
    i = pl.program_id(0)

    def body(x, o, first):
        y = _ffn_compute(x, ffn_w, ffn_scr, ffn_layers, first)
        o[0][...] = y
        for ref, val in zip(o[1:], _gla_proj_compute(y, *proj_w)):
            ref[...] = val

    @pl.when(i < n_tiles)
    def _():
        body(_select(i, x_refs, x_firsts), outs, i == 0)

    @pl.when(i == n_tiles)
    def _():
        body(xm_ref[...], outs_m, None)


def _ffn_gla_proj(x_segs, x_meta, ffn, proj):
    n_tiles = sum(s[2] for s in x_segs)
    n_main = n_tiles * TM
    n_mt = x_meta.shape[0]
    widths = [(D_MODEL, F32), (GLA_KEY, BF16), (GLA_KEY, BF16), (GLA_VAL, BF16), (GLA_VAL, BF16),
              (GLA_KEY, F32)]
    in_specs = [_tiles((TM, D_MODEL), 0, f, c) for _, f, c in x_segs]
    in_specs += [_whole((n_mt, D_MODEL))] + _ffn_specs(ffn) + _layer_specs(proj)
    return pl.pallas_call(
        functools.partial(_ffn_gla_proj_kernel, x_firsts=tuple(s[1] for s in x_segs), n_tiles=n_tiles,
                          ffn_layers=_ffn_layers(ffn)),
        grid=(n_tiles + 1,),
        in_specs=in_specs,
        out_specs=[_tiles((TM, w), 0, 0, n_tiles) for w, _ in widths]
        + [_whole((n_mt, w)) for w, _ in widths],
        out_shape=[jax.ShapeDtypeStruct((n_main, w), d) for w, d in widths]
        + [jax.ShapeDtypeStruct((n_mt, w), d) for w, d in widths],
        scratch_shapes=_ffn_scratch(),
        compiler_params=_params("arbitrary"),
        name="ffn_gla_proj",
    )(*[s[0] for s in x_segs], x_meta, *_arrays(ffn), *_arrays(proj))


def _gla_scan_kernel(*refs, chunk, n_chunks, streams, group, n_steps, has_s0):
    q_ref, k_ref, v_ref, gl_ref = refs[:4]
    s0_ref = refs[4] if has_s0 else None
    o_ref, s_out_ref, st_scr = refs[4 + has_s0:]
    step = pl.program_id(1)

    @pl.when(step == 0)
    def _():
        for s in range(streams):
            for h in range(GLA_HEADS):
                if has_s0:
                    st_scr[s * GLA_HEADS + h] = s0_ref[s, h]
                else:
                    st_scr[s * GLA_HEADS + h] = jnp.zeros((GLA_DK, GLA_DV), F32)

    total = n_chunks * streams
    g_rows = chunk * group
    row = lax.broadcasted_iota(jnp.int32, (g_rows, g_rows), 0)
    col = lax.broadcasted_iota(jnp.int32, (g_rows, g_rows), 1)
    causal = ((row // chunk) == (col // chunk)) & (row >= col)
    tri = jnp.where(causal, 1.0, 0.0).astype(BF16)
    scale = GLA_DK ** -0.5
    groups = [slice(g * g_rows, (g + 1) * g_rows) for g in range(total // group)]
    in_group = [slice(c * chunk, (c + 1) * chunk) for c in range(group)]
    key_slices = [slice(h * GLA_DK, (h + 1) * GLA_DK) for h in range(GLA_HEADS)]
    val_slices = [slice(h * GLA_DV, (h + 1) * GLA_DV) for h in range(GLA_HEADS)]

    gcs = []
    for rows in groups:
        glog = gl_ref[rows, :]
        hi = glog.astype(BF16)
        lo = (glog - hi.astype(F32)).astype(BF16)
        gcs.append(_dot(tri, hi) + _dot(tri, lo))
    sub = lax.broadcasted_iota(jnp.int32, (GLA_DK, GLA_KEY), 0)
    dec_rows = jnp.zeros((GLA_DK, GLA_KEY), F32)
    qgs, kgs, kds = [], [], []
    for g, (rows, gc) in enumerate(zip(groups, gcs)):
        glast = [gc[rs.stop - 1:rs.stop, :] for rs in in_group]
        g_end = jnp.concatenate([jnp.broadcast_to(t, (chunk, GLA_KEY)) for t in glast], axis=0)
        for c, t in enumerate(glast):
            dec_rows = jnp.where(sub == g * group + c, jnp.exp(t), dec_rows)
        q = q_ref[rows, :].astype(F32) * scale
        k = k_ref[rows, :].astype(F32)
        qgs.append((q * jnp.exp(gc)).astype(BF16))
        kgs.append((k * jnp.exp(-gc)).astype(BF16))
        kds.append((k * jnp.exp(g_end - gc)).astype(BF16))

    values = [[v_ref[rows, vs] for vs in val_slices] for rows in groups]
    scores = [[_dot_nt(qg[:, ks], kg[:, ks]) for ks in key_slices] for qg, kg in zip(qgs, kgs)]
    updates = [[_dot_tn(kds[g][rs, ks], values[g][h][rs, :]) for g in range(len(groups)) for rs in in_group]
               for h, ks in enumerate(key_slices)]
    intra = [[_dot(jnp.where(causal, a, 0.0).astype(BF16), vh) for a, vh in zip(sg, vg)]
             for sg, vg in zip(scores, values)]
    for h, ks in enumerate(key_slices):
        dec_cols = dec_rows[:, ks].T
        states = []
        for s in range(streams):
            state = st_scr[s * GLA_HEADS + h]
            for c in range(s * n_chunks, (s + 1) * n_chunks):
                states.append(state.astype(BF16))
                state = state * dec_cols[:, c:c + 1] + updates[h][c]
            st_scr[s * GLA_HEADS + h] = state
        for g, rows in enumerate(groups):
            for c, rs in enumerate(in_group):
                out_rows = slice(rows.start + rs.start, rows.start + rs.stop)
                o_ref[out_rows, val_slices[h]] = (
                    intra[g][h][rs, :] + _dot(qgs[g][rs, ks], states[g * group + c])).astype(BF16)

    @pl.when(step == n_steps - 1)
    def _():
        for s in range(streams):
            for h in range(GLA_HEADS):
                s_out_ref[s, h] = st_scr[s * GLA_HEADS + h]


def _gla_scan(q, k, v, gl, s0, *, row0, n_batch, n_steps, chunk, n_chunks, name, streams=1):
    assert n_batch % streams == 0 and (streams == 1 or n_steps == 1)
    rb = chunk * n_chunks * streams
    base = row0 // rb

    def rmap(b, s):
        return (base + b * n_steps + s, 0)

    def omap(b, s):
        return (b * n_steps + s, 0)

    def smap(b, s):
        return (b, 0, 0, 0)

    ins = [q, k, v, gl]
    specs = [pl.BlockSpec((rb, GLA_KEY), rmap), pl.BlockSpec((rb, GLA_KEY), rmap),
             pl.BlockSpec((rb, GLA_VAL), rmap), pl.BlockSpec((rb, GLA_KEY), rmap)]
    state_block = (streams, GLA_HEADS, GLA_DK, GLA_DV)
    if s0 is not None:
        ins.append(s0)
        specs.append(pl.BlockSpec(state_block, smap))
    group = min(n_chunks * streams, GLA_GROUP_ROWS // chunk)
    assert (n_chunks * streams) % group == 0
    kern = functools.partial(_gla_scan_kernel, chunk=chunk, n_chunks=n_chunks, streams=streams,
                             group=group, n_steps=n_steps, has_s0=s0 is not None)
    return pl.pallas_call(
        kern,
        grid=(n_batch // streams, n_steps),
        in_specs=specs,
        out_specs=[pl.BlockSpec((rb, GLA_VAL), omap), pl.BlockSpec(state_block, smap)],
        out_shape=[jax.ShapeDtypeStruct((n_batch * n_steps * chunk * n_chunks, GLA_VAL), BF16),
                   jax.ShapeDtypeStruct((n_batch,) + state_block[1:], F32)],
        scratch_shapes=[pltpu.VMEM((streams * GLA_HEADS, GLA_DK, GLA_DV), F32)],
        compiler_params=_params("parallel", "arbitrary"),
        name=name,
    )(*ins)


def _gla_out_ffn_kernel(*refs, o_firsts, n_tiles, ffn_layers):
    ns = len(o_firsts)
    o_refs = refs[:ns]
    r_ref, h_ref, om_ref, rm_ref, hm_ref, hn_ref, wo_ref = refs[ns:ns + 7]
    ffn_w = refs[ns + 7:ns + 11]
    y_ref, ym_ref = refs[ns + 11:ns + 13]
    ffn_scr = refs[ns + 13:]
    i = pl.program_id(0)

    def body(o, r, h, first):
        return _ffn_compute(_gla_out_compute(o, r, h, hn_ref, wo_ref), ffn_w, ffn_scr, ffn_layers, first)

    @pl.when(i < n_tiles)
    def _():
        y_ref[...] = body(_select(i, o_refs, o_firsts), r_ref[...], h_ref[...], i == 0)

    @pl.when(i == n_tiles)
    def _():
        ym_ref[...] = body(om_ref[...], rm_ref[...], hm_ref[...], None)


def _gla_out_ffn(o_segs, r, h, o_meta, r_meta, h_meta, out, ffn):
    n_tiles = sum(s[2] for s in o_segs)
    n_mt = h_meta.shape[0]
    in_specs = [_tiles((TM, GLA_VAL), 0, f, c) for _, f, c in o_segs]
    in_specs += [_tiles((TM, GLA_VAL), 0, 0, n_tiles), _tiles((TM, D_MODEL), 0, 0, n_tiles),
                 _whole((n_mt, GLA_VAL)), _whole((n_mt, GLA_VAL)), _whole((n_mt, D_MODEL))]
    in_specs += _layer_specs(out) + _ffn_specs(ffn)
    return pl.pallas_call(
        functools.partial(_gla_out_ffn_kernel, o_firsts=tuple(s[1] for s in o_segs), n_tiles=n_tiles,
                          ffn_layers=_ffn_layers(ffn)),
        grid=(n_tiles + 1,),
        in_specs=in_specs,
        out_specs=[_tiles((TM, D_MODEL), 0, 0, n_tiles), _whole((n_mt, D_MODEL))],
        out_shape=[jax.ShapeDtypeStruct(h.shape, F32), jax.ShapeDtypeStruct(h_meta.shape, F32)],
        scratch_shapes=_ffn_scratch(),
        compiler_params=_params("arbitrary"),
        name="gla_out_ffn",
    )(*[s[0] for s in o_segs], r, h, o_meta, r_meta, h_meta, *_arrays(out), *_arrays(ffn))


def _ffn_mla_proj_kernel(*refs, n_tiles, n_prompt, stream_tiles, ffn_layers):
    x_ref, cos_ref, sin_ref, xm_ref, cosm_ref, sinm_ref = refs[:6]
    ffn_w = refs[6:10]
    proj_w = refs[10:17]
    y_ref, q_out, kc_out, ckv_p, kr_p, ckv_b, kr_b, vt_out = refs[17:25]
    ym_ref, qm_out, kcm_out = refs[25:28]
    ffn_scr = refs[28:28 + N_FFN_SCRATCH]
    ckv_buf, kr_buf, ckvm_buf, krm_buf, sems, meta_sems = refs[28 + N_FFN_SCRATCH:]
    first_tiles = n_prompt * stream_tiles
    i = pl.program_id(0)

    def tile_copies(t):
        slot = t % 2
        stream = t // stream_tiles
        rows = pl.ds(N_META + (t % stream_tiles) * TM, TM)
        return (pltpu.make_async_copy(ckv_buf.at[slot], ckv_p.at[stream, rows, :], sems.at[slot, 0]),
                pltpu.make_async_copy(kr_buf.at[slot], kr_p.at[stream, rows, :], sems.at[slot, 1]))

    def meta_copies():
        n_meta = xm_ref.shape[0] // n_prompt
        out = []
        for b in range(n_prompt):
            src = pl.ds(b * n_meta, n_meta)
            out.append(pltpu.make_async_copy(ckvm_buf.at[src], ckv_p.at[b, pl.ds(0, n_meta), :],
                                             meta_sems.at[0]))
            out.append(pltpu.make_async_copy(krm_buf.at[src], kr_p.at[b, pl.ds(0, n_meta), :],
                                             meta_sems.at[1]))
        return out

    @pl.when(i < n_tiles)
    def _():
        y = _ffn_compute(x_ref[...], ffn_w, ffn_scr, ffn_layers, i == 0)
        y_ref[...] = y
        ckv, kr = _mla_proj_compute(y, cos_ref[...], sin_ref[...], *proj_w, q_out, kc_out, vt_out)

        @pl.when(i < first_tiles)
        def _():
            @pl.when(i >= 2)
            def _():
                for cp in tile_copies(i - 2):
                    cp.wait()

            slot = i % 2
            ckv_buf[slot] = ckv
            kr_buf[slot] = kr
            for cp in tile_copies(i):
                cp.start()

        @pl.when(i >= first_tiles)
        def _():
            ckv_b[...] = ckv
            kr_b[...] = kr

    @pl.when(i == n_tiles)
    def _():
        for t in (first_tiles - 2, first_tiles - 1):
            for cp in tile_copies(t):
                cp.wait()
        y = _ffn_compute(xm_ref[...], ffn_w, ffn_scr, ffn_layers, None)
        ym_ref[...] = y
        ckv, kr = _mla_proj_compute(y, cosm_ref[...], sinm_ref[...], *proj_w, qm_out, kcm_out, None)
        ckvm_buf[...] = ckv
        krm_buf[...] = kr
        for cp in meta_copies():
            cp.start()
        for cp in meta_copies():
            cp.wait()


def _ffn_mla_proj(x, x_meta, cos_t, sin_t, table_tile, cos_m, sin_m, ffn, proj_w, n_prompt, seq):
    n_main = x.shape[0]
    n_tiles = n_main // TM
    n_mt = x_meta.shape[0]
    kt_per_tile = TM // TQ
    stream_tiles = seq // TM
    first_tiles = n_prompt * stream_tiles
    rest_tiles = n_tiles - first_tiles
    assert first_tiles >= 2 and n_mt == n_prompt * N_META
    any_space = pl.BlockSpec(memory_space=pl.ANY)

    def rows(w):
        return _tiles((TM, w), 0, 0, n_tiles)

    table = pl.BlockSpec((TM, LANES), lambda i: (table_tile(jnp.minimum(i, n_tiles - 1)), 0))
    in_specs = [rows(D_MODEL), table, table, _whole((n_mt, D_MODEL)), _whole((n_mt, LANES)),
                _whole((n_mt, LANES))] + _ffn_specs(ffn)
    in_specs += [_resident(w.shape) for w in proj_w]
    return pl.pallas_call(
        functools.partial(_ffn_mla_proj_kernel, n_tiles=n_tiles, n_prompt=n_prompt,
                          stream_tiles=stream_tiles, ffn_layers=_ffn_layers(ffn)),
        grid=(n_tiles + 1,),
        in_specs=in_specs,
        out_specs=[rows(D_MODEL), _tiles((MLA_HEADS, TM, QK_W), 1, 0, n_tiles), rows(QK_W),
                   any_space, any_space,
                   _tiles((TM, MLA_KV_RANK), 0, first_tiles, rest_tiles),
                   _tiles((TM, MLA_ROPE), 0, first_tiles, rest_tiles),
                   _tiles((kt_per_tile, VT_ROWS, TQ), 0, 0, n_tiles),
                   _whole((n_mt, D_MODEL)), _whole((MLA_HEADS, n_mt, QK_W)), _whole((n_mt, QK_W))],
        out_shape=[jax.ShapeDtypeStruct((n_main, D_MODEL), F32),
                   jax.ShapeDtypeStruct((MLA_HEADS, n_main, QK_W), BF16),
                   jax.ShapeDtypeStruct((n_main, QK_W), BF16),
                   jax.ShapeDtypeStruct((n_prompt, N_META + seq, MLA_KV_RANK), F32),
                   jax.ShapeDtypeStruct((n_prompt, N_META + seq, MLA_ROPE), F32),
                   jax.ShapeDtypeStruct((rest_tiles * TM, MLA_KV_RANK), F32),
                   jax.ShapeDtypeStruct((rest_tiles * TM, MLA_ROPE), F32),
                   jax.ShapeDtypeStruct((n_main // TQ, VT_ROWS, TQ), BF16),
                   jax.ShapeDtypeStruct((n_mt, D_MODEL), F32),
                   jax.ShapeDtypeStruct((MLA_HEADS, n_mt, QK_W), BF16),
                   jax.ShapeDtypeStruct((n_mt, QK_W), BF16)],
        scratch_shapes=_ffn_scratch() + [
            pltpu.VMEM((2, TM, MLA_KV_RANK), F32), pltpu.VMEM((2, TM, MLA_ROPE), F32),
            pltpu.VMEM((n_mt, MLA_KV_RANK), F32), pltpu.VMEM((n_mt, MLA_ROPE), F32),
            pltpu.SemaphoreType.DMA((2, 2)), pltpu.SemaphoreType.DMA((2,))],
        compiler_params=_params("arbitrary"),
        name="ffn_mla_proj",
    )(x, cos_t, sin_t, x_meta, cos_m, sin_m, *_arrays(ffn), *proj_w)


def _attn_prompt_kernel(*refs, n_prob):
    ins = [refs[5 * p:5 * p + 5] for p in range(n_prob)]
    o_refs = refs[5 * n_prob:6 * n_prob]
    scr = [refs[6 * n_prob + 4 * p:6 * n_prob + 4 * p + 4] for p in range(n_prob)]
    i = pl.program_id(1)
    cols = MLA_HEADS * TQ
    qs = [q_ref[...].reshape(cols, QK_W) for q_ref, *_ in ins]

    def scores(p, t):
        return _dot_nt(ins[p][1][t], qs[p])

    def consume(p, parts):
        _, _, m_scr, acc_scr = scr[p]
        m_cur = None
        for s, _ in parts:
            mx = jnp.max(s, axis=0, keepdims=True)
            m_cur = mx if m_cur is None else jnp.maximum(m_cur, mx)
        m_old = m_scr[...]
        m_new = jnp.maximum(m_old, m_cur)
        alpha = jnp.exp2(m_old - m_new)
        pv = None
        for s, vt in parts:
            term = _dot(vt, jnp.exp2(s - m_new).astype(BF16))
            pv = term if pv is None else pv + term
        acc_scr[...] = alpha * acc_scr[...] + pv
        m_scr[...] = m_new

    for p in range(n_prob):
        s_scr, sm_scr, m_scr, acc_scr = scr[p]
        m_scr[...] = jnp.full(m_scr.shape, -jnp.inf, F32)
        acc_scr[...] = jnp.zeros(acc_scr.shape, F32)
        s_first = _dot_nt(jnp.concatenate([ins[p][1][0], ins[p][2][...]], axis=0), qs[p])
        s_scr[0] = s_first[:TQ]
        sm_scr[...] = s_first[TQ:]

    def pair(jj, carry):
        j = 2 * jj
        for cur, nxt, t in ((0, 1, j), (1, 0, j + 1)):
            for p in range(n_prob):
                scr[p][0][nxt] = scores(p, t + 1)
            for p in range(n_prob):
                consume(p, [(scr[p][0][cur], ins[p][3][t])])
        return carry

    lax.fori_loop(0, lax.shift_right_logical(i, 1), pair, 0)

    @pl.when((i & 1) == 1)
    def _():
        for p in range(n_prob):
            scr[p][0][1] = scores(p, i)
        for p in range(n_prob):
            consume(p, [(scr[p][0][0], ins[p][3][i - 1])])

    key = lax.broadcasted_iota(jnp.int32, (TQ, cols), 0)
    qry = lax.broadcasted_iota(jnp.int32, (TQ, cols), 1) % TQ
    visible = (key // CHUNK) <= (qry // CHUNK)
    for p in range(n_prob):
        s_own = jnp.where(visible, scr[p][0][i & 1], -jnp.inf)
        consume(p, [(s_own, ins[p][3][i]), (scr[p][1][...], ins[p][4][0])])

    for p in range(n_prob):
        acc_scr = scr[p][3]
        for h in range(MLA_HEADS):
            cs = slice(h * TQ, (h + 1) * TQ)
            total = acc_scr[MLA_KV_RANK:MLA_KV_RANK + 1, cs]
            o_refs[p][h] = (acc_scr[:MLA_KV_RANK, cs] / total).T.astype(BF16)


def _attn_prompt(q, k_tiles, kc_meta, vt_tiles, vt_meta, *, n_batch, seq, n_prob):
    steps = seq // TQ
    cols = MLA_HEADS * TQ
    per = n_batch // n_prob
    assert per * n_prob == n_batch
    in_specs = []
    for p in range(n_prob):
        def stream(b, p=p):
            return p * per + b
        in_specs += [pl.BlockSpec((MLA_HEADS, TQ, QK_W), lambda b, i, f=stream: (0, f(b) * steps + i, 0)),
                     pl.BlockSpec((steps, TQ, QK_W), lambda b, i, f=stream: (f(b), 0, 0)),
                     pl.BlockSpec((N_META, QK_W), lambda b, i, f=stream: (f(b), 0)),
                     pl.BlockSpec((steps, VT_ROWS, TQ), lambda b, i, f=stream: (f(b), 0, 0)),
                     pl.BlockSpec((1, VT_ROWS, N_META), lambda b, i, f=stream: (f(b), 0, 0))]
    scratch = [pltpu.VMEM((2, TQ, cols), F32), pltpu.VMEM((N_META, cols), F32),
               pltpu.VMEM((1, cols), F32), pltpu.VMEM((VT_ROWS, cols), F32)]
    return pl.pallas_call(
        functools.partial(_attn_prompt_kernel, n_prob=n_prob),
        grid=(per, steps),
        in_specs=in_specs,
        out_specs=[pl.BlockSpec((MLA_HEADS, TQ, MLA_KV_RANK), lambda b, i: (0, b * steps + i, 0))] * n_prob,
        out_shape=[jax.ShapeDtypeStruct((MLA_HEADS, per * seq, MLA_KV_RANK), BF16)] * n_prob,
        scratch_shapes=scratch * n_prob,
        compiler_params=_params("parallel", "arbitrary"),
        name="mla_attn_prompt",
    )(*([q, k_tiles, kc_meta, vt_tiles, vt_meta] * n_prob))


def _attn_full_kernel(*refs, n_q, streams, has_past):
    if has_past:
        q_ref, kn_ref, pl_ref, pr_ref, o_ref = refs
    else:
        q_ref, kn_ref, o_ref = refs
    rows = MLA_HEADS * n_q
    blocks = [slice(s * n_q, (s + 1) * n_q) for s in range(streams)]
    qs = [q_ref[:, rs, :].reshape(rows, QK_W) for rs in blocks]
    kns = [kn_ref[rs, :] for rs in blocks]
    s_new = [_dot_nt(q, kn) for q, kn in zip(qs, kns)]
    if has_past:
        lats = [pl_ref[s].astype(BF16) for s in range(streams)]
        rp_ts = [pr_ref[s].astype(BF16) for s in range(streams)]
        s_past = [_dot_nt(q[:, :MLA_KV_RANK], lat) + _dot(q[:, MLA_KV_RANK:MLA_KV_RANK + MLA_ROPE], rp_t)
                  for q, lat, rp_t in zip(qs, lats, rp_ts)]
    for s, rs in enumerate(blocks):
        m = jnp.max(s_new[s], axis=1, keepdims=True)
        if has_past:
            m = jnp.maximum(m, jnp.max(s_past[s], axis=1, keepdims=True))
        p_n = jnp.exp2(s_new[s] - m)
        l = jnp.sum(p_n, axis=1, keepdims=True)
        acc = _dot(p_n.astype(BF16), kns[s][:, :MLA_KV_RANK])
        if has_past:
            p_p = jnp.exp2(s_past[s] - m)
            l = l + jnp.sum(p_p, axis=1, keepdims=True)
            acc = acc + _dot(p_p.astype(BF16), lats[s])
        o_ref[:, rs, :] = (acc / l).astype(BF16).reshape(MLA_HEADS, n_q, MLA_KV_RANK)


def _attn_full(q, kc, past_lat, past_rope_t, *, row0, n_batch, n_q, streams, name):
    assert n_batch % streams == 0
    nb = n_q * streams
    base = row0 // nb
    has_past = past_lat is not None
    ins = [q, kc]
    specs = [pl.BlockSpec((MLA_HEADS, nb, QK_W), lambda b: (0, base + b, 0)),
             pl.BlockSpec((nb, QK_W), lambda b: (base + b, 0))]
    if has_past:
        past = past_lat.shape[1]
        ins += [past_lat, past_rope_t]
        specs += [pl.BlockSpec((streams, past, MLA_KV_RANK), lambda b: (b, 0, 0)),
                  pl.BlockSpec((streams, MLA_ROPE, past), lambda b: (b, 0, 0))]
    return pl.pallas_call(
        functools.partial(_attn_full_kernel, n_q=n_q, streams=streams, has_past=has_past),
        grid=(n_batch // streams,),
        in_specs=specs,
        out_specs=pl.BlockSpec((MLA_HEADS, nb, MLA_KV_RANK), lambda b: (0, b, 0)),
        out_shape=jax.ShapeDtypeStruct((MLA_HEADS, n_batch * n_q, MLA_KV_RANK), BF16),
        compiler_params=_params("parallel"),
        name=name,
    )(*ins)


def _mla_out_ffn_final_kernel(*refs, o_firsts, y_firsts, n_tiles, ffn_layers):
    ns, ny = len(o_firsts), len(y_firsts)
    o_refs = refs[:ns]
    h_ref, om_ref, hm_ref, wuv_ref, wo_ref = refs[ns:ns + 5]
    ffn_w = refs[ns + 5:ns + 9]
    fw_ref = refs[ns + 9]
    y_refs = refs[ns + 10:ns + 10 + ny]
    ym_ref = refs[ns + 10 + ny]
    ffn_scr = refs[ns + 11 + ny:]
    i = pl.program_id(0)

    def body(o_lat, h, first):
        y = _mla_out_compute(o_lat, h, wuv_ref, wo_ref)
        return _rms(_ffn_compute(y, ffn_w, ffn_scr, ffn_layers, first), fw_ref[...])

    @pl.when(i < n_tiles)
    def _():
        y = body(_select(i, o_refs, o_firsts), h_ref[...], i == 0)
        bounds = list(y_firsts[1:]) + [n_tiles]
        for y_ref, lo, hi in zip(y_refs, y_firsts, bounds):
            @pl.when((i >= lo) & (i < hi))
            def _(y_ref=y_ref):
                y_ref[...] = y

    @pl.when(i == n_tiles)
    def _():
        ym_ref[...] = body(om_ref[...], hm_ref[...], None)


def _mla_out_ffn_final(o_segs, y_segs, h, o_meta, h_meta, w_uv_t, w_out, ffn, final_w):
    n_tiles = sum(s[2] for s in o_segs)
    n_mt = h_meta.shape[0]
    in_specs = [_tiles((MLA_HEADS, TM, MLA_KV_RANK), 1, f, c) for _, f, c in o_segs]
    in_specs += [_tiles((TM, D_MODEL), 0, 0, n_tiles), _whole(o_meta.shape), _whole(h_meta.shape),
                 _resident(w_uv_t.shape), _resident(w_out.shape)]
    in_specs += _ffn_specs(ffn) + [_resident((1, D_MODEL))]
    return pl.pallas_call(
        functools.partial(_mla_out_ffn_final_kernel, o_firsts=tuple(s[1] for s in o_segs),
                          y_firsts=tuple(f for f, _ in y_segs), n_tiles=n_tiles, ffn_layers=_ffn_layers(ffn)),
        grid=(n_tiles + 1,),
        in_specs=in_specs,
        out_specs=[_tiles((TM, D_MODEL), 0, f, c) for f, c in y_segs] + [_whole((n_mt, D_MODEL))],
        out_shape=[jax.ShapeDtypeStruct((c * TM, D_MODEL), F32) for _, c in y_segs]
        + [jax.ShapeDtypeStruct((n_mt, D_MODEL), F32)],
        scratch_shapes=_ffn_scratch(),
        compiler_params=_params("arbitrary"),
        name="mla_out_ffn_final",
    )(*[s[0] for s in o_segs], h, o_meta, h_meta, w_uv_t, w_out, *_arrays(ffn), final_w.reshape(1, D_MODEL))


def _rope_tables(pos):
    half = MLA_ROPE // 2
    lane = jnp.arange(LANES)
    inv = ROPE_THETA ** (-(lane % half).astype(F32) / half)
    sign = jnp.where((lane // half) % 2 == 0, -1.0, 1.0).astype(F32)
    ang = pos[:, None] * inv[None, :]
    return jnp.cos(ang), jnp.sin(ang) * sign[None, :]


def kernel(x_prompt, x_sample, state_gla, cache_mla_latent, cache_mla_rope, meta_tokens, ffn1_norm, ffn1_w_gate, ffn1_w_up, ffn1_w_down, mix_norm, gla_w_in, gla_w_gate_up, gla_b_gate, gla_head_norm, gla_w_out, mla_w_down, mla_q_norm, mla_w_uq, mla_kv_norm, mla_w_uk, mla_w_uv, mla_w_out, ffn2_norm, ffn2_w_gate, ffn2_w_up, ffn2_w_down, final_norm):
    bp, seq, _ = x_prompt.shape
    bs, ls, _ = x_sample.shape
    past = cache_mla_latent.shape[2]
    n_fr = bp * seq
    n_sm = bs * ls
    n_mt = bp * N_META
    assert ffn1_norm.shape[0] == 2
    assert n_fr % TM == 0 and n_sm % TM == 0 and TM % TQ == 0 and seq % TQ == 0 and TQ % CHUNK == 0
    assert past % CHUNK == 0 and ls <= CHUNK and N_META <= CHUNK
    fr_tiles = n_fr // TM
    sm_tiles = n_sm // TM

    def vec(p):
        return p.reshape(p.shape[0], 1, p.shape[1])

    ffn1 = (vec(ffn1_norm), ffn1_w_gate, ffn1_w_up, ffn1_w_down)
    ffn2 = (vec(ffn2_norm), ffn2_w_gate, ffn2_w_up, ffn2_w_down)

    def at(params, layer):
        return [(p, layer) for p in params]

    x_meta = jnp.tile(meta_tokens.astype(F32), (bp, 1))
    gla_proj = [(vec(mix_norm), 0)] + at((jnp.swapaxes(gla_w_in, 1, 2).astype(BF16), gla_w_gate_up.astype(BF16),
                                          vec(gla_b_gate)), 0)
    (h, q, k, v, r, gl, h_m, q_m, k_m, v_m, r_m, gl_m) = _ffn_gla_proj(
        [(x_prompt.reshape(n_fr, D_MODEL), 0, fr_tiles), (x_sample.reshape(n_sm, D_MODEL), fr_tiles, sm_tiles)],
        x_meta, at(ffn1, 0), gla_proj)
    o_m, s_meta = _gla_scan(q_m, k_m, v_m, gl_m, None, row0=0, n_batch=bp, n_steps=1,
                            chunk=N_META, n_chunks=1, streams=bp, name="gla_scan_meta")
    o_f, s_p = _gla_scan(q, k, v, gl, s_meta, row0=0, n_batch=bp, n_steps=seq // GLA_STEP_ROWS,
                         chunk=CHUNK, n_chunks=GLA_STEP_ROWS // CHUNK, name="gla_scan_frames")
    o_s, s_s = _gla_scan(q, k, v, gl, state_gla.reshape(state_gla.shape[1:]), row0=n_fr, n_batch=bs,
                         n_steps=1, chunk=ls, n_chunks=1, streams=TQ // ls, name="gla_scan_sample")
    h, h_m = _gla_out_ffn([(o_f, 0, fr_tiles), (o_s, fr_tiles, sm_tiles)], r, h, o_m, r_m, h_m,
                          at((vec(gla_head_norm), gla_w_out.astype(BF16)), 0), at(ffn2, 0))

    seq_blocks = seq // TM
    cos_t, sin_t = _rope_tables(jnp.concatenate([N_META + jnp.arange(seq, dtype=F32),
                                                 jnp.tile(past + jnp.arange(ls, dtype=F32), TM // ls)]))
    cos_m, sin_m = _rope_tables(jnp.tile(jnp.arange(N_META, dtype=F32), bp))

    def table_tile(i):
        return jnp.where(i < fr_tiles, i % seq_blocks, seq_blocks)
    w_down = mla_w_down[0]
    w_down = jnp.concatenate([w_down, w_down[:, -MLA_ROPE:]], axis=1).astype(BF16)
    w_uq = mla_w_uq[0].reshape(MLA_Q_RANK, MLA_HEADS, MLA_NOPE + MLA_ROPE)
    w_qn = w_uq[:, :, :MLA_NOPE].reshape(MLA_Q_RANK, MLA_HEADS * MLA_NOPE).astype(BF16)
    w_qr = w_uq[:, :, MLA_NOPE:].reshape(MLA_Q_RANK, MLA_HEADS * MLA_ROPE).astype(BF16)
    w_uk_t = jnp.transpose(mla_w_uk[0], (1, 2, 0)).astype(BF16)
    w_uv_t = jnp.transpose(mla_w_uv[0], (1, 0, 2)).astype(BF16)
    mla_proj_w = (mix_norm[1].reshape(1, D_MODEL), w_down, mla_q_norm[0].reshape(1, MLA_Q_RANK), w_qn, w_qr,
                  mla_kv_norm[0].reshape(1, MLA_KV_RANK), w_uk_t)
    (h, qa, kc, ckv_p, kr_p, ckv_s, kr_s, vt, h_m, qa_m, kc_m) = _ffn_mla_proj(
        h, h_m, cos_t, sin_t, table_tile, cos_m, sin_m, at(ffn1, 1), mla_proj_w, bp, seq)
    k_tiles = kc.reshape((n_fr + n_sm) // TQ, TQ, QK_W)
    vt_meta = jnp.swapaxes(kc_m[:, :MLA_KV_RANK].reshape(bp, N_META, MLA_KV_RANK), 1, 2)
    vt_meta = jnp.concatenate([vt_meta, jnp.ones((bp, VT_ROWS - MLA_KV_RANK, N_META), BF16)], axis=1)
    ol_f = _attn_prompt(qa, k_tiles, kc_m, vt, vt_meta, n_batch=bp, seq=seq, n_prob=ATT_STREAMS)
    part_tiles = fr_tiles // ATT_STREAMS
    past_rope_t = jnp.swapaxes(cache_mla_rope.reshape(bs, past, MLA_ROPE), 1, 2)
    ol_s = _attn_full(qa, kc, cache_mla_latent.reshape(bs, past, MLA_KV_RANK), past_rope_t,
                      row0=n_fr, n_batch=bs, n_q=ls, streams=4, name="mla_attn_sample")
    ol_m = _attn_full(qa_m, kc_m, None, None, row0=0, n_batch=bp, n_q=N_META, streams=bp,
                      name="mla_attn_meta")
    y_prompt, y_sample, _ = _mla_out_ffn_final(
        [(o, p * part_tiles, part_tiles) for p, o in enumerate(ol_f)] + [(ol_s, fr_tiles, sm_tiles)],
        [(0, fr_tiles), (fr_tiles, sm_tiles)], h, ol_m, h_m, w_uv_t, mla_w_out[0].astype(BF16),
        at(ffn2, 1), final_norm)

    return (y_prompt.reshape(bp, seq, D_MODEL), y_sample.reshape(bs, ls, D_MODEL),
            s_p[None], s_s[None], ckv_p[None], kr_p[None],
            ckv_s.reshape(1, bs, ls, MLA_KV_RANK), kr_s.reshape(1, bs, ls, MLA_ROPE))
```

```python
import functools

import jax
import jax.numpy as jnp
from jax import lax
from jax.experimental import pallas as pl
from jax.experimental.pallas import tpu as pltpu

F32 = jnp.float32
BF16 = jnp.bfloat16

D_MODEL = 1024
D_FF = 2816
RMS_EPS = 1e-6
N_META = 16
CHUNK = 64

GLA_HEADS = 4
GLA_DK = 128
GLA_DV = 256
GLA_KEY = GLA_HEADS * GLA_DK
GLA_VAL = GLA_HEADS * GLA_DV
GLA_RANK = 16
GLA_GATE_NORMALIZER = 16.0

MLA_HEADS = 8
MLA_Q_RANK = 384
MLA_KV_RANK = 256
MLA_NOPE = 128
MLA_ROPE = 64
ROPE_THETA = 10000.0
LOG2_E = 1.4426950408889634

LANES = 128
BF16_ROWS = 16
VMEM_LIMIT = 58 * 1024 * 1024
QK_W = MLA_KV_RANK + LANES
VT_ROWS = MLA_KV_RANK + BF16_ROWS

TM = 512
FF_CHUNK = 256
TQ = 256
ATT_STREAMS = 2
GLA_GROUP_ROWS = 256
GLA_STEP_ROWS = 512
N_SLABS = D_FF // FF_CHUNK

NT_DIMS = (((1,), (1,)), ((), ()))
TN_DIMS = (((0,), (0,)), ((), ()))


def _dot(a, b):
    return jnp.dot(a, b, preferred_element_type=F32)


def _dot_nt(a, b):
    return lax.dot_general(a, b, NT_DIMS, preferred_element_type=F32)


def _dot_tn(a, b):
    return lax.dot_general(a, b, TN_DIMS, preferred_element_type=F32)


def _rms(x, w):
    return x * lax.rsqrt(jnp.mean(x * x, axis=-1, keepdims=True) + RMS_EPS) * w


def _silu(x):
    return x * jax.nn.sigmoid(x)


def _params(*sem):
    return pltpu.CompilerParams(dimension_semantics=sem, vmem_limit_bytes=VMEM_LIMIT)


def _resident(shape):
    zeros = (0,) * len(shape)
    return pl.BlockSpec(shape, lambda *_: zeros, pipeline_mode=pl.Buffered(1))


def _whole(shape):
    zeros = (0,) * len(shape)
    return pl.BlockSpec(shape, lambda *_: zeros)


def _tiles(block, row_axis=0, first=0, count=None):
    nd = len(block)

    def imap(i):
        t = i - first
        if count is not None:
            t = jnp.clip(t, 0, count - 1)
        idx = [0] * nd
        idx[row_axis] = t
        return tuple(idx)

    return pl.BlockSpec(block, imap)


def _select(i, refs, firsts):
    val = refs[0][...]
    for ref, first in zip(refs[1:], firsts[1:]):
        val = jnp.where(i >= first, ref[...], val)
    return val


def _ffn_compute(x, ffn_w, ffn_scr, layers, first):
    nw_ref, wg_hbm, wu_hbm, wd_hbm = ffn_w
    h_scr, wg_scr, wu_scr, wd_scr, land_g, land_u, land_d, sems = ffn_scr
    m = x.shape[0]
    xn = _rms(x, nw_ref[...]).astype(BF16)

    def slab_copy(kind, c):
        slot = c % 2
        cols = pl.ds(c * FF_CHUNK, FF_CHUNK)
        if kind == 0:
            src, dst = wg_hbm.at[layers[0], :, cols], land_g.at[slot]
        elif kind == 1:
            src, dst = wu_hbm.at[layers[1], :, cols], land_u.at[slot]
        else:
            src, dst = wd_hbm.at[layers[2], cols, :], land_d.at[slot]
        return pltpu.make_async_copy(src, dst, sems.at[kind, slot])

    def hidden(c):
        g = _dot(xn, wg_scr[c])
        u = _dot(xn, wu_scr[c])
        h_scr[:m, c * FF_CHUNK:(c + 1) * FF_CHUNK] = (_silu(g) * u).astype(BF16)

    def fill():
        for c in range(N_SLABS):
            hidden(c)

    def fill_streaming():
        for c in range(2):
            for kind in range(3):
                slab_copy(kind, c).start()
        for c in range(N_SLABS):
            for kind, (land, scr) in enumerate(((land_g, wg_scr), (land_u, wu_scr), (land_d, wd_scr))):
                slab_copy(kind, c).wait()
                scr[c] = land[c % 2].astype(BF16)
                if c + 2 < N_SLABS:
                    slab_copy(kind, c + 2).start()
            hidden(c)

    if first:
        fill_streaming()
    else:
        fill()
    return x + 0.5 * _dot(h_scr[:m, :], wd_scr[...].reshape(D_FF, D_MODEL))


def _gla_proj_compute(x, nw_ref, w_ref, wgu_ref, bg_ref):
    c_v = 2 * GLA_KEY
    c_r = c_v + GLA_VAL
    c_g = c_r + GLA_VAL
    xn = _rms(x, nw_ref[...]).astype(BF16)
    q = _dot_nt(xn, w_ref[0:GLA_KEY, :]).astype(BF16)
    k = _dot_nt(xn, w_ref[GLA_KEY:c_v, :]).astype(BF16)
    v = _dot_nt(xn, w_ref[c_v:c_r, :]).astype(BF16)
    r = _dot_nt(xn, w_ref[c_r:c_g, :]).astype(BF16)
    gd = _dot_nt(xn, w_ref[c_g:, :])
    z = _dot(gd.astype(BF16), wgu_ref[...]) + bg_ref[...]
    log_sig = jnp.minimum(z, 0.0) - jnp.log1p(jnp.exp(-jnp.abs(z)))
    return q, k, v, r, log_sig * (1.0 / GLA_GATE_NORMALIZER)


def _gla_out_compute(o, r, h, hn_ref, wo_ref):
    parts = []
    for hd in range(GLA_HEADS):
        vs = slice(hd * GLA_DV, (hd + 1) * GLA_DV)
        on = _rms(o[:, vs].astype(F32), hn_ref[...])
        parts.append((on * _silu(r[:, vs].astype(F32))).astype(BF16))
    return h + _dot(jnp.concatenate(parts, axis=1), wo_ref[...])


def _mla_proj_compute(x, cos_t, sin_t, nw_ref, wd_ref, qn_ref, wqn_ref, wqr_ref, kvn_ref, wuk_ref,
                      q_out, kc_out, vt_out):
    m = x.shape[0]
    scale = (MLA_NOPE + MLA_ROPE) ** -0.5 * LOG2_E
    xn = _rms(x, nw_ref[...]).astype(BF16)
    xd = _dot(xn, wd_ref[...])
    cqn = _rms(xd[:, :MLA_Q_RANK], qn_ref[...]).astype(BF16)
    ckv = _rms(xd[:, MLA_Q_RANK:MLA_Q_RANK + MLA_KV_RANK], kvn_ref[...])
    lane = lax.broadcasted_iota(jnp.int32, (m, LANES), 1)
    first_half = (lane & (MLA_ROPE // 2)) == 0

    def rope(t):
        swapped = jnp.where(first_half, pltpu.roll(t, LANES - MLA_ROPE // 2, 1),
                            pltpu.roll(t, MLA_ROPE // 2, 1))
        return t * cos_t + swapped * sin_t

    kr2 = rope(xd[:, MLA_Q_RANK + MLA_KV_RANK:])
    kc_out[:, :MLA_KV_RANK] = ckv.astype(BF16)
    kc_out[:, MLA_KV_RANK:] = jnp.where(lane < MLA_ROPE, kr2, 0.0).astype(BF16)
    if vt_out is not None:
        for j in range(m // TQ):
            vt_out[j, :MLA_KV_RANK, :] = ckv[j * TQ:(j + 1) * TQ, :].T.astype(BF16)
            vt_out[j, MLA_KV_RANK:, :] = jnp.ones((VT_ROWS - MLA_KV_RANK, TQ), BF16)

    qn = _dot(cqn, wqn_ref[...])
    qr = _dot(cqn, wqr_ref[...])
    for c in range(MLA_HEADS // 2):
        rr = rope(qr[:, c * LANES:(c + 1) * LANES]) * scale
        for e in range(2):
            hd = 2 * c + e
            ql = _dot(qn[:, hd * MLA_NOPE:(hd + 1) * MLA_NOPE].astype(BF16), wuk_ref[hd]) * scale
            q_out[hd, :, :MLA_KV_RANK] = ql.astype(BF16)
            rot = rr if e == 0 else pltpu.roll(rr, MLA_ROPE, 1)
            q_out[hd, :, MLA_KV_RANK:] = rot.astype(BF16)
    return ckv, kr2[:, :MLA_ROPE]


def _mla_out_compute(o_lat, h, wuv_ref, wo_ref):
    parts = [_dot(o_lat[hd], wuv_ref[hd]).astype(BF16) for hd in range(MLA_HEADS)]
    return h + _dot(jnp.concatenate(parts, axis=1), wo_ref[...])


def _layer(shape, layer):
    zeros = (0,) * len(shape)
    return pl.BlockSpec((None,) + tuple(shape), lambda *_: (layer,) + zeros, pipeline_mode=pl.Buffered(1))


def _layer_specs(params):
    return [_layer(a.shape[1:], layer) for a, layer in params]


def _arrays(params):
    return [a for a, _ in params]


def _ffn_specs(ffn):
    nw, l_n = ffn[0]
    return [_layer(nw.shape[1:], l_n)] + [pl.BlockSpec(memory_space=pl.ANY)] * 3


def _ffn_layers(ffn):
    return tuple(layer for _, layer in ffn[1:])


def _ffn_scratch():
    return [pltpu.VMEM((TM, D_FF), BF16),
            pltpu.VMEM((N_SLABS, D_MODEL, FF_CHUNK), BF16), pltpu.VMEM((N_SLABS, D_MODEL, FF_CHUNK), BF16),
            pltpu.VMEM((N_SLABS, FF_CHUNK, D_MODEL), BF16),
            pltpu.VMEM((2, D_MODEL, FF_CHUNK), F32), pltpu.VMEM((2, D_MODEL, FF_CHUNK), F32),
            pltpu.VMEM((2, FF_CHUNK, D_MODEL), F32), pltpu.SemaphoreType.DMA((3, 2))]


N_FFN_SCRATCH = 8


def _ffn_gla_proj_kernel(*refs, x_firsts, n_tiles, ffn_layers):
    ns = len(x_firsts)
    x_refs, xm_ref = refs[:ns], refs[ns]
    ffn_w = refs[ns + 1:ns + 5]
    proj_w = refs[ns + 5:ns + 9]
    outs = refs[ns + 9:ns + 15]
    outs_m = refs[ns + 15:ns + 21]
    ffn_scr = refs[ns + 21:]
    i = pl.program_id(0)

    def body(x, o, first):
        y = _ffn_compute(x, ffn_w, ffn_scr, ffn_layers, first)
        o[0][...] = y
        for ref, val in zip(o[1:], _gla_proj_compute(y, *proj_w)):
            ref[...] = val

    @pl.when(i == 0)
    def _():
        body(_select(i, x_refs, x_firsts), outs, True)

    @pl.when((i > 0) & (i < n_tiles))
    def _():
        body(_select(i, x_refs, x_firsts), outs, False)

    @pl.when(i == n_tiles)
    def _():
        body(xm_ref[...], outs_m, False)


def _ffn_gla_proj(x_segs, x_meta, ffn, proj):
    n_tiles = sum(s[2] for s in x_segs)
    n_main = n_tiles * TM
    n_mt = x_meta.shape[0]
    widths = [(D_MODEL, F32), (GLA_KEY, BF16), (GLA_KEY, BF16), (GLA_VAL, BF16), (GLA_VAL, BF16),
              (GLA_KEY, F32)]
    in_specs = [_tiles((TM, D_MODEL), 0, f, c) for _, f, c in x_segs]
    in_specs += [_whole((n_mt, D_MODEL))] + _ffn_specs(ffn) + _layer_specs(proj)
    return pl.pallas_call(
        functools.partial(_ffn_gla_proj_kernel, x_firsts=tuple(s[1] for s in x_segs), n_tiles=n_tiles,
                          ffn_layers=_ffn_layers(ffn)),
        grid=(n_tiles + 1,),
        in_specs=in_specs,
        out_specs=[_tiles((TM, w), 0, 0, n_tiles) for w, _ in widths]
        + [_whole((n_mt, w)) for w, _ in widths],
        out_shape=[jax.ShapeDtypeStruct((n_main, w), d) for w, d in widths]
        + [jax.ShapeDtypeStruct((n_mt, w), d) for w, d in widths],
        scratch_shapes=_ffn_scratch(),
        compiler_params=_params("arbitrary"),
        name="ffn_gla_proj",
    )(*[s[0] for s in x_segs], x_meta, *_arrays(ffn), *_arrays(proj))


def _gla_scan_kernel(*refs, chunk, n_chunks, streams, group, n_steps, has_s0):
    q_ref, k_ref, v_ref, gl_ref = refs[:4]
    s0_ref = refs[4] if has_s0 else None
    o_ref, s_out_ref, st_scr = refs[4 + has_s0:]
    step = pl.program_id(1)

    @pl.when(step == 0)
    def _():
        for s in range(streams):
            for h in range(GLA_HEADS):
                if has_s0:
                    st_scr[s * GLA_HEADS + h] = s0_ref[s, h]
                else:
                    st_scr[s * GLA_HEADS + h] = jnp.zeros((GLA_DK, GLA_DV), F32)

    total = n_chunks * streams
    g_rows = chunk * group
    row = lax.broadcasted_iota(jnp.int32, (g_rows, g_rows), 0)
    col = lax.broadcasted_iota(jnp.int32, (g_rows, g_rows), 1)
    causal = ((row // chunk) == (col // chunk)) & (row >= col)
    tri = jnp.where(causal, 1.0, 0.0).astype(BF16)
    scale = GLA_DK ** -0.5
    groups = [slice(g * g_rows, (g + 1) * g_rows) for g in range(total // group)]
    in_group = [slice(c * chunk, (c + 1) * chunk) for c in range(group)]
    key_slices = [slice(h * GLA_DK, (h + 1) * GLA_DK) for h in range(GLA_HEADS)]
    val_slices = [slice(h * GLA_DV, (h + 1) * GLA_DV) for h in range(GLA_HEADS)]

    gcs = []
    for rows in groups:
        glog = gl_ref[rows, :]
        hi = glog.astype(BF16)
        lo = (glog - hi.astype(F32)).astype(BF16)
        gcs.append(_dot(tri, hi) + _dot(tri, lo))
    sub = lax.broadcasted_iota(jnp.int32, (GLA_DK, GLA_KEY), 0)
    dec_rows = jnp.zeros((GLA_DK, GLA_KEY), F32)
    qgs, kgs, kds = [], [], []
    for g, (rows, gc) in enumerate(zip(groups, gcs)):
        glast = [gc[rs.stop - 1:rs.stop, :] for rs in in_group]
        g_end = jnp.concatenate([jnp.broadcast_to(t, (chunk, GLA_KEY)) for t in glast], axis=0)
        for c, t in enumerate(glast):
            dec_rows = jnp.where(sub == g * group + c, jnp.exp(t), dec_rows)
        q = q_ref[rows, :].astype(F32) * scale
        k = k_ref[rows, :].astype(F32)
        qgs.append((q * jnp.exp(gc)).astype(BF16))
        kgs.append((k * jnp.exp(-gc)).astype(BF16))
        kds.append((k * jnp.exp(g_end - gc)).astype(BF16))

    values = [[v_ref[rows, vs] for vs in val_slices] for rows in groups]
    scores = [[_dot_nt(qg[:, ks], kg[:, ks]) for ks in key_slices] for qg, kg in zip(qgs, kgs)]
    updates = [[_dot_tn(kds[g][rs, ks], values[g][h][rs, :]) for g in range(len(groups)) for rs in in_group]
               for h, ks in enumerate(key_slices)]
    intra = [[_dot(jnp.where(causal, a, 0.0).astype(BF16), vh) for a, vh in zip(sg, vg)]
             for sg, vg in zip(scores, values)]
    for h, ks in enumerate(key_slices):
        dec_cols = dec_rows[:, ks].T
        states = []
        for s in range(streams):
            state = st_scr[s * GLA_HEADS + h]
            for c in range(s * n_chunks, (s + 1) * n_chunks):
                states.append(state.astype(BF16))
                state = state * dec_cols[:, c:c + 1] + updates[h][c]
            st_scr[s * GLA_HEADS + h] = state
        for g, rows in enumerate(groups):
            for c, rs in enumerate(in_group):
                out_rows = slice(rows.start + rs.start, rows.start + rs.stop)
                o_ref[out_rows, val_slices[h]] = (
                    intra[g][h][rs, :] + _dot(qgs[g][rs, ks], states[g * group + c])).astype(BF16)

    @pl.when(step == n_steps - 1)
    def _():
        for s in range(streams):
            for h in range(GLA_HEADS):
                s_out_ref[s, h] = st_scr[s * GLA_HEADS + h]


def _gla_scan(q, k, v, gl, s0, *, row0, n_batch, n_steps, chunk, n_chunks, name, streams=1):
    assert n_batch % streams == 0 and (streams == 1 or n_steps == 1)
    rb = chunk * n_chunks * streams
    base = row0 // rb

    def rmap(b, s):
        return (base + b * n_steps + s, 0)

    def omap(b, s):
        return (b * n_steps + s, 0)

    def smap(b, s):
        return (b, 0, 0, 0)

    ins = [q, k, v, gl]
    specs = [pl.BlockSpec((rb, GLA_KEY), rmap), pl.BlockSpec((rb, GLA_KEY), rmap),
             pl.BlockSpec((rb, GLA_VAL), rmap), pl.BlockSpec((rb, GLA_KEY), rmap)]
    state_block = (streams, GLA_HEADS, GLA_DK, GLA_DV)
    if s0 is not None:
        ins.append(s0)
        specs.append(pl.BlockSpec(state_block, smap))
    group = min(n_chunks * streams, GLA_GROUP_ROWS // chunk)
    assert (n_chunks * streams) % group == 0
    kern = functools.partial(_gla_scan_kernel, chunk=chunk, n_chunks=n_chunks, streams=streams,
                             group=group, n_steps=n_steps, has_s0=s0 is not None)
    return pl.pallas_call(
        kern,
        grid=(n_batch // streams, n_steps),
        in_specs=specs,
        out_specs=[pl.BlockSpec((rb, GLA_VAL), omap), pl.BlockSpec(state_block, smap)],
        out_shape=[jax.ShapeDtypeStruct((n_batch * n_steps * chunk * n_chunks, GLA_VAL), BF16),
                   jax.ShapeDtypeStruct((n_batch,) + state_block[1:], F32)],
        scratch_shapes=[pltpu.VMEM((streams * GLA_HEADS, GLA_DK, GLA_DV), F32)],
        compiler_params=_params("parallel", "arbitrary"),
        name=name,
    )(*ins)


def _gla_out_ffn_kernel(*refs, o_firsts, n_tiles, ffn_layers):
    ns = len(o_firsts)
    o_refs = refs[:ns]
    r_ref, h_ref, om_ref, rm_ref, hm_ref, hn_ref, wo_ref = refs[ns:ns + 7]
    ffn_w = refs[ns + 7:ns + 11]
    y_ref, ym_ref = refs[ns + 11:ns + 13]
    ffn_scr = refs[ns + 13:]
    i = pl.program_id(0)

    def body(o, r, h, first):
        return _ffn_compute(_gla_out_compute(o, r, h, hn_ref, wo_ref), ffn_w, ffn_scr, ffn_layers, first)

    @pl.when(i == 0)
    def _():
        y_ref[...] = body(_select(i, o_refs, o_firsts), r_ref[...], h_ref[...], True)

    @pl.when((i > 0) & (i < n_tiles))
    def _():
        y_ref[...] = body(_select(i, o_refs, o_firsts), r_ref[...], h_ref[...], False)

    @pl.when(i == n_tiles)
    def _():
        ym_ref[...] = body(om_ref[...], rm_ref[...], hm_ref[...], False)


def _gla_out_ffn(o_segs, r, h, o_meta, r_meta, h_meta, out, ffn):
    n_tiles = sum(s[2] for s in o_segs)
    n_mt = h_meta.shape[0]
    in_specs = [_tiles((TM, GLA_VAL), 0, f, c) for _, f, c in o_segs]
    in_specs += [_tiles((TM, GLA_VAL), 0, 0, n_tiles), _tiles((TM, D_MODEL), 0, 0, n_tiles),
                 _whole((n_mt, GLA_VAL)), _whole((n_mt, GLA_VAL)), _whole((n_mt, D_MODEL))]
    in_specs += _layer_specs(out) + _ffn_specs(ffn)
    return pl.pallas_call(
        functools.partial(_gla_out_ffn_kernel, o_firsts=tuple(s[1] for s in o_segs), n_tiles=n_tiles,
                          ffn_layers=_ffn_layers(ffn)),
        grid=(n_tiles + 1,),
        in_specs=in_specs,
        out_specs=[_tiles((TM, D_MODEL), 0, 0, n_tiles), _whole((n_mt, D_MODEL))],
        out_shape=[jax.ShapeDtypeStruct(h.shape, F32), jax.ShapeDtypeStruct(h_meta.shape, F32)],
        scratch_shapes=_ffn_scratch(),
        compiler_params=_params("arbitrary"),
        name="gla_out_ffn",
    )(*[s[0] for s in o_segs], r, h, o_meta, r_meta, h_meta, *_arrays(out), *_arrays(ffn))


def _ffn_mla_proj_kernel(*refs, n_tiles, n_prompt, stream_tiles, ffn_layers):
    x_ref, cos_ref, sin_ref, xm_ref, cosm_ref, sinm_ref = refs[:6]
    ffn_w = refs[6:10]
    proj_w = refs[10:17]
    y_ref, q_out, kc_out, ckv_p, kr_p, ckv_b, kr_b, vt_out = refs[17:25]
    ym_ref, qm_out, kcm_out = refs[25:28]
    ffn_scr = refs[28:28 + N_FFN_SCRATCH]
    ckv_buf, kr_buf, ckvm_buf, krm_buf, sems, meta_sems = refs[28 + N_FFN_SCRATCH:]
    first_tiles = n_prompt * stream_tiles
    i = pl.program_id(0)

    def tile_copies(t):
        slot = t % 2
        stream = t // stream_tiles
        rows = pl.ds(N_META + (t % stream_tiles) * TM, TM)
        return (pltpu.make_async_copy(ckv_buf.at[slot], ckv_p.at[stream, rows, :], sems.at[slot, 0]),
                pltpu.make_async_copy(kr_buf.at[slot], kr_p.at[stream, rows, :], sems.at[slot, 1]))

    def meta_copies():
        n_meta = xm_ref.shape[0] // n_prompt
        out = []
        for b in range(n_prompt):
            src = pl.ds(b * n_meta, n_meta)
            out.append(pltpu.make_async_copy(ckvm_buf.at[src], ckv_p.at[b, pl.ds(0, n_meta), :],
                                             meta_sems.at[0]))
            out.append(pltpu.make_async_copy(krm_buf.at[src], kr_p.at[b, pl.ds(0, n_meta), :],
                                             meta_sems.at[1]))
        return out

    def main_tile(first):
        y = _ffn_compute(x_ref[...], ffn_w, ffn_scr, ffn_layers, first)
        y_ref[...] = y
        ckv, kr = _mla_proj_compute(y, cos_ref[...], sin_ref[...], *proj_w, q_out, kc_out, vt_out)

        @pl.when(i < first_tiles)
        def _():
            @pl.when(i >= 2)
            def _():
                for cp in tile_copies(i - 2):
                    cp.wait()

            slot = i % 2
            ckv_buf[slot] = ckv
            kr_buf[slot] = kr
            for cp in tile_copies(i):
                cp.start()

        @pl.when(i >= first_tiles)
        def _():
            ckv_b[...] = ckv
            kr_b[...] = kr

    pl.when(i == 0)(functools.partial(main_tile, True))
    pl.when((i > 0) & (i < n_tiles))(functools.partial(main_tile, False))

    @pl.when(i == n_tiles)
    def _():
        for t in (first_tiles - 2, first_tiles - 1):
            for cp in tile_copies(t):
                cp.wait()
        y = _ffn_compute(xm_ref[...], ffn_w, ffn_scr, ffn_layers, False)
        ym_ref[...] = y
        ckv, kr = _mla_proj_compute(y, cosm_ref[...], sinm_ref[...], *proj_w, qm_out, kcm_out, None)
        ckvm_buf[...] = ckv
        krm_buf[...] = kr
        for cp in meta_copies():
            cp.start()
        for cp in meta_copies():
            cp.wait()


def _ffn_mla_proj(x, x_meta, cos_t, sin_t, table_tile, cos_m, sin_m, ffn, proj_w, n_prompt, seq):
    n_main = x.shape[0]
    n_tiles = n_main // TM
    n_mt = x_meta.shape[0]
    kt_per_tile = TM // TQ
    stream_tiles = seq // TM
    first_tiles = n_prompt * stream_tiles
    rest_tiles = n_tiles - first_tiles
    assert first_tiles >= 2 and n_mt == n_prompt * N_META
    any_space = pl.BlockSpec(memory_space=pl.ANY)

    def rows(w):
        return _tiles((TM, w), 0, 0, n_tiles)

    table = pl.BlockSpec((TM, LANES), lambda i: (table_tile(jnp.minimum(i, n_tiles - 1)), 0))
    in_specs = [rows(D_MODEL), table, table, _whole((n_mt, D_MODEL)), _whole((n_mt, LANES)),
                _whole((n_mt, LANES))] + _ffn_specs(ffn)
    in_specs += [_resident(w.shape) for w in proj_w]
    return pl.pallas_call(
        functools.partial(_ffn_mla_proj_kernel, n_tiles=n_tiles, n_prompt=n_prompt,
                          stream_tiles=stream_tiles, ffn_layers=_ffn_layers(ffn)),
        grid=(n_tiles + 1,),
        in_specs=in_specs,
        out_specs=[rows(D_MODEL), _tiles((MLA_HEADS, TM, QK_W), 1, 0, n_tiles), rows(QK_W),
                   any_space, any_space,
                   _tiles((TM, MLA_KV_RANK), 0, first_tiles, rest_tiles),
                   _tiles((TM, MLA_ROPE), 0, first_tiles, rest_tiles),
                   _tiles((kt_per_tile, VT_ROWS, TQ), 0, 0, n_tiles),
                   _whole((n_mt, D_MODEL)), _whole((MLA_HEADS, n_mt, QK_W)), _whole((n_mt, QK_W))],
        out_shape=[jax.ShapeDtypeStruct((n_main, D_MODEL), F32),
                   jax.ShapeDtypeStruct((MLA_HEADS, n_main, QK_W), BF16),
                   jax.ShapeDtypeStruct((n_main, QK_W), BF16),
                   jax.ShapeDtypeStruct((n_prompt, N_META + seq, MLA_KV_RANK), F32),
                   jax.ShapeDtypeStruct((n_prompt, N_META + seq, MLA_ROPE), F32),
                   jax.ShapeDtypeStruct((rest_tiles * TM, MLA_KV_RANK), F32),
                   jax.ShapeDtypeStruct((rest_tiles * TM, MLA_ROPE), F32),
                   jax.ShapeDtypeStruct((n_main // TQ, VT_ROWS, TQ), BF16),
                   jax.ShapeDtypeStruct((n_mt, D_MODEL), F32),
                   jax.ShapeDtypeStruct((MLA_HEADS, n_mt, QK_W), BF16),
                   jax.ShapeDtypeStruct((n_mt, QK_W), BF16)],
        scratch_shapes=_ffn_scratch() + [
            pltpu.VMEM((2, TM, MLA_KV_RANK), F32), pltpu.VMEM((2, TM, MLA_ROPE), F32),
            pltpu.VMEM((n_mt, MLA_KV_RANK), F32), pltpu.VMEM((n_mt, MLA_ROPE), F32),
            pltpu.SemaphoreType.DMA((2, 2)), pltpu.SemaphoreType.DMA((2,))],
        compiler_params=_params("arbitrary"),
        name="ffn_mla_proj",
    )(x, cos_t, sin_t, x_meta, cos_m, sin_m, *_arrays(ffn), *proj_w)


def _attn_prompt_kernel(*refs, n_prob):
    ins = [refs[5 * p:5 * p + 5] for p in range(n_prob)]
    o_refs = refs[5 * n_prob:6 * n_prob]
    scr = [refs[6 * n_prob + 4 * p:6 * n_prob + 4 * p + 4] for p in range(n_prob)]
    i = pl.program_id(1)
    cols = MLA_HEADS * TQ
    qs = [q_ref[...].reshape(cols, QK_W) for q_ref, *_ in ins]

    def scores(p, t):
        return _dot_nt(ins[p][1][t], qs[p])

    def consume(p, parts):
        _, _, m_scr, acc_scr = scr[p]
        m_cur = None
        for s, _ in parts:
            mx = jnp.max(s, axis=0, keepdims=True)
            m_cur = mx if m_cur is None else jnp.maximum(m_cur, mx)
        m_old = m_scr[...]
        m_new = jnp.maximum(m_old, m_cur)
        alpha = jnp.exp2(m_old - m_new)
        pv = None
        for s, vt in parts:
            term = _dot(vt, jnp.exp2(s - m_new).astype(BF16))
            pv = term if pv is None else pv + term
        acc_scr[...] = alpha * acc_scr[...] + pv
        m_scr[...] = m_new

    for p in range(n_prob):
        s_scr, sm_scr, m_scr, acc_scr = scr[p]
        m_scr[...] = jnp.full(m_scr.shape, -jnp.inf, F32)
        acc_scr[...] = jnp.zeros(acc_scr.shape, F32)
        s_first = _dot_nt(jnp.concatenate([ins[p][1][0], ins[p][2][...]], axis=0), qs[p])
        s_scr[0] = s_first[:TQ]
        sm_scr[...] = s_first[TQ:]

    def pair(jj, carry):
        j = 2 * jj
        for cur, nxt, t in ((0, 1, j), (1, 0, j + 1)):
            for p in range(n_prob):
                scr[p][0][nxt] = scores(p, t + 1)
            for p in range(n_prob):
                consume(p, [(scr[p][0][cur], ins[p][3][t])])
        return carry

    lax.fori_loop(0, lax.shift_right_logical(i, 1), pair, 0)

    @pl.when((i & 1) == 1)
    def _():
        for p in range(n_prob):
            scr[p][0][1] = scores(p, i)
        for p in range(n_prob):
            consume(p, [(scr[p][0][0], ins[p][3][i - 1])])

    key = lax.broadcasted_iota(jnp.int32, (TQ, cols), 0)
    qry = lax.broadcasted_iota(jnp.int32, (TQ, cols), 1) % TQ
    visible = (key // CHUNK) <= (qry // CHUNK)
    for p in range(n_prob):
        s_own = jnp.where(visible, scr[p][0][i & 1], -jnp.inf)
        consume(p, [(s_own, ins[p][3][i]), (scr[p][1][...], ins[p][4][0])])

    for p in range(n_prob):
        acc_scr = scr[p][3]
        for h in range(MLA_HEADS):
            cs = slice(h * TQ, (h + 1) * TQ)
            total = acc_scr[MLA_KV_RANK:MLA_KV_RANK + 1, cs]
            o_refs[p][h] = (acc_scr[:MLA_KV_RANK, cs] / total).T.astype(BF16)


def _attn_prompt(q, k_tiles, kc_meta, vt_tiles, vt_meta, *, n_batch, seq, n_prob):
    steps = seq // TQ
    cols = MLA_HEADS * TQ
    per = n_batch // n_prob
    assert per * n_prob == n_batch
    in_specs = []
    for p in range(n_prob):
        def stream(b, p=p):
            return p * per + b
        in_specs += [pl.BlockSpec((MLA_HEADS, TQ, QK_W), lambda b, i, f=stream: (0, f(b) * steps + i, 0)),
                     pl.BlockSpec((steps, TQ, QK_W), lambda b, i, f=stream: (f(b), 0, 0)),
                     pl.BlockSpec((N_META, QK_W), lambda b, i, f=stream: (f(b), 0)),
                     pl.BlockSpec((steps, VT_ROWS, TQ), lambda b, i, f=stream: (f(b), 0, 0)),
                     pl.BlockSpec((1, VT_ROWS, N_META), lambda b, i, f=stream: (f(b), 0, 0))]
    scratch = [pltpu.VMEM((2, TQ, cols), F32), pltpu.VMEM((N_META, cols), F32),
               pltpu.VMEM((1, cols), F32), pltpu.VMEM((VT_ROWS, cols), F32)]
    return pl.pallas_call(
        functools.partial(_attn_prompt_kernel, n_prob=n_prob),
        grid=(per, steps),
        in_specs=in_specs,
        out_specs=[pl.BlockSpec((MLA_HEADS, TQ, MLA_KV_RANK), lambda b, i: (0, b * steps + i, 0))] * n_prob,
        out_shape=[jax.ShapeDtypeStruct((MLA_HEADS, per * seq, MLA_KV_RANK), BF16)] * n_prob,
        scratch_shapes=scratch * n_prob,
        compiler_params=_params("parallel", "arbitrary"),
        name="mla_attn_prompt",
    )(*([q, k_tiles, kc_meta, vt_tiles, vt_meta] * n_prob))


def _attn_full_kernel(*refs, n_q, streams, has_past):
    if has_past:
        q_ref, kn_ref, pl_ref, pr_ref, o_ref = refs
    else:
        q_ref, kn_ref, o_ref = refs
    rows = MLA_HEADS * n_q
    blocks = [slice(s * n_q, (s + 1) * n_q) for s in range(streams)]
    qs = [q_ref[:, rs, :].reshape(rows, QK_W) for rs in blocks]
    kns = [kn_ref[rs, :] for rs in blocks]
    s_new = [_dot_nt(q, kn) for q, kn in zip(qs, kns)]
    if has_past:
        lats = [pl_ref[s].astype(BF16) for s in range(streams)]
        rp_ts = [pr_ref[s].astype(BF16) for s in range(streams)]
        s_past = [_dot_nt(q[:, :MLA_KV_RANK], lat) + _dot(q[:, MLA_KV_RANK:MLA_KV_RANK + MLA_ROPE], rp_t)
                  for q, lat, rp_t in zip(qs, lats, rp_ts)]
    for s, rs in enumerate(blocks):
        m = jnp.max(s_new[s], axis=1, keepdims=True)
        if has_past:
            m = jnp.maximum(m, jnp.max(s_past[s], axis=1, keepdims=True))
        p_n = jnp.exp2(s_new[s] - m)
        l = jnp.sum(p_n, axis=1, keepdims=True)
        acc = _dot(p_n.astype(BF16), kns[s][:, :MLA_KV_RANK])
        if has_past:
            p_p = jnp.exp2(s_past[s] - m)
            l = l + jnp.sum(p_p, axis=1, keepdims=True)
            acc = acc + _dot(p_p.astype(BF16), lats[s])
        o_ref[:, rs, :] = (acc / l).astype(BF16).reshape(MLA_HEADS, n_q, MLA_KV_RANK)


def _attn_full(q, kc, past_lat, past_rope_t, *, row0, n_batch, n_q, streams, name):
    assert n_batch % streams == 0
    nb = n_q * streams
    base = row0 // nb
    has_past = past_lat is not None
    ins = [q, kc]
    specs = [pl.BlockSpec((MLA_HEADS, nb, QK_W), lambda b: (0, base + b, 0)),
             pl.BlockSpec((nb, QK_W), lambda b: (base + b, 0))]
    if has_past:
        past = past_lat.shape[1]
        ins += [past_lat, past_rope_t]
        specs += [pl.BlockSpec((streams, past, MLA_KV_RANK), lambda b: (b, 0, 0)),
                  pl.BlockSpec((streams, MLA_ROPE, past), lambda b: (b, 0, 0))]
    return pl.pallas_call(
        functools.partial(_attn_full_kernel, n_q=n_q, streams=streams, has_past=has_past),
        grid=(n_batch // streams,),
        in_specs=specs,
        out_specs=pl.BlockSpec((MLA_HEADS, nb, MLA_KV_RANK), lambda b: (0, b, 0)),
        out_shape=jax.ShapeDtypeStruct((MLA_HEADS, n_batch * n_q, MLA_KV_RANK), BF16),
        compiler_params=_params("parallel"),
        name=name,
    )(*ins)


def _mla_out_ffn_final_kernel(*refs, o_firsts, y_firsts, n_tiles, ffn_layers):
    ns, ny = len(o_firsts), len(y_firsts)
    o_refs = refs[:ns]
    h_ref, om_ref, hm_ref, wuv_ref, wo_ref = refs[ns:ns + 5]
    ffn_w = refs[ns + 5:ns + 9]
    fw_ref = refs[ns + 9]
    y_refs = refs[ns + 10:ns + 10 + ny]
    ym_ref = refs[ns + 10 + ny]
    ffn_scr = refs[ns + 11 + ny:]
    i = pl.program_id(0)

    def body(o_lat, h, first):
        y = _mla_out_compute(o_lat, h, wuv_ref, wo_ref)
        return _rms(_ffn_compute(y, ffn_w, ffn_scr, ffn_layers, first), fw_ref[...])

    def main_tile(first):
        y = body(_select(i, o_refs, o_firsts), h_ref[...], first)
        bounds = list(y_firsts[1:]) + [n_tiles]
        for y_ref, lo, hi in zip(y_refs, y_firsts, bounds):
            @pl.when((i >= lo) & (i < hi))
            def _(y_ref=y_ref):
                y_ref[...] = y

    pl.when(i == 0)(functools.partial(main_tile, True))
    pl.when((i > 0) & (i < n_tiles))(functools.partial(main_tile, False))

    @pl.when(i == n_tiles)
    def _():
        ym_ref[...] = body(om_ref[...], hm_ref[...], False)


def _mla_out_ffn_final(o_segs, y_segs, h, o_meta, h_meta, w_uv_t, w_out, ffn, final_w):
    n_tiles = sum(s[2] for s in o_segs)
    n_mt = h_meta.shape[0]
    in_specs = [_tiles((MLA_HEADS, TM, MLA_KV_RANK), 1, f, c) for _, f, c in o_segs]
    in_specs += [_tiles((TM, D_MODEL), 0, 0, n_tiles), _whole(o_meta.shape), _whole(h_meta.shape),
                 _resident(w_uv_t.shape), _resident(w_out.shape)]
    in_specs += _ffn_specs(ffn) + [_resident((1, D_MODEL))]
    return pl.pallas_call(
        functools.partial(_mla_out_ffn_final_kernel, o_firsts=tuple(s[1] for s in o_segs),
                          y_firsts=tuple(f for f, _ in y_segs), n_tiles=n_tiles, ffn_layers=_ffn_layers(ffn)),
        grid=(n_tiles + 1,),
        in_specs=in_specs,
        out_specs=[_tiles((TM, D_MODEL), 0, f, c) for f, c in y_segs] + [_whole((n_mt, D_MODEL))],
        out_shape=[jax.ShapeDtypeStruct((c * TM, D_MODEL), F32) for _, c in y_segs]
        + [jax.ShapeDtypeStruct((n_mt, D_MODEL), F32)],
        scratch_shapes=_ffn_scratch(),
        compiler_params=_params("arbitrary"),
        name="mla_out_ffn_final",
    )(*[s[0] for s in o_segs], h, o_meta, h_meta, w_uv_t, w_out, *_arrays(ffn), final_w.reshape(1, D_MODEL))


def _rope_tables(pos):
    half = MLA_ROPE // 2
    lane = jnp.arange(LANES)
    inv = ROPE_THETA ** (-(lane % half).astype(F32) / half)
    sign = jnp.where((lane // half) % 2 == 0, -1.0, 1.0).astype(F32)
    ang = pos[:, None] * inv[None, :]
    return jnp.cos(ang), jnp.sin(ang) * sign[None, :]


def kernel(x_prompt, x_sample, state_gla, cache_mla_latent, cache_mla_rope, meta_tokens, ffn1_norm, ffn1_w_gate, ffn1_w_up, ffn1_w_down, mix_norm, gla_w_in, gla_w_gate_up, gla_b_gate, gla_head_norm, gla_w_out, mla_w_down, mla_q_norm, mla_w_uq, mla_kv_norm, mla_w_uk, mla_w_uv, mla_w_out, ffn2_norm, ffn2_w_gate, ffn2_w_up, ffn2_w_down, final_norm):
    bp, seq, _ = x_prompt.shape
    bs, ls, _ = x_sample.shape
    past = cache_mla_latent.shape[2]
    n_fr = bp * seq
    n_sm = bs * ls
    n_mt = bp * N_META
    assert ffn1_norm.shape[0] == 2
    assert n_fr % TM == 0 and n_sm % TM == 0 and TM % TQ == 0 and seq % TQ == 0 and TQ % CHUNK == 0
    assert past % CHUNK == 0 and ls <= CHUNK and N_META <= CHUNK
    fr_tiles = n_fr // TM
    sm_tiles = n_sm // TM

    def vec(p):
        return p.reshape(p.shape[0], 1, p.shape[1])

    ffn1 = (vec(ffn1_norm), ffn1_w_gate, ffn1_w_up, ffn1_w_down)
    ffn2 = (vec(ffn2_norm), ffn2_w_gate, ffn2_w_up, ffn2_w_down)

    def at(params, layer):
        return [(p, layer) for p in params]

    x_meta = jnp.tile(meta_tokens.astype(F32), (bp, 1))
    gla_proj = [(vec(mix_norm), 0)] + at((jnp.swapaxes(gla_w_in, 1, 2).astype(BF16), gla_w_gate_up.astype(BF16),
                                          vec(gla_b_gate)), 0)
    (h, q, k, v, r, gl, h_m, q_m, k_m, v_m, r_m, gl_m) = _ffn_gla_proj(
        [(x_prompt.reshape(n_fr, D_MODEL), 0, fr_tiles), (x_sample.reshape(n_sm, D_MODEL), fr_tiles, sm_tiles)],
        x_meta, at(ffn1, 0), gla_proj)
    o_m, s_meta = _gla_scan(q_m, k_m, v_m, gl_m, None, row0=0, n_batch=bp, n_steps=1,
                            chunk=N_META, n_chunks=1, streams=bp, name="gla_scan_meta")
    o_f, s_p = _gla_scan(q, k, v, gl, s_meta, row0=0, n_batch=bp, n_steps=seq // GLA_STEP_ROWS,
                         chunk=CHUNK, n_chunks=GLA_STEP_ROWS // CHUNK, name="gla_scan_frames")
    o_s, s_s = _gla_scan(q, k, v, gl, state_gla.reshape(state_gla.shape[1:]), row0=n_fr, n_batch=bs,
                         n_steps=1, chunk=ls, n_chunks=1, streams=TQ // ls, name="gla_scan_sample")
    h, h_m = _gla_out_ffn([(o_f, 0, fr_tiles), (o_s, fr_tiles, sm_tiles)], r, h, o_m, r_m, h_m,
                          at((vec(gla_head_norm), gla_w_out.astype(BF16)), 0), at(ffn2, 0))

    seq_blocks = seq // TM
    cos_t, sin_t = _rope_tables(jnp.concatenate([N_META + jnp.arange(seq, dtype=F32),
                                                 jnp.tile(past + jnp.arange(ls, dtype=F32), TM // ls)]))
    cos_m, sin_m = _rope_tables(jnp.tile(jnp.arange(N_META, dtype=F32), bp))

    def table_tile(i):
        return jnp.where(i < fr_tiles, i % seq_blocks, seq_blocks)
    w_down = mla_w_down[0]
    w_down = jnp.concatenate([w_down, w_down[:, -MLA_ROPE:]], axis=1).astype(BF16)
    w_uq = mla_w_uq[0].reshape(MLA_Q_RANK, MLA_HEADS, MLA_NOPE + MLA_ROPE)
    w_qn = w_uq[:, :, :MLA_NOPE].reshape(MLA_Q_RANK, MLA_HEADS * MLA_NOPE).astype(BF16)
    w_qr = w_uq[:, :, MLA_NOPE:].reshape(MLA_Q_RANK, MLA_HEADS * MLA_ROPE).astype(BF16)
    w_uk_t = jnp.transpose(mla_w_uk[0], (1, 2, 0)).astype(BF16)
    w_uv_t = jnp.transpose(mla_w_uv[0], (1, 0, 2)).astype(BF16)
    mla_proj_w = (mix_norm[1].reshape(1, D_MODEL), w_down, mla_q_norm[0].reshape(1, MLA_Q_RANK), w_qn, w_qr,
                  mla_kv_norm[0].reshape(1, MLA_KV_RANK), w_uk_t)
    (h, qa, kc, ckv_p, kr_p, ckv_s, kr_s, vt, h_m, qa_m, kc_m) = _ffn_mla_proj(
        h, h_m, cos_t, sin_t, table_tile, cos_m, sin_m, at(ffn1, 1), mla_proj_w, bp, seq)
    k_tiles = kc.reshape((n_fr + n_sm) // TQ, TQ, QK_W)
    vt_meta = jnp.swapaxes(kc_m[:, :MLA_KV_RANK].reshape(bp, N_META, MLA_KV_RANK), 1, 2)
    vt_meta = jnp.concatenate([vt_meta, jnp.ones((bp, VT_ROWS - MLA_KV_RANK, N_META), BF16)], axis=1)
    ol_f = _attn_prompt(qa, k_tiles, kc_m, vt, vt_meta, n_batch=bp, seq=seq, n_prob=ATT_STREAMS)
    part_tiles = fr_tiles // ATT_STREAMS
    past_rope_t = jnp.swapaxes(cache_mla_rope.reshape(bs, past, MLA_ROPE), 1, 2)
    ol_s = _attn_full(qa, kc, cache_mla_latent.reshape(bs, past, MLA_KV_RANK), past_rope_t,
                      row0=n_fr, n_batch=bs, n_q=ls, streams=4, name="mla_attn_sample")
    ol_m = _attn_full(qa_m, kc_m, None, None, row0=0, n_batch=bp, n_q=N_META, streams=bp,
                      name="mla_attn_meta")
    y_prompt, y_sample, _ = _mla_out_ffn_final(
        [(o, p * part_tiles, part_tiles) for p, o in enumerate(ol_f)] + [(ol_s, fr_tiles, sm_tiles)],
        [(0, fr_tiles), (fr_tiles, sm_tiles)], h, ol_m, h_m, w_uv_t, mla_w_out[0].astype(BF16),
        at(ffn2, 1), final_norm)

    return (y_prompt.reshape(bp, seq, D_MODEL), y_sample.reshape(bs, ls, D_MODEL),
            s_p[None], s_s[None], ckv_p[None], kr_p[None],
            ckv_s.reshape(1, bs, ls, MLA_KV_RANK), kr_s.reshape(1, bs, ls, MLA_ROPE))
```

```python
import functools

import jax
import jax.numpy as jnp
from jax import lax
from jax.experimental import pallas as pl
from jax.experimental.pallas import tpu as pltpu

F32 = jnp.float32
BF16 = jnp.bfloat16

D_MODEL = 1024
D_FF = 2816
RMS_EPS = 1e-6
N_META = 16
CHUNK = 64

GLA_HEADS = 4
GLA_DK = 128
GLA_DV = 256
GLA_KEY = GLA_HEADS * GLA_DK
GLA_VAL = GLA_HEADS * GLA_DV
GLA_RANK = 16
GLA_GATE_NORMALIZER = 16.0

MLA_HEADS = 8
MLA_Q_RANK = 384
MLA_KV_RANK = 256
MLA_NOPE = 128
MLA_ROPE = 64
ROPE_THETA = 10000.0
LOG2_E = 1.4426950408889634

LANES = 128
BF16_ROWS = 16
VMEM_LIMIT = 58 * 1024 * 1024
QK_W = MLA_KV_RANK + LANES
VT_ROWS = MLA_KV_RANK + BF16_ROWS

TM = 512
FF_CHUNK = 256
TQ = 256
ATT_STREAMS = 2
GLA_GROUP_ROWS = 256
GLA_STEP_ROWS = 512
N_SLABS = D_FF // FF_CHUNK

NT_DIMS = (((1,), (1,)), ((), ()))
TN_DIMS = (((0,), (0,)), ((), ()))


def _dot(a, b):
    return jnp.dot(a, b, preferred_element_type=F32)


def _dot_nt(a, b):
    return lax.dot_general(a, b, NT_DIMS, preferred_element_type=F32)


def _dot_tn(a, b):
    return lax.dot_general(a, b, TN_DIMS, preferred_element_type=F32)


def _rms(x, w):
    return x * lax.rsqrt(jnp.mean(x * x, axis=-1, keepdims=True) + RMS_EPS) * w


def _silu(x):
    return x * jax.nn.sigmoid(x)


def _params(*sem):
    return pltpu.CompilerParams(dimension_semantics=sem, vmem_limit_bytes=VMEM_LIMIT)


def _resident(shape):
    zeros = (0,) * len(shape)
    return pl.BlockSpec(shape, lambda *_: zeros, pipeline_mode=pl.Buffered(1))


def _whole(shape):
    zeros = (0,) * len(shape)
    return pl.BlockSpec(shape, lambda *_: zeros)


def _tiles(block, row_axis=0, first=0, count=None):
    nd = len(block)

    def imap(i):
        t = i - first
        if count is not None:
            t = jnp.clip(t, 0, count - 1)
        idx = [0] * nd
        idx[row_axis] = t
        return tuple(idx)

    return pl.BlockSpec(block, imap)


def _select(i, refs, firsts):
    val = refs[0][...]
    for ref, first in zip(refs[1:], firsts[1:]):
        val = jnp.where(i >= first, ref[...], val)
    return val


def _ffn_compute(x, ffn_w, ffn_scr, layers, first):
    nw_ref, wg_hbm, wu_hbm, wd_hbm = ffn_w
    h_scr, wg_scr, wu_scr, wd_scr, land_g, land_u, land_d, sems = ffn_scr
    m = x.shape[0]
    xn = _rms(x, nw_ref[...]).astype(BF16)

    def slab_copy(kind, c):
        slot = c % 2
        cols = pl.ds(c * FF_CHUNK, FF_CHUNK)
        if kind == 0:
            src, dst = wg_hbm.at[layers[0], :, cols], land_g.at[slot]
        elif kind == 1:
            src, dst = wu_hbm.at[layers[1], :, cols], land_u.at[slot]
        else:
            src, dst = wd_hbm.at[layers[2], cols, :], land_d.at[slot]
        return pltpu.make_async_copy(src, dst, sems.at[kind, slot])

    def hidden(c):
        g = _dot(xn, wg_scr[c])
        u = _dot(xn, wu_scr[c])
        h_scr[:m, c * FF_CHUNK:(c + 1) * FF_CHUNK] = (_silu(g) * u).astype(BF16)

    def fill():
        for c in range(N_SLABS):
            hidden(c)

    def land_slab(kind, c, land, scr):
        slab_copy(kind, c).wait()
        scr[c] = land[c % 2].astype(BF16)
        if c + 2 < N_SLABS:
            slab_copy(kind, c + 2).start()

    def fill_streaming():
        for kind in (0, 1, 2):
            for c in range(2):
                slab_copy(kind, c).start()
        for c in range(N_SLABS):
            land_slab(0, c, land_g, wg_scr)
            land_slab(1, c, land_u, wu_scr)
            hidden(c)
            land_slab(2, c, land_d, wd_scr)

    if first:
        fill_streaming()
    else:
        fill()
    return x + 0.5 * _dot(h_scr[:m, :], wd_scr[...].reshape(D_FF, D_MODEL))


def _gla_proj_compute(x, nw_ref, w_ref, wgu_ref, bg_ref):
    c_v = 2 * GLA_KEY
    c_r = c_v + GLA_VAL
    c_g = c_r + GLA_VAL
    xn = _rms(x, nw_ref[...]).astype(BF16)
    q = _dot_nt(xn, w_ref[0:GLA_KEY, :]).astype(BF16)
    k = _dot_nt(xn, w_ref[GLA_KEY:c_v, :]).astype(BF16)
    v = _dot_nt(xn, w_ref[c_v:c_r, :]).astype(BF16)
    r = _dot_nt(xn, w_ref[c_r:c_g, :]).astype(BF16)
    gd = _dot_nt(xn, w_ref[c_g:, :])
    z = _dot(gd.astype(BF16), wgu_ref[...]) + bg_ref[...]
    log_sig = jnp.minimum(z, 0.0) - jnp.log1p(jnp.exp(-jnp.abs(z)))
    return q, k, v, r, log_sig * (1.0 / GLA_GATE_NORMALIZER)


def _gla_out_compute(o, r, h, hn_ref, wo_ref):
    parts = []
    for hd in range(GLA_HEADS):
        vs = slice(hd * GLA_DV, (hd + 1) * GLA_DV)
        on = _rms(o[:, vs].astype(F32), hn_ref[...])
        parts.append((on * _silu(r[:, vs].astype(F32))).astype(BF16))
    return h + _dot(jnp.concatenate(parts, axis=1), wo_ref[...])


def _mla_proj_compute(x, cos_t, sin_t, nw_ref, wd_ref, qn_ref, wqn_ref, wqr_ref, kvn_ref, wuk_ref,
                      q_out, kc_out, vt_out):
    m = x.shape[0]
    scale = (MLA_NOPE + MLA_ROPE) ** -0.5 * LOG2_E
    xn = _rms(x, nw_ref[...]).astype(BF16)
    xd = _dot(xn, wd_ref[...])
    cqn = _rms(xd[:, :MLA_Q_RANK], qn_ref[...]).astype(BF16)
    ckv = _rms(xd[:, MLA_Q_RANK:MLA_Q_RANK + MLA_KV_RANK], kvn_ref[...])
    lane = lax.broadcasted_iota(jnp.int32, (m, LANES), 1)
    first_half = (lane & (MLA_ROPE // 2)) == 0

    def rope(t):
        swapped = jnp.where(first_half, pltpu.roll(t, LANES - MLA_ROPE // 2, 1),
                            pltpu.roll(t, MLA_ROPE // 2, 1))
        return t * cos_t + swapped * sin_t

    kr2 = rope(xd[:, MLA_Q_RANK + MLA_KV_RANK:])
    kc_out[:, :MLA_KV_RANK] = ckv.astype(BF16)
    kc_out[:, MLA_KV_RANK:] = jnp.where(lane < MLA_ROPE, kr2, 0.0).astype(BF16)
    if vt_out is not None:
        for j in range(m // TQ):
            vt_out[j, :MLA_KV_RANK, :] = ckv[j * TQ:(j + 1) * TQ, :].T.astype(BF16)
            vt_out[j, MLA_KV_RANK:, :] = jnp.ones((VT_ROWS - MLA_KV_RANK, TQ), BF16)

    qn = _dot(cqn, wqn_ref[...])
    qr = _dot(cqn, wqr_ref[...])
    for c in range(MLA_HEADS // 2):
        rr = rope(qr[:, c * LANES:(c + 1) * LANES]) * scale
        for e in range(2):
            hd = 2 * c + e
            ql = _dot(qn[:, hd * MLA_NOPE:(hd + 1) * MLA_NOPE].astype(BF16), wuk_ref[hd]) * scale
            q_out[hd, :, :MLA_KV_RANK] = ql.astype(BF16)
            rot = rr if e == 0 else pltpu.roll(rr, MLA_ROPE, 1)
            q_out[hd, :, MLA_KV_RANK:] = rot.astype(BF16)
    return ckv, kr2[:, :MLA_ROPE]


def _mla_out_compute(o_lat, h, wuv_ref, wo_ref):
    parts = [_dot(o_lat[hd], wuv_ref[hd]).astype(BF16) for hd in range(MLA_HEADS)]
    return h + _dot(jnp.concatenate(parts, axis=1), wo_ref[...])


def _layer(shape, layer):
    zeros = (0,) * len(shape)
    return pl.BlockSpec((None,) + tuple(shape), lambda *_: (layer,) + zeros, pipeline_mode=pl.Buffered(1))


def _layer_specs(params):
    return [_layer(a.shape[1:], layer) for a, layer in params]


def _arrays(params):
    return [a for a, _ in params]


def _ffn_specs(ffn):
    nw, l_n = ffn[0]
    return [_layer(nw.shape[1:], l_n)] + [pl.BlockSpec(memory_space=pl.ANY)] * 3


def _ffn_layers(ffn):
    return tuple(layer for _, layer in ffn[1:])


def _ffn_scratch():
    return [pltpu.VMEM((TM, D_FF), BF16),
            pltpu.VMEM((N_SLABS, D_MODEL, FF_CHUNK), BF16), pltpu.VMEM((N_SLABS, D_MODEL, FF_CHUNK), BF16),
            pltpu.VMEM((N_SLABS, FF_CHUNK, D_MODEL), BF16),
            pltpu.VMEM((2, D_MODEL, FF_CHUNK), F32), pltpu.VMEM((2, D_MODEL, FF_CHUNK), F32),
            pltpu.VMEM((2, FF_CHUNK, D_MODEL), F32), pltpu.SemaphoreType.DMA((3, 2))]


N_FFN_SCRATCH = 8


def _ffn_gla_proj_kernel(*refs, x_firsts, n_tiles, ffn_layers):
    ns = len(x_firsts)
    x_refs, xm_ref = refs[:ns], refs[ns]
    ffn_w = refs[ns + 1:ns + 5]
    proj_w = refs[ns + 5:ns + 9]
    outs = refs[ns + 9:ns + 15]
    outs_m = refs[ns + 15:ns + 21]
    ffn_scr = refs[ns + 21:]
    i = pl.program_id(0)

    def body(x, o, first):
        y = _ffn_compute(x, ffn_w, ffn_scr, ffn_layers, first)
        o[0][...] = y
        for ref, val in zip(o[1:], _gla_proj_compute(y, *proj_w)):
            ref[...] = val

    @pl.when(i == 0)
    def _():
        body(_select(i, x_refs, x_firsts), outs, True)

    @pl.when((i > 0) & (i < n_tiles))
    def _():
        body(_select(i, x_refs, x_firsts), outs, False)

    @pl.when(i == n_tiles)
    def _():
        body(xm_ref[...], outs_m, False)


def _ffn_gla_proj(x_segs, x_meta, ffn, proj):
    n_tiles = sum(s[2] for s in x_segs)
    n_main = n_tiles * TM
    n_mt = x_meta.shape[0]
    widths = [(D_MODEL, F32), (GLA_KEY, BF16), (GLA_KEY, BF16), (GLA_VAL, BF16), (GLA_VAL, BF16),
              (GLA_KEY, F32)]
    in_specs = [_tiles((TM, D_MODEL), 0, f, c) for _, f, c in x_segs]
    in_specs += [_whole((n_mt, D_MODEL))] + _ffn_specs(ffn) + _layer_specs(proj)
    return pl.pallas_call(
        functools.partial(_ffn_gla_proj_kernel, x_firsts=tuple(s[1] for s in x_segs), n_tiles=n_tiles,
                          ffn_layers=_ffn_layers(ffn)),
        grid=(n_tiles + 1,),
        in_specs=in_specs,
        out_specs=[_tiles((TM, w), 0, 0, n_tiles) for w, _ in widths]
        + [_whole((n_mt, w)) for w, _ in widths],
        out_shape=[jax.ShapeDtypeStruct((n_main, w), d) for w, d in widths]
        + [jax.ShapeDtypeStruct((n_mt, w), d) for w, d in widths],
        scratch_shapes=_ffn_scratch(),
        compiler_params=_params("arbitrary"),
        name="ffn_gla_proj",
    )(*[s[0] for s in x_segs], x_meta, *_arrays(ffn), *_arrays(proj))


def _gla_scan_kernel(*refs, chunk, n_chunks, streams, group, n_steps, has_s0):
    q_ref, k_ref, v_ref, gl_ref = refs[:4]
    s0_ref = refs[4] if has_s0 else None
    o_ref, s_out_ref, st_scr = refs[4 + has_s0:]
    step = pl.program_id(1)

    @pl.when(step == 0)
    def _():
        for s in range(streams):
            for h in range(GLA_HEADS):
                if has_s0:
                    st_scr[s * GLA_HEADS + h] = s0_ref[s, h]
                else:
                    st_scr[s * GLA_HEADS + h] = jnp.zeros((GLA_DK, GLA_DV), F32)

    total = n_chunks * streams
    g_rows = chunk * group
    row = lax.broadcasted_iota(jnp.int32, (g_rows, g_rows), 0)
    col = lax.broadcasted_iota(jnp.int32, (g_rows, g_rows), 1)
    causal = ((row // chunk) == (col // chunk)) & (row >= col)
    tri = jnp.where(causal, 1.0, 0.0).astype(BF16)
    scale = GLA_DK ** -0.5
    groups = [slice(g * g_rows, (g + 1) * g_rows) for g in range(total // group)]
    in_group = [slice(c * chunk, (c + 1) * chunk) for c in range(group)]
    key_slices = [slice(h * GLA_DK, (h + 1) * GLA_DK) for h in range(GLA_HEADS)]
    val_slices = [slice(h * GLA_DV, (h + 1) * GLA_DV) for h in range(GLA_HEADS)]

    gcs = []
    for rows in groups:
        glog = gl_ref[rows, :]
        hi = glog.astype(BF16)
        lo = (glog - hi.astype(F32)).astype(BF16)
        gcs.append(_dot(tri, hi) + _dot(tri, lo))
    sub = lax.broadcasted_iota(jnp.int32, (GLA_DK, GLA_KEY), 0)
    dec_rows = jnp.zeros((GLA_DK, GLA_KEY), F32)
    qgs, kgs, kds = [], [], []
    for g, (rows, gc) in enumerate(zip(groups, gcs)):
        glast = [gc[rs.stop - 1:rs.stop, :] for rs in in_group]
        g_end = jnp.concatenate([jnp.broadcast_to(t, (chunk, GLA_KEY)) for t in glast], axis=0)
        for c, t in enumerate(glast):
            dec_rows = jnp.where(sub == g * group + c, jnp.exp(t), dec_rows)
        q = q_ref[rows, :].astype(F32) * scale
        k = k_ref[rows, :].astype(F32)
        qgs.append((q * jnp.exp(gc)).astype(BF16))
        kgs.append((k * jnp.exp(-gc)).astype(BF16))
        kds.append((k * jnp.exp(g_end - gc)).astype(BF16))

    values = [[v_ref[rows, vs] for vs in val_slices] for rows in groups]
    scores = [[_dot_nt(qg[:, ks], kg[:, ks]) for ks in key_slices] for qg, kg in zip(qgs, kgs)]
    updates = [[_dot_tn(kds[g][rs, ks], values[g][h][rs, :]) for g in range(len(groups)) for rs in in_group]
               for h, ks in enumerate(key_slices)]
    intra = [[_dot(jnp.where(causal, a, 0.0).astype(BF16), vh) for a, vh in zip(sg, vg)]
             for sg, vg in zip(scores, values)]
    for h, ks in enumerate(key_slices):
        dec_cols = dec_rows[:, ks].T
        states = []
        for s in range(streams):
            state = st_scr[s * GLA_HEADS + h]
            for c in range(s * n_chunks, (s + 1) * n_chunks):
                states.append(state.astype(BF16))
                state = state * dec_cols[:, c:c + 1] + updates[h][c]
            st_scr[s * GLA_HEADS + h] = state
        for g, rows in enumerate(groups):
            for c, rs in enumerate(in_group):
                out_rows = slice(rows.start + rs.start, rows.start + rs.stop)
                o_ref[out_rows, val_slices[h]] = (
                    intra[g][h][rs, :] + _dot(qgs[g][rs, ks], states[g * group + c])).astype(BF16)

    @pl.when(step == n_steps - 1)
    def _():
        for s in range(streams):
            for h in range(GLA_HEADS):
                s_out_ref[s, h] = st_scr[s * GLA_HEADS + h]


def _gla_scan(q, k, v, gl, s0, *, row0, n_batch, n_steps, chunk, n_chunks, name, streams=1):
    assert n_batch % streams == 0 and (streams == 1 or n_steps == 1)
    rb = chunk * n_chunks * streams
    base = row0 // rb

    def rmap(b, s):
        return (base + b * n_steps + s, 0)

    def omap(b, s):
        return (b * n_steps + s, 0)

    def smap(b, s):
        return (b, 0, 0, 0)

    ins = [q, k, v, gl]
    specs = [pl.BlockSpec((rb, GLA_KEY), rmap), pl.BlockSpec((rb, GLA_KEY), rmap),
             pl.BlockSpec((rb, GLA_VAL), rmap), pl.BlockSpec((rb, GLA_KEY), rmap)]
    state_block = (streams, GLA_HEADS, GLA_DK, GLA_DV)
    if s0 is not None:
        ins.append(s0)
        specs.append(pl.BlockSpec(state_block, smap))
    group = min(n_chunks * streams, GLA_GROUP_ROWS // chunk)
    assert (n_chunks * streams) % group == 0
    kern = functools.partial(_gla_scan_kernel, chunk=chunk, n_chunks=n_chunks, streams=streams,
                             group=group, n_steps=n_steps, has_s0=s0 is not None)
    return pl.pallas_call(
        kern,
        grid=(n_batch // streams, n_steps),
        in_specs=specs,
        out_specs=[pl.BlockSpec((rb, GLA_VAL), omap), pl.BlockSpec(state_block, smap)],
        out_shape=[jax.ShapeDtypeStruct((n_batch * n_steps * chunk * n_chunks, GLA_VAL), BF16),
                   jax.ShapeDtypeStruct((n_batch,) + state_block[1:], F32)],
        scratch_shapes=[pltpu.VMEM((streams * GLA_HEADS, GLA_DK, GLA_DV), F32)],
        compiler_params=_params("parallel", "arbitrary"),
        name=name,
    )(*ins)


def _gla_out_ffn_kernel(*refs, o_firsts, n_tiles, ffn_layers):
    ns = len(o_firsts)
    o_refs = refs[:ns]
    r_ref, h_ref, om_ref, rm_ref, hm_ref, hn_ref, wo_ref = refs[ns:ns + 7]
    ffn_w = refs[ns + 7:ns + 11]
    y_ref, ym_ref = refs[ns + 11:ns + 13]
    ffn_scr = refs[ns + 13:]
    i = pl.program_id(0)

    def body(o, r, h, first):
        return _ffn_compute(_gla_out_compute(o, r, h, hn_ref, wo_ref), ffn_w, ffn_scr, ffn_layers, first)

    @pl.when(i == 0)
    def _():
        y_ref[...] = body(_select(i, o_refs, o_firsts), r_ref[...], h_ref[...], True)

    @pl.when((i > 0) & (i < n_tiles))
    def _():
        y_ref[...] = body(_select(i, o_refs, o_firsts), r_ref[...], h_ref[...], False)

    @pl.when(i == n_tiles)
    def _():
        ym_ref[...] = body(om_ref[...], rm_ref[...], hm_ref[...], False)


def _gla_out_ffn(o_segs, r, h, o_meta, r_meta, h_meta, out, ffn):
    n_tiles = sum(s[2] for s in o_segs)
    n_mt = h_meta.shape[0]
    in_specs = [_tiles((TM, GLA_VAL), 0, f, c) for _, f, c in o_segs]
    in_specs += [_tiles((TM, GLA_VAL), 0, 0, n_tiles), _tiles((TM, D_MODEL), 0, 0, n_tiles),
                 _whole((n_mt, GLA_VAL)), _whole((n_mt, GLA_VAL)), _whole((n_mt, D_MODEL))]
    in_specs += _layer_specs(out) + _ffn_specs(ffn)
    return pl.pallas_call(
        functools.partial(_gla_out_ffn_kernel, o_firsts=tuple(s[1] for s in o_segs), n_tiles=n_tiles,
                          ffn_layers=_ffn_layers(ffn)),
        grid=(n_tiles + 1,),
        in_specs=in_specs,
        out_specs=[_tiles((TM, D_MODEL), 0, 0, n_tiles), _whole((n_mt, D_MODEL))],
        out_shape=[jax.ShapeDtypeStruct(h.shape, F32), jax.ShapeDtypeStruct(h_meta.shape, F32)],
        scratch_shapes=_ffn_scratch(),
        compiler_params=_params("arbitrary"),
        name="gla_out_ffn",
    )(*[s[0] for s in o_segs], r, h, o_meta, r_meta, h_meta, *_arrays(out), *_arrays(ffn))


def _ffn_mla_proj_kernel(*refs, n_tiles, n_prompt, stream_tiles, ffn_layers):
    x_ref, cos_ref, sin_ref, xm_ref, cosm_ref, sinm_ref = refs[:6]
    ffn_w = refs[6:10]
    proj_w = refs[10:17]
    y_ref, q_out, kc_out, ckv_p, kr_p, ckv_b, kr_b, vt_out = refs[17:25]
    ym_ref, qm_out, kcm_out = refs[25:28]
    ffn_scr = refs[28:28 + N_FFN_SCRATCH]
    ckv_buf, kr_buf, ckvm_buf, krm_buf, sems, meta_sems = refs[28 + N_FFN_SCRATCH:]
    first_tiles = n_prompt * stream_tiles
    i = pl.program_id(0)

    def tile_copies(t):
        slot = t % 2
        stream = t // stream_tiles
        rows = pl.ds(N_META + (t % stream_tiles) * TM, TM)
        return (pltpu.make_async_copy(ckv_buf.at[slot], ckv_p.at[stream, rows, :], sems.at[slot, 0]),
                pltpu.make_async_copy(kr_buf.at[slot], kr_p.at[stream, rows, :], sems.at[slot, 1]))

    def meta_copies():
        n_meta = xm_ref.shape[0] // n_prompt
        out = []
        for b in range(n_prompt):
            src = pl.ds(b * n_meta, n_meta)
            out.append(pltpu.make_async_copy(ckvm_buf.at[src], ckv_p.at[b, pl.ds(0, n_meta), :],
                                             meta_sems.at[0]))
            out.append(pltpu.make_async_copy(krm_buf.at[src], kr_p.at[b, pl.ds(0, n_meta), :],
                                             meta_sems.at[1]))
        return out

    def main_tile(first):
        y = _ffn_compute(x_ref[...], ffn_w, ffn_scr, ffn_layers, first)
        y_ref[...] = y
        ckv, kr = _mla_proj_compute(y, cos_ref[...], sin_ref[...], *proj_w, q_out, kc_out, vt_out)

        @pl.when(i < first_tiles)
        def _():
            @pl.when(i >= 2)
            def _():
                for cp in tile_copies(i - 2):
                    cp.wait()

            slot = i % 2
            ckv_buf[slot] = ckv
            kr_buf[slot] = kr
            for cp in tile_copies(i):
                cp.start()

        @pl.when(i >= first_tiles)
        def _():
            ckv_b[...] = ckv
            kr_b[...] = kr

    pl.when(i == 0)(functools.partial(main_tile, True))
    pl.when((i > 0) & (i < n_tiles))(functools.partial(main_tile, False))

    @pl.when(i == n_tiles)
    def _():
        for t in (first_tiles - 2, first_tiles - 1):
            for cp in tile_copies(t):
                cp.wait()
        y = _ffn_compute(xm_ref[...], ffn_w, ffn_scr, ffn_layers, False)
        ym_ref[...] = y
        ckv, kr = _mla_proj_compute(y, cosm_ref[...], sinm_ref[...], *proj_w, qm_out, kcm_out, None)
        ckvm_buf[...] = ckv
        krm_buf[...] = kr
        for cp in meta_copies():
            cp.start()
        for cp in meta_copies():
            cp.wait()


def _ffn_mla_proj(x, x_meta, cos_t, sin_t, table_tile, cos_m, sin_m, ffn, proj_w, n_prompt, seq):
    n_main = x.shape[0]
    n_tiles = n_main // TM
    n_mt = x_meta.shape[0]
    kt_per_tile = TM // TQ
    stream_tiles = seq // TM
    first_tiles = n_prompt * stream_tiles
    rest_tiles = n_tiles - first_tiles
    assert first_tiles >= 2 and n_mt == n_prompt * N_META
    any_space = pl.BlockSpec(memory_space=pl.ANY)

    def rows(w):
        return _tiles((TM, w), 0, 0, n_tiles)

    table = pl.BlockSpec((TM, LANES), lambda i: (table_tile(jnp.minimum(i, n_tiles - 1)), 0))
    in_specs = [rows(D_MODEL), table, table, _whole((n_mt, D_MODEL)), _whole((n_mt, LANES)),
                _whole((n_mt, LANES))] + _ffn_specs(ffn)
    in_specs += [_resident(w.shape) for w in proj_w]
    return pl.pallas_call(
        functools.partial(_ffn_mla_proj_kernel, n_tiles=n_tiles, n_prompt=n_prompt,
                          stream_tiles=stream_tiles, ffn_layers=_ffn_layers(ffn)),
        grid=(n_tiles + 1,),
        in_specs=in_specs,
        out_specs=[rows(D_MODEL), _tiles((MLA_HEADS, TM, QK_W), 1, 0, n_tiles), rows(QK_W),
                   any_space, any_space,
                   _tiles((TM, MLA_KV_RANK), 0, first_tiles, rest_tiles),
                   _tiles((TM, MLA_ROPE), 0, first_tiles, rest_tiles),
                   _tiles((kt_per_tile, VT_ROWS, TQ), 0, 0, n_tiles),
                   _whole((n_mt, D_MODEL)), _whole((MLA_HEADS, n_mt, QK_W)), _whole((n_mt, QK_W))],
        out_shape=[jax.ShapeDtypeStruct((n_main, D_MODEL), F32),
                   jax.ShapeDtypeStruct((MLA_HEADS, n_main, QK_W), BF16),
                   jax.ShapeDtypeStruct((n_main, QK_W), BF16),
                   jax.ShapeDtypeStruct((n_prompt, N_META + seq, MLA_KV_RANK), F32),
                   jax.ShapeDtypeStruct((n_prompt, N_META + seq, MLA_ROPE), F32),
                   jax.ShapeDtypeStruct((rest_tiles * TM, MLA_KV_RANK), F32),
                   jax.ShapeDtypeStruct((rest_tiles * TM, MLA_ROPE), F32),
                   jax.ShapeDtypeStruct((n_main // TQ, VT_ROWS, TQ), BF16),
                   jax.ShapeDtypeStruct((n_mt, D_MODEL), F32),
                   jax.ShapeDtypeStruct((MLA_HEADS, n_mt, QK_W), BF16),
                   jax.ShapeDtypeStruct((n_mt, QK_W), BF16)],
        scratch_shapes=_ffn_scratch() + [
            pltpu.VMEM((2, TM, MLA_KV_RANK), F32), pltpu.VMEM((2, TM, MLA_ROPE), F32),
            pltpu.VMEM((n_mt, MLA_KV_RANK), F32), pltpu.VMEM((n_mt, MLA_ROPE), F32),
            pltpu.SemaphoreType.DMA((2, 2)), pltpu.SemaphoreType.DMA((2,))],
        compiler_params=_params("arbitrary"),
        name="ffn_mla_proj",
    )(x, cos_t, sin_t, x_meta, cos_m, sin_m, *_arrays(ffn), *proj_w)


def _attn_prompt_kernel(*refs, n_prob):
    ins = [refs[5 * p:5 * p + 5] for p in range(n_prob)]
    o_refs = refs[5 * n_prob:6 * n_prob]
    scr = [refs[6 * n_prob + 4 * p:6 * n_prob + 4 * p + 4] for p in range(n_prob)]
    i = pl.program_id(1)
    cols = MLA_HEADS * TQ
    qs = [q_ref[...].reshape(cols, QK_W) for q_ref, *_ in ins]

    def scores(p, t):
        return _dot_nt(ins[p][1][t], qs[p])

    def consume(p, parts):
        _, _, m_scr, acc_scr = scr[p]
        m_cur = None
        for s, _ in parts:
            mx = jnp.max(s, axis=0, keepdims=True)
            m_cur = mx if m_cur is None else jnp.maximum(m_cur, mx)
        m_old = m_scr[...]
        m_new = jnp.maximum(m_old, m_cur)
        alpha = jnp.exp2(m_old - m_new)
        pv = None
        for s, vt in parts:
            term = _dot(vt, jnp.exp2(s - m_new).astype(BF16))
            pv = term if pv is None else pv + term
        acc_scr[...] = alpha * acc_scr[...] + pv
        m_scr[...] = m_new

    for p in range(n_prob):
        s_scr, sm_scr, m_scr, acc_scr = scr[p]
        m_scr[...] = jnp.full(m_scr.shape, -jnp.inf, F32)
        acc_scr[...] = jnp.zeros(acc_scr.shape, F32)
        s_first = _dot_nt(jnp.concatenate([ins[p][1][0], ins[p][2][...]], axis=0), qs[p])
        s_scr[0] = s_first[:TQ]
        sm_scr[...] = s_first[TQ:]

    def pair(jj, carry):
        j = 2 * jj
        for cur, nxt, t in ((0, 1, j), (1, 0, j + 1)):
            for p in range(n_prob):
                scr[p][0][nxt] = scores(p, t + 1)
            for p in range(n_prob):
                consume(p, [(scr[p][0][cur], ins[p][3][t])])
        return carry

    lax.fori_loop(0, lax.shift_right_logical(i, 1), pair, 0)

    @pl.when((i & 1) == 1)
    def _():
        for p in range(n_prob):
            scr[p][0][1] = scores(p, i)
        for p in range(n_prob):
            consume(p, [(scr[p][0][0], ins[p][3][i - 1])])

    key = lax.broadcasted_iota(jnp.int32, (TQ, cols), 0)
    qry = lax.broadcasted_iota(jnp.int32, (TQ, cols), 1) % TQ
    visible = (key // CHUNK) <= (qry // CHUNK)
    for p in range(n_prob):
        s_own = jnp.where(visible, scr[p][0][i & 1], -jnp.inf)
        consume(p, [(s_own, ins[p][3][i]), (scr[p][1][...], ins[p][4][0])])

    for p in range(n_prob):
        acc_scr = scr[p][3]
        for h in range(MLA_HEADS):
            cs = slice(h * TQ, (h + 1) * TQ)
            total = acc_scr[MLA_KV_RANK:MLA_KV_RANK + 1, cs]
            o_refs[p][h] = (acc_scr[:MLA_KV_RANK, cs] / total).T.astype(BF16)


def _attn_prompt(q, k_tiles, kc_meta, vt_tiles, vt_meta, *, n_batch, seq, n_prob):
    steps = seq // TQ
    cols = MLA_HEADS * TQ
    per = n_batch // n_prob
    assert per * n_prob == n_batch
    in_specs = []
    for p in range(n_prob):
        def stream(b, p=p):
            return p * per + b
        in_specs += [pl.BlockSpec((MLA_HEADS, TQ, QK_W), lambda b, i, f=stream: (0, f(b) * steps + i, 0)),
                     pl.BlockSpec((steps, TQ, QK_W), lambda b, i, f=stream: (f(b), 0, 0)),
                     pl.BlockSpec((N_META, QK_W), lambda b, i, f=stream: (f(b), 0)),
                     pl.BlockSpec((steps, VT_ROWS, TQ), lambda b, i, f=stream: (f(b), 0, 0)),
                     pl.BlockSpec((1, VT_ROWS, N_META), lambda b, i, f=stream: (f(b), 0, 0))]
    scratch = [pltpu.VMEM((2, TQ, cols), F32), pltpu.VMEM((N_META, cols), F32),
               pltpu.VMEM((1, cols), F32), pltpu.VMEM((VT_ROWS, cols), F32)]
    return pl.pallas_call(
        functools.partial(_attn_prompt_kernel, n_prob=n_prob),
        grid=(per, steps),
        in_specs=in_specs,
        out_specs=[pl.BlockSpec((MLA_HEADS, TQ, MLA_KV_RANK), lambda b, i: (0, b * steps + i, 0))] * n_prob,
        out_shape=[jax.ShapeDtypeStruct((MLA_HEADS, per * seq, MLA_KV_RANK), BF16)] * n_prob,
        scratch_shapes=scratch * n_prob,
        compiler_params=_params("parallel", "arbitrary"),
        name="mla_attn_prompt",
    )(*([q, k_tiles, kc_meta, vt_tiles, vt_meta] * n_prob))


def _attn_full_kernel(*refs, n_q, streams, has_past):
    if has_past:
        q_ref, kn_ref, pl_ref, pr_ref, o_ref = refs
    else:
        q_ref, kn_ref, o_ref = refs
    rows = MLA_HEADS * n_q
    blocks = [slice(s * n_q, (s + 1) * n_q) for s in range(streams)]
    qs = [q_ref[:, rs, :].reshape(rows, QK_W) for rs in blocks]
    kns = [kn_ref[rs, :] for rs in blocks]
    s_new = [_dot_nt(q, kn) for q, kn in zip(qs, kns)]
    if has_past:
        lats = [pl_ref[s].astype(BF16) for s in range(streams)]
        rp_ts = [pr_ref[s].astype(BF16) for s in range(streams)]
        s_past = [_dot_nt(q[:, :MLA_KV_RANK], lat) + _dot(q[:, MLA_KV_RANK:MLA_KV_RANK + MLA_ROPE], rp_t)
                  for q, lat, rp_t in zip(qs, lats, rp_ts)]
    for s, rs in enumerate(blocks):
        m = jnp.max(s_new[s], axis=1, keepdims=True)
        if has_past:
            m = jnp.maximum(m, jnp.max(s_past[s], axis=1, keepdims=True))
        p_n = jnp.exp2(s_new[s] - m)
        l = jnp.sum(p_n, axis=1, keepdims=True)
        acc = _dot(p_n.astype(BF16), kns[s][:, :MLA_KV_RANK])
        if has_past:
            p_p = jnp.exp2(s_past[s] - m)
            l = l + jnp.sum(p_p, axis=1, keepdims=True)
            acc = acc + _dot(p_p.astype(BF16), lats[s])
        o_ref[:, rs, :] = (acc / l).astype(BF16).reshape(MLA_HEADS, n_q, MLA_KV_RANK)


def _attn_full(q, kc, past_lat, past_rope_t, *, row0, n_batch, n_q, streams, name):
    assert n_batch % streams == 0
    nb = n_q * streams
    base = row0 // nb
    has_past = past_lat is not None
    ins = [q, kc]
    specs = [pl.BlockSpec((MLA_HEADS, nb, QK_W), lambda b: (0, base + b, 0)),
             pl.BlockSpec((nb, QK_W), lambda b: (base + b, 0))]
    if has_past:
        past = past_lat.shape[1]
        ins += [past_lat, past_rope_t]
        specs += [pl.BlockSpec((streams, past, MLA_KV_RANK), lambda b: (b, 0, 0)),
                  pl.BlockSpec((streams, MLA_ROPE, past), lambda b: (b, 0, 0))]
    return pl.pallas_call(
        functools.partial(_attn_full_kernel, n_q=n_q, streams=streams, has_past=has_past),
        grid=(n_batch // streams,),
        in_specs=specs,
        out_specs=pl.BlockSpec((MLA_HEADS, nb, MLA_KV_RANK), lambda b: (0, b, 0)),
        out_shape=jax.ShapeDtypeStruct((MLA_HEADS, n_batch * n_q, MLA_KV_RANK), BF16),
        compiler_params=_params("parallel"),
        name=name,
    )(*ins)


def _mla_out_ffn_final_kernel(*refs, o_firsts, y_firsts, n_tiles, ffn_layers):
    ns, ny = len(o_firsts), len(y_firsts)
    o_refs = refs[:ns]
    h_ref, om_ref, hm_ref, wuv_ref, wo_ref = refs[ns:ns + 5]
    ffn_w = refs[ns + 5:ns + 9]
    fw_ref = refs[ns + 9]
    y_refs = refs[ns + 10:ns + 10 + ny]
    ym_ref = refs[ns + 10 + ny]
    ffn_scr = refs[ns + 11 + ny:]
    i = pl.program_id(0)

    def body(o_lat, h, first):
        y = _mla_out_compute(o_lat, h, wuv_ref, wo_ref)
        return _rms(_ffn_compute(y, ffn_w, ffn_scr, ffn_layers, first), fw_ref[...])

    def main_tile(first):
        y = body(_select(i, o_refs, o_firsts), h_ref[...], first)
        bounds = list(y_firsts[1:]) + [n_tiles]
        for y_ref, lo, hi in zip(y_refs, y_firsts, bounds):
            @pl.when((i >= lo) & (i < hi))
            def _(y_ref=y_ref):
                y_ref[...] = y

    pl.when(i == 0)(functools.partial(main_tile, True))
    pl.when((i > 0) & (i < n_tiles))(functools.partial(main_tile, False))

    @pl.when(i == n_tiles)
    def _():
        ym_ref[...] = body(om_ref[...], hm_ref[...], False)


def _mla_out_ffn_final(o_segs, y_segs, h, o_meta, h_meta, w_uv_t, w_out, ffn, final_w):
    n_tiles = sum(s[2] for s in o_segs)
    n_mt = h_meta.shape[0]
    in_specs = [_tiles((MLA_HEADS, TM, MLA_KV_RANK), 1, f, c) for _, f, c in o_segs]
    in_specs += [_tiles((TM, D_MODEL), 0, 0, n_tiles), _whole(o_meta.shape), _whole(h_meta.shape),
                 _resident(w_uv_t.shape), _resident(w_out.shape)]
    in_specs += _ffn_specs(ffn) + [_resident((1, D_MODEL))]
    return pl.pallas_call(
        functools.partial(_mla_out_ffn_final_kernel, o_firsts=tuple(s[1] for s in o_segs),
                          y_firsts=tuple(f for f, _ in y_segs), n_tiles=n_tiles, ffn_layers=_ffn_layers(ffn)),
        grid=(n_tiles + 1,),
        in_specs=in_specs,
        out_specs=[_tiles((TM, D_MODEL), 0, f, c) for f, c in y_segs] + [_whole((n_mt, D_MODEL))],
        out_shape=[jax.ShapeDtypeStruct((c * TM, D_MODEL), F32) for _, c in y_segs]
        + [jax.ShapeDtypeStruct((n_mt, D_MODEL), F32)],
        scratch_shapes=_ffn_scratch(),
        compiler_params=_params("arbitrary"),
        name="mla_out_ffn_final",
    )(*[s[0] for s in o_segs], h, o_meta, h_meta, w_uv_t, w_out, *_arrays(ffn), final_w.reshape(1, D_MODEL))


def _rope_tables(pos):
    half = MLA_ROPE // 2
    lane = jnp.arange(LANES)
    inv = ROPE_THETA ** (-(lane % half).astype(F32) / half)
    sign = jnp.where((lane // half) % 2 == 0, -1.0, 1.0).astype(F32)
    ang = pos[:, None] * inv[None, :]
    return jnp.cos(ang), jnp.sin(ang) * sign[None, :]


def kernel(x_prompt, x_sample, state_gla, cache_mla_latent, cache_mla_rope, meta_tokens, ffn1_norm, ffn1_w_gate, ffn1_w_up, ffn1_w_down, mix_norm, gla_w_in, gla_w_gate_up, gla_b_gate, gla_head_norm, gla_w_out, mla_w_down, mla_q_norm, mla_w_uq, mla_kv_norm, mla_w_uk, mla_w_uv, mla_w_out, ffn2_norm, ffn2_w_gate, ffn2_w_up, ffn2_w_down, final_norm):
    bp, seq, _ = x_prompt.shape
    bs, ls, _ = x_sample.shape
    past = cache_mla_latent.shape[2]
    n_fr = bp * seq
    n_sm = bs * ls
    n_mt = bp * N_META
    assert ffn1_norm.shape[0] == 2
    assert n_fr % TM == 0 and n_sm % TM == 0 and TM % TQ == 0 and seq % TQ == 0 and TQ % CHUNK == 0
    assert past % CHUNK == 0 and ls <= CHUNK and N_META <= CHUNK
    fr_tiles = n_fr // TM
    sm_tiles = n_sm // TM

    def vec(p):
        return p.reshape(p.shape[0], 1, p.shape[1])

    ffn1 = (vec(ffn1_norm), ffn1_w_gate, ffn1_w_up, ffn1_w_down)
    ffn2 = (vec(ffn2_norm), ffn2_w_gate, ffn2_w_up, ffn2_w_down)

    def at(params, layer):
        return [(p, layer) for p in params]

    x_meta = jnp.tile(meta_tokens.astype(F32), (bp, 1))
    gla_proj = [(vec(mix_norm), 0)] + at((jnp.swapaxes(gla_w_in, 1, 2).astype(BF16), gla_w_gate_up.astype(BF16),
                                          vec(gla_b_gate)), 0)
    (h, q, k, v, r, gl, h_m, q_m, k_m, v_m, r_m, gl_m) = _ffn_gla_proj(
        [(x_prompt.reshape(n_fr, D_MODEL), 0, fr_tiles), (x_sample.reshape(n_sm, D_MODEL), fr_tiles, sm_tiles)],
        x_meta, at(ffn1, 0), gla_proj)
    o_m, s_meta = _gla_scan(q_m, k_m, v_m, gl_m, None, row0=0, n_batch=bp, n_steps=1,
                            chunk=N_META, n_chunks=1, streams=bp, name="gla_scan_meta")
    o_f, s_p = _gla_scan(q, k, v, gl, s_meta, row0=0, n_batch=bp, n_steps=seq // GLA_STEP_ROWS,
                         chunk=CHUNK, n_chunks=GLA_STEP_ROWS // CHUNK, name="gla_scan_frames")
    o_s, s_s = _gla_scan(q, k, v, gl, state_gla.reshape(state_gla.shape[1:]), row0=n_fr, n_batch=bs,
                         n_steps=1, chunk=ls, n_chunks=1, streams=TQ // ls, name="gla_scan_sample")
    h, h_m = _gla_out_ffn([(o_f, 0, fr_tiles), (o_s, fr_tiles, sm_tiles)], r, h, o_m, r_m, h_m,
                          at((vec(gla_head_norm), gla_w_out.astype(BF16)), 0), at(ffn2, 0))

    seq_blocks = seq // TM
    cos_t, sin_t = _rope_tables(jnp.concatenate([N_META + jnp.arange(seq, dtype=F32),
                                                 jnp.tile(past + jnp.arange(ls, dtype=F32), TM // ls)]))
    cos_m, sin_m = _rope_tables(jnp.tile(jnp.arange(N_META, dtype=F32), bp))

    def table_tile(i):
        return jnp.where(i < fr_tiles, i % seq_blocks, seq_blocks)
    w_down = mla_w_down[0]
    w_down = jnp.concatenate([w_down, w_down[:, -MLA_ROPE:]], axis=1).astype(BF16)
    w_uq = mla_w_uq[0].reshape(MLA_Q_RANK, MLA_HEADS, MLA_NOPE + MLA_ROPE)
    w_qn = w_uq[:, :, :MLA_NOPE].reshape(MLA_Q_RANK, MLA_HEADS * MLA_NOPE).astype(BF16)
    w_qr = w_uq[:, :, MLA_NOPE:].reshape(MLA_Q_RANK, MLA_HEADS * MLA_ROPE).astype(BF16)
    w_uk_t = jnp.transpose(mla_w_uk[0], (1, 2, 0)).astype(BF16)
    w_uv_t = jnp.transpose(mla_w_uv[0], (1, 0, 2)).astype(BF16)
    mla_proj_w = (mix_norm[1].reshape(1, D_MODEL), w_down, mla_q_norm[0].reshape(1, MLA_Q_RANK), w_qn, w_qr,
                  mla_kv_norm[0].reshape(1, MLA_KV_RANK), w_uk_t)
    (h, qa, kc, ckv_p, kr_p, ckv_s, kr_s, vt, h_m, qa_m, kc_m) = _ffn_mla_proj(
        h, h_m, cos_t, sin_t, table_tile, cos_m, sin_m, at(ffn1, 1), mla_proj_w, bp, seq)
    k_tiles = kc.reshape((n_fr + n_sm) // TQ, TQ, QK_W)
    vt_meta = jnp.swapaxes(kc_m[:, :MLA_KV_RANK].reshape(bp, N_META, MLA_KV_RANK), 1, 2)
    vt_meta = jnp.concatenate([vt_meta, jnp.ones((bp, VT_ROWS - MLA_KV_RANK, N_META), BF16)], axis=1)
    ol_f = _attn_prompt(qa, k_tiles, kc_m, vt, vt_meta, n_batch=bp, seq=seq, n_prob=ATT_STREAMS)
    part_tiles = fr_tiles // ATT_STREAMS
    past_rope_t = jnp.swapaxes(cache_mla_rope.reshape(bs, past, MLA_ROPE), 1, 2)
    ol_s = _attn_full(qa, kc, cache_mla_latent.reshape(bs, past, MLA_KV_RANK), past_rope_t,
                      row0=n_fr, n_batch=bs, n_q=ls, streams=4, name="mla_attn_sample")
    ol_m = _attn_full(qa_m, kc_m, None, None, row0=0, n_batch=bp, n_q=N_META, streams=bp,
                      name="mla_attn_meta")
    y_prompt, y_sample, _ = _mla_out_ffn_final(
        [(o, p * part_tiles, part_tiles) for p, o in enumerate(ol_f)] + [(ol_s, fr_tiles, sm_tiles)],
        [(0, fr_tiles), (fr_tiles, sm_tiles)], h, ol_m, h_m, w_uv_t, mla_w_out[0].astype(BF16),
        at(ffn2, 1), final_norm)

    return (y_prompt.reshape(bp, seq, D_MODEL), y_sample.reshape(bs, ls, D_MODEL),
            s_p[None], s_s[None], ckv_p[None], kr_p[None],
            ckv_s.reshape(1, bs, ls, MLA_KV_RANK), kr_s.reshape(1, bs, ls, MLA_ROPE))
```
